```python
import math
import numpy as np
import jax, jax.numpy as jnp
from jax import lax

D_MODEL = 1024
BATCH = 2
SEQ = 8192
DEPTH = 1
DEC_BATCH = 32
DEC_SEQ = 16
PAST_LEN = 4096

CHUNK = 64
Q_BLOCK = 128
FOX_DIM = 64
FOX_HEADS = (D_MODEL // 2) // FOX_DIM
FOX_WIDTH = FOX_HEADS * FOX_DIM
GDN_DK = 128
GDN_DV = 128
GDN_HEADS = (D_MODEL // 2) // GDN_DV
GDN_QK = GDN_HEADS * GDN_DK
GDN_WIDTH = GDN_HEADS * GDN_DV
GDN_CONV_CH = 2 * GDN_QK + GDN_WIDTH
GDN_CONV = 4
MIX_WIDTH = FOX_WIDTH + GDN_WIDTH
D_FF = ((8 * D_MODEL // 3 + 127) // 128) * 128
FFN_CONV = 3
N_IN = 3 * FOX_WIDTH + FOX_HEADS + GDN_CONV_CH + 2 * GDN_HEADS + GDN_WIDTH
EPS = 1e-6

kernel_name = 'hybrid_fox_gdn_convffn_stream_step'


def rmsnorm(x, w):
    xf = x.astype(jnp.float32)
    y = xf * lax.rsqrt(jnp.mean(xf * xf, axis=-1, keepdims=True) + EPS)
    return (y * w.astype(jnp.float32)).astype(x.dtype)


def l2norm(x):
    xf = x.astype(jnp.float32)
    return xf * lax.rsqrt(jnp.sum(xf * xf, axis=-1, keepdims=True) + EPS)


def causal_dwconv(x_hist, w):
    width = w.shape[0]
    T = x_hist.shape[1] - (width - 1)
    acc = x_hist[:, 0:T] * w[0]
    for i in range(1, width):
        acc = acc + x_hist[:, i:i + T] * w[i]
    return acc


def fox_block(qi, cq, qpos, k, v, ck):
    scale = FOX_DIM ** -0.5
    logits = jnp.einsum('bqhd,bkhd->bhqk', qi, k, preferred_element_type=jnp.float32) * scale
    logits = logits + cq[..., :, None] - ck[..., None, :]
    mask = jnp.arange(k.shape[1])[None, :] <= qpos[:, None]
    logits = jnp.where(mask, logits, -jnp.inf)
    p = jax.nn.softmax(logits, axis=-1).astype(v.dtype)
    return jnp.einsum('bhqk,bkhd->bqhd', p, v)


def fox_prompt(q, k, v, logf):
    B, S, H, D = q.shape
    nb = S // Q_BLOCK
    c = jnp.cumsum(logf, axis=1).transpose(0, 2, 1)
    qb = q.reshape(B, nb, Q_BLOCK, H, D).transpose(1, 0, 2, 3, 4)
    cqb = c.reshape(B, H, nb, Q_BLOCK).transpose(2, 0, 1, 3)
    posb = jnp.arange(S).reshape(nb, Q_BLOCK)
    o = lax.map(lambda a: fox_block(a[0], a[1], a[2], k, v, c), (qb, cqb, posb))
    return o.transpose(1, 0, 2, 3, 4).reshape(B, S, H * D)


def gdn_chunked(q, k, v, g, beta, s0, chunk):
    B, T, H, DK = q.shape
    DV = v.shape[-1]
    n = T // chunk

    def blocks(a):
        return a.reshape(B, n, chunk, H, a.shape[-1]).transpose(1, 0, 3, 2, 4)

    qc, kc, vc = blocks(q), blocks(k), blocks(v)
    gc = g.reshape(B, n, chunk, H).transpose(1, 0, 3, 2)
    bc = beta.reshape(B, n, chunk, H).transpose(1, 0, 3, 2)
    G = jnp.cumsum(gc, axis=-1)
    incl = jnp.tril(jnp.ones((chunk, chunk), dtype=bool))
    strict = jnp.tril(jnp.ones((chunk, chunk), dtype=bool), -1)
    decay = jnp.exp(jnp.where(incl, G[..., :, None] - G[..., None, :], -jnp.inf))
    kb = kc * bc[..., None]
    L = jnp.where(strict, jnp.einsum('nbhid,nbhjd->nbhij', kb, kc) * decay, 0.0)
    eye = jnp.eye(chunk, dtype=jnp.float32)
    rhs = jnp.concatenate([vc * bc[..., None], kb * jnp.exp(G)[..., None]], axis=-1)
    sol = lax.linalg.triangular_solve(eye + L, rhs, left_side=True, lower=True)
    u, w = sol[..., :DV], sol[..., DV:]
    a_qk = jnp.einsum('nbhid,nbhjd->nbhij', qc, kc) * decay

    def step(S, xs):
        qi, ki, Gi, ui, wi, ai = xs
        delta = ui - jnp.einsum('bhck,bhkv->bhcv', wi, S)
        o = (jnp.einsum('bhck,bhkv->bhcv', qi * jnp.exp(Gi)[..., None], S)
             + jnp.einsum('bhij,bhjv->bhiv', ai, delta))
        g_last = Gi[..., -1:]
        S = (S * jnp.exp(g_last)[..., None]
             + jnp.einsum('bhck,bhcv->bhkv', ki * jnp.exp(g_last - Gi)[..., None], delta))
        return S, o

    s_final, o = lax.scan(step, s0, (qc, kc, G, u, w, a_qk))
    o = o.transpose(1, 0, 3, 2, 4).reshape(B, T, H, DV)
    return o, s_final


def hybrid_layer(x, fox_past, gdn_conv_hist, gdn_state, ffn_conv_hist,
                 w_in, b_fox_f, gdn_conv_w, gdn_a_log, gdn_dt_bias, gdn_norm_w, w_o,
                 norm_mix_w, norm_ffn_w, w_up, ffn_conv_w, ffn_conv_b, w_down):
    f32 = jnp.float32
    B, T, _ = x.shape
    xn = rmsnorm(x, norm_mix_w)
    proj = jnp.einsum('btd,de->bte', xn, w_in)
    sizes = (FOX_WIDTH, FOX_WIDTH, FOX_WIDTH, FOX_HEADS, GDN_CONV_CH, GDN_HEADS, GDN_HEADS, GDN_WIDTH)
    idx = np.cumsum(sizes)[:-1].tolist()
    fq, fk, fv, ff, gqkv, ga, gb, gz = jnp.split(proj, idx, axis=-1)

    q = fq.reshape(B, T, FOX_HEADS, FOX_DIM)
    k = fk.reshape(B, T, FOX_HEADS, FOX_DIM)
    v = fv.reshape(B, T, FOX_HEADS, FOX_DIM)
    logf = jax.nn.log_sigmoid(ff.astype(f32) + b_fox_f.astype(f32))
    if fox_past is None:
        o_fox = fox_prompt(q, k, v, logf)
    else:
        k_past, v_past, logf_past = fox_past
        P = k_past.shape[1]
        k_all = jnp.concatenate([k_past.astype(k.dtype), k], axis=1)
        v_all = jnp.concatenate([v_past.astype(v.dtype), v], axis=1)
        c = jnp.cumsum(jnp.concatenate([logf_past.astype(f32), logf], axis=1), axis=1).transpose(0, 2, 1)
        o_fox = fox_block(q, c[:, :, P:], P + jnp.arange(T), k_all, v_all, c).reshape(B, T, FOX_WIDTH)

    hist = jnp.concatenate([gdn_conv_hist.astype(gqkv.dtype), gqkv], axis=1)
    new_gdn_conv = hist[:, T:]
    qkv = jax.nn.silu(causal_dwconv(hist, gdn_conv_w))
    gq, gk, gv = jnp.split(qkv, [GDN_QK, 2 * GDN_QK], axis=-1)
    qd = l2norm(gq.reshape(B, T, GDN_HEADS, GDN_DK)) * (GDN_DK ** -0.5)
    kd = l2norm(gk.reshape(B, T, GDN_HEADS, GDN_DK))
    vd = gv.reshape(B, T, GDN_HEADS, GDN_DV).astype(f32)
    g = -jnp.exp(gdn_a_log.astype(f32)) * jax.nn.softplus(ga.astype(f32) + gdn_dt_bias.astype(f32))
    beta = jax.nn.sigmoid(gb.astype(f32))
    o_gdn, new_gdn_state = gdn_chunked(qd, kd, vd, g, beta, gdn_state.astype(f32), min(CHUNK, T))
    z = jax.nn.silu(gz.astype(f32)).reshape(B, T, GDN_HEADS, GDN_DV)
    o_gdn = (rmsnorm(o_gdn, gdn_norm_w) * z).reshape(B, T, GDN_WIDTH)

    mix = jnp.concatenate([o_fox.astype(x.dtype), o_gdn.astype(x.dtype)], axis=-1)
    h = x + jnp.einsum('bte,ed->btd', mix, w_o)

    hn = rmsnorm(h, norm_ffn_w)
    up = jnp.einsum('btd,df->btf', hn, w_up)
    uhist = jnp.concatenate([ffn_conv_hist.astype(up.dtype), up], axis=1)
    new_ffn_conv = uhist[:, T:]
    uc = causal_dwconv(uhist, ffn_conv_w) + ffn_conv_b
    gate, val = jnp.split(uc, 2, axis=-1)
    h = h + jnp.einsum('btf,fd->btd', jax.nn.silu(gate) * val, w_down)
    return h, (k, v, logf, new_gdn_state, new_gdn_conv, new_ffn_conv)


def setup_inputs(seed: int = 0) -> dict:
    key = jax.random.key(seed)
    ks = jax.random.split(key, 24)
    f32 = jnp.float32

    def nrm(k, shape, s):
        return s * jax.random.normal(k, shape, f32)

    x_prompt = nrm(ks[0], (BATCH, SEQ, D_MODEL), 1.0)
    x_sample = nrm(ks[1], (DEC_BATCH, DEC_SEQ, D_MODEL), 1.0)
    cache_fox_k = nrm(ks[2], (DEPTH, DEC_BATCH, PAST_LEN, FOX_HEADS, FOX_DIM), 1.0)
    cache_fox_v = nrm(ks[3], (DEPTH, DEC_BATCH, PAST_LEN, FOX_HEADS, FOX_DIM), 1.0)
    cache_fox_logf = jax.nn.log_sigmoid(3.0 + nrm(ks[4], (DEPTH, DEC_BATCH, PAST_LEN, FOX_HEADS), 0.5))
    state_gdn = nrm(ks[5], (DEPTH, DEC_BATCH, GDN_HEADS, GDN_DK, GDN_DV), 0.3)
    state_gdn_conv = nrm(ks[6], (DEPTH, DEC_BATCH, GDN_CONV - 1, GDN_CONV_CH), 1.0)
    state_ffn_conv = nrm(ks[7], (DEPTH, DEC_BATCH, FFN_CONV - 1, 2 * D_FF), 1.0)
    w_in = nrm(ks[8], (DEPTH, D_MODEL, N_IN), D_MODEL ** -0.5)
    b_fox_f = 3.0 + nrm(ks[9], (DEPTH, FOX_HEADS), 0.5)
    gdn_conv_w = nrm(ks[10], (DEPTH, GDN_CONV, GDN_CONV_CH), GDN_CONV ** -0.5)
    gdn_a_log = jnp.log(jax.random.uniform(ks[11], (DEPTH, GDN_HEADS), f32, 1.0, 16.0))
    dt = jnp.exp(jax.random.uniform(ks[12], (DEPTH, GDN_HEADS), f32, math.log(1e-3), math.log(1e-1)))
    gdn_dt_bias = dt + jnp.log(-jnp.expm1(-dt))
    gdn_norm_w = 1.0 + nrm(ks[13], (DEPTH, GDN_DV), 0.02)
    w_o = nrm(ks[14], (DEPTH, MIX_WIDTH, D_MODEL), MIX_WIDTH ** -0.5)
    norm_mix_w = 1.0 + nrm(ks[15], (DEPTH, D_MODEL), 0.02)
    norm_ffn_w = 1.0 + nrm(ks[16], (DEPTH, D_MODEL), 0.02)
    w_up = nrm(ks[17], (DEPTH, D_MODEL, 2 * D_FF), D_MODEL ** -0.5)
    ffn_conv_w = nrm(ks[18], (DEPTH, FFN_CONV, 2 * D_FF), FFN_CONV ** -0.5)
    ffn_conv_b = nrm(ks[19], (DEPTH, 2 * D_FF), 0.01)
    w_down = nrm(ks[20], (DEPTH, D_FF, D_MODEL), D_FF ** -0.5)
    norm_final_w = 1.0 + nrm(ks[21], (D_MODEL,), 0.02)
    return {'x_prompt': x_prompt, 'x_sample': x_sample,
            'cache_fox_k': cache_fox_k, 'cache_fox_v': cache_fox_v, 'cache_fox_logf': cache_fox_logf,
            'state_gdn': state_gdn, 'state_gdn_conv': state_gdn_conv, 'state_ffn_conv': state_ffn_conv,
            'w_in': w_in, 'b_fox_f': b_fox_f, 'gdn_conv_w': gdn_conv_w, 'gdn_a_log': gdn_a_log,
            'gdn_dt_bias': gdn_dt_bias, 'gdn_norm_w': gdn_norm_w, 'w_o': w_o,
            'norm_mix_w': norm_mix_w, 'norm_ffn_w': norm_ffn_w, 'w_up': w_up,
            'ffn_conv_w': ffn_conv_w, 'ffn_conv_b': ffn_conv_b, 'w_down': w_down,
            'norm_final_w': norm_final_w}


def reference(x_prompt, x_sample, cache_fox_k, cache_fox_v, cache_fox_logf, state_gdn,
              state_gdn_conv, state_ffn_conv, w_in, b_fox_f, gdn_conv_w, gdn_a_log, gdn_dt_bias,
              gdn_norm_w, w_o, norm_mix_w, norm_ffn_w, w_up, ffn_conv_w, ffn_conv_b, w_down,
              norm_final_w):
    hp, hs = x_prompt, x_sample
    Bp = x_prompt.shape[0]
    sp_list, ss_list = [], []
    for l in range(DEPTH):
        lw = (w_in[l], b_fox_f[l], gdn_conv_w[l], gdn_a_log[l], gdn_dt_bias[l], gdn_norm_w[l], w_o[l],
              norm_mix_w[l], norm_ffn_w[l], w_up[l], ffn_conv_w[l], ffn_conv_b[l], w_down[l])
        hp, sp = hybrid_layer(
            hp, None,
            jnp.zeros((Bp, GDN_CONV - 1, GDN_CONV_CH), hp.dtype),
            jnp.zeros((Bp, GDN_HEADS, GDN_DK, GDN_DV), jnp.float32),
            jnp.zeros((Bp, FFN_CONV - 1, 2 * D_FF), hp.dtype), *lw)
        hs, ss = hybrid_layer(
            hs, (cache_fox_k[l], cache_fox_v[l], cache_fox_logf[l]),
            state_gdn_conv[l], state_gdn[l], state_ffn_conv[l], *lw)
        sp_list.append(sp)
        ss_list.append(ss)
    pk, pv, pf, ps, pc, pfc = [jnp.stack(a) for a in zip(*sp_list)]
    sk, sv, sf, sst, sc, sfc = [jnp.stack(a) for a in zip(*ss_list)]
    y_prompt = rmsnorm(hp, norm_final_w)
    y_sample = rmsnorm(hs, norm_final_w)
    return (y_prompt, y_sample, pk, pv, pf, ps, pc, pfc, sk, sv, sf, sst, sc, sfc)
```

```python
import functools

import jax
import jax.numpy as jnp
from jax import lax
from jax.experimental import pallas as pl
from jax.experimental.pallas import tpu as pltpu

F32 = jnp.float32
BF16 = jnp.bfloat16

D_MODEL = 1024
FOX_DIM = 64
FOX_HEADS = 8
FOX_WIDTH = FOX_HEADS * FOX_DIM
GDN_DK = 128
GDN_DV = 128
GDN_HEADS = 4
GDN_QK = GDN_HEADS * GDN_DK
GDN_WIDTH = GDN_HEADS * GDN_DV
GDN_CONV_CH = 2 * GDN_QK + GDN_WIDTH
GDN_CONV = 4
D_FF = 2816
FFN_CONV = 3
EPS = 1e-6
CHUNK = 64

LANES = 128
SUBLANES = 8
VMEM_LIMIT = 56 * 1024 * 1024
MASKED = -1e30

LOGF_LANE = 0
G_LANE = FOX_HEADS
BETA_LANE = FOX_HEADS + GDN_HEADS

COL_FOX = 0
COL_GQKV = 3 * FOX_WIDTH
COL_GZ = COL_GQKV + GDN_CONV_CH
COL_SMALL = COL_GZ + GDN_WIDTH
N_COLS = COL_SMALL + LANES


def _dot(a, b):
    return jnp.dot(a, b, preferred_element_type=F32)


def _dot_nt(a, b):
    return lax.dot_general(a, b, (((1,), (1,)), ((), ())), preferred_element_type=F32)


def _dot_tn(a, b):
    return lax.dot_general(a, b, (((0,), (0,)), ((), ())), preferred_element_type=F32)


def _split3(x):
    hi = x.astype(BF16)
    r = x - hi.astype(F32)
    lo = r.astype(BF16)
    lo2 = (r - lo.astype(F32)).astype(BF16)
    return hi, lo, lo2


def _exact_left(mat, x):
    hi, lo, lo2 = _split3(x)
    return _dot(mat, hi) + _dot(mat, lo) + _dot(mat, lo2)


def _exact_right(x, mat):
    hi, lo, lo2 = _split3(x)
    return _dot(hi, mat) + _dot(lo, mat) + _dot(lo2, mat)


def _dot3(a, b):
    ah = a.astype(BF16)
    al = (a - ah.astype(F32)).astype(BF16)
    bh = b.astype(BF16)
    bl = (b - bh.astype(F32)).astype(BF16)
    return _dot(ah, bh) + (_dot(ah, bl) + _dot(al, bh))


def _sigmoid(x):
    return 1.0 / (1.0 + jnp.exp(-x))


def _tri(n, strict=False):
    r = lax.broadcasted_iota(jnp.int32, (n, n), 0)
    c = lax.broadcasted_iota(jnp.int32, (n, n), 1)
    return (r > c) if strict else (r >= c)


def _inproj_kernel(x_ref, nw_ref, w_ref, prm_ref, *refs, tiles_per_seq, prompt):
    if prompt:
        q_ref, k_ref, v_ref, kb_ref, vb_ref, g_ref, z_ref, sm_ref, c_ref, ct_ref, carry_ref = refs
    else:
        q_ref, k_ref, v_ref, g_ref, z_ref, sm_ref = refs
    x = x_ref[...]
    xn = x * lax.rsqrt(jnp.mean(x * x, axis=-1, keepdims=True) + EPS) * nw_ref[...]
    xb = xn.astype(BF16)

    q_ref[...] = (_dot(xb, w_ref[:, 0:FOX_WIDTH]) * (FOX_DIM ** -0.5)).astype(BF16)
    kf = _dot(xb, w_ref[:, FOX_WIDTH:2 * FOX_WIDTH])
    vf = _dot(xb, w_ref[:, 2 * FOX_WIDTH:3 * FOX_WIDTH])
    k_ref[...] = kf
    v_ref[...] = vf
    if prompt:
        kb_ref[...] = kf.astype(BF16)
        vb_ref[...] = vf.astype(BF16)
    g_ref[...] = _dot(xb, w_ref[:, COL_GQKV:COL_GZ])
    z_ref[...] = _dot(xb, w_ref[:, COL_GZ:COL_SMALL])

    z = _dot(xb, w_ref[:, COL_SMALL:N_COLS])
    lane = lax.broadcasted_iota(jnp.int32, z.shape, 1)
    zb = z + prm_ref[0:1, :]
    e = jnp.exp(-jnp.abs(zb))
    l1p = jnp.log(1.0 + e)
    logf = jnp.minimum(zb, 0.0) - l1p
    g = -jnp.exp(prm_ref[1:2, :]) * (jnp.maximum(zb, 0.0) + l1p)
    beta = _sigmoid(z)
    sm = jnp.where(lane < G_LANE, logf,
                   jnp.where(lane < BETA_LANE, g,
                             jnp.where(lane < BETA_LANE + GDN_HEADS, beta, 0.0)))
    sm_ref[...] = sm

    if prompt:
        i = pl.program_id(0)

        @pl.when(i % tiles_per_seq == 0)
        def _():
            carry_ref[...] = jnp.zeros_like(carry_ref)

        tm = sm.shape[0]
        c = _exact_left(_tri(tm).astype(BF16), sm) + carry_ref[0:1, :]
        carry_ref[0:1, :] = c[tm - 1:tm, :]
        c_ref[...] = c
        ct_ref[0, 0] = c.T[0:SUBLANES, :]


def _inproj(x2d, norm_w, w_cat, prm, *, tm, seq_len, prompt):
    n = x2d.shape[0]
    assert n % tm == 0 and (not prompt or seq_len % tm == 0)
    tps = seq_len // tm if prompt else 1
    row = lambda w: pl.BlockSpec((tm, w), lambda i: (i, 0))
    const = lambda a: pl.BlockSpec(a.shape, lambda i: (0,) * a.ndim, pipeline_mode=pl.Buffered(1))
    out_shape = [jax.ShapeDtypeStruct((n, FOX_WIDTH), BF16),
                 jax.ShapeDtypeStruct((n, FOX_WIDTH), F32),
                 jax.ShapeDtypeStruct((n, FOX_WIDTH), F32)]
    out_specs = [row(FOX_WIDTH), row(FOX_WIDTH), row(FOX_WIDTH)]
    if prompt:
        out_shape += [jax.ShapeDtypeStruct((n, FOX_WIDTH), BF16)] * 2
        out_specs += [row(FOX_WIDTH)] * 2
    out_shape += [jax.ShapeDtypeStruct((n, GDN_CONV_CH), F32),
                  jax.ShapeDtypeStruct((n, GDN_WIDTH), F32),
                  jax.ShapeDtypeStruct((n, LANES), F32)]
    out_specs += [row(GDN_CONV_CH), row(GDN_WIDTH), row(LANES)]
    scratch = []
    if prompt:
        out_shape += [jax.ShapeDtypeStruct((n, LANES), F32),
                      jax.ShapeDtypeStruct((n // seq_len, tps, SUBLANES, tm), F32)]
        out_specs += [row(LANES),
                      pl.BlockSpec((1, 1, SUBLANES, tm), lambda i: (i // tps, i % tps, 0, 0))]
        scratch = [pltpu.VMEM((SUBLANES, LANES), F32)]
    return pl.pallas_call(
        functools.partial(_inproj_kernel, tiles_per_seq=tps, prompt=prompt),
        grid=(n // tm,),
        in_specs=[row(D_MODEL), const(norm_w), const(w_cat), const(prm)],
        out_specs=out_specs, out_shape=out_shape, scratch_shapes=scratch,
        compiler_params=pltpu.CompilerParams(dimension_semantics=("arbitrary",),
                                             vmem_limit_bytes=VMEM_LIMIT),
        name="inproj_prompt" if prompt else "inproj_sample",
    )(x2d, norm_w, w_cat, prm)


def _fox_prompt_kernel(q_ref, k_ref, v_ref, cq_ref, ck_ref, o_ref, *, t):
    pair = pl.program_id(1)
    i = pl.program_id(2)
    q = q_ref[0]
    lane = lax.broadcasted_iota(jnp.int32, (t, LANES), 1)
    low = lane < FOX_DIM
    cq_all = cq_ref[...]
    qh, cqh = [], []
    for h in range(2):
        qh.append(jnp.where(low if h == 0 else jnp.logical_not(low), q, jnp.zeros_like(q)))
        cqh.append(jnp.sum(jnp.where(lane == 2 * pair + h, cq_all, 0.0), axis=1, keepdims=True))

    def block(j, carry, diag):
        start = pl.multiple_of(j * t, t)
        kj = k_ref[0, pl.ds(start, t), :]
        vj = v_ref[0, pl.ds(start, t), :]
        ckj = ck_ref[0, j, 0]
        out = []
        for h in range(2):
            m, l, acc = carry[h]
            s = _dot_nt(qh[h], kj) + (cqh[h] - ckj[h:h + 1, :])
            if diag:
                s = jnp.where(_tri(t), s, MASKED)
            m_new = jnp.maximum(m, jnp.max(s, axis=1, keepdims=True))
            alpha = jnp.exp(m - m_new)
            p = jnp.exp(s - m_new)
            l = alpha * l + jnp.sum(p, axis=1, keepdims=True)
            acc = alpha * acc + _dot(p.astype(BF16), vj)
            out.append((m_new, l, acc))
        return tuple(out)

    init = tuple((jnp.full((t, 1), MASKED, F32), jnp.zeros((t, 1), F32), jnp.zeros((t, LANES), F32))
                 for _ in range(2))
    carry = lax.fori_loop(0, i, lambda j, c: block(j, c, False), init)
    (_, l0, a0), (_, l1, a1) = block(i, carry, True)
    o_ref[0] = jnp.where(low, a0 / l0, a1 / l1).astype(BF16)


def _fox_prompt(q, kb, vb, c, ct, *, t):
    b, s, _ = q.shape
    nblk = s // t
    assert s % t == 0 and ct.shape == (b, nblk, FOX_HEADS // 2, 2, t)
    npair = FOX_HEADS // 2
    return pl.pallas_call(
        functools.partial(_fox_prompt_kernel, t=t),
        grid=(b, npair, nblk),
        in_specs=[pl.BlockSpec((1, t, LANES), lambda bi, p, i: (bi, i, p)),
                  pl.BlockSpec((1, s, LANES), lambda bi, p, i: (bi, 0, p)),
                  pl.BlockSpec((1, s, LANES), lambda bi, p, i: (bi, 0, p)),
                  pl.BlockSpec((t, LANES), lambda bi, p, i: (bi * nblk + i, 0)),
                  pl.BlockSpec((1, nblk, 1, 2, t), lambda bi, p, i: (bi, 0, p, 0, 0))],
        out_specs=pl.BlockSpec((1, t, LANES), lambda bi, p, i: (bi, i, p)),
        out_shape=jax.ShapeDtypeStruct((b, s, FOX_WIDTH), BF16),
        compiler_params=pltpu.CompilerParams(
            dimension_semantics=("arbitrary", "arbitrary", "arbitrary"), vmem_limit_bytes=VMEM_LIMIT),
        name="fox_prompt",
    )(q, kb, vb, c, ct)


def _fox_sample_kernel(q_ref, kn_ref, vn_ref, sm_ref, lft_ref, kp_ref, vp_ref, o_ref,
                       rk_ref, m_ref, l_ref, acc_ref, *, kb, t):
    j = pl.program_id(1)
    nkv = pl.num_programs(1)
    rows = FOX_HEADS * t
    lane = lax.broadcasted_iota(jnp.int32, (t, FOX_WIDTH), 1)

    q = q_ref[0].astype(F32)
    qbd = jnp.concatenate(
        [jnp.where((lane >= h * FOX_DIM) & (lane < (h + 1) * FOX_DIM), q, 0.0) for h in range(FOX_HEADS)],
        axis=0).astype(BF16)

    lf_new = sm_ref[0]
    cn = _exact_left(_tri(t).astype(BF16), lf_new)

    @pl.when(j == 0)
    def _():
        p = lft_ref.shape[2]
        nb = p // LANES
        x = jnp.concatenate([lft_ref[0, :, i * LANES:(i + 1) * LANES] for i in range(nb)], axis=0)
        r = lax.broadcasted_iota(jnp.int32, (LANES, LANES), 0)
        cc = lax.broadcasted_iota(jnp.int32, (LANES, LANES), 1)
        y = _exact_right(x, (r > cc).astype(BF16))
        tot = jnp.sum(x, axis=1, keepdims=True)
        off = jnp.zeros((FOX_HEADS, 1), F32)
        for i in range(nb - 1, -1, -1):
            rk_ref[:, i * LANES:(i + 1) * LANES] = y[i * FOX_HEADS:(i + 1) * FOX_HEADS, :] + off
            off = off + tot[i * FOX_HEADS:(i + 1) * FOX_HEADS, :]
        m_ref[...] = jnp.full_like(m_ref, MASKED)
        l_ref[...] = jnp.zeros_like(l_ref)
        acc_ref[...] = jnp.zeros_like(acc_ref)

    def update(s, v):
        m = m_ref[:, 0:1]
        m_new = jnp.maximum(m, jnp.max(s, axis=1, keepdims=True))
        alpha = jnp.exp(m - m_new)
        p = jnp.exp(s - m_new)
        l_ref[...] = jnp.broadcast_to(alpha * l_ref[:, 0:1] + jnp.sum(p, axis=1, keepdims=True), l_ref.shape)
        acc_ref[...] = alpha * acc_ref[...] + _dot(p.astype(BF16), v)
        m_ref[...] = jnp.broadcast_to(m_new, m_ref.shape)

    start = pl.multiple_of(j * kb, kb)
    rk = rk_ref[:, pl.ds(start, kb)]
    bias = jnp.concatenate([cn[:, h:h + 1] + rk[h:h + 1, :] for h in range(FOX_HEADS)], axis=0)
    s = _dot_nt(qbd, kp_ref[0].astype(BF16)) + bias
    update(s, vp_ref[0].astype(BF16))

    @pl.when(j == nkv - 1)
    def _():
        r = lax.broadcasted_iota(jnp.int32, (t, t), 0)
        cc = lax.broadcasted_iota(jnp.int32, (t, t), 1)
        after = (r > cc).astype(F32)
        tri = _tri(t).astype(BF16)
        d = jnp.concatenate([_exact_left(tri, lf_new[:, h:h + 1] * after) for h in range(FOX_HEADS)], axis=0)
        causal = jnp.concatenate([_tri(t)] * FOX_HEADS, axis=0)
        s_new = jnp.where(causal, _dot_nt(qbd, kn_ref[0].astype(BF16)) + d, MASKED)
        update(s_new, vn_ref[0].astype(BF16))
        o_full = acc_ref[...] / l_ref[:, 0:1]
        o = jnp.zeros((t, FOX_WIDTH), F32)
        for h in range(FOX_HEADS):
            sel = (lane >= h * FOX_DIM) & (lane < (h + 1) * FOX_DIM)
            o = o + jnp.where(sel, o_full[h * t:(h + 1) * t, :], 0.0)
        o_ref[0] = o.astype(BF16)


def _fox_sample(q, kn, vn, sm, lft, kp, vp, *, kb):
    b, t, _ = q.shape
    p = kp.shape[1]
    assert p % kb == 0 and kb % LANES == 0
    rows = FOX_HEADS * t
    per_b = lambda shape: pl.BlockSpec((1,) + shape, lambda bi, j: (bi, 0, 0))
    return pl.pallas_call(
        functools.partial(_fox_sample_kernel, kb=kb, t=t),
        grid=(b, p // kb),
        in_specs=[per_b((t, FOX_WIDTH)), per_b((t, FOX_WIDTH)), per_b((t, FOX_WIDTH)), per_b((t, LANES)),
                  per_b((FOX_HEADS, p)),
                  pl.BlockSpec((1, kb, FOX_WIDTH), lambda bi, j: (bi, j, 0)),
                  pl.BlockSpec((1, kb, FOX_WIDTH), lambda bi, j: (bi, j, 0))],
        out_specs=per_b((t, FOX_WIDTH)),
        out_shape=jax.ShapeDtypeStruct((b, t, FOX_WIDTH), BF16),
        scratch_shapes=[pltpu.VMEM((FOX_HEADS, p), F32),
                        pltpu.VMEM((rows, LANES), F32), pltpu.VMEM((rows, LANES), F32),
                        pltpu.VMEM((rows, FOX_WIDTH), F32)],
        compiler_params=pltpu.CompilerParams(dimension_semantics=("arbitrary", "arbitrary"),
                                             vmem_limit_bytes=VMEM_LIMIT),
        name="fox_sample",
    )(q, kn, vn, sm, lft, kp, vp)


def _gdn_kernel(cur_ref, hist_ref, sm_ref, z_ref, *refs, c, zero_init):
    if zero_init:
        cw_ref, nw_ref, o_ref, sout_ref, xs_ref, s_ref = refs
    else:
        s0_ref, cw_ref, nw_ref, o_ref, sout_ref, xs_ref, s_ref = refs
    i = pl.program_id(1)

    @pl.when(i == 0)
    def _():
        if zero_init:
            s_ref[...] = jnp.zeros_like(s_ref)
        else:
            s_ref[...] = s0_ref[0]

    hist = hist_ref[0]
    if zero_init:
        hist = jnp.where(i > 0, hist, 0.0)
    xs_ref[0:SUBLANES, :] = hist
    xs_ref[SUBLANES:SUBLANES + c, :] = cur_ref[0]
    base = SUBLANES - (GDN_CONV - 1)
    acc = xs_ref[base:base + c, :] * cw_ref[0:1, :]
    for tap in range(1, GDN_CONV):
        acc = acc + xs_ref[base + tap:base + tap + c, :] * cw_ref[tap:tap + 1, :]
    qkv = acc * _sigmoid(acc)

    sm = sm_ref[0]
    incl = _tri(c)
    strict = _tri(c, strict=True)
    eye = incl & jnp.logical_not(strict)
    gs = _exact_left(incl.astype(BF16), sm)
    ones = jnp.ones((c, c), BF16)
    ident = eye.astype(F32)

    for h in range(GDN_HEADS):
        qh = qkv[:, h * GDN_DK:(h + 1) * GDN_DK]
        kh = qkv[:, GDN_QK + h * GDN_DK:GDN_QK + (h + 1) * GDN_DK]
        vh = qkv[:, 2 * GDN_QK + h * GDN_DV:2 * GDN_QK + (h + 1) * GDN_DV]
        qn = qh * (lax.rsqrt(jnp.sum(qh * qh, axis=-1, keepdims=True) + EPS) * (GDN_DK ** -0.5))
        kn = kh * lax.rsqrt(jnp.sum(kh * kh, axis=-1, keepdims=True) + EPS)
        g = gs[:, G_LANE + h:G_LANE + h + 1]
        g_last = gs[c - 1:c, G_LANE + h:G_LANE + h + 1]
        beta = sm[:, BETA_LANE + h:BETA_LANE + h + 1]
        eg = jnp.exp(g)

        g_row = _exact_left(ones, jnp.where(eye, g, 0.0))
        decay = jnp.exp(jnp.where(incl, g - g_row, MASKED))
        kbeta = kn * beta
        knb = kn.astype(BF16)
        lmat = jnp.where(strict, _dot_nt(kbeta.astype(BF16), knb) * decay, 0.0)
        a_qk = _dot_nt(qn.astype(BF16), knb) * decay

        tinv = ident - lmat
        power = _dot3(lmat, lmat)
        span = 2
        while True:
            tinv = tinv + _dot3(tinv, power)
            span *= 2
            if span >= c:
                break
            power = _dot3(power, power)

        rhs = jnp.concatenate([vh * beta, kbeta * eg], axis=1)
        sol = _dot3(tinv, rhs)
        u = sol[:, :GDN_DV]
        w = sol[:, GDN_DV:]

        sh = s_ref[h]
        shb = sh.astype(BF16)
        delta = u - _dot(w.astype(BF16), shb)
        db = delta.astype(BF16)
        o = _dot((qn * eg).astype(BF16), shb) + _dot(a_qk.astype(BF16), db)
        kd = kn * jnp.exp(g_last - g)
        s_ref[h] = sh * jnp.exp(g_last) + _dot_tn(kd.astype(BF16), db)

        on = o * lax.rsqrt(jnp.mean(o * o, axis=-1, keepdims=True) + EPS) * nw_ref[...]
        zh = z_ref[0, :, h * GDN_DV:(h + 1) * GDN_DV]
        o_ref[0, :, h * GDN_DV:(h + 1) * GDN_DV] = (on * (zh * _sigmoid(zh))).astype(BF16)

    @pl.when(i == pl.num_programs(1) - 1)
    def _():
        sout_ref[0] = s_ref[...]


def _gdn(gqkv, hist, sm, gz, s0, conv_w, norm_w, *, c):
    b, t, _ = gqkv.shape
    assert t % c == 0 and c % SUBLANES == 0
    zero_init = hist is None
    nchunk = t // c
    cpb = c // SUBLANES
    blk = lambda w: pl.BlockSpec((1, c, w), lambda bi, i: (bi, i, 0))
    const = lambda a: pl.BlockSpec(a.shape, lambda bi, i: (0,) * a.ndim)
    state_spec = pl.BlockSpec((1, GDN_HEADS, GDN_DK, GDN_DV), lambda bi, i: (bi, 0, 0, 0))
    if zero_init:
        hist_arr = gqkv
        hist_spec = pl.BlockSpec((1, SUBLANES, GDN_CONV_CH),
                                 lambda bi, i: (bi, jnp.maximum(i * cpb - 1, 0), 0))
        extra, extra_specs = [], []
    else:
        assert nchunk == 1
        hist_arr = hist
        hist_spec = pl.BlockSpec((1, SUBLANES, GDN_CONV_CH), lambda bi, i: (bi, 0, 0))
        extra, extra_specs = [s0], [state_spec]
    return pl.pallas_call(
        functools.partial(_gdn_kernel, c=c, zero_init=zero_init),
        grid=(b, nchunk),
        in_specs=[blk(GDN_CONV_CH), hist_spec, blk(LANES), blk(GDN_WIDTH)] + extra_specs
                 + [const(conv_w), const(norm_w)],
        out_specs=[blk(GDN_WIDTH), state_spec],
        out_shape=[jax.ShapeDtypeStruct((b, t, GDN_WIDTH), BF16),
                   jax.ShapeDtypeStruct((b, GDN_HEADS, GDN_DK, GDN_DV), F32)],
        scratch_shapes=[pltpu.VMEM((SUBLANES + c, GDN_CONV_CH), F32),
                        pltpu.VMEM((GDN_HEADS, GDN_DK, GDN_DV), F32)],
        compiler_params=pltpu.CompilerParams(dimension_semantics=("arbitrary", "arbitrary"),
                                             vmem_limit_bytes=VMEM_LIMIT),
        name="gdn_prompt" if zero_init else "gdn_sample",
    )(gqkv, hist_arr, sm, gz, *extra, conv_w, norm_w)


FFN_COLS = 256


def _ffn_kernel(of_ref, og_ref, x_ref, wo_ref, nfw_ref, wup_ref, cw_ref, cb_ref, wdn_ref, nlw_ref, *refs,
                nseq, rows, tiles_per_seq, has_state):
    if has_state:
        st_ref, y_ref, nst_ref, wg_ref, wv_ref, act_ref = refs
    else:
        y_ref, nst_ref, wg_ref, wv_ref, act_ref, carry_ref = refs
    i = pl.program_id(0)
    hist0 = SUBLANES - (FFN_CONV - 1)

    h = (x_ref[...] + _dot(of_ref[...], wo_ref[0:FOX_WIDTH, :])
         + _dot(og_ref[...], wo_ref[FOX_WIDTH:FOX_WIDTH + GDN_WIDTH, :]))
    hn = (h * lax.rsqrt(jnp.mean(h * h, axis=-1, keepdims=True) + EPS) * nfw_ref[...]).astype(BF16)

    if not has_state:
        @pl.when(i % tiles_per_seq == 0)
        def _():
            carry_ref[...] = jnp.zeros_like(carry_ref)

    def conv_cols(buf_ref, col):
        up = _dot(hn, wup_ref[:, col:col + FFN_COLS])
        buf_ref[:, SUBLANES:SUBLANES + rows, :] = up.reshape(nseq, rows, FFN_COLS)
        if has_state:
            buf_ref[:, hist0:SUBLANES, :] = st_ref[:, :, col:col + FFN_COLS]
        else:
            buf_ref[:, hist0:SUBLANES, :] = carry_ref[:, hist0:SUBLANES, col:col + FFN_COLS]
        last = buf_ref[:, SUBLANES + rows - (FFN_CONV - 1):SUBLANES + rows, :]
        nst_ref[:, :, col:col + FFN_COLS] = last
        if not has_state:
            carry_ref[:, hist0:SUBLANES, col:col + FFN_COLS] = last
        uc = cb_ref[:, col:col + FFN_COLS][None]
        for tap in range(FFN_CONV):
            uc = uc + buf_ref[:, hist0 + tap:hist0 + tap + rows, :] * cw_ref[tap:tap + 1, col:col + FFN_COLS][None]
        return uc.reshape(nseq * rows, FFN_COLS)

    for ci in range(D_FF // FFN_COLS):
        col = ci * FFN_COLS
        gate = conv_cols(wg_ref, col)
        val = conv_cols(wv_ref, D_FF + col)
        act_ref[:, col:col + FFN_COLS] = (gate * _sigmoid(gate) * val).astype(BF16)

    h2 = h + _dot(act_ref[...], wdn_ref[...])
    y_ref[...] = h2 * lax.rsqrt(jnp.mean(h2 * h2, axis=-1, keepdims=True) + EPS) * nlw_ref[...]


def _ffn(ofox, ogdn, x2d, w_o, norm_ffn_w, w_up, conv_w, conv_b, w_down, norm_final_w, state, *,
         nseq, rows, seq_len):
    n = x2d.shape[0]
    tm = nseq * rows
    assert n % tm == 0 and seq_len % rows == 0 and rows % SUBLANES == 0 and D_FF % FFN_COLS == 0
    has_state = state is not None
    assert (rows == seq_len) if has_state else (nseq == 1)
    tps = seq_len // rows
    nbatch = n // seq_len
    row = lambda w: pl.BlockSpec((tm, w), lambda i: (i, 0))
    const = lambda a: pl.BlockSpec(a.shape, lambda i: (0,) * a.ndim, pipeline_mode=pl.Buffered(1))
    st_spec = pl.BlockSpec((nseq, FFN_CONV - 1, 2 * D_FF), lambda i: (i // tps, 0, 0))
    ins = [ofox, ogdn, x2d, w_o, norm_ffn_w, w_up, conv_w, conv_b, w_down, norm_final_w]
    in_specs = [row(FOX_WIDTH), row(GDN_WIDTH), row(D_MODEL)] + [const(a) for a in ins[3:]]
    scratch = [pltpu.VMEM((nseq, SUBLANES + rows, FFN_COLS), F32),
               pltpu.VMEM((nseq, SUBLANES + rows, FFN_COLS), F32),
               pltpu.VMEM((tm, D_FF), BF16)]
    if has_state:
        ins.append(state)
        in_specs.append(st_spec)
    else:
        scratch.append(pltpu.VMEM((nseq, SUBLANES, 2 * D_FF), F32))
    return pl.pallas_call(
        functools.partial(_ffn_kernel, nseq=nseq, rows=rows, tiles_per_seq=tps, has_state=has_state),
        grid=(n // tm,),
        in_specs=in_specs,
        out_specs=[row(D_MODEL), st_spec],
        out_shape=[jax.ShapeDtypeStruct((n, D_MODEL), F32),
                   jax.ShapeDtypeStruct((nbatch, FFN_CONV - 1, 2 * D_FF), F32)],
        scratch_shapes=scratch,
        compiler_params=pltpu.CompilerParams(dimension_semantics=("arbitrary",),
                                             vmem_limit_bytes=VMEM_LIMIT),
        name="ffn_sample" if has_state else "ffn_prompt",
    )(*ins)


ROW_TILE = 512
SAMPLE_KV_BLOCK = 2048


def _prep_weights(w_in, b_fox_f, gdn_a_log, gdn_dt_bias):
    c0 = 3 * FOX_WIDTH
    c1 = c0 + FOX_HEADS
    c2 = c1 + GDN_CONV_CH
    c3 = c2 + 2 * GDN_HEADS
    small = jnp.concatenate([w_in[:, c0:c1], w_in[:, c2:c3],
                             jnp.zeros((D_MODEL, LANES - FOX_HEADS - 2 * GDN_HEADS), w_in.dtype)], axis=1)
    w_cat = jnp.concatenate([w_in[:, :c0], w_in[:, c1:c2], w_in[:, c3:], small], axis=1).astype(BF16)
    pad = jnp.zeros((LANES - FOX_HEADS - GDN_HEADS,), F32)
    bias = jnp.concatenate([b_fox_f.astype(F32), gdn_dt_bias.astype(F32), pad])
    alog = jnp.concatenate([jnp.zeros((FOX_HEADS,), F32), gdn_a_log.astype(F32), pad])
    prm = jnp.zeros((SUBLANES, LANES), F32).at[0].set(bias).at[1].set(alog)
    return w_cat, prm


def _layer(xp, xs, fox_k, fox_v, fox_logf, st_gdn, st_gconv, st_fconv,
           w_in, b_fox_f, gdn_conv_w, gdn_a_log, gdn_dt_bias, gdn_norm_w, w_o,
           norm_mix_w, norm_ffn_w, w_up, ffn_conv_w, ffn_conv_b, w_down, norm_out_w):
    bp, sp, _ = xp.shape
    bs, ts, _ = xs.shape
    w_cat, prm = _prep_weights(w_in, b_fox_f, gdn_a_log, gdn_dt_bias)
    nmw = norm_mix_w.reshape(1, D_MODEL).astype(F32)
    nfw = norm_ffn_w.reshape(1, D_MODEL).astype(F32)
    now = norm_out_w.reshape(1, D_MODEL).astype(F32)
    gnw = gdn_norm_w.reshape(1, GDN_DV).astype(F32)
    w_o_b, w_up_b, w_dn_b = w_o.astype(BF16), w_up.astype(BF16), w_down.astype(BF16)
    cb = ffn_conv_b.reshape(1, 2 * D_FF).astype(F32)

    xp2 = xp.reshape(bp * sp, D_MODEL)
    q, k, v, kb, vb, gqkv, gz, sm, c, ct = _inproj(xp2, nmw, w_cat, prm, tm=ROW_TILE, seq_len=sp, prompt=True)
    nblk = sp // ROW_TILE
    ct = ct.reshape(bp, nblk, FOX_HEADS // 2, 2, ROW_TILE)
    r3 = lambda a: a.reshape(bp, sp, a.shape[-1])
    o_fox = _fox_prompt(r3(q), r3(kb), r3(vb), c, ct, t=ROW_TILE)
    gqkv3 = r3(gqkv)
    o_gdn, p_state = _gdn(gqkv3, None, r3(sm), r3(gz), None, gdn_conv_w, gnw, c=min(CHUNK, sp))
    yp, p_fconv = _ffn(o_fox.reshape(bp * sp, FOX_WIDTH), o_gdn.reshape(bp * sp, GDN_WIDTH), xp2,
                       w_o_b, nfw, w_up_b, ffn_conv_w, cb, w_dn_b, now, None,
                       nseq=1, rows=ROW_TILE, seq_len=sp)
    p_out = (k.reshape(bp, sp, FOX_HEADS, FOX_DIM), v.reshape(bp, sp, FOX_HEADS, FOX_DIM),
             r3(sm)[:, :, LOGF_LANE:LOGF_LANE + FOX_HEADS], p_state,
             gqkv3[:, sp - (GDN_CONV - 1):], p_fconv)

    xs2 = xs.reshape(bs * ts, D_MODEL)
    q, k, v, gqkv, gz, sm = _inproj(xs2, nmw, w_cat, prm, tm=bs * ts, seq_len=ts, prompt=False)
    s3 = lambda a: a.reshape(bs, ts, a.shape[-1])
    past = fox_k.shape[1]
    lft = jnp.swapaxes(fox_logf.astype(F32), 1, 2)
    o_fox = _fox_sample(s3(q), s3(k), s3(v), s3(sm), lft,
                        fox_k.reshape(bs, past, FOX_WIDTH), fox_v.reshape(bs, past, FOX_WIDTH),
                        kb=min(SAMPLE_KV_BLOCK, past))
    gqkv3 = s3(gqkv)
    hist = jnp.concatenate([jnp.zeros((bs, SUBLANES - (GDN_CONV - 1), GDN_CONV_CH), F32),
                            st_gconv.astype(F32)], axis=1)
    o_gdn, s_state = _gdn(gqkv3, hist, s3(sm), s3(gz), st_gdn.astype(F32), gdn_conv_w, gnw, c=min(CHUNK, ts))
    ys, s_fconv = _ffn(o_fox.reshape(bs * ts, FOX_WIDTH), o_gdn.reshape(bs * ts, GDN_WIDTH), xs2,
                       w_o_b, nfw, w_up_b, ffn_conv_w, cb, w_dn_b, now, st_fconv.astype(F32),
                       nseq=bs, rows=ts, seq_len=ts)
    full = jnp.concatenate([st_gconv.astype(F32), gqkv3], axis=1)
    s_out = (k.reshape(bs, ts, FOX_HEADS, FOX_DIM), v.reshape(bs, ts, FOX_HEADS, FOX_DIM),
             s3(sm)[:, :, LOGF_LANE:LOGF_LANE + FOX_HEADS], s_state,
             full[:, ts:], s_fconv)
    return yp.reshape(bp, sp, D_MODEL), ys.reshape(bs, ts, D_MODEL), p_out, s_out


def kernel(x_prompt, x_sample, cache_fox_k, cache_fox_v, cache_fox_logf, state_gdn, state_gdn_conv,
           state_ffn_conv, w_in, b_fox_f, gdn_conv_w, gdn_a_log, gdn_dt_bias, gdn_norm_w, w_o, norm_mix_w,
           norm_ffn_w, w_up, ffn_conv_w, ffn_conv_b, w_down, norm_final_w):
    depth = w_in.shape[0]
    assert depth == 1, "the final RMSNorm is fused into the single layer's FFN kernel"
    yp, ys, p_out, s_out = _layer(
        x_prompt, x_sample, cache_fox_k[0], cache_fox_v[0], cache_fox_logf[0], state_gdn[0],
        state_gdn_conv[0], state_ffn_conv[0], w_in[0], b_fox_f[0], gdn_conv_w[0], gdn_a_log[0],
        gdn_dt_bias[0], gdn_norm_w[0], w_o[0], norm_mix_w[0], norm_ffn_w[0], w_up[0], ffn_conv_w[0],
        ffn_conv_b[0], w_down[0], norm_final_w)
    return (yp, ys) + tuple(a[None] for a in p_out) + tuple(a[None] for a in s_out)
```

```python
import functools

import jax
import jax.numpy as jnp
from jax import lax
from jax.experimental import pallas as pl
from jax.experimental.pallas import tpu as pltpu

F32 = jnp.float32
BF16 = jnp.bfloat16

D_MODEL = 1024
FOX_DIM = 64
FOX_HEADS = 8
FOX_WIDTH = FOX_HEADS * FOX_DIM
GDN_DK = 128
GDN_DV = 128
GDN_HEADS = 4
GDN_QK = GDN_HEADS * GDN_DK
GDN_WIDTH = GDN_HEADS * GDN_DV
GDN_CONV_CH = 2 * GDN_QK + GDN_WIDTH
GDN_CONV = 4
D_FF = 2816
FFN_CONV = 3
EPS = 1e-6

LANES = 128
SUBLANES = 8
VMEM_LIMIT = 56 * 1024 * 1024
MASKED = -1e30

LOGF_LANE = 0
G_LANE = FOX_HEADS
BETA_LANE = FOX_HEADS + GDN_HEADS

COL_FOX = 0
COL_GQKV = 3 * FOX_WIDTH
COL_GZ = COL_GQKV + GDN_CONV_CH
COL_SMALL = COL_GZ + GDN_WIDTH
N_COLS = COL_SMALL + LANES


def _dot(a, b):
    return jnp.dot(a, b, preferred_element_type=F32)


def _dot_nt(a, b):
    return lax.dot_general(a, b, (((1,), (1,)), ((), ())), preferred_element_type=F32)


def _dot_tn(a, b):
    return lax.dot_general(a, b, (((0,), (0,)), ((), ())), preferred_element_type=F32)


def _split3(x):
    hi = x.astype(BF16)
    r = x - hi.astype(F32)
    lo = r.astype(BF16)
    lo2 = (r - lo.astype(F32)).astype(BF16)
    return hi, lo, lo2


def _exact_left(mat, x):
    hi, lo, lo2 = _split3(x)
    return _dot(mat, hi) + _dot(mat, lo) + _dot(mat, lo2)


def _exact_right(x, mat):
    hi, lo, lo2 = _split3(x)
    return _dot(hi, mat) + _dot(lo, mat) + _dot(lo2, mat)


def _dot3(a, b):
    ah = a.astype(BF16)
    al = (a - ah.astype(F32)).astype(BF16)
    bh = b.astype(BF16)
    bl = (b - bh.astype(F32)).astype(BF16)
    return _dot(ah, bh) + (_dot(ah, bl) + _dot(al, bh))


def _exact_left_c(mat, x):
    return _dot(jnp.concatenate([mat, mat, mat], axis=1), jnp.concatenate(_split3(x), axis=0))


def _dot3c(a, b):
    ah = a.astype(BF16)
    al = (a - ah.astype(F32)).astype(BF16)
    bh = b.astype(BF16)
    bl = (b - bh.astype(F32)).astype(BF16)
    return _dot(jnp.concatenate([ah, ah, al], axis=1), jnp.concatenate([bh, bl, bh], axis=0))


def _sigmoid(x):
    return 1.0 / (1.0 + jnp.exp(-x))


def _tri(n, strict=False):
    r = lax.broadcasted_iota(jnp.int32, (n, n), 0)
    c = lax.broadcasted_iota(jnp.int32, (n, n), 1)
    return (r > c) if strict else (r >= c)


def _inproj_kernel(x_ref, nw_ref, w_ref, prm_ref, *refs, tiles_per_seq, prompt):
    if prompt:
        q_ref, k_ref, v_ref, kb_ref, vb_ref, g_ref, z_ref, sm_ref, c_ref, ct_ref, carry_ref = refs
    else:
        q_ref, k_ref, v_ref, g_ref, z_ref, sm_ref = refs
    x = x_ref[...]
    xn = x * lax.rsqrt(jnp.mean(x * x, axis=-1, keepdims=True) + EPS) * nw_ref[...]
    xb = xn.astype(BF16)

    q_ref[...] = (_dot(xb, w_ref[:, 0:FOX_WIDTH]) * (FOX_DIM ** -0.5)).astype(BF16)
    kf = _dot(xb, w_ref[:, FOX_WIDTH:2 * FOX_WIDTH])
    vf = _dot(xb, w_ref[:, 2 * FOX_WIDTH:3 * FOX_WIDTH])
    k_ref[...] = kf
    v_ref[...] = vf
    if prompt:
        kb_ref[...] = kf.astype(BF16)
        vb_ref[...] = vf.astype(BF16)
    g_ref[...] = _dot(xb, w_ref[:, COL_GQKV:COL_GZ])
    z_ref[...] = _dot(xb, w_ref[:, COL_GZ:COL_SMALL])

    z = _dot(xb, w_ref[:, COL_SMALL:N_COLS])
    lane = lax.broadcasted_iota(jnp.int32, z.shape, 1)
    zb = z + prm_ref[0:1, :]
    e = jnp.exp(-jnp.abs(zb))
    l1p = jnp.log(1.0 + e)
    logf = jnp.minimum(zb, 0.0) - l1p
    g = -jnp.exp(prm_ref[1:2, :]) * (jnp.maximum(zb, 0.0) + l1p)
    beta = _sigmoid(z)
    sm = jnp.where(lane < G_LANE, logf,
                   jnp.where(lane < BETA_LANE, g,
                             jnp.where(lane < BETA_LANE + GDN_HEADS, beta, 0.0)))
    sm_ref[...] = sm

    if prompt:
        i = pl.program_id(0)

        @pl.when(i % tiles_per_seq == 0)
        def _():
            carry_ref[...] = jnp.zeros_like(carry_ref)

        tm = sm.shape[0]
        c = _exact_left(_tri(tm).astype(BF16), sm) + carry_ref[0:1, :]
        carry_ref[0:1, :] = c[tm - 1:tm, :]
        c_ref[...] = c
        ct_ref[0, 0] = c.T[0:SUBLANES, :]


def _inproj(x2d, norm_w, w_cat, prm, *, tm, seq_len, prompt):
    n = x2d.shape[0]
    assert n % tm == 0 and (not prompt or seq_len % tm == 0)
    tps = seq_len // tm if prompt else 1
    row = lambda w: pl.BlockSpec((tm, w), lambda i: (i, 0))
    const = lambda a: pl.BlockSpec(a.shape, lambda i: (0,) * a.ndim, pipeline_mode=pl.Buffered(1))
    out_shape = [jax.ShapeDtypeStruct((n, FOX_WIDTH), BF16),
                 jax.ShapeDtypeStruct((n, FOX_WIDTH), F32),
                 jax.ShapeDtypeStruct((n, FOX_WIDTH), F32)]
    out_specs = [row(FOX_WIDTH), row(FOX_WIDTH), row(FOX_WIDTH)]
    if prompt:
        out_shape += [jax.ShapeDtypeStruct((n, FOX_WIDTH), BF16)] * 2
        out_specs += [row(FOX_WIDTH)] * 2
    out_shape += [jax.ShapeDtypeStruct((n, GDN_CONV_CH), F32),
                  jax.ShapeDtypeStruct((n, GDN_WIDTH), F32),
                  jax.ShapeDtypeStruct((n, LANES), F32)]
    out_specs += [row(GDN_CONV_CH), row(GDN_WIDTH), row(LANES)]
    scratch = []
    if prompt:
        out_shape += [jax.ShapeDtypeStruct((n, LANES), F32),
                      jax.ShapeDtypeStruct((n // seq_len, tps, SUBLANES, tm), F32)]
        out_specs += [row(LANES),
                      pl.BlockSpec((1, 1, SUBLANES, tm), lambda i: (i // tps, i % tps, 0, 0))]
        scratch = [pltpu.VMEM((SUBLANES, LANES), F32)]
    return pl.pallas_call(
        functools.partial(_inproj_kernel, tiles_per_seq=tps, prompt=prompt),
        grid=(n // tm,),
        in_specs=[row(D_MODEL), const(norm_w), const(w_cat), const(prm)],
        out_specs=out_specs, out_shape=out_shape, scratch_shapes=scratch,
        compiler_params=pltpu.CompilerParams(dimension_semantics=("arbitrary",),
                                             vmem_limit_bytes=VMEM_LIMIT),
        name="inproj_prompt" if prompt else "inproj_sample",
    )(x2d, norm_w, w_cat, prm)


def _fox_prompt_kernel(q_ref, k_ref, v_ref, cq_ref, ck_ref, o_ref, *, t):
    pair = pl.program_id(1)
    i = pl.program_id(2)
    q = q_ref[0]
    lane = lax.broadcasted_iota(jnp.int32, (t, LANES), 1)
    low = lane < FOX_DIM
    cq_all = cq_ref[...]
    qh, cqh = [], []
    for h in range(2):
        qh.append(jnp.where(low if h == 0 else jnp.logical_not(low), q, jnp.zeros_like(q)))
        cqh.append(jnp.sum(jnp.where(lane == 2 * pair + h, cq_all, 0.0), axis=1, keepdims=True))

    def block(j, carry, diag):
        start = pl.multiple_of(j * t, t)
        kj = k_ref[0, pl.ds(start, t), :]
        vj = v_ref[0, pl.ds(start, t), :]
        ckj = ck_ref[0, j, 0]
        out = []
        for h in range(2):
            m, l, acc = carry[h]
            s = _dot_nt(qh[h], kj) + (cqh[h] - ckj[h:h + 1, :])
            if diag:
                s = jnp.where(_tri(t), s, MASKED)
            m_new = jnp.maximum(m, jnp.max(s, axis=1, keepdims=True))
            alpha = jnp.exp(m - m_new)
            p = jnp.exp(s - m_new)
            l = alpha * l + jnp.sum(p, axis=1, keepdims=True)
            acc = alpha * acc + _dot(p.astype(BF16), vj)
            out.append((m_new, l, acc))
        return tuple(out)

    init = tuple((jnp.full((t, 1), MASKED, F32), jnp.zeros((t, 1), F32), jnp.zeros((t, LANES), F32))
                 for _ in range(2))
    carry = lax.fori_loop(0, i, lambda j, c: block(j, c, False), init)
    (_, l0, a0), (_, l1, a1) = block(i, carry, True)
    o_ref[0] = jnp.where(low, a0 / l0, a1 / l1).astype(BF16)


def _fox_prompt(q, kb, vb, c, ct, *, t):
    b, s, _ = q.shape
    nblk = s // t
    assert s % t == 0 and ct.shape == (b, nblk, FOX_HEADS // 2, 2, t)
    npair = FOX_HEADS // 2
    return pl.pallas_call(
        functools.partial(_fox_prompt_kernel, t=t),
        grid=(b, npair, nblk),
        in_specs=[pl.BlockSpec((1, t, LANES), lambda bi, p, i: (bi, i, p)),
                  pl.BlockSpec((1, s, LANES), lambda bi, p, i: (bi, 0, p)),
                  pl.BlockSpec((1, s, LANES), lambda bi, p, i: (bi, 0, p)),
                  pl.BlockSpec((t, LANES), lambda bi, p, i: (bi * nblk + i, 0)),
                  pl.BlockSpec((1, nblk, 1, 2, t), lambda bi, p, i: (bi, 0, p, 0, 0))],
        out_specs=pl.BlockSpec((1, t, LANES), lambda bi, p, i: (bi, i, p)),
        out_shape=jax.ShapeDtypeStruct((b, s, FOX_WIDTH), BF16),
        compiler_params=pltpu.CompilerParams(
            dimension_semantics=("arbitrary", "arbitrary", "arbitrary"), vmem_limit_bytes=VMEM_LIMIT),
        name="fox_prompt",
    )(q, kb, vb, c, ct)


def _fox_sample_kernel(q_ref, kn_ref, vn_ref, sm_ref, lft_ref, kp_ref, vp_ref, o_ref,
                       rk_ref, m_ref, l_ref, acc_ref, *, kb, t):
    j = pl.program_id(1)
    nkv = pl.num_programs(1)
    rows = FOX_HEADS * t
    lane = lax.broadcasted_iota(jnp.int32, (t, FOX_WIDTH), 1)

    q = q_ref[0].astype(F32)
    qbd = jnp.concatenate(
        [jnp.where((lane >= h * FOX_DIM) & (lane < (h + 1) * FOX_DIM), q, 0.0) for h in range(FOX_HEADS)],
        axis=0).astype(BF16)

    lf_new = sm_ref[0]
    cn = _exact_left(_tri(t).astype(BF16), lf_new)

    @pl.when(j == 0)
    def _():
        p = lft_ref.shape[2]
        nb = p // LANES
        x = jnp.concatenate([lft_ref[0, :, i * LANES:(i + 1) * LANES] for i in range(nb)], axis=0)
        r = lax.broadcasted_iota(jnp.int32, (LANES, LANES), 0)
        cc = lax.broadcasted_iota(jnp.int32, (LANES, LANES), 1)
        y = _exact_right(x, (r > cc).astype(BF16))
        tot = jnp.sum(x, axis=1, keepdims=True)
        off = jnp.zeros((FOX_HEADS, 1), F32)
        for i in range(nb - 1, -1, -1):
            rk_ref[:, i * LANES:(i + 1) * LANES] = y[i * FOX_HEADS:(i + 1) * FOX_HEADS, :] + off
            off = off + tot[i * FOX_HEADS:(i + 1) * FOX_HEADS, :]
        m_ref[...] = jnp.full_like(m_ref, MASKED)
        l_ref[...] = jnp.zeros_like(l_ref)
        acc_ref[...] = jnp.zeros_like(acc_ref)

    def update(s, v):
        m = m_ref[:, 0:1]
        m_new = jnp.maximum(m, jnp.max(s, axis=1, keepdims=True))
        alpha = jnp.exp(m - m_new)
        p = jnp.exp(s - m_new)
        l_ref[...] = jnp.broadcast_to(alpha * l_ref[:, 0:1] + jnp.sum(p, axis=1, keepdims=True), l_ref.shape)
        acc_ref[...] = alpha * acc_ref[...] + _dot(p.astype(BF16), v)
        m_ref[...] = jnp.broadcast_to(m_new, m_ref.shape)

    start = pl.multiple_of(j * kb, kb)
    rk = rk_ref[:, pl.ds(start, kb)]
    bias = jnp.concatenate([cn[:, h:h + 1] + rk[h:h + 1, :] for h in range(FOX_HEADS)], axis=0)
    s = _dot_nt(qbd, kp_ref[0].astype(BF16)) + bias
    update(s, vp_ref[0].astype(BF16))

    @pl.when(j == nkv - 1)
    def _():
        r = lax.broadcasted_iota(jnp.int32, (t, t), 0)
        cc = lax.broadcasted_iota(jnp.int32, (t, t), 1)
        after = (r > cc).astype(F32)
        tri = _tri(t).astype(BF16)
        d = jnp.concatenate([_exact_left(tri, lf_new[:, h:h + 1] * after) for h in range(FOX_HEADS)], axis=0)
        causal = jnp.concatenate([_tri(t)] * FOX_HEADS, axis=0)
        s_new = jnp.where(causal, _dot_nt(qbd, kn_ref[0].astype(BF16)) + d, MASKED)
        update(s_new, vn_ref[0].astype(BF16))
        o_full = acc_ref[...] / l_ref[:, 0:1]
        o = jnp.zeros((t, FOX_WIDTH), F32)
        for h in range(FOX_HEADS):
            sel = (lane >= h * FOX_DIM) & (lane < (h + 1) * FOX_DIM)
            o = o + jnp.where(sel, o_full[h * t:(h + 1) * t, :], 0.0)
        o_ref[0] = o.astype(BF16)


def _fox_sample(q, kn, vn, sm, lft, kp, vp, *, kb):
    b, t, _ = q.shape
    p = kp.shape[1]
    assert p % kb == 0 and kb % LANES == 0
    rows = FOX_HEADS * t
    per_b = lambda shape: pl.BlockSpec((1,) + shape, lambda bi, j: (bi, 0, 0))
    return pl.pallas_call(
        functools.partial(_fox_sample_kernel, kb=kb, t=t),
        grid=(b, p // kb),
        in_specs=[per_b((t, FOX_WIDTH)), per_b((t, FOX_WIDTH)), per_b((t, FOX_WIDTH)), per_b((t, LANES)),
                  per_b((FOX_HEADS, p)),
                  pl.BlockSpec((1, kb, FOX_WIDTH), lambda bi, j: (bi, j, 0)),
                  pl.BlockSpec((1, kb, FOX_WIDTH), lambda bi, j: (bi, j, 0))],
        out_specs=per_b((t, FOX_WIDTH)),
        out_shape=jax.ShapeDtypeStruct((b, t, FOX_WIDTH), BF16),
        scratch_shapes=[pltpu.VMEM((FOX_HEADS, p), F32),
                        pltpu.VMEM((rows, LANES), F32), pltpu.VMEM((rows, LANES), F32),
                        pltpu.VMEM((rows, FOX_WIDTH), F32)],
        compiler_params=pltpu.CompilerParams(dimension_semantics=("arbitrary", "arbitrary"),
                                             vmem_limit_bytes=VMEM_LIMIT),
        name="fox_sample",
    )(q, kn, vn, sm, lft, kp, vp)


def _gdn_unit(qh, kh, vh, g, g_row, g_last, beta, sh, c):
    wide = c % LANES == 0
    dot3 = _dot3c if wide else _dot3
    incl = _tri(c)
    strict = _tri(c, strict=True)
    qn = qh * (lax.rsqrt(jnp.sum(qh * qh, axis=-1, keepdims=True) + EPS) * (GDN_DK ** -0.5))
    kn = kh * lax.rsqrt(jnp.sum(kh * kh, axis=-1, keepdims=True) + EPS)
    eg = jnp.exp(g)
    decay = jnp.exp(jnp.where(incl, g - g_row, MASKED))
    kbeta = kn * beta
    kq = _dot_nt(jnp.concatenate([kbeta, qn], axis=0).astype(BF16), kn.astype(BF16))
    lmat = jnp.where(strict, kq[:c] * decay, 0.0)
    a_qk = kq[c:] * decay

    tinv = (incl & jnp.logical_not(strict)).astype(F32) - lmat
    power = dot3(lmat, lmat)
    span = 2
    while 2 * span < c:
        if wide:
            both = dot3(power, jnp.concatenate([tinv, power], axis=1))
            tinv = tinv + both[:, :c]
            power = both[:, c:]
        else:
            tinv = tinv + dot3(power, tinv)
            power = dot3(power, power)
        span *= 2
    tinv = tinv + dot3(power, tinv)

    sol = dot3(tinv, jnp.concatenate([vh * beta, kbeta * eg], axis=1))
    u = sol[:, :GDN_DV]
    w = sol[:, GDN_DV:]
    ws = _dot(jnp.concatenate([w, qn * eg], axis=0).astype(BF16), sh.astype(BF16))
    db = (u - ws[:c]).astype(BF16)
    kd = kn * jnp.exp(g_last - g)
    decayed = sh * jnp.exp(g_last)
    if wide:
        ak = _dot(jnp.concatenate([a_qk, kd.T], axis=0).astype(BF16), db)
        return ws[c:] + ak[:c], decayed + ak[c:]
    return ws[c:] + _dot(a_qk.astype(BF16), db), decayed + _dot_tn(kd.astype(BF16), db)


def _gdn_kernel(cur_ref, hist_ref, sm_ref, z_ref, *refs, c, nb, cpb, zero_init):
    if zero_init:
        cw_ref, nw_ref, o_ref, sout_ref, xs_ref, s_ref = refs
    else:
        s0_ref, cw_ref, nw_ref, o_ref, sout_ref, xs_ref, s_ref = refs
    i = pl.program_id(1)
    rows = c * cpb
    wide = c % LANES == 0

    @pl.when(i == 0)
    def _():
        if zero_init:
            s_ref[...] = jnp.zeros_like(s_ref)
        else:
            s_ref[...] = s0_ref[...]

    hist = hist_ref[...]
    if zero_init:
        hist = jnp.where(i > 0, hist, 0.0)
    xs_ref[:, 0:SUBLANES, :] = hist
    xs_ref[:, SUBLANES:SUBLANES + rows, :] = cur_ref[...]
    base = SUBLANES - (GDN_CONV - 1)

    def conv_silu(b, r0, col):
        acc = xs_ref[b, base + r0:base + r0 + c, col:col + LANES] * cw_ref[0:1, col:col + LANES]
        for tap in range(1, GDN_CONV):
            acc = acc + (xs_ref[b, base + r0 + tap:base + r0 + tap + c, col:col + LANES]
                         * cw_ref[tap:tap + 1, col:col + LANES])
        return acc * _sigmoid(acc)

    incl = _tri(c)
    eye = incl & jnp.logical_not(_tri(c, strict=True))
    for b in range(nb):
        state = [s_ref[b, h] for h in range(GDN_HEADS)]
        for ck in range(cpb):
            r0 = ck * c
            sm = sm_ref[b, r0:r0 + c, :]
            if wide:
                gs = _exact_left_c(incl.astype(BF16), sm)
                gst = gs.T
            else:
                gs = _exact_left(incl.astype(BF16), sm)
            for h in range(GDN_HEADS):
                g = gs[:, G_LANE + h:G_LANE + h + 1]
                if wide:
                    g_row = gst[G_LANE + h:G_LANE + h + 1, :]
                else:
                    g_row = _exact_left(jnp.ones((c, c), BF16), jnp.where(eye, g, 0.0))
                o, state[h] = _gdn_unit(
                    conv_silu(b, r0, h * GDN_DK), conv_silu(b, r0, GDN_QK + h * GDN_DK),
                    conv_silu(b, r0, 2 * GDN_QK + h * GDN_DV), g, g_row,
                    gs[c - 1:c, G_LANE + h:G_LANE + h + 1], sm[:, BETA_LANE + h:BETA_LANE + h + 1],
                    state[h], c)
                on = o * lax.rsqrt(jnp.mean(o * o, axis=-1, keepdims=True) + EPS) * nw_ref[...]
                zh = z_ref[b, r0:r0 + c, h * GDN_DV:(h + 1) * GDN_DV]
                o_ref[b, r0:r0 + c, h * GDN_DV:(h + 1) * GDN_DV] = (on * (zh * _sigmoid(zh))).astype(BF16)
        for h in range(GDN_HEADS):
            s_ref[b, h] = state[h]

    @pl.when(i == pl.num_programs(1) - 1)
    def _():
        sout_ref[...] = s_ref[...]


def _gdn(gqkv, hist, sm, gz, s0, conv_w, norm_w, *, c, nb, cpb):
    b, t, _ = gqkv.shape
    rows = c * cpb
    assert t % rows == 0 and b % nb == 0 and c % SUBLANES == 0
    zero_init = hist is None
    blk = lambda w: pl.BlockSpec((nb, rows, w), lambda bi, i: (bi, i, 0))
    const = lambda a: pl.BlockSpec(a.shape, lambda bi, i: (0,) * a.ndim)
    state_spec = pl.BlockSpec((nb, GDN_HEADS, GDN_DK, GDN_DV), lambda bi, i: (bi, 0, 0, 0))
    if zero_init:
        assert nb == 1
        hist_arr = gqkv
        hist_spec = pl.BlockSpec((1, SUBLANES, GDN_CONV_CH),
                                 lambda bi, i: (bi, jnp.maximum(i * (rows // SUBLANES) - 1, 0), 0))
        extra, extra_specs = [], []
    else:
        assert t == rows
        hist_arr = hist
        hist_spec = pl.BlockSpec((nb, SUBLANES, GDN_CONV_CH), lambda bi, i: (bi, 0, 0))
        extra, extra_specs = [s0], [state_spec]
    return pl.pallas_call(
        functools.partial(_gdn_kernel, c=c, nb=nb, cpb=cpb, zero_init=zero_init),
        grid=(b // nb, t // rows),
        in_specs=[blk(GDN_CONV_CH), hist_spec, blk(LANES), blk(GDN_WIDTH)] + extra_specs
                 + [const(conv_w), const(norm_w)],
        out_specs=[blk(GDN_WIDTH), state_spec],
        out_shape=[jax.ShapeDtypeStruct((b, t, GDN_WIDTH), BF16),
                   jax.ShapeDtypeStruct((b, GDN_HEADS, GDN_DK, GDN_DV), F32)],
        scratch_shapes=[pltpu.VMEM((nb, SUBLANES + rows, GDN_CONV_CH), F32),
                        pltpu.VMEM((nb, GDN_HEADS, GDN_DK, GDN_DV), F32)],
        compiler_params=pltpu.CompilerParams(dimension_semantics=("arbitrary", "arbitrary"),
                                             vmem_limit_bytes=VMEM_LIMIT),
        name="gdn_prompt" if zero_init else "gdn_sample",
    )(gqkv, hist_arr, sm, gz, *extra, conv_w, norm_w)


FFN_COLS = 256


def _ffn_kernel(of_ref, og_ref, x_ref, wo_ref, nfw_ref, wup_ref, cw_ref, cb_ref, wdn_ref, nlw_ref, *refs,
                nseq, rows, tiles_per_seq, has_state):
    if has_state:
        st_ref, y_ref, nst_ref, wg_ref, wv_ref, act_ref = refs
    else:
        y_ref, nst_ref, wg_ref, wv_ref, act_ref, carry_ref = refs
    i = pl.program_id(0)
    hist0 = SUBLANES - (FFN_CONV - 1)

    h = (x_ref[...] + _dot(of_ref[...], wo_ref[0:FOX_WIDTH, :])
         + _dot(og_ref[...], wo_ref[FOX_WIDTH:FOX_WIDTH + GDN_WIDTH, :]))
    hn = (h * lax.rsqrt(jnp.mean(h * h, axis=-1, keepdims=True) + EPS) * nfw_ref[...]).astype(BF16)

    if not has_state:
        @pl.when(i % tiles_per_seq == 0)
        def _():
            carry_ref[...] = jnp.zeros_like(carry_ref)

    def conv_cols(buf_ref, col):
        up = _dot(hn, wup_ref[:, col:col + FFN_COLS])
        buf_ref[:, SUBLANES:SUBLANES + rows, :] = up.reshape(nseq, rows, FFN_COLS)
        if has_state:
            buf_ref[:, hist0:SUBLANES, :] = st_ref[:, :, col:col + FFN_COLS]
        else:
            buf_ref[:, hist0:SUBLANES, :] = carry_ref[:, hist0:SUBLANES, col:col + FFN_COLS]
        last = buf_ref[:, SUBLANES + rows - (FFN_CONV - 1):SUBLANES + rows, :]
        nst_ref[:, :, col:col + FFN_COLS] = last
        if not has_state:
            carry_ref[:, hist0:SUBLANES, col:col + FFN_COLS] = last
        uc = cb_ref[:, col:col + FFN_COLS][None]
        for tap in range(FFN_CONV):
            uc = uc + buf_ref[:, hist0 + tap:hist0 + tap + rows, :] * cw_ref[tap:tap + 1, col:col + FFN_COLS][None]
        return uc.reshape(nseq * rows, FFN_COLS)

    for ci in range(D_FF // FFN_COLS):
        col = ci * FFN_COLS
        gate = conv_cols(wg_ref, col)
        val = conv_cols(wv_ref, D_FF + col)
        act_ref[:, col:col + FFN_COLS] = (gate * _sigmoid(gate) * val).astype(BF16)

    h2 = h + _dot(act_ref[...], wdn_ref[...])
    y_ref[...] = h2 * lax.rsqrt(jnp.mean(h2 * h2, axis=-1, keepdims=True) + EPS) * nlw_ref[...]


def _ffn(ofox, ogdn, x2d, w_o, norm_ffn_w, w_up, conv_w, conv_b, w_down, norm_final_w, state, *,
         nseq, rows, seq_len):
    n = x2d.shape[0]
    tm = nseq * rows
    assert n % tm == 0 and seq_len % rows == 0 and rows % SUBLANES == 0 and D_FF % FFN_COLS == 0
    has_state = state is not None
    assert (rows == seq_len) if has_state else (nseq == 1)
    tps = seq_len // rows
    nbatch = n // seq_len
    row = lambda w: pl.BlockSpec((tm, w), lambda i: (i, 0))
    const = lambda a: pl.BlockSpec(a.shape, lambda i: (0,) * a.ndim, pipeline_mode=pl.Buffered(1))
    st_spec = pl.BlockSpec((nseq, FFN_CONV - 1, 2 * D_FF), lambda i: (i // tps, 0, 0))
    ins = [ofox, ogdn, x2d, w_o, norm_ffn_w, w_up, conv_w, conv_b, w_down, norm_final_w]
    in_specs = [row(FOX_WIDTH), row(GDN_WIDTH), row(D_MODEL)] + [const(a) for a in ins[3:]]
    scratch = [pltpu.VMEM((nseq, SUBLANES + rows, FFN_COLS), F32),
               pltpu.VMEM((nseq, SUBLANES + rows, FFN_COLS), F32),
               pltpu.VMEM((tm, D_FF), BF16)]
    if has_state:
        ins.append(state)
        in_specs.append(st_spec)
    else:
        scratch.append(pltpu.VMEM((nseq, SUBLANES, 2 * D_FF), F32))
    return pl.pallas_call(
        functools.partial(_ffn_kernel, nseq=nseq, rows=rows, tiles_per_seq=tps, has_state=has_state),
        grid=(n // tm,),
        in_specs=in_specs,
        out_specs=[row(D_MODEL), st_spec],
        out_shape=[jax.ShapeDtypeStruct((n, D_MODEL), F32),
                   jax.ShapeDtypeStruct((nbatch, FFN_CONV - 1, 2 * D_FF), F32)],
        scratch_shapes=scratch,
        compiler_params=pltpu.CompilerParams(dimension_semantics=("arbitrary",),
                                             vmem_limit_bytes=VMEM_LIMIT),
        name="ffn_sample" if has_state else "ffn_prompt",
    )(*ins)


ROW_TILE = 512
SAMPLE_KV_BLOCK = 2048
GDN_CHUNK = 128
GDN_CHUNKS_PER_STEP = 2
GDN_SAMPLE_SEQS_PER_STEP = 4


def _prep_weights(w_in, b_fox_f, gdn_a_log, gdn_dt_bias):
    c0 = 3 * FOX_WIDTH
    c1 = c0 + FOX_HEADS
    c2 = c1 + GDN_CONV_CH
    c3 = c2 + 2 * GDN_HEADS
    small = jnp.concatenate([w_in[:, c0:c1], w_in[:, c2:c3],
                             jnp.zeros((D_MODEL, LANES - FOX_HEADS - 2 * GDN_HEADS), w_in.dtype)], axis=1)
    w_cat = jnp.concatenate([w_in[:, :c0], w_in[:, c1:c2], w_in[:, c3:], small], axis=1).astype(BF16)
    pad = jnp.zeros((LANES - FOX_HEADS - GDN_HEADS,), F32)
    bias = jnp.concatenate([b_fox_f.astype(F32), gdn_dt_bias.astype(F32), pad])
    alog = jnp.concatenate([jnp.zeros((FOX_HEADS,), F32), gdn_a_log.astype(F32), pad])
    prm = jnp.zeros((SUBLANES, LANES), F32).at[0].set(bias).at[1].set(alog)
    return w_cat, prm


def _layer(xp, xs, fox_k, fox_v, fox_logf, st_gdn, st_gconv, st_fconv,
           w_in, b_fox_f, gdn_conv_w, gdn_a_log, gdn_dt_bias, gdn_norm_w, w_o,
           norm_mix_w, norm_ffn_w, w_up, ffn_conv_w, ffn_conv_b, w_down, norm_out_w):
    bp, sp, _ = xp.shape
    bs, ts, _ = xs.shape
    w_cat, prm = _prep_weights(w_in, b_fox_f, gdn_a_log, gdn_dt_bias)
    nmw = norm_mix_w.reshape(1, D_MODEL).astype(F32)
    nfw = norm_ffn_w.reshape(1, D_MODEL).astype(F32)
    now = norm_out_w.reshape(1, D_MODEL).astype(F32)
    gnw = gdn_norm_w.reshape(1, GDN_DV).astype(F32)
    w_o_b, w_up_b, w_dn_b = w_o.astype(BF16), w_up.astype(BF16), w_down.astype(BF16)
    cb = ffn_conv_b.reshape(1, 2 * D_FF).astype(F32)

    xp2 = xp.reshape(bp * sp, D_MODEL)
    q, k, v, kb, vb, gqkv, gz, sm, c, ct = _inproj(xp2, nmw, w_cat, prm, tm=ROW_TILE, seq_len=sp, prompt=True)
    nblk = sp // ROW_TILE
    ct = ct.reshape(bp, nblk, FOX_HEADS // 2, 2, ROW_TILE)
    r3 = lambda a: a.reshape(bp, sp, a.shape[-1])
    o_fox = _fox_prompt(r3(q), r3(kb), r3(vb), c, ct, t=ROW_TILE)
    gqkv3 = r3(gqkv)
    o_gdn, p_state = _gdn(gqkv3, None, r3(sm), r3(gz), None, gdn_conv_w, gnw,
                          c=GDN_CHUNK, nb=1, cpb=GDN_CHUNKS_PER_STEP)
    yp, p_fconv = _ffn(o_fox.reshape(bp * sp, FOX_WIDTH), o_gdn.reshape(bp * sp, GDN_WIDTH), xp2,
                       w_o_b, nfw, w_up_b, ffn_conv_w, cb, w_dn_b, now, None,
                       nseq=1, rows=ROW_TILE, seq_len=sp)
    p_out = (k.reshape(bp, sp, FOX_HEADS, FOX_DIM), v.reshape(bp, sp, FOX_HEADS, FOX_DIM),
             r3(sm)[:, :, LOGF_LANE:LOGF_LANE + FOX_HEADS], p_state,
             gqkv3[:, sp - (GDN_CONV - 1):], p_fconv)

    xs2 = xs.reshape(bs * ts, D_MODEL)
    q, k, v, gqkv, gz, sm = _inproj(xs2, nmw, w_cat, prm, tm=bs * ts, seq_len=ts, prompt=False)
    s3 = lambda a: a.reshape(bs, ts, a.shape[-1])
    past = fox_k.shape[1]
    lft = jnp.swapaxes(fox_logf.astype(F32), 1, 2)
    o_fox = _fox_sample(s3(q), s3(k), s3(v), s3(sm), lft,
                        fox_k.reshape(bs, past, FOX_WIDTH), fox_v.reshape(bs, past, FOX_WIDTH),
                        kb=min(SAMPLE_KV_BLOCK, past))
    gqkv3 = s3(gqkv)
    hist = jnp.concatenate([jnp.zeros((bs, SUBLANES - (GDN_CONV - 1), GDN_CONV_CH), F32),
                            st_gconv.astype(F32)], axis=1)
    o_gdn, s_state = _gdn(gqkv3, hist, s3(sm), s3(gz), st_gdn.astype(F32), gdn_conv_w, gnw,
                          c=ts, nb=GDN_SAMPLE_SEQS_PER_STEP, cpb=1)
    ys, s_fconv = _ffn(o_fox.reshape(bs * ts, FOX_WIDTH), o_gdn.reshape(bs * ts, GDN_WIDTH), xs2,
                       w_o_b, nfw, w_up_b, ffn_conv_w, cb, w_dn_b, now, st_fconv.astype(F32),
                       nseq=bs, rows=ts, seq_len=ts)
    full = jnp.concatenate([st_gconv.astype(F32), gqkv3], axis=1)
    s_out = (k.reshape(bs, ts, FOX_HEADS, FOX_DIM), v.reshape(bs, ts, FOX_HEADS, FOX_DIM),
             s3(sm)[:, :, LOGF_LANE:LOGF_LANE + FOX_HEADS], s_state,
             full[:, ts:], s_fconv)
    return yp.reshape(bp, sp, D_MODEL), ys.reshape(bs, ts, D_MODEL), p_out, s_out


def kernel(x_prompt, x_sample, cache_fox_k, cache_fox_v, cache_fox_logf, state_gdn, state_gdn_conv,
           state_ffn_conv, w_in, b_fox_f, gdn_conv_w, gdn_a_log, gdn_dt_bias, gdn_norm_w, w_o, norm_mix_w,
           norm_ffn_w, w_up, ffn_conv_w, ffn_conv_b, w_down, norm_final_w):
    depth = w_in.shape[0]
    assert depth == 1, "the final RMSNorm is fused into the single layer's FFN kernel"
    yp, ys, p_out, s_out = _layer(
        x_prompt, x_sample, cache_fox_k[0], cache_fox_v[0], cache_fox_logf[0], state_gdn[0],
        state_gdn_conv[0], state_ffn_conv[0], w_in[0], b_fox_f[0], gdn_conv_w[0], gdn_a_log[0],
        gdn_dt_bias[0], gdn_norm_w[0], w_o[0], norm_mix_w[0], norm_ffn_w[0], w_up[0], ffn_conv_w[0],
        ffn_conv_b[0], w_down[0], norm_final_w)
    return (yp, ys) + tuple(a[None] for a in p_out) + tuple(a[None] for a in s_out)
```

```python
import functools
import math

import numpy as np
import jax
import jax.numpy as jnp
from jax import lax
from jax.experimental import pallas as pl
from jax.experimental.pallas import tpu as pltpu

F32 = jnp.float32
BF16 = jnp.bfloat16

D_MODEL = 1024
FOX_DIM = 64
FOX_HEADS = 8
FOX_WIDTH = FOX_HEADS * FOX_DIM
GDN_DK = 128
GDN_DV = 128
GDN_HEADS = 4
GDN_QK = GDN_HEADS * GDN_DK
GDN_WIDTH = GDN_HEADS * GDN_DV
GDN_CONV_CH = 2 * GDN_QK + GDN_WIDTH
GDN_CONV = 4
D_FF = 2816
FFN_CONV = 3
EPS = 1e-6

LANES = 128
SUBLANES = 8
VMEM_LIMIT = 56 * 1024 * 1024
MASKED = -1e30
LOG2E = math.log2(math.e)

LOGF_LANE = 0
G_LANE = FOX_HEADS
BETA_LANE = FOX_HEADS + GDN_HEADS

COL_FOX = 0
COL_GQKV = 3 * FOX_WIDTH
COL_GZ = COL_GQKV + GDN_CONV_CH
COL_SMALL = COL_GZ + GDN_WIDTH
N_COLS = COL_SMALL + LANES


def _dot(a, b):
    return jnp.dot(a, b, preferred_element_type=F32)


def _dot_nt(a, b):
    return lax.dot_general(a, b, (((1,), (1,)), ((), ())), preferred_element_type=F32)


def _dot_tn(a, b):
    return lax.dot_general(a, b, (((0,), (0,)), ((), ())), preferred_element_type=F32)


def _split3(x):
    hi = x.astype(BF16)
    r = x - hi.astype(F32)
    lo = r.astype(BF16)
    lo2 = (r - lo.astype(F32)).astype(BF16)
    return hi, lo, lo2


def _exact_left(mat, x):
    hi, lo, lo2 = _split3(x)
    return _dot(mat, hi) + _dot(mat, lo) + _dot(mat, lo2)


def _exact_right(x, mat):
    hi, lo, lo2 = _split3(x)
    return _dot(hi, mat) + _dot(lo, mat) + _dot(lo2, mat)


def _dot3(a, b):
    ah = a.astype(BF16)
    al = (a - ah.astype(F32)).astype(BF16)
    bh = b.astype(BF16)
    bl = (b - bh.astype(F32)).astype(BF16)
    return _dot(ah, bh) + (_dot(ah, bl) + _dot(al, bh))


def _exact_left_c(mat, x):
    return _dot(jnp.concatenate([mat, mat, mat], axis=1), jnp.concatenate(_split3(x), axis=0))


def _dot3c(a, b):
    ah = a.astype(BF16)
    al = (a - ah.astype(F32)).astype(BF16)
    bh = b.astype(BF16)
    bl = (b - bh.astype(F32)).astype(BF16)
    return _dot(jnp.concatenate([ah, ah, al], axis=1), jnp.concatenate([bh, bl, bh], axis=0))


def _sigmoid(x):
    return 1.0 / (1.0 + jnp.exp(-x))


def _tri(n, strict=False):
    r = lax.broadcasted_iota(jnp.int32, (n, n), 0)
    c = lax.broadcasted_iota(jnp.int32, (n, n), 1)
    return (r > c) if strict else (r >= c)


def _inproj_kernel(x_ref, nw_ref, w_ref, prm_ref, *refs, tiles_per_seq, prompt):
    if prompt:
        (aug_ref, augc_ref, q_ref, k_ref, v_ref, kb_ref, qa_ref, ka_ref, va_ref, g_ref, z_ref, sm_ref,
         carry_ref) = refs
    else:
        q_ref, k_ref, v_ref, g_ref, z_ref, sm_ref = refs
    x = x_ref[...]
    xn = x * lax.rsqrt(jnp.mean(x * x, axis=-1, keepdims=True) + EPS) * nw_ref[...]
    xb = xn.astype(BF16)

    q_scale = FOX_DIM ** -0.5 * (LOG2E if prompt else 1.0)
    q_ref[...] = (_dot(xb, w_ref[:, 0:FOX_WIDTH]) * q_scale).astype(BF16)
    kf = _dot(xb, w_ref[:, FOX_WIDTH:2 * FOX_WIDTH])
    vf = _dot(xb, w_ref[:, 2 * FOX_WIDTH:3 * FOX_WIDTH])
    k_ref[...] = kf
    v_ref[...] = vf
    if prompt:
        kb_ref[...] = kf.astype(BF16)
        tm = x.shape[0]
        low = lax.broadcasted_iota(jnp.int32, (tm, LANES), 1) < FOX_DIM
        for p in range(FOX_HEADS // 2):
            pair = vf[:, p * LANES:(p + 1) * LANES]
            va_ref[:, 2 * p * LANES:(2 * p + 1) * LANES] = jnp.where(low, pair, 1.0).astype(BF16)
            va_ref[:, (2 * p + 1) * LANES:(2 * p + 2) * LANES] = jnp.where(
                low, pltpu.roll(pair, FOX_DIM, axis=1), 1.0).astype(BF16)
    g_ref[...] = _dot(xb, w_ref[:, COL_GQKV:COL_GZ])
    z_ref[...] = _dot(xb, w_ref[:, COL_GZ:COL_SMALL])

    z = _dot(xb, w_ref[:, COL_SMALL:N_COLS])
    lane = lax.broadcasted_iota(jnp.int32, z.shape, 1)
    zb = z + prm_ref[0:1, :]
    e = jnp.exp(-jnp.abs(zb))
    l1p = jnp.log(1.0 + e)
    logf = jnp.minimum(zb, 0.0) - l1p
    g = -jnp.exp(prm_ref[1:2, :]) * (jnp.maximum(zb, 0.0) + l1p)
    beta = _sigmoid(z)
    sm = jnp.where(lane < G_LANE, logf,
                   jnp.where(lane < BETA_LANE, g,
                             jnp.where(lane < BETA_LANE + GDN_HEADS, beta, 0.0)))
    sm_ref[...] = sm

    if prompt:
        i = pl.program_id(0)

        @pl.when(i % tiles_per_seq == 0)
        def _():
            carry_ref[...] = jnp.zeros_like(carry_ref)

        c = _exact_left(_tri(tm).astype(BF16), sm) + carry_ref[0:1, :]
        carry_ref[0:1, :] = c[tm - 1:tm, :]
        parts = jnp.concatenate(_split3(c * LOG2E), axis=1)
        aug = _dot(parts, aug_ref[...]) + augc_ref[...]
        qa_ref[...] = aug[:, :FOX_WIDTH].astype(BF16)
        ka_ref[...] = aug[:, FOX_WIDTH:].astype(BF16)


AUG_LANES = 6


def _aug_tables():
    place = np.zeros((3 * LANES, 2 * FOX_WIDTH), np.float32)
    ones = np.zeros((1, 2 * FOX_WIDTH), np.float32)
    for h in range(FOX_HEADS):
        q0 = (h // 2) * LANES + (h % 2) * AUG_LANES
        k0 = FOX_WIDTH + q0
        for part in range(3):
            place[part * LANES + LOGF_LANE + h, q0 + part] = 1.0
            place[part * LANES + LOGF_LANE + h, k0 + 3 + part] = -1.0
            ones[0, q0 + 3 + part] = 1.0
            ones[0, k0 + part] = 1.0
    return jnp.asarray(place, BF16), jnp.asarray(ones, F32)


def _inproj(x2d, norm_w, w_cat, prm, *, tm, seq_len, prompt):
    n = x2d.shape[0]
    assert n % tm == 0 and (not prompt or seq_len % tm == 0)
    tps = seq_len // tm if prompt else 1
    row = lambda w: pl.BlockSpec((tm, w), lambda i: (i, 0))
    const = lambda a: pl.BlockSpec(a.shape, lambda i: (0,) * a.ndim, pipeline_mode=pl.Buffered(1))
    ins = [x2d, norm_w, w_cat, prm]
    out_shape = [jax.ShapeDtypeStruct((n, FOX_WIDTH), BF16),
                 jax.ShapeDtypeStruct((n, FOX_WIDTH), F32),
                 jax.ShapeDtypeStruct((n, FOX_WIDTH), F32)]
    out_specs = [row(FOX_WIDTH), row(FOX_WIDTH), row(FOX_WIDTH)]
    scratch = []
    if prompt:
        ins += list(_aug_tables())
        out_shape += [jax.ShapeDtypeStruct((n, FOX_WIDTH), BF16)] * 3 + [jax.ShapeDtypeStruct((n, 2 * FOX_WIDTH), BF16)]
        out_specs += [row(FOX_WIDTH)] * 3 + [row(2 * FOX_WIDTH)]
        scratch = [pltpu.VMEM((SUBLANES, LANES), F32)]
    out_shape += [jax.ShapeDtypeStruct((n, GDN_CONV_CH), F32),
                  jax.ShapeDtypeStruct((n, GDN_WIDTH), F32),
                  jax.ShapeDtypeStruct((n, LANES), F32)]
    out_specs += [row(GDN_CONV_CH), row(GDN_WIDTH), row(LANES)]
    return pl.pallas_call(
        functools.partial(_inproj_kernel, tiles_per_seq=tps, prompt=prompt),
        grid=(n // tm,),
        in_specs=[row(D_MODEL)] + [const(a) for a in ins[1:]],
        out_specs=out_specs, out_shape=out_shape, scratch_shapes=scratch,
        compiler_params=pltpu.CompilerParams(dimension_semantics=("arbitrary",),
                                             vmem_limit_bytes=VMEM_LIMIT),
        name="inproj_prompt" if prompt else "inproj_sample",
    )(*ins)


def _fox_prompt_kernel(q_ref, qa_ref, k_ref, ka_ref, va_ref, o_ref, *, t):
    i = pl.program_id(2)
    q = q_ref[0]
    qa = qa_ref[0]
    lane = lax.broadcasted_iota(jnp.int32, (t, LANES), 1)
    low = lane < FOX_DIM
    zero = jnp.zeros_like(q)
    qcat = []
    for h in range(2):
        mine = (lane >= h * AUG_LANES) & (lane < (h + 1) * AUG_LANES)
        qcat.append(jnp.concatenate([jnp.where(low if h == 0 else jnp.logical_not(low), q, zero),
                                     jnp.where(mine, qa, zero)], axis=1))

    def block(j, carry, diag):
        start = pl.multiple_of(j * t, t)
        kcat = jnp.concatenate([k_ref[0, pl.ds(start, t), :], ka_ref[0, pl.ds(start, t), :]], axis=1)
        out = []
        for h in range(2):
            m, acc = carry[h]
            s = _dot_nt(qcat[h], kcat)
            if diag:
                s = jnp.where(_tri(t), s, MASKED)
            m_new = jnp.maximum(m, jnp.max(s, axis=1, keepdims=True))
            p = jnp.exp2(s - m_new)
            acc = jnp.exp2(m - m_new) * acc + _dot(p.astype(BF16), va_ref[0, pl.ds(start, t), h * LANES:(h + 1) * LANES])
            out.append((m_new, acc))
        return tuple(out)

    init = tuple((jnp.full((t, 1), MASKED, F32), jnp.zeros((t, LANES), F32)) for _ in range(2))
    carry = lax.fori_loop(0, i, lambda j, c: block(j, c, False), init)
    (_, a0), (_, a1) = block(i, carry, True)
    o_ref[0] = jnp.where(low, a0 / pltpu.roll(a0, FOX_DIM, axis=1),
                         pltpu.roll(a1, FOX_DIM, axis=1) / a1).astype(BF16)


def _fox_prompt(q, qa, kb, ka, va, *, t):
    b, s, _ = q.shape
    nblk = s // t
    assert s % t == 0
    npair = FOX_HEADS // 2
    qblk = pl.BlockSpec((1, t, LANES), lambda bi, p, i: (bi, i, p))
    kblk = pl.BlockSpec((1, s, LANES), lambda bi, p, i: (bi, 0, p))
    return pl.pallas_call(
        functools.partial(_fox_prompt_kernel, t=t),
        grid=(b, npair, nblk),
        in_specs=[qblk, qblk, kblk, kblk, pl.BlockSpec((1, s, 2 * LANES), lambda bi, p, i: (bi, 0, p))],
        out_specs=qblk,
        out_shape=jax.ShapeDtypeStruct((b, s, FOX_WIDTH), BF16),
        compiler_params=pltpu.CompilerParams(
            dimension_semantics=("arbitrary", "arbitrary", "arbitrary"), vmem_limit_bytes=VMEM_LIMIT),
        name="fox_prompt",
    )(q, qa, kb, ka, va)


def _fox_sample_kernel(q_ref, kn_ref, vn_ref, sm_ref, lft_ref, kp_ref, vp_ref, o_ref,
                       rk_ref, m_ref, l_ref, acc_ref, *, kb, t):
    j = pl.program_id(1)
    nkv = pl.num_programs(1)
    rows = FOX_HEADS * t
    lane = lax.broadcasted_iota(jnp.int32, (t, FOX_WIDTH), 1)

    q = q_ref[0].astype(F32)
    qbd = jnp.concatenate(
        [jnp.where((lane >= h * FOX_DIM) & (lane < (h + 1) * FOX_DIM), q, 0.0) for h in range(FOX_HEADS)],
        axis=0).astype(BF16)

    lf_new = sm_ref[0]
    cn = _exact_left(_tri(t).astype(BF16), lf_new)

    @pl.when(j == 0)
    def _():
        p = lft_ref.shape[2]
        nb = p // LANES
        x = jnp.concatenate([lft_ref[0, :, i * LANES:(i + 1) * LANES] for i in range(nb)], axis=0)
        r = lax.broadcasted_iota(jnp.int32, (LANES, LANES), 0)
        cc = lax.broadcasted_iota(jnp.int32, (LANES, LANES), 1)
        y = _exact_right(x, (r > cc).astype(BF16))
        tot = jnp.sum(x, axis=1, keepdims=True)
        off = jnp.zeros((FOX_HEADS, 1), F32)
        for i in range(nb - 1, -1, -1):
            rk_ref[:, i * LANES:(i + 1) * LANES] = y[i * FOX_HEADS:(i + 1) * FOX_HEADS, :] + off
            off = off + tot[i * FOX_HEADS:(i + 1) * FOX_HEADS, :]
        m_ref[...] = jnp.full_like(m_ref, MASKED)
        l_ref[...] = jnp.zeros_like(l_ref)
        acc_ref[...] = jnp.zeros_like(acc_ref)

    def update(s, v):
        m = m_ref[:, 0:1]
        m_new = jnp.maximum(m, jnp.max(s, axis=1, keepdims=True))
        alpha = jnp.exp(m - m_new)
        p = jnp.exp(s - m_new)
        l_ref[...] = jnp.broadcast_to(alpha * l_ref[:, 0:1] + jnp.sum(p, axis=1, keepdims=True), l_ref.shape)
        acc_ref[...] = alpha * acc_ref[...] + _dot(p.astype(BF16), v)
        m_ref[...] = jnp.broadcast_to(m_new, m_ref.shape)

    start = pl.multiple_of(j * kb, kb)
    rk = rk_ref[:, pl.ds(start, kb)]
    bias = jnp.concatenate([cn[:, h:h + 1] + rk[h:h + 1, :] for h in range(FOX_HEADS)], axis=0)
    s = _dot_nt(qbd, kp_ref[0].astype(BF16)) + bias
    update(s, vp_ref[0].astype(BF16))

    @pl.when(j == nkv - 1)
    def _():
        r = lax.broadcasted_iota(jnp.int32, (t, t), 0)
        cc = lax.broadcasted_iota(jnp.int32, (t, t), 1)
        after = (r > cc).astype(F32)
        tri = _tri(t).astype(BF16)
        d = jnp.concatenate([_exact_left(tri, lf_new[:, h:h + 1] * after) for h in range(FOX_HEADS)], axis=0)
        causal = jnp.concatenate([_tri(t)] * FOX_HEADS, axis=0)
        s_new = jnp.where(causal, _dot_nt(qbd, kn_ref[0].astype(BF16)) + d, MASKED)
        update(s_new, vn_ref[0].astype(BF16))
        o_full = acc_ref[...] / l_ref[:, 0:1]
        o = jnp.zeros((t, FOX_WIDTH), F32)
        for h in range(FOX_HEADS):
            sel = (lane >= h * FOX_DIM) & (lane < (h + 1) * FOX_DIM)
            o = o + jnp.where(sel, o_full[h * t:(h + 1) * t, :], 0.0)
        o_ref[0] = o.astype(BF16)


def _fox_sample(q, kn, vn, sm, lft, kp, vp, *, kb):
    b, t, _ = q.shape
    p = kp.shape[1]
    assert p % kb == 0 and kb % LANES == 0
    rows = FOX_HEADS * t
    per_b = lambda shape: pl.BlockSpec((1,) + shape, lambda bi, j: (bi, 0, 0))
    return pl.pallas_call(
        functools.partial(_fox_sample_kernel, kb=kb, t=t),
        grid=(b, p // kb),
        in_specs=[per_b((t, FOX_WIDTH)), per_b((t, FOX_WIDTH)), per_b((t, FOX_WIDTH)), per_b((t, LANES)),
                  per_b((FOX_HEADS, p)),
                  pl.BlockSpec((1, kb, FOX_WIDTH), lambda bi, j: (bi, j, 0)),
                  pl.BlockSpec((1, kb, FOX_WIDTH), lambda bi, j: (bi, j, 0))],
        out_specs=per_b((t, FOX_WIDTH)),
        out_shape=jax.ShapeDtypeStruct((b, t, FOX_WIDTH), BF16),
        scratch_shapes=[pltpu.VMEM((FOX_HEADS, p), F32),
                        pltpu.VMEM((rows, LANES), F32), pltpu.VMEM((rows, LANES), F32),
                        pltpu.VMEM((rows, FOX_WIDTH), F32)],
        compiler_params=pltpu.CompilerParams(dimension_semantics=("arbitrary", "arbitrary"),
                                             vmem_limit_bytes=VMEM_LIMIT),
        name="fox_sample",
    )(q, kn, vn, sm, lft, kp, vp)


def _gdn_unit(qh, kh, vh, g, g_row, g_last, beta, sh, c):
    wide = c % LANES == 0
    dot3 = _dot3c if wide else _dot3
    incl = _tri(c)
    strict = _tri(c, strict=True)
    qn = qh * (lax.rsqrt(jnp.sum(qh * qh, axis=-1, keepdims=True) + EPS) * (GDN_DK ** -0.5))
    kn = kh * lax.rsqrt(jnp.sum(kh * kh, axis=-1, keepdims=True) + EPS)
    eg = jnp.exp(g)
    decay = jnp.exp(jnp.where(incl, g - g_row, MASKED))
    kbeta = kn * beta
    kq = _dot_nt(jnp.concatenate([kbeta, qn], axis=0).astype(BF16), kn.astype(BF16))
    lmat = jnp.where(strict, kq[:c] * decay, 0.0)
    a_qk = kq[c:] * decay

    tinv = (incl & jnp.logical_not(strict)).astype(F32) - lmat
    power = dot3(lmat, lmat)
    span = 2
    while 2 * span < c:
        if wide:
            both = dot3(power, jnp.concatenate([tinv, power], axis=1))
            tinv = tinv + both[:, :c]
            power = both[:, c:]
        else:
            tinv = tinv + dot3(power, tinv)
            power = dot3(power, power)
        span *= 2
    tinv = tinv + dot3(power, tinv)

    sol = dot3(tinv, jnp.concatenate([vh * beta, kbeta * eg], axis=1))
    u = sol[:, :GDN_DV]
    w = sol[:, GDN_DV:]
    ws = _dot(jnp.concatenate([w, qn * eg], axis=0).astype(BF16), sh.astype(BF16))
    db = (u - ws[:c]).astype(BF16)
    kd = kn * jnp.exp(g_last - g)
    decayed = sh * jnp.exp(g_last)
    if wide:
        ak = _dot(jnp.concatenate([a_qk, kd.T], axis=0).astype(BF16), db)
        return ws[c:] + ak[:c], decayed + ak[c:]
    return ws[c:] + _dot(a_qk.astype(BF16), db), decayed + _dot_tn(kd.astype(BF16), db)


def _gdn_kernel(cur_ref, hist_ref, sm_ref, z_ref, *refs, c, nb, cpb, zero_init):
    if zero_init:
        cw_ref, nw_ref, o_ref, sout_ref, xs_ref, s_ref = refs
    else:
        s0_ref, cw_ref, nw_ref, o_ref, sout_ref, xs_ref, s_ref = refs
    i = pl.program_id(1)
    rows = c * cpb
    wide = c % LANES == 0

    @pl.when(i == 0)
    def _():
        if zero_init:
            s_ref[...] = jnp.zeros_like(s_ref)
        else:
            s_ref[...] = s0_ref[...]

    hist = hist_ref[...]
    if zero_init:
        hist = jnp.where(i > 0, hist, 0.0)
    xs_ref[:, 0:SUBLANES, :] = hist
    xs_ref[:, SUBLANES:SUBLANES + rows, :] = cur_ref[...]
    base = SUBLANES - (GDN_CONV - 1)

    def conv_silu(b, r0, col):
        acc = xs_ref[b, base + r0:base + r0 + c, col:col + LANES] * cw_ref[0:1, col:col + LANES]
        for tap in range(1, GDN_CONV):
            acc = acc + (xs_ref[b, base + r0 + tap:base + r0 + tap + c, col:col + LANES]
                         * cw_ref[tap:tap + 1, col:col + LANES])
        return acc * _sigmoid(acc)

    incl = _tri(c)
    eye = incl & jnp.logical_not(_tri(c, strict=True))
    for b in range(nb):
        state = [s_ref[b, h] for h in range(GDN_HEADS)]
        for ck in range(cpb):
            r0 = ck * c
            sm = sm_ref[b, r0:r0 + c, :]
            if wide:
                gs = _exact_left_c(incl.astype(BF16), sm)
                gst = gs.T
            else:
                gs = _exact_left(incl.astype(BF16), sm)
            for h in range(GDN_HEADS):
                g = gs[:, G_LANE + h:G_LANE + h + 1]
                if wide:
                    g_row = gst[G_LANE + h:G_LANE + h + 1, :]
                else:
                    g_row = _exact_left(jnp.ones((c, c), BF16), jnp.where(eye, g, 0.0))
                o, state[h] = _gdn_unit(
                    conv_silu(b, r0, h * GDN_DK), conv_silu(b, r0, GDN_QK + h * GDN_DK),
                    conv_silu(b, r0, 2 * GDN_QK + h * GDN_DV), g, g_row,
                    gs[c - 1:c, G_LANE + h:G_LANE + h + 1], sm[:, BETA_LANE + h:BETA_LANE + h + 1],
                    state[h], c)
                on = o * lax.rsqrt(jnp.mean(o * o, axis=-1, keepdims=True) + EPS) * nw_ref[...]
                zh = z_ref[b, r0:r0 + c, h * GDN_DV:(h + 1) * GDN_DV]
                o_ref[b, r0:r0 + c, h * GDN_DV:(h + 1) * GDN_DV] = (on * (zh * _sigmoid(zh))).astype(BF16)
        for h in range(GDN_HEADS):
            s_ref[b, h] = state[h]

    @pl.when(i == pl.num_programs(1) - 1)
    def _():
        sout_ref[...] = s_ref[...]


def _gdn(gqkv, hist, sm, gz, s0, conv_w, norm_w, *, c, nb, cpb):
    b, t, _ = gqkv.shape
    rows = c * cpb
    assert t % rows == 0 and b % nb == 0 and c % SUBLANES == 0
    zero_init = hist is None
    blk = lambda w: pl.BlockSpec((nb, rows, w), lambda bi, i: (bi, i, 0))
    const = lambda a: pl.BlockSpec(a.shape, lambda bi, i: (0,) * a.ndim)
    state_spec = pl.BlockSpec((nb, GDN_HEADS, GDN_DK, GDN_DV), lambda bi, i: (bi, 0, 0, 0))
    if zero_init:
        assert nb == 1
        hist_arr = gqkv
        hist_spec = pl.BlockSpec((1, SUBLANES, GDN_CONV_CH),
                                 lambda bi, i: (bi, jnp.maximum(i * (rows // SUBLANES) - 1, 0), 0))
        extra, extra_specs = [], []
    else:
        assert t == rows
        hist_arr = hist
        hist_spec = pl.BlockSpec((nb, SUBLANES, GDN_CONV_CH), lambda bi, i: (bi, 0, 0))
        extra, extra_specs = [s0], [state_spec]
    return pl.pallas_call(
        functools.partial(_gdn_kernel, c=c, nb=nb, cpb=cpb, zero_init=zero_init),
        grid=(b // nb, t // rows),
        in_specs=[blk(GDN_CONV_CH), hist_spec, blk(LANES), blk(GDN_WIDTH)] + extra_specs
                 + [const(conv_w), const(norm_w)],
        out_specs=[blk(GDN_WIDTH), state_spec],
        out_shape=[jax.ShapeDtypeStruct((b, t, GDN_WIDTH), BF16),
                   jax.ShapeDtypeStruct((b, GDN_HEADS, GDN_DK, GDN_DV), F32)],
        scratch_shapes=[pltpu.VMEM((nb, SUBLANES + rows, GDN_CONV_CH), F32),
                        pltpu.VMEM((nb, GDN_HEADS, GDN_DK, GDN_DV), F32)],
        compiler_params=pltpu.CompilerParams(dimension_semantics=("arbitrary", "arbitrary"),
                                             vmem_limit_bytes=VMEM_LIMIT),
        name="gdn_prompt" if zero_init else "gdn_sample",
    )(gqkv, hist_arr, sm, gz, *extra, conv_w, norm_w)


FFN_COLS = 256


def _ffn_kernel(of_ref, og_ref, x_ref, wo_ref, nfw_ref, wup_ref, cw_ref, cb_ref, wdn_ref, nlw_ref, *refs,
                nseq, rows, tiles_per_seq, has_state):
    if has_state:
        st_ref, y_ref, nst_ref, wg_ref, wv_ref, act_ref = refs
    else:
        y_ref, nst_ref, wg_ref, wv_ref, act_ref, carry_ref = refs
    i = pl.program_id(0)
    hist0 = SUBLANES - (FFN_CONV - 1)

    h = (x_ref[...] + _dot(of_ref[...], wo_ref[0:FOX_WIDTH, :])
         + _dot(og_ref[...], wo_ref[FOX_WIDTH:FOX_WIDTH + GDN_WIDTH, :]))
    hn = (h * lax.rsqrt(jnp.mean(h * h, axis=-1, keepdims=True) + EPS) * nfw_ref[...]).astype(BF16)

    if not has_state:
        @pl.when(i % tiles_per_seq == 0)
        def _():
            carry_ref[...] = jnp.zeros_like(carry_ref)

    def conv_cols(buf_ref, col):
        up = _dot(hn, wup_ref[:, col:col + FFN_COLS])
        buf_ref[:, SUBLANES:SUBLANES + rows, :] = up.reshape(nseq, rows, FFN_COLS)
        if has_state:
            buf_ref[:, hist0:SUBLANES, :] = st_ref[:, :, col:col + FFN_COLS]
        else:
            buf_ref[:, hist0:SUBLANES, :] = carry_ref[:, hist0:SUBLANES, col:col + FFN_COLS]
        last = buf_ref[:, SUBLANES + rows - (FFN_CONV - 1):SUBLANES + rows, :]
        nst_ref[:, :, col:col + FFN_COLS] = last
        if not has_state:
            carry_ref[:, hist0:SUBLANES, col:col + FFN_COLS] = last
        uc = cb_ref[:, col:col + FFN_COLS][None]
        for tap in range(FFN_CONV):
            uc = uc + buf_ref[:, hist0 + tap:hist0 + tap + rows, :] * cw_ref[tap:tap + 1, col:col + FFN_COLS][None]
        return uc.reshape(nseq * rows, FFN_COLS)

    for ci in range(D_FF // FFN_COLS):
        col = ci * FFN_COLS
        gate = conv_cols(wg_ref, col)
        val = conv_cols(wv_ref, D_FF + col)
        act_ref[:, col:col + FFN_COLS] = (gate * _sigmoid(gate) * val).astype(BF16)

    h2 = h + _dot(act_ref[...], wdn_ref[...])
    y_ref[...] = h2 * lax.rsqrt(jnp.mean(h2 * h2, axis=-1, keepdims=True) + EPS) * nlw_ref[...]


def _ffn(ofox, ogdn, x2d, w_o, norm_ffn_w, w_up, conv_w, conv_b, w_down, norm_final_w, state, *,
         nseq, rows, seq_len):
    n = x2d.shape[0]
    tm = nseq * rows
    assert n % tm == 0 and seq_len % rows == 0 and rows % SUBLANES == 0 and D_FF % FFN_COLS == 0
    has_state = state is not None
    assert (rows == seq_len) if has_state else (nseq == 1)
    tps = seq_len // rows
    nbatch = n // seq_len
    row = lambda w: pl.BlockSpec((tm, w), lambda i: (i, 0))
    const = lambda a: pl.BlockSpec(a.shape, lambda i: (0,) * a.ndim, pipeline_mode=pl.Buffered(1))
    st_spec = pl.BlockSpec((nseq, FFN_CONV - 1, 2 * D_FF), lambda i: (i // tps, 0, 0))
    ins = [ofox, ogdn, x2d, w_o, norm_ffn_w, w_up, conv_w, conv_b, w_down, norm_final_w]
    in_specs = [row(FOX_WIDTH), row(GDN_WIDTH), row(D_MODEL)] + [const(a) for a in ins[3:]]
    scratch = [pltpu.VMEM((nseq, SUBLANES + rows, FFN_COLS), F32),
               pltpu.VMEM((nseq, SUBLANES + rows, FFN_COLS), F32),
               pltpu.VMEM((tm, D_FF), BF16)]
    if has_state:
        ins.append(state)
        in_specs.append(st_spec)
    else:
        scratch.append(pltpu.VMEM((nseq, SUBLANES, 2 * D_FF), F32))
    return pl.pallas_call(
        functools.partial(_ffn_kernel, nseq=nseq, rows=rows, tiles_per_seq=tps, has_state=has_state),
        grid=(n // tm,),
        in_specs=in_specs,
        out_specs=[row(D_MODEL), st_spec],
        out_shape=[jax.ShapeDtypeStruct((n, D_MODEL), F32),
                   jax.ShapeDtypeStruct((nbatch, FFN_CONV - 1, 2 * D_FF), F32)],
        scratch_shapes=scratch,
        compiler_params=pltpu.CompilerParams(dimension_semantics=("arbitrary",),
                                             vmem_limit_bytes=VMEM_LIMIT),
        name="ffn_sample" if has_state else "ffn_prompt",
    )(*ins)


ROW_TILE = 512
SAMPLE_KV_BLOCK = 2048
GDN_CHUNK = 128
GDN_CHUNKS_PER_STEP = 2
GDN_SAMPLE_SEQS_PER_STEP = 4


def _prep_weights(w_in, b_fox_f, gdn_a_log, gdn_dt_bias):
    c0 = 3 * FOX_WIDTH
    c1 = c0 + FOX_HEADS
    c2 = c1 + GDN_CONV_CH
    c3 = c2 + 2 * GDN_HEADS
    small = jnp.concatenate([w_in[:, c0:c1], w_in[:, c2:c3],
                             jnp.zeros((D_MODEL, LANES - FOX_HEADS - 2 * GDN_HEADS), w_in.dtype)], axis=1)
    w_cat = jnp.concatenate([w_in[:, :c0], w_in[:, c1:c2], w_in[:, c3:], small], axis=1).astype(BF16)
    pad = jnp.zeros((LANES - FOX_HEADS - GDN_HEADS,), F32)
    bias = jnp.concatenate([b_fox_f.astype(F32), gdn_dt_bias.astype(F32), pad])
    alog = jnp.concatenate([jnp.zeros((FOX_HEADS,), F32), gdn_a_log.astype(F32), pad])
    prm = jnp.zeros((SUBLANES, LANES), F32).at[0].set(bias).at[1].set(alog)
    return w_cat, prm


def _layer(xp, xs, fox_k, fox_v, fox_logf, st_gdn, st_gconv, st_fconv,
           w_in, b_fox_f, gdn_conv_w, gdn_a_log, gdn_dt_bias, gdn_norm_w, w_o,
           norm_mix_w, norm_ffn_w, w_up, ffn_conv_w, ffn_conv_b, w_down, norm_out_w):
    bp, sp, _ = xp.shape
    bs, ts, _ = xs.shape
    w_cat, prm = _prep_weights(w_in, b_fox_f, gdn_a_log, gdn_dt_bias)
    nmw = norm_mix_w.reshape(1, D_MODEL).astype(F32)
    nfw = norm_ffn_w.reshape(1, D_MODEL).astype(F32)
    now = norm_out_w.reshape(1, D_MODEL).astype(F32)
    gnw = gdn_norm_w.reshape(1, GDN_DV).astype(F32)
    w_o_b, w_up_b, w_dn_b = w_o.astype(BF16), w_up.astype(BF16), w_down.astype(BF16)
    cb = ffn_conv_b.reshape(1, 2 * D_FF).astype(F32)

    xp2 = xp.reshape(bp * sp, D_MODEL)
    q, k, v, kb, qa, ka, va, gqkv, gz, sm = _inproj(xp2, nmw, w_cat, prm, tm=ROW_TILE, seq_len=sp, prompt=True)
    r3 = lambda a: a.reshape(bp, sp, a.shape[-1])
    o_fox = _fox_prompt(r3(q), r3(qa), r3(kb), r3(ka), r3(va), t=ROW_TILE)
    gqkv3 = r3(gqkv)
    o_gdn, p_state = _gdn(gqkv3, None, r3(sm), r3(gz), None, gdn_conv_w, gnw,
                          c=GDN_CHUNK, nb=1, cpb=GDN_CHUNKS_PER_STEP)
    yp, p_fconv = _ffn(o_fox.reshape(bp * sp, FOX_WIDTH), o_gdn.reshape(bp * sp, GDN_WIDTH), xp2,
                       w_o_b, nfw, w_up_b, ffn_conv_w, cb, w_dn_b, now, None,
                       nseq=1, rows=ROW_TILE, seq_len=sp)
    p_out = (k.reshape(bp, sp, FOX_HEADS, FOX_DIM), v.reshape(bp, sp, FOX_HEADS, FOX_DIM),
             r3(sm)[:, :, LOGF_LANE:LOGF_LANE + FOX_HEADS], p_state,
             gqkv3[:, sp - (GDN_CONV - 1):], p_fconv)

    xs2 = xs.reshape(bs * ts, D_MODEL)
    q, k, v, gqkv, gz, sm = _inproj(xs2, nmw, w_cat, prm, tm=bs * ts, seq_len=ts, prompt=False)
    s3 = lambda a: a.reshape(bs, ts, a.shape[-1])
    past = fox_k.shape[1]
    lft = jnp.swapaxes(fox_logf.astype(F32), 1, 2)
    o_fox = _fox_sample(s3(q), s3(k), s3(v), s3(sm), lft,
                        fox_k.reshape(bs, past, FOX_WIDTH), fox_v.reshape(bs, past, FOX_WIDTH),
                        kb=min(SAMPLE_KV_BLOCK, past))
    gqkv3 = s3(gqkv)
    hist = jnp.concatenate([jnp.zeros((bs, SUBLANES - (GDN_CONV - 1), GDN_CONV_CH), F32),
                            st_gconv.astype(F32)], axis=1)
    o_gdn, s_state = _gdn(gqkv3, hist, s3(sm), s3(gz), st_gdn.astype(F32), gdn_conv_w, gnw,
                          c=ts, nb=GDN_SAMPLE_SEQS_PER_STEP, cpb=1)
    ys, s_fconv = _ffn(o_fox.reshape(bs * ts, FOX_WIDTH), o_gdn.reshape(bs * ts, GDN_WIDTH), xs2,
                       w_o_b, nfw, w_up_b, ffn_conv_w, cb, w_dn_b, now, st_fconv.astype(F32),
                       nseq=bs, rows=ts, seq_len=ts)
    full = jnp.concatenate([st_gconv.astype(F32), gqkv3], axis=1)
    s_out = (k.reshape(bs, ts, FOX_HEADS, FOX_DIM), v.reshape(bs, ts, FOX_HEADS, FOX_DIM),
             s3(sm)[:, :, LOGF_LANE:LOGF_LANE + FOX_HEADS], s_state,
             full[:, ts:], s_fconv)
    return yp.reshape(bp, sp, D_MODEL), ys.reshape(bs, ts, D_MODEL), p_out, s_out


def kernel(x_prompt, x_sample, cache_fox_k, cache_fox_v, cache_fox_logf, state_gdn, state_gdn_conv,
           state_ffn_conv, w_in, b_fox_f, gdn_conv_w, gdn_a_log, gdn_dt_bias, gdn_norm_w, w_o, norm_mix_w,
           norm_ffn_w, w_up, ffn_conv_w, ffn_conv_b, w_down, norm_final_w):
    depth = w_in.shape[0]
    assert depth == 1, "the final RMSNorm is fused into the single layer's FFN kernel"
    yp, ys, p_out, s_out = _layer(
        x_prompt, x_sample, cache_fox_k[0], cache_fox_v[0], cache_fox_logf[0], state_gdn[0],
        state_gdn_conv[0], state_ffn_conv[0], w_in[0], b_fox_f[0], gdn_conv_w[0], gdn_a_log[0],
        gdn_dt_bias[0], gdn_norm_w[0], w_o[0], norm_mix_w[0], norm_ffn_w[0], w_up[0], ffn_conv_w[0],
        ffn_conv_b[0], w_down[0], norm_final_w)
    return (yp, ys) + tuple(a[None] for a in p_out) + tuple(a[None] for a in s_out)
```

```python
import functools
import math

import numpy as np
import jax
import jax.numpy as jnp
from jax import lax
from jax.experimental import pallas as pl
from jax.experimental.pallas import tpu as pltpu

F32 = jnp.float32
BF16 = jnp.bfloat16

D_MODEL = 1024
FOX_DIM = 64
FOX_HEADS = 8
FOX_WIDTH = FOX_HEADS * FOX_DIM
GDN_DK = 128
GDN_DV = 128
GDN_HEADS = 4
GDN_QK = GDN_HEADS * GDN_DK
GDN_WIDTH = GDN_HEADS * GDN_DV
GDN_CONV_CH = 2 * GDN_QK + GDN_WIDTH
GDN_CONV = 4
D_FF = 2816
FFN_CONV = 3
EPS = 1e-6

LANES = 128
SUBLANES = 8
VMEM_LIMIT = 56 * 1024 * 1024
MASKED = -1e30
LOG2E = math.log2(math.e)

LOGF_LANE = 0
G_LANE = FOX_HEADS
BETA_LANE = FOX_HEADS + GDN_HEADS

COL_FOX = 0
COL_GQKV = 3 * FOX_WIDTH
COL_GZ = COL_GQKV + GDN_CONV_CH
COL_SMALL = COL_GZ + GDN_WIDTH
N_COLS = COL_SMALL + LANES


def _dot(a, b):
    return jnp.dot(a, b, preferred_element_type=F32)


def _dot_nt(a, b):
    return lax.dot_general(a, b, (((1,), (1,)), ((), ())), preferred_element_type=F32)


def _dot_tn(a, b):
    return lax.dot_general(a, b, (((0,), (0,)), ((), ())), preferred_element_type=F32)


def _split3(x):
    hi = x.astype(BF16)
    r = x - hi.astype(F32)
    lo = r.astype(BF16)
    lo2 = (r - lo.astype(F32)).astype(BF16)
    return hi, lo, lo2


def _exact_left(mat, x):
    hi, lo, lo2 = _split3(x)
    return _dot(mat, hi) + _dot(mat, lo) + _dot(mat, lo2)


def _exact_right(x, mat):
    hi, lo, lo2 = _split3(x)
    return _dot(hi, mat) + _dot(lo, mat) + _dot(lo2, mat)


def _dot3(a, b):
    ah = a.astype(BF16)
    al = (a - ah.astype(F32)).astype(BF16)
    bh = b.astype(BF16)
    bl = (b - bh.astype(F32)).astype(BF16)
    return _dot(ah, bh) + (_dot(ah, bl) + _dot(al, bh))


def _exact_left_c(mat, x):
    return _dot(jnp.concatenate([mat, mat, mat], axis=1), jnp.concatenate(_split3(x), axis=0))


def _dot3c(a, b):
    ah = a.astype(BF16)
    al = (a - ah.astype(F32)).astype(BF16)
    bh = b.astype(BF16)
    bl = (b - bh.astype(F32)).astype(BF16)
    return _dot(jnp.concatenate([ah, ah, al], axis=1), jnp.concatenate([bh, bl, bh], axis=0))


def _sigmoid(x):
    return 1.0 / (1.0 + jnp.exp(-x))


def _tri(n, strict=False):
    r = lax.broadcasted_iota(jnp.int32, (n, n), 0)
    c = lax.broadcasted_iota(jnp.int32, (n, n), 1)
    return (r > c) if strict else (r >= c)


def _inproj_kernel(x_ref, nw_ref, w_ref, prm_ref, *refs, tiles_per_seq, prompt):
    if prompt:
        (aug_ref, augc_ref, q_ref, k_ref, v_ref, kb_ref, qa_ref, ka_ref, va_ref, g_ref, z_ref, sm_ref,
         carry_ref) = refs
    else:
        q_ref, k_ref, v_ref, g_ref, z_ref, sm_ref = refs
    x = x_ref[...]
    xn = x * lax.rsqrt(jnp.mean(x * x, axis=-1, keepdims=True) + EPS) * nw_ref[...]
    xb = xn.astype(BF16)

    q_scale = FOX_DIM ** -0.5 * (LOG2E if prompt else 1.0)
    q_ref[...] = (_dot(xb, w_ref[:, 0:FOX_WIDTH]) * q_scale).astype(BF16)
    kf = _dot(xb, w_ref[:, FOX_WIDTH:2 * FOX_WIDTH])
    vf = _dot(xb, w_ref[:, 2 * FOX_WIDTH:3 * FOX_WIDTH])
    k_ref[...] = kf
    v_ref[...] = vf
    if prompt:
        kb_ref[...] = kf.astype(BF16)
        tm = x.shape[0]
        low = lax.broadcasted_iota(jnp.int32, (tm, LANES), 1) < FOX_DIM
        for p in range(FOX_HEADS // 2):
            pair = vf[:, p * LANES:(p + 1) * LANES]
            va_ref[:, 2 * p * LANES:(2 * p + 1) * LANES] = jnp.where(low, pair, 1.0).astype(BF16)
            va_ref[:, (2 * p + 1) * LANES:(2 * p + 2) * LANES] = jnp.where(
                low, pltpu.roll(pair, FOX_DIM, axis=1), 1.0).astype(BF16)
    g_ref[...] = _dot(xb, w_ref[:, COL_GQKV:COL_GZ])
    z_ref[...] = _dot(xb, w_ref[:, COL_GZ:COL_SMALL])

    z = _dot(xb, w_ref[:, COL_SMALL:N_COLS])
    lane = lax.broadcasted_iota(jnp.int32, z.shape, 1)
    zb = z + prm_ref[0:1, :]
    e = jnp.exp(-jnp.abs(zb))
    l1p = jnp.log(1.0 + e)
    logf = jnp.minimum(zb, 0.0) - l1p
    g = -jnp.exp(prm_ref[1:2, :]) * (jnp.maximum(zb, 0.0) + l1p)
    beta = _sigmoid(z)
    sm = jnp.where(lane < G_LANE, logf,
                   jnp.where(lane < BETA_LANE, g,
                             jnp.where(lane < BETA_LANE + GDN_HEADS, beta, 0.0)))
    sm_ref[...] = sm

    if prompt:
        i = pl.program_id(0)

        @pl.when(i % tiles_per_seq == 0)
        def _():
            carry_ref[...] = jnp.zeros_like(carry_ref)

        c = _exact_left(_tri(tm).astype(BF16), sm) + carry_ref[0:1, :]
        carry_ref[0:1, :] = c[tm - 1:tm, :]
        parts = jnp.concatenate(_split3(c * LOG2E), axis=1)
        aug = _dot(parts, aug_ref[...]) + augc_ref[...]
        qa_ref[...] = aug[:, :FOX_WIDTH].astype(BF16)
        ka_ref[...] = aug[:, FOX_WIDTH:].astype(BF16)


AUG_LANES = 6


def _aug_tables():
    place = np.zeros((3 * LANES, 2 * FOX_WIDTH), np.float32)
    ones = np.zeros((1, 2 * FOX_WIDTH), np.float32)
    for h in range(FOX_HEADS):
        q0 = (h // 2) * LANES + (h % 2) * AUG_LANES
        k0 = FOX_WIDTH + q0
        for part in range(3):
            place[part * LANES + LOGF_LANE + h, q0 + part] = 1.0
            place[part * LANES + LOGF_LANE + h, k0 + 3 + part] = -1.0
            ones[0, q0 + 3 + part] = 1.0
            ones[0, k0 + part] = 1.0
    return jnp.asarray(place, BF16), jnp.asarray(ones, F32)


def _inproj(x2d, norm_w, w_cat, prm, *, tm, seq_len, prompt):
    n = x2d.shape[0]
    assert n % tm == 0 and (not prompt or seq_len % tm == 0)
    tps = seq_len // tm if prompt else 1
    row = lambda w: pl.BlockSpec((tm, w), lambda i: (i, 0))
    const = lambda a: pl.BlockSpec(a.shape, lambda i: (0,) * a.ndim, pipeline_mode=pl.Buffered(1))
    ins = [x2d, norm_w, w_cat, prm]
    out_shape = [jax.ShapeDtypeStruct((n, FOX_WIDTH), BF16),
                 jax.ShapeDtypeStruct((n, FOX_WIDTH), F32),
                 jax.ShapeDtypeStruct((n, FOX_WIDTH), F32)]
    out_specs = [row(FOX_WIDTH), row(FOX_WIDTH), row(FOX_WIDTH)]
    scratch = []
    if prompt:
        ins += list(_aug_tables())
        out_shape += [jax.ShapeDtypeStruct((n, FOX_WIDTH), BF16)] * 3 + [jax.ShapeDtypeStruct((n, 2 * FOX_WIDTH), BF16)]
        out_specs += [row(FOX_WIDTH)] * 3 + [row(2 * FOX_WIDTH)]
        scratch = [pltpu.VMEM((SUBLANES, LANES), F32)]
    out_shape += [jax.ShapeDtypeStruct((n, GDN_CONV_CH), F32),
                  jax.ShapeDtypeStruct((n, GDN_WIDTH), F32),
                  jax.ShapeDtypeStruct((n, LANES), F32)]
    out_specs += [row(GDN_CONV_CH), row(GDN_WIDTH), row(LANES)]
    return pl.pallas_call(
        functools.partial(_inproj_kernel, tiles_per_seq=tps, prompt=prompt),
        grid=(n // tm,),
        in_specs=[row(D_MODEL)] + [const(a) for a in ins[1:]],
        out_specs=out_specs, out_shape=out_shape, scratch_shapes=scratch,
        compiler_params=pltpu.CompilerParams(dimension_semantics=("arbitrary",),
                                             vmem_limit_bytes=VMEM_LIMIT),
        name="inproj_prompt" if prompt else "inproj_sample",
    )(*ins)


def _fox_prompt_kernel(q_ref, qa_ref, k_ref, ka_ref, va_ref, o_ref, sa_ref, sb_ref, *, t):
    i = pl.program_id(2)
    q = q_ref[0]
    qa = qa_ref[0]
    lane = lax.broadcasted_iota(jnp.int32, (t, LANES), 1)
    low = lane < FOX_DIM
    zero = jnp.zeros_like(q)
    qcat = []
    for h in range(2):
        mine = (lane >= h * AUG_LANES) & (lane < (h + 1) * AUG_LANES)
        qcat.append(jnp.concatenate([jnp.where(low if h == 0 else jnp.logical_not(low), q, zero),
                                     jnp.where(mine, qa, zero)], axis=1))

    def logits(j, s_ref):
        start = pl.multiple_of(j * t, t)
        kcat = jnp.concatenate([k_ref[0, pl.ds(start, t), :], ka_ref[0, pl.ds(start, t), :]], axis=1)
        for h in range(2):
            s_ref[h] = _dot_nt(qcat[h], kcat)

    def consume(j, s_ref, carry, diag):
        start = pl.multiple_of(j * t, t)
        out = []
        for h in range(2):
            m, acc = carry[h]
            s = s_ref[h]
            if diag:
                s = jnp.where(_tri(t), s, MASKED)
            m_new = jnp.maximum(m, jnp.max(s, axis=1, keepdims=True))
            p = jnp.exp2(s - m_new)
            acc = jnp.exp2(m - m_new) * acc + _dot(p.astype(BF16), va_ref[0, pl.ds(start, t), h * LANES:(h + 1) * LANES])
            out.append((m_new, acc))
        return tuple(out)

    init = tuple((jnp.full((t, 1), MASKED, F32), jnp.zeros((t, LANES), F32)) for _ in range(2))
    logits(0, sa_ref)

    def two_blocks(jj, carry):
        j = 2 * jj
        logits(j + 1, sb_ref)
        carry = consume(j, sa_ref, carry, False)
        logits(j + 2, sa_ref)
        return consume(j + 1, sb_ref, carry, False)

    carry = lax.fori_loop(0, i // 2, two_blocks, init)

    def odd_tail(carry):
        logits(i, sb_ref)
        return consume(i, sb_ref, consume(i - 1, sa_ref, carry, False), True)

    (_, a0), (_, a1) = lax.cond(i % 2 == 1, odd_tail, lambda c: consume(i, sa_ref, c, True), carry)
    o_ref[0] = jnp.where(low, a0 / pltpu.roll(a0, FOX_DIM, axis=1),
                         pltpu.roll(a1, FOX_DIM, axis=1) / a1).astype(BF16)


def _fox_prompt(q, qa, kb, ka, va, *, t):
    b, s, _ = q.shape
    nblk = s // t
    assert s % t == 0
    npair = FOX_HEADS // 2
    qblk = pl.BlockSpec((1, t, LANES), lambda bi, p, i: (bi, i, p))
    kblk = pl.BlockSpec((1, s, LANES), lambda bi, p, i: (bi, 0, p))
    return pl.pallas_call(
        functools.partial(_fox_prompt_kernel, t=t),
        grid=(b, npair, nblk),
        in_specs=[qblk, qblk, kblk, kblk, pl.BlockSpec((1, s, 2 * LANES), lambda bi, p, i: (bi, 0, p))],
        out_specs=qblk,
        out_shape=jax.ShapeDtypeStruct((b, s, FOX_WIDTH), BF16),
        scratch_shapes=[pltpu.VMEM((2, t, t), F32), pltpu.VMEM((2, t, t), F32)],
        compiler_params=pltpu.CompilerParams(
            dimension_semantics=("arbitrary", "arbitrary", "arbitrary"), vmem_limit_bytes=VMEM_LIMIT),
        name="fox_prompt",
    )(q, qa, kb, ka, va)


def _fox_sample_kernel(q_ref, kn_ref, vn_ref, sm_ref, lft_ref, kp_ref, vp_ref, o_ref,
                       rk_ref, m_ref, l_ref, acc_ref, *, kb, t):
    j = pl.program_id(1)
    nkv = pl.num_programs(1)
    rows = FOX_HEADS * t
    lane = lax.broadcasted_iota(jnp.int32, (t, FOX_WIDTH), 1)

    q = q_ref[0].astype(F32)
    qbd = jnp.concatenate(
        [jnp.where((lane >= h * FOX_DIM) & (lane < (h + 1) * FOX_DIM), q, 0.0) for h in range(FOX_HEADS)],
        axis=0).astype(BF16)

    lf_new = sm_ref[0]
    cn = _exact_left(_tri(t).astype(BF16), lf_new)

    @pl.when(j == 0)
    def _():
        p = lft_ref.shape[2]
        nb = p // LANES
        x = jnp.concatenate([lft_ref[0, :, i * LANES:(i + 1) * LANES] for i in range(nb)], axis=0)
        r = lax.broadcasted_iota(jnp.int32, (LANES, LANES), 0)
        cc = lax.broadcasted_iota(jnp.int32, (LANES, LANES), 1)
        y = _exact_right(x, (r > cc).astype(BF16))
        tot = jnp.sum(x, axis=1, keepdims=True)
        off = jnp.zeros((FOX_HEADS, 1), F32)
        for i in range(nb - 1, -1, -1):
            rk_ref[:, i * LANES:(i + 1) * LANES] = y[i * FOX_HEADS:(i + 1) * FOX_HEADS, :] + off
            off = off + tot[i * FOX_HEADS:(i + 1) * FOX_HEADS, :]
        m_ref[...] = jnp.full_like(m_ref, MASKED)
        l_ref[...] = jnp.zeros_like(l_ref)
        acc_ref[...] = jnp.zeros_like(acc_ref)

    def update(s, v):
        m = m_ref[:, 0:1]
        m_new = jnp.maximum(m, jnp.max(s, axis=1, keepdims=True))
        alpha = jnp.exp(m - m_new)
        p = jnp.exp(s - m_new)
        l_ref[...] = jnp.broadcast_to(alpha * l_ref[:, 0:1] + jnp.sum(p, axis=1, keepdims=True), l_ref.shape)
        acc_ref[...] = alpha * acc_ref[...] + _dot(p.astype(BF16), v)
        m_ref[...] = jnp.broadcast_to(m_new, m_ref.shape)

    start = pl.multiple_of(j * kb, kb)
    rk = rk_ref[:, pl.ds(start, kb)]
    bias = jnp.concatenate([cn[:, h:h + 1] + rk[h:h + 1, :] for h in range(FOX_HEADS)], axis=0)
    s = _dot_nt(qbd, kp_ref[0].astype(BF16)) + bias
    update(s, vp_ref[0].astype(BF16))

    @pl.when(j == nkv - 1)
    def _():
        r = lax.broadcasted_iota(jnp.int32, (t, t), 0)
        cc = lax.broadcasted_iota(jnp.int32, (t, t), 1)
        after = (r > cc).astype(F32)
        tri = _tri(t).astype(BF16)
        d = jnp.concatenate([_exact_left(tri, lf_new[:, h:h + 1] * after) for h in range(FOX_HEADS)], axis=0)
        causal = jnp.concatenate([_tri(t)] * FOX_HEADS, axis=0)
        s_new = jnp.where(causal, _dot_nt(qbd, kn_ref[0].astype(BF16)) + d, MASKED)
        update(s_new, vn_ref[0].astype(BF16))
        o_full = acc_ref[...] / l_ref[:, 0:1]
        o = jnp.zeros((t, FOX_WIDTH), F32)
        for h in range(FOX_HEADS):
            sel = (lane >= h * FOX_DIM) & (lane < (h + 1) * FOX_DIM)
            o = o + jnp.where(sel, o_full[h * t:(h + 1) * t, :], 0.0)
        o_ref[0] = o.astype(BF16)


def _fox_sample(q, kn, vn, sm, lft, kp, vp, *, kb):
    b, t, _ = q.shape
    p = kp.shape[1]
    assert p % kb == 0 and kb % LANES == 0
    rows = FOX_HEADS * t
    per_b = lambda shape: pl.BlockSpec((1,) + shape, lambda bi, j: (bi, 0, 0))
    return pl.pallas_call(
        functools.partial(_fox_sample_kernel, kb=kb, t=t),
        grid=(b, p // kb),
        in_specs=[per_b((t, FOX_WIDTH)), per_b((t, FOX_WIDTH)), per_b((t, FOX_WIDTH)), per_b((t, LANES)),
                  per_b((FOX_HEADS, p)),
                  pl.BlockSpec((1, kb, FOX_WIDTH), lambda bi, j: (bi, j, 0)),
                  pl.BlockSpec((1, kb, FOX_WIDTH), lambda bi, j: (bi, j, 0))],
        out_specs=per_b((t, FOX_WIDTH)),
        out_shape=jax.ShapeDtypeStruct((b, t, FOX_WIDTH), BF16),
        scratch_shapes=[pltpu.VMEM((FOX_HEADS, p), F32),
                        pltpu.VMEM((rows, LANES), F32), pltpu.VMEM((rows, LANES), F32),
                        pltpu.VMEM((rows, FOX_WIDTH), F32)],
        compiler_params=pltpu.CompilerParams(dimension_semantics=("arbitrary", "arbitrary"),
                                             vmem_limit_bytes=VMEM_LIMIT),
        name="fox_sample",
    )(q, kn, vn, sm, lft, kp, vp)


def _interleave(gens):
    results = [None] * len(gens)
    alive = list(range(len(gens)))
    while alive:
        for idx in list(alive):
            try:
                next(gens[idx])
            except StopIteration as stop:
                results[idx] = stop.value
                alive.remove(idx)
    return results


def _gdn_local(qh, kh, vh, g, g_row, g_last, beta, c):
    wide = c % LANES == 0
    dot3 = _dot3c if wide else _dot3
    incl = _tri(c)
    strict = _tri(c, strict=True)
    qn = qh * (lax.rsqrt(jnp.sum(qh * qh, axis=-1, keepdims=True) + EPS) * (GDN_DK ** -0.5))
    kn = kh * lax.rsqrt(jnp.sum(kh * kh, axis=-1, keepdims=True) + EPS)
    eg = jnp.exp(g)
    decay = jnp.exp(jnp.where(incl, g - g_row, MASKED))
    kbeta = kn * beta
    kq = _dot_nt(jnp.concatenate([kbeta, qn], axis=0).astype(BF16), kn.astype(BF16))
    yield
    lmat = jnp.where(strict, kq[:c] * decay, 0.0)
    a_qk = kq[c:] * decay

    tinv = (incl & jnp.logical_not(strict)).astype(F32) - lmat
    power = dot3(lmat, lmat)
    yield
    span = 2
    while 2 * span < c:
        if wide:
            both = dot3(power, jnp.concatenate([tinv, power], axis=1))
            yield
            tinv = tinv + both[:, :c]
            power = both[:, c:]
        else:
            step = dot3(power, tinv)
            power = dot3(power, power)
            yield
            tinv = tinv + step
        span *= 2
    step = dot3(power, tinv)
    yield
    tinv = tinv + step

    sol = dot3(tinv, jnp.concatenate([vh * beta, kbeta * eg], axis=1))
    yield
    kd = kn * jnp.exp(g_last - g)
    return sol[:, :GDN_DV], sol[:, GDN_DV:], qn * eg, a_qk, (kd.T if wide else kd)


def _gdn_state_step(locals_, states, g_lasts, c):
    wide = c % LANES == 0
    ws = [_dot(jnp.concatenate([w, qeg], axis=0).astype(BF16), sh.astype(BF16))
          for (_, w, qeg, _, _), sh in zip(locals_, states)]
    outs, new_states = [], []
    for (u, _, _, a_qk, kd), wsh, sh, g_last in zip(locals_, ws, states, g_lasts):
        db = (u - wsh[:c]).astype(BF16)
        decayed = sh * jnp.exp(g_last)
        if wide:
            ak = _dot(jnp.concatenate([a_qk, kd], axis=0).astype(BF16), db)
            outs.append(wsh[c:] + ak[:c])
            new_states.append(decayed + ak[c:])
        else:
            outs.append(wsh[c:] + _dot(a_qk.astype(BF16), db))
            new_states.append(decayed + _dot_tn(kd.astype(BF16), db))
    return outs, new_states


def _gdn_kernel(cur_ref, hist_ref, sm_ref, z_ref, *refs, c, nb, cpb, zero_init):
    if zero_init:
        cw_ref, nw_ref, o_ref, sout_ref, xs_ref, s_ref = refs
    else:
        s0_ref, cw_ref, nw_ref, o_ref, sout_ref, xs_ref, s_ref = refs
    i = pl.program_id(1)
    rows = c * cpb
    wide = c % LANES == 0

    @pl.when(i == 0)
    def _():
        if zero_init:
            s_ref[...] = jnp.zeros_like(s_ref)
        else:
            s_ref[...] = s0_ref[...]

    hist = hist_ref[...]
    if zero_init:
        hist = jnp.where(i > 0, hist, 0.0)
    xs_ref[:, 0:SUBLANES, :] = hist
    xs_ref[:, SUBLANES:SUBLANES + rows, :] = cur_ref[...]
    base = SUBLANES - (GDN_CONV - 1)

    def conv_silu(b, r0, col):
        acc = xs_ref[b, base + r0:base + r0 + c, col:col + LANES] * cw_ref[0:1, col:col + LANES]
        for tap in range(1, GDN_CONV):
            acc = acc + (xs_ref[b, base + r0 + tap:base + r0 + tap + c, col:col + LANES]
                         * cw_ref[tap:tap + 1, col:col + LANES])
        return acc * _sigmoid(acc)

    incl = _tri(c)
    eye = incl & jnp.logical_not(_tri(c, strict=True))
    gens, g_lasts = [], []
    for b in range(nb):
        for ck in range(cpb):
            r0 = ck * c
            sm = sm_ref[b, r0:r0 + c, :]
            if wide:
                gs = _exact_left_c(incl.astype(BF16), sm)
                gst = gs.T
            else:
                gs = _exact_left(incl.astype(BF16), sm)
            for h in range(GDN_HEADS):
                g = gs[:, G_LANE + h:G_LANE + h + 1]
                if wide:
                    g_row = gst[G_LANE + h:G_LANE + h + 1, :]
                else:
                    g_row = _exact_left(jnp.ones((c, c), BF16), jnp.where(eye, g, 0.0))
                g_last = gs[c - 1:c, G_LANE + h:G_LANE + h + 1]
                g_lasts.append(g_last)
                gens.append(_gdn_local(
                    conv_silu(b, r0, h * GDN_DK), conv_silu(b, r0, GDN_QK + h * GDN_DK),
                    conv_silu(b, r0, 2 * GDN_QK + h * GDN_DV), g, g_row, g_last,
                    sm[:, BETA_LANE + h:BETA_LANE + h + 1], c))
    locals_ = _interleave(gens)

    states = [s_ref[b, h] for b in range(nb) for h in range(GDN_HEADS)]
    for ck in range(cpb):
        r0 = ck * c
        pick = [(b * cpb + ck) * GDN_HEADS + h for b in range(nb) for h in range(GDN_HEADS)]
        outs, states = _gdn_state_step([locals_[u] for u in pick], states, [g_lasts[u] for u in pick], c)
        for idx, o in enumerate(outs):
            b, h = divmod(idx, GDN_HEADS)
            on = o * lax.rsqrt(jnp.mean(o * o, axis=-1, keepdims=True) + EPS) * nw_ref[...]
            zh = z_ref[b, r0:r0 + c, h * GDN_DV:(h + 1) * GDN_DV]
            o_ref[b, r0:r0 + c, h * GDN_DV:(h + 1) * GDN_DV] = (on * (zh * _sigmoid(zh))).astype(BF16)
    for idx, st in enumerate(states):
        b, h = divmod(idx, GDN_HEADS)
        s_ref[b, h] = st

    @pl.when(i == pl.num_programs(1) - 1)
    def _():
        sout_ref[...] = s_ref[...]


def _gdn(gqkv, hist, sm, gz, s0, conv_w, norm_w, *, c, nb, cpb):
    b, t, _ = gqkv.shape
    rows = c * cpb
    assert t % rows == 0 and b % nb == 0 and c % SUBLANES == 0
    zero_init = hist is None
    blk = lambda w: pl.BlockSpec((nb, rows, w), lambda bi, i: (bi, i, 0))
    const = lambda a: pl.BlockSpec(a.shape, lambda bi, i: (0,) * a.ndim)
    state_spec = pl.BlockSpec((nb, GDN_HEADS, GDN_DK, GDN_DV), lambda bi, i: (bi, 0, 0, 0))
    if zero_init:
        assert nb == 1
        hist_arr = gqkv
        hist_spec = pl.BlockSpec((1, SUBLANES, GDN_CONV_CH),
                                 lambda bi, i: (bi, jnp.maximum(i * (rows // SUBLANES) - 1, 0), 0))
        extra, extra_specs = [], []
    else:
        assert t == rows
        hist_arr = hist
        hist_spec = pl.BlockSpec((nb, SUBLANES, GDN_CONV_CH), lambda bi, i: (bi, 0, 0))
        extra, extra_specs = [s0], [state_spec]
    return pl.pallas_call(
        functools.partial(_gdn_kernel, c=c, nb=nb, cpb=cpb, zero_init=zero_init),
        grid=(b // nb, t // rows),
        in_specs=[blk(GDN_CONV_CH), hist_spec, blk(LANES), blk(GDN_WIDTH)] + extra_specs
                 + [const(conv_w), const(norm_w)],
        out_specs=[blk(GDN_WIDTH), state_spec],
        out_shape=[jax.ShapeDtypeStruct((b, t, GDN_WIDTH), BF16),
                   jax.ShapeDtypeStruct((b, GDN_HEADS, GDN_DK, GDN_DV), F32)],
        scratch_shapes=[pltpu.VMEM((nb, SUBLANES + rows, GDN_CONV_CH), F32),
                        pltpu.VMEM((nb, GDN_HEADS, GDN_DK, GDN_DV), F32)],
        compiler_params=pltpu.CompilerParams(dimension_semantics=("arbitrary", "arbitrary"),
                                             vmem_limit_bytes=VMEM_LIMIT),
        name="gdn_prompt" if zero_init else "gdn_sample",
    )(gqkv, hist_arr, sm, gz, *extra, conv_w, norm_w)


FFN_COLS = 256


def _ffn_kernel(of_ref, og_ref, x_ref, wo_ref, nfw_ref, wup_ref, cw_ref, cb_ref, wdn_ref, nlw_ref, *refs,
                nseq, rows, tiles_per_seq, has_state):
    if has_state:
        st_ref, y_ref, nst_ref, wg_ref, wv_ref, act_ref = refs
    else:
        y_ref, nst_ref, wg_ref, wv_ref, act_ref, carry_ref = refs
    i = pl.program_id(0)
    hist0 = SUBLANES - (FFN_CONV - 1)

    h = (x_ref[...] + _dot(of_ref[...], wo_ref[0:FOX_WIDTH, :])
         + _dot(og_ref[...], wo_ref[FOX_WIDTH:FOX_WIDTH + GDN_WIDTH, :]))
    hn = (h * lax.rsqrt(jnp.mean(h * h, axis=-1, keepdims=True) + EPS) * nfw_ref[...]).astype(BF16)

    if not has_state:
        @pl.when(i % tiles_per_seq == 0)
        def _():
            carry_ref[...] = jnp.zeros_like(carry_ref)

    def conv_cols(buf_ref, col):
        up = _dot(hn, wup_ref[:, col:col + FFN_COLS])
        buf_ref[:, SUBLANES:SUBLANES + rows, :] = up.reshape(nseq, rows, FFN_COLS)
        if has_state:
            buf_ref[:, hist0:SUBLANES, :] = st_ref[:, :, col:col + FFN_COLS]
        else:
            buf_ref[:, hist0:SUBLANES, :] = carry_ref[:, hist0:SUBLANES, col:col + FFN_COLS]
        last = buf_ref[:, SUBLANES + rows - (FFN_CONV - 1):SUBLANES + rows, :]
        nst_ref[:, :, col:col + FFN_COLS] = last
        if not has_state:
            carry_ref[:, hist0:SUBLANES, col:col + FFN_COLS] = last
        uc = cb_ref[:, col:col + FFN_COLS][None]
        for tap in range(FFN_CONV):
            uc = uc + buf_ref[:, hist0 + tap:hist0 + tap + rows, :] * cw_ref[tap:tap + 1, col:col + FFN_COLS][None]
        return uc.reshape(nseq * rows, FFN_COLS)

    for ci in range(D_FF // FFN_COLS):
        col = ci * FFN_COLS
        gate = conv_cols(wg_ref, col)
        val = conv_cols(wv_ref, D_FF + col)
        act_ref[:, col:col + FFN_COLS] = (gate * _sigmoid(gate) * val).astype(BF16)

    h2 = h + _dot(act_ref[...], wdn_ref[...])
    y_ref[...] = h2 * lax.rsqrt(jnp.mean(h2 * h2, axis=-1, keepdims=True) + EPS) * nlw_ref[...]


def _ffn(ofox, ogdn, x2d, w_o, norm_ffn_w, w_up, conv_w, conv_b, w_down, norm_final_w, state, *,
         nseq, rows, seq_len):
    n = x2d.shape[0]
    tm = nseq * rows
    assert n % tm == 0 and seq_len % rows == 0 and rows % SUBLANES == 0 and D_FF % FFN_COLS == 0
    has_state = state is not None
    assert (rows == seq_len) if has_state else (nseq == 1)
    tps = seq_len // rows
    nbatch = n // seq_len
    row = lambda w: pl.BlockSpec((tm, w), lambda i: (i, 0))
    const = lambda a: pl.BlockSpec(a.shape, lambda i: (0,) * a.ndim, pipeline_mode=pl.Buffered(1))
    st_spec = pl.BlockSpec((nseq, FFN_CONV - 1, 2 * D_FF), lambda i: (i // tps, 0, 0))
    ins = [ofox, ogdn, x2d, w_o, norm_ffn_w, w_up, conv_w, conv_b, w_down, norm_final_w]
    in_specs = [row(FOX_WIDTH), row(GDN_WIDTH), row(D_MODEL)] + [const(a) for a in ins[3:]]
    scratch = [pltpu.VMEM((nseq, SUBLANES + rows, FFN_COLS), F32),
               pltpu.VMEM((nseq, SUBLANES + rows, FFN_COLS), F32),
               pltpu.VMEM((tm, D_FF), BF16)]
    if has_state:
        ins.append(state)
        in_specs.append(st_spec)
    else:
        scratch.append(pltpu.VMEM((nseq, SUBLANES, 2 * D_FF), F32))
    return pl.pallas_call(
        functools.partial(_ffn_kernel, nseq=nseq, rows=rows, tiles_per_seq=tps, has_state=has_state),
        grid=(n // tm,),
        in_specs=in_specs,
        out_specs=[row(D_MODEL), st_spec],
        out_shape=[jax.ShapeDtypeStruct((n, D_MODEL), F32),
                   jax.ShapeDtypeStruct((nbatch, FFN_CONV - 1, 2 * D_FF), F32)],
        scratch_shapes=scratch,
        compiler_params=pltpu.CompilerParams(dimension_semantics=("arbitrary",),
                                             vmem_limit_bytes=VMEM_LIMIT),
        name="ffn_sample" if has_state else "ffn_prompt",
    )(*ins)


ROW_TILE = 512
SAMPLE_KV_BLOCK = 2048
GDN_CHUNK = 128
GDN_CHUNKS_PER_STEP = 2
GDN_SAMPLE_SEQS_PER_STEP = 4


def _prep_weights(w_in, b_fox_f, gdn_a_log, gdn_dt_bias):
    c0 = 3 * FOX_WIDTH
    c1 = c0 + FOX_HEADS
    c2 = c1 + GDN_CONV_CH
    c3 = c2 + 2 * GDN_HEADS
    small = jnp.concatenate([w_in[:, c0:c1], w_in[:, c2:c3],
                             jnp.zeros((D_MODEL, LANES - FOX_HEADS - 2 * GDN_HEADS), w_in.dtype)], axis=1)
    w_cat = jnp.concatenate([w_in[:, :c0], w_in[:, c1:c2], w_in[:, c3:], small], axis=1).astype(BF16)
    pad = jnp.zeros((LANES - FOX_HEADS - GDN_HEADS,), F32)
    bias = jnp.concatenate([b_fox_f.astype(F32), gdn_dt_bias.astype(F32), pad])
    alog = jnp.concatenate([jnp.zeros((FOX_HEADS,), F32), gdn_a_log.astype(F32), pad])
    prm = jnp.zeros((SUBLANES, LANES), F32).at[0].set(bias).at[1].set(alog)
    return w_cat, prm


def _layer(xp, xs, fox_k, fox_v, fox_logf, st_gdn, st_gconv, st_fconv,
           w_in, b_fox_f, gdn_conv_w, gdn_a_log, gdn_dt_bias, gdn_norm_w, w_o,
           norm_mix_w, norm_ffn_w, w_up, ffn_conv_w, ffn_conv_b, w_down, norm_out_w):
    bp, sp, _ = xp.shape
    bs, ts, _ = xs.shape
    w_cat, prm = _prep_weights(w_in, b_fox_f, gdn_a_log, gdn_dt_bias)
    nmw = norm_mix_w.reshape(1, D_MODEL).astype(F32)
    nfw = norm_ffn_w.reshape(1, D_MODEL).astype(F32)
    now = norm_out_w.reshape(1, D_MODEL).astype(F32)
    gnw = gdn_norm_w.reshape(1, GDN_DV).astype(F32)
    w_o_b, w_up_b, w_dn_b = w_o.astype(BF16), w_up.astype(BF16), w_down.astype(BF16)
    cb = ffn_conv_b.reshape(1, 2 * D_FF).astype(F32)

    xp2 = xp.reshape(bp * sp, D_MODEL)
    q, k, v, kb, qa, ka, va, gqkv, gz, sm = _inproj(xp2, nmw, w_cat, prm, tm=ROW_TILE, seq_len=sp, prompt=True)
    r3 = lambda a: a.reshape(bp, sp, a.shape[-1])
    o_fox = _fox_prompt(r3(q), r3(qa), r3(kb), r3(ka), r3(va), t=ROW_TILE)
    gqkv3 = r3(gqkv)
    o_gdn, p_state = _gdn(gqkv3, None, r3(sm), r3(gz), None, gdn_conv_w, gnw,
                          c=GDN_CHUNK, nb=1, cpb=GDN_CHUNKS_PER_STEP)
    yp, p_fconv = _ffn(o_fox.reshape(bp * sp, FOX_WIDTH), o_gdn.reshape(bp * sp, GDN_WIDTH), xp2,
                       w_o_b, nfw, w_up_b, ffn_conv_w, cb, w_dn_b, now, None,
                       nseq=1, rows=ROW_TILE, seq_len=sp)
    p_out = (k.reshape(bp, sp, FOX_HEADS, FOX_DIM), v.reshape(bp, sp, FOX_HEADS, FOX_DIM),
             r3(sm)[:, :, LOGF_LANE:LOGF_LANE + FOX_HEADS], p_state,
             gqkv3[:, sp - (GDN_CONV - 1):], p_fconv)

    xs2 = xs.reshape(bs * ts, D_MODEL)
    q, k, v, gqkv, gz, sm = _inproj(xs2, nmw, w_cat, prm, tm=bs * ts, seq_len=ts, prompt=False)
    s3 = lambda a: a.reshape(bs, ts, a.shape[-1])
    past = fox_k.shape[1]
    lft = jnp.swapaxes(fox_logf.astype(F32), 1, 2)
    o_fox = _fox_sample(s3(q), s3(k), s3(v), s3(sm), lft,
                        fox_k.reshape(bs, past, FOX_WIDTH), fox_v.reshape(bs, past, FOX_WIDTH),
                        kb=min(SAMPLE_KV_BLOCK, past))
    gqkv3 = s3(gqkv)
    hist = jnp.concatenate([jnp.zeros((bs, SUBLANES - (GDN_CONV - 1), GDN_CONV_CH), F32),
                            st_gconv.astype(F32)], axis=1)
    o_gdn, s_state = _gdn(gqkv3, hist, s3(sm), s3(gz), st_gdn.astype(F32), gdn_conv_w, gnw,
                          c=ts, nb=GDN_SAMPLE_SEQS_PER_STEP, cpb=1)
    ys, s_fconv = _ffn(o_fox.reshape(bs * ts, FOX_WIDTH), o_gdn.reshape(bs * ts, GDN_WIDTH), xs2,
                       w_o_b, nfw, w_up_b, ffn_conv_w, cb, w_dn_b, now, st_fconv.astype(F32),
                       nseq=bs, rows=ts, seq_len=ts)
    full = jnp.concatenate([st_gconv.astype(F32), gqkv3], axis=1)
    s_out = (k.reshape(bs, ts, FOX_HEADS, FOX_DIM), v.reshape(bs, ts, FOX_HEADS, FOX_DIM),
             s3(sm)[:, :, LOGF_LANE:LOGF_LANE + FOX_HEADS], s_state,
             full[:, ts:], s_fconv)
    return yp.reshape(bp, sp, D_MODEL), ys.reshape(bs, ts, D_MODEL), p_out, s_out


def kernel(x_prompt, x_sample, cache_fox_k, cache_fox_v, cache_fox_logf, state_gdn, state_gdn_conv,
           state_ffn_conv, w_in, b_fox_f, gdn_conv_w, gdn_a_log, gdn_dt_bias, gdn_norm_w, w_o, norm_mix_w,
           norm_ffn_w, w_up, ffn_conv_w, ffn_conv_b, w_down, norm_final_w):
    depth = w_in.shape[0]
    assert depth == 1, "the final RMSNorm is fused into the single layer's FFN kernel"
    yp, ys, p_out, s_out = _layer(
        x_prompt, x_sample, cache_fox_k[0], cache_fox_v[0], cache_fox_logf[0], state_gdn[0],
        state_gdn_conv[0], state_ffn_conv[0], w_in[0], b_fox_f[0], gdn_conv_w[0], gdn_a_log[0],
        gdn_dt_bias[0], gdn_norm_w[0], w_o[0], norm_mix_w[0], norm_ffn_w[0], w_up[0], ffn_conv_w[0],
        ffn_conv_b[0], w_down[0], norm_final_w)
    return (yp, ys) + tuple(a[None] for a in p_out) + tuple(a[None] for a in s_out)
```

```python
import functools
import math

import numpy as np
import jax
import jax.numpy as jnp
from jax import lax
from jax.experimental import pallas as pl
from jax.experimental.pallas import tpu as pltpu

F32 = jnp.float32
BF16 = jnp.bfloat16

D_MODEL = 1024
FOX_DIM = 64
FOX_HEADS = 8
FOX_WIDTH = FOX_HEADS * FOX_DIM
GDN_DK = 128
GDN_DV = 128
GDN_HEADS = 4
GDN_QK = GDN_HEADS * GDN_DK
GDN_WIDTH = GDN_HEADS * GDN_DV
GDN_CONV_CH = 2 * GDN_QK + GDN_WIDTH
GDN_CONV = 4
D_FF = 2816
FFN_CONV = 3
EPS = 1e-6

LANES = 128
SUBLANES = 8
VMEM_LIMIT = 56 * 1024 * 1024
MASKED = -1e30
LOG2E = math.log2(math.e)

LOGF_LANE = 0
G_LANE = FOX_HEADS
BETA_LANE = FOX_HEADS + GDN_HEADS

COL_FOX = 0
COL_GQKV = 3 * FOX_WIDTH
COL_GZ = COL_GQKV + GDN_CONV_CH
COL_SMALL = COL_GZ + GDN_WIDTH
N_COLS = COL_SMALL + LANES


def _dot(a, b):
    return jnp.dot(a, b, preferred_element_type=F32)


def _dot_nt(a, b):
    return lax.dot_general(a, b, (((1,), (1,)), ((), ())), preferred_element_type=F32)


def _dot_tn(a, b):
    return lax.dot_general(a, b, (((0,), (0,)), ((), ())), preferred_element_type=F32)


def _split3(x):
    hi = x.astype(BF16)
    r = x - hi.astype(F32)
    lo = r.astype(BF16)
    lo2 = (r - lo.astype(F32)).astype(BF16)
    return hi, lo, lo2


def _exact_left(mat, x):
    hi, lo, lo2 = _split3(x)
    return _dot(mat, hi) + _dot(mat, lo) + _dot(mat, lo2)


def _exact_right(x, mat):
    hi, lo, lo2 = _split3(x)
    return _dot(hi, mat) + _dot(lo, mat) + _dot(lo2, mat)


def _dot3(a, b):
    ah = a.astype(BF16)
    al = (a - ah.astype(F32)).astype(BF16)
    bh = b.astype(BF16)
    bl = (b - bh.astype(F32)).astype(BF16)
    return _dot(ah, bh) + (_dot(ah, bl) + _dot(al, bh))


def _exact_left_c(mat, x):
    return _dot(jnp.concatenate([mat, mat, mat], axis=1), jnp.concatenate(_split3(x), axis=0))


def _dot3c(a, b):
    ah = a.astype(BF16)
    al = (a - ah.astype(F32)).astype(BF16)
    bh = b.astype(BF16)
    bl = (b - bh.astype(F32)).astype(BF16)
    return _dot(jnp.concatenate([ah, ah, al], axis=1), jnp.concatenate([bh, bl, bh], axis=0))


def _sigmoid(x):
    return 1.0 / (1.0 + jnp.exp(-x))


def _tri(n, strict=False):
    r = lax.broadcasted_iota(jnp.int32, (n, n), 0)
    c = lax.broadcasted_iota(jnp.int32, (n, n), 1)
    return (r > c) if strict else (r >= c)


def _inproj_kernel(x_ref, nw_ref, w_ref, prm_ref, *refs, tiles_per_seq, prompt):
    if prompt:
        (aug_ref, augc_ref, q_ref, k_ref, v_ref, kb_ref, qa_ref, ka_ref, va_ref, g_ref, z_ref, sm_ref,
         carry_ref) = refs
    else:
        q_ref, k_ref, v_ref, g_ref, z_ref, sm_ref = refs
    x = x_ref[...]
    xn = x * lax.rsqrt(jnp.mean(x * x, axis=-1, keepdims=True) + EPS) * nw_ref[...]
    xb = xn.astype(BF16)

    q_scale = FOX_DIM ** -0.5 * (LOG2E if prompt else 1.0)
    q_ref[...] = (_dot(xb, w_ref[:, 0:FOX_WIDTH]) * q_scale).astype(BF16)
    kf = _dot(xb, w_ref[:, FOX_WIDTH:2 * FOX_WIDTH])
    vf = _dot(xb, w_ref[:, 2 * FOX_WIDTH:3 * FOX_WIDTH])
    if not prompt:
        k_ref[...] = kf
        v_ref[...] = vf
    else:
        k_ref[0] = kf.T.reshape(FOX_HEADS, FOX_DIM, kf.shape[0])
        v_ref[0] = vf.T.reshape(FOX_HEADS, FOX_DIM, vf.shape[0])
        kb_ref[...] = kf.astype(BF16)
        tm = x.shape[0]
        low = lax.broadcasted_iota(jnp.int32, (tm, LANES), 1) < FOX_DIM
        for p in range(FOX_HEADS // 2):
            pair = vf[:, p * LANES:(p + 1) * LANES]
            va_ref[:, 2 * p * LANES:(2 * p + 1) * LANES] = jnp.where(low, pair, 1.0).astype(BF16)
            va_ref[:, (2 * p + 1) * LANES:(2 * p + 2) * LANES] = jnp.where(
                low, pltpu.roll(pair, FOX_DIM, axis=1), 1.0).astype(BF16)
    g_ref[...] = _dot(xb, w_ref[:, COL_GQKV:COL_GZ])
    z_ref[...] = _dot(xb, w_ref[:, COL_GZ:COL_SMALL])

    z = _dot(xb, w_ref[:, COL_SMALL:N_COLS])
    lane = lax.broadcasted_iota(jnp.int32, z.shape, 1)
    zb = z + prm_ref[0:1, :]
    e = jnp.exp(-jnp.abs(zb))
    l1p = jnp.log(1.0 + e)
    logf = jnp.minimum(zb, 0.0) - l1p
    g = -jnp.exp(prm_ref[1:2, :]) * (jnp.maximum(zb, 0.0) + l1p)
    beta = _sigmoid(z)
    sm = jnp.where(lane < G_LANE, logf,
                   jnp.where(lane < BETA_LANE, g,
                             jnp.where(lane < BETA_LANE + GDN_HEADS, beta, 0.0)))
    sm_ref[...] = sm

    if prompt:
        i = pl.program_id(0)

        @pl.when(i % tiles_per_seq == 0)
        def _():
            carry_ref[...] = jnp.zeros_like(carry_ref)

        c = _exact_left(_tri(tm).astype(BF16), sm) + carry_ref[0:1, :]
        carry_ref[0:1, :] = c[tm - 1:tm, :]
        parts = jnp.concatenate(_split3(c * LOG2E), axis=1)
        aug = _dot(parts, aug_ref[...]) + augc_ref[...]
        qa_ref[...] = aug[:, :FOX_WIDTH].astype(BF16)
        ka_ref[...] = aug[:, FOX_WIDTH:].astype(BF16)


AUG_LANES = 6


def _aug_tables():
    place = np.zeros((3 * LANES, 2 * FOX_WIDTH), np.float32)
    ones = np.zeros((1, 2 * FOX_WIDTH), np.float32)
    for h in range(FOX_HEADS):
        q0 = (h // 2) * LANES + (h % 2) * AUG_LANES
        k0 = FOX_WIDTH + q0
        for part in range(3):
            place[part * LANES + LOGF_LANE + h, q0 + part] = 1.0
            place[part * LANES + LOGF_LANE + h, k0 + 3 + part] = -1.0
            ones[0, q0 + 3 + part] = 1.0
            ones[0, k0 + part] = 1.0
    return jnp.asarray(place, BF16), jnp.asarray(ones, F32)


def _inproj(x2d, norm_w, w_cat, prm, *, tm, seq_len, prompt):
    n = x2d.shape[0]
    assert n % tm == 0 and (not prompt or seq_len % tm == 0)
    tps = seq_len // tm if prompt else 1
    row = lambda w: pl.BlockSpec((tm, w), lambda i: (i, 0))
    const = lambda a: pl.BlockSpec(a.shape, lambda i: (0,) * a.ndim, pipeline_mode=pl.Buffered(1))
    ins = [x2d, norm_w, w_cat, prm]
    out_shape = [jax.ShapeDtypeStruct((n, FOX_WIDTH), BF16),
                 jax.ShapeDtypeStruct((n, FOX_WIDTH), F32),
                 jax.ShapeDtypeStruct((n, FOX_WIDTH), F32)]
    out_specs = [row(FOX_WIDTH), row(FOX_WIDTH), row(FOX_WIDTH)]
    scratch = []
    if prompt:
        kv_t = jax.ShapeDtypeStruct((n // seq_len, FOX_HEADS, FOX_DIM, seq_len), F32)
        out_shape[1:3] = [kv_t, kv_t]
        out_specs[1:3] = [pl.BlockSpec((1, FOX_HEADS, FOX_DIM, tm), lambda i: (i // tps, 0, 0, i % tps))] * 2
        ins += list(_aug_tables())
        out_shape += [jax.ShapeDtypeStruct((n, FOX_WIDTH), BF16)] * 3 + [jax.ShapeDtypeStruct((n, 2 * FOX_WIDTH), BF16)]
        out_specs += [row(FOX_WIDTH)] * 3 + [row(2 * FOX_WIDTH)]
        scratch = [pltpu.VMEM((SUBLANES, LANES), F32)]
    out_shape += [jax.ShapeDtypeStruct((n, GDN_CONV_CH), F32),
                  jax.ShapeDtypeStruct((n, GDN_WIDTH), F32),
                  jax.ShapeDtypeStruct((n, LANES), F32)]
    out_specs += [row(GDN_CONV_CH), row(GDN_WIDTH), row(LANES)]
    return pl.pallas_call(
        functools.partial(_inproj_kernel, tiles_per_seq=tps, prompt=prompt),
        grid=(n // tm,),
        in_specs=[row(D_MODEL)] + [const(a) for a in ins[1:]],
        out_specs=out_specs, out_shape=out_shape, scratch_shapes=scratch,
        compiler_params=pltpu.CompilerParams(dimension_semantics=("arbitrary",),
                                             vmem_limit_bytes=VMEM_LIMIT),
        name="inproj_prompt" if prompt else "inproj_sample",
    )(*ins)


def _fox_prompt_kernel(q_ref, qa_ref, k_ref, ka_ref, va_ref, o_ref, sa_ref, sb_ref, pa_ref, pb_ref, *, t):
    i = pl.program_id(2)
    q = q_ref[0]
    qa = qa_ref[0]
    lane = lax.broadcasted_iota(jnp.int32, (t, LANES), 1)
    low = lane < FOX_DIM
    zero = jnp.zeros_like(q)
    qcat = []
    for h in range(2):
        mine = (lane >= h * AUG_LANES) & (lane < (h + 1) * AUG_LANES)
        qcat.append(jnp.concatenate([jnp.where(low if h == 0 else jnp.logical_not(low), q, zero),
                                     jnp.where(mine, qa, zero)], axis=1))

    def logits(j, s_ref):
        start = pl.multiple_of(j * t, t)
        kcat = jnp.concatenate([k_ref[0, pl.ds(start, t), :], ka_ref[0, pl.ds(start, t), :]], axis=1)
        for h in range(2):
            s_ref[h] = _dot_nt(qcat[h], kcat)

    def softmax(s_ref, p_ref, carry, diag):
        out = []
        for h in range(2):
            m, _, acc = carry[h]
            s = s_ref[h]
            if diag:
                s = jnp.where(_tri(t), s, MASKED)
            m_new = jnp.maximum(m, jnp.max(s, axis=1, keepdims=True))
            p_ref[h] = jnp.exp2(s - m_new).astype(BF16)
            out.append((m_new, jnp.exp2(m - m_new), acc))
        return tuple(out)

    def values(j, p_ref, scale, carry):
        start = pl.multiple_of(jnp.maximum(j, 0) * t, t)
        return tuple((m, alpha, sc * acc + _dot(p_ref[h], va_ref[0, pl.ds(start, t), h * LANES:(h + 1) * LANES]))
                     for h, ((m, alpha, acc), sc) in enumerate(zip(carry, scale)))

    def stage(j, s_cur, p_cur, p_prev, carry, diag=False):
        pending = [c[1] for c in carry]
        return values(j - 1, p_prev, pending, softmax(s_cur, p_cur, carry, diag))

    pb_ref[...] = jnp.zeros_like(pb_ref)
    init = tuple((jnp.full((t, 1), MASKED, F32), jnp.ones((t, 1), F32), jnp.zeros((t, LANES), F32))
                 for _ in range(2))
    logits(0, sa_ref)

    def two_blocks(jj, carry):
        j = 2 * jj
        logits(j + 1, sb_ref)
        carry = stage(j, sa_ref, pa_ref, pb_ref, carry)
        logits(j + 2, sa_ref)
        return stage(j + 1, sb_ref, pb_ref, pa_ref, carry)

    carry = lax.fori_loop(0, i // 2, two_blocks, init)

    def even_tail(carry):
        carry = stage(i, sa_ref, pa_ref, pb_ref, carry, diag=True)
        return values(i, pa_ref, [c[1] for c in carry], carry)

    def odd_tail(carry):
        logits(i, sb_ref)
        carry = stage(i - 1, sa_ref, pa_ref, pb_ref, carry)
        carry = stage(i, sb_ref, pb_ref, pa_ref, carry, diag=True)
        return values(i, pb_ref, [c[1] for c in carry], carry)

    (_, _, a0), (_, _, a1) = lax.cond(i % 2 == 1, odd_tail, even_tail, carry)
    o_ref[0] = jnp.where(low, a0 / pltpu.roll(a0, FOX_DIM, axis=1),
                         pltpu.roll(a1, FOX_DIM, axis=1) / a1).astype(BF16)


def _fox_prompt(q, qa, kb, ka, va, *, t):
    b, s, _ = q.shape
    nblk = s // t
    assert s % t == 0
    npair = FOX_HEADS // 2
    qblk = pl.BlockSpec((1, t, LANES), lambda bi, p, i: (bi, i, p))
    kblk = pl.BlockSpec((1, s, LANES), lambda bi, p, i: (bi, 0, p))
    return pl.pallas_call(
        functools.partial(_fox_prompt_kernel, t=t),
        grid=(b, npair, nblk),
        in_specs=[qblk, qblk, kblk, kblk, pl.BlockSpec((1, s, 2 * LANES), lambda bi, p, i: (bi, 0, p))],
        out_specs=qblk,
        out_shape=jax.ShapeDtypeStruct((b, s, FOX_WIDTH), BF16),
        scratch_shapes=[pltpu.VMEM((2, t, t), F32), pltpu.VMEM((2, t, t), F32),
                        pltpu.VMEM((2, t, t), BF16), pltpu.VMEM((2, t, t), BF16)],
        compiler_params=pltpu.CompilerParams(
            dimension_semantics=("arbitrary", "arbitrary", "arbitrary"), vmem_limit_bytes=VMEM_LIMIT),
        name="fox_prompt",
    )(q, qa, kb, ka, va)


def _fox_sample_kernel(q_ref, kn_ref, vn_ref, sm_ref, lft_ref, kt_ref, vt_ref, o_ref,
                       rk_ref, m_ref, l_ref, acc_ref, *, kb, t):
    j = pl.program_id(1)
    nkv = pl.num_programs(1)

    def heads(x):
        return [x[:, h * FOX_DIM:(h + 1) * FOX_DIM] for h in range(FOX_HEADS)]

    qh = [x.astype(BF16) for x in heads(q_ref[0].astype(F32))]

    lf_new = sm_ref[0]
    cn = _exact_left(_tri(t).astype(BF16), lf_new)

    @pl.when(j == 0)
    def _():
        p = lft_ref.shape[2]
        nb = p // LANES
        x = jnp.concatenate([lft_ref[0, :, i * LANES:(i + 1) * LANES] for i in range(nb)], axis=0)
        r = lax.broadcasted_iota(jnp.int32, (LANES, LANES), 0)
        cc = lax.broadcasted_iota(jnp.int32, (LANES, LANES), 1)
        y = _exact_right(x, (r > cc).astype(BF16))
        tot = jnp.sum(x, axis=1, keepdims=True)
        off = jnp.zeros((FOX_HEADS, 1), F32)
        for i in range(nb - 1, -1, -1):
            rk_ref[:, i * LANES:(i + 1) * LANES] = y[i * FOX_HEADS:(i + 1) * FOX_HEADS, :] + off
            off = off + tot[i * FOX_HEADS:(i + 1) * FOX_HEADS, :]
        m_ref[...] = jnp.full_like(m_ref, MASKED)
        l_ref[...] = jnp.zeros_like(l_ref)
        acc_ref[...] = jnp.zeros_like(acc_ref)

    def update(s, pv):
        m = m_ref[:, 0:1]
        m_new = jnp.maximum(m, jnp.max(s, axis=1, keepdims=True))
        alpha = jnp.exp(m - m_new)
        p = jnp.exp(s - m_new)
        l_ref[...] = jnp.broadcast_to(alpha * l_ref[:, 0:1] + jnp.sum(p, axis=1, keepdims=True), l_ref.shape)
        acc_ref[...] = alpha * acc_ref[...] + pv(p.astype(BF16))
        m_ref[...] = jnp.broadcast_to(m_new, m_ref.shape)

    start = pl.multiple_of(j * kb, kb)
    rk = rk_ref[:, pl.ds(start, kb)]
    s = jnp.concatenate([_dot(qh[h], kt_ref[0, h].astype(BF16)) + (cn[:, h:h + 1] + rk[h:h + 1, :])
                         for h in range(FOX_HEADS)], axis=0)
    update(s, lambda p: jnp.concatenate(
        [_dot_nt(p[h * t:(h + 1) * t], vt_ref[0, h].astype(BF16)) for h in range(FOX_HEADS)], axis=0))

    @pl.when(j == nkv - 1)
    def _():
        r = lax.broadcasted_iota(jnp.int32, (t, t), 0)
        cc = lax.broadcasted_iota(jnp.int32, (t, t), 1)
        after = (r > cc).astype(F32)
        tri = _tri(t).astype(BF16)
        knh = heads(kn_ref[0])
        vnh = [x.astype(BF16) for x in heads(vn_ref[0])]
        s_new = jnp.concatenate(
            [jnp.where(_tri(t), _dot_nt(qh[h], knh[h].astype(BF16))
                       + _exact_left(tri, lf_new[:, h:h + 1] * after), MASKED) for h in range(FOX_HEADS)], axis=0)
        update(s_new, lambda p: jnp.concatenate(
            [_dot(p[h * t:(h + 1) * t], vnh[h]) for h in range(FOX_HEADS)], axis=0))
        o_full = acc_ref[...] / l_ref[:, 0:1]
        o_ref[0] = jnp.concatenate([o_full[h * t:(h + 1) * t] for h in range(FOX_HEADS)], axis=1).astype(BF16)


def _fox_sample(q, kn, vn, sm, lft, kt, vt, *, kb):
    b, t, _ = q.shape
    p = kt.shape[3]
    assert p % kb == 0 and kb % LANES == 0
    rows = FOX_HEADS * t
    per_b = lambda shape: pl.BlockSpec((1,) + shape, lambda bi, j: (bi, 0, 0))
    cache = pl.BlockSpec((1, FOX_HEADS, FOX_DIM, kb), lambda bi, j: (bi, 0, 0, j))
    return pl.pallas_call(
        functools.partial(_fox_sample_kernel, kb=kb, t=t),
        grid=(b, p // kb),
        in_specs=[per_b((t, FOX_WIDTH)), per_b((t, FOX_WIDTH)), per_b((t, FOX_WIDTH)), per_b((t, LANES)),
                  per_b((FOX_HEADS, p)), cache, cache],
        out_specs=per_b((t, FOX_WIDTH)),
        out_shape=jax.ShapeDtypeStruct((b, t, FOX_WIDTH), BF16),
        scratch_shapes=[pltpu.VMEM((FOX_HEADS, p), F32),
                        pltpu.VMEM((rows, LANES), F32), pltpu.VMEM((rows, LANES), F32),
                        pltpu.VMEM((rows, FOX_DIM), F32)],
        compiler_params=pltpu.CompilerParams(dimension_semantics=("arbitrary", "arbitrary"),
                                             vmem_limit_bytes=VMEM_LIMIT),
        name="fox_sample",
    )(q, kn, vn, sm, lft, kt, vt)


def _interleave(gens):
    results = [None] * len(gens)
    alive = list(range(len(gens)))
    while alive:
        for idx in list(alive):
            try:
                next(gens[idx])
            except StopIteration as stop:
                results[idx] = stop.value
                alive.remove(idx)
    return results


def _gdn_local(qh, kh, vh, g, g_row, g_last, beta, c):
    wide = c % LANES == 0
    dot3 = _dot3c if wide else _dot3
    incl = _tri(c)
    strict = _tri(c, strict=True)
    qn = qh * (lax.rsqrt(jnp.sum(qh * qh, axis=-1, keepdims=True) + EPS) * (GDN_DK ** -0.5))
    kn = kh * lax.rsqrt(jnp.sum(kh * kh, axis=-1, keepdims=True) + EPS)
    eg = jnp.exp(g)
    decay = jnp.exp(jnp.where(incl, g - g_row, MASKED))
    kbeta = kn * beta
    kq = _dot_nt(jnp.concatenate([kbeta, qn], axis=0).astype(BF16), kn.astype(BF16))
    yield
    lmat = jnp.where(strict, kq[:c] * decay, 0.0)
    a_qk = kq[c:] * decay

    tinv = (incl & jnp.logical_not(strict)).astype(F32) - lmat
    power = dot3(lmat, lmat)
    yield
    span = 2
    while 2 * span < c:
        if wide:
            both = dot3(power, jnp.concatenate([tinv, power], axis=1))
            yield
            tinv = tinv + both[:, :c]
            power = both[:, c:]
        else:
            step = dot3(power, tinv)
            power = dot3(power, power)
            yield
            tinv = tinv + step
        span *= 2
    step = dot3(power, tinv)
    yield
    tinv = tinv + step

    sol = dot3(tinv, jnp.concatenate([vh * beta, kbeta * eg], axis=1))
    yield
    kd = kn * jnp.exp(g_last - g)
    return sol[:, :GDN_DV], sol[:, GDN_DV:], qn * eg, a_qk, (kd.T if wide else kd)


def _gdn_state_step(locals_, states, g_lasts, c):
    wide = c % LANES == 0
    ws = [_dot(jnp.concatenate([w, qeg], axis=0).astype(BF16), sh.astype(BF16))
          for (_, w, qeg, _, _), sh in zip(locals_, states)]
    outs, new_states = [], []
    for (u, _, _, a_qk, kd), wsh, sh, g_last in zip(locals_, ws, states, g_lasts):
        db = (u - wsh[:c]).astype(BF16)
        decayed = sh * jnp.exp(g_last)
        if wide:
            ak = _dot(jnp.concatenate([a_qk, kd], axis=0).astype(BF16), db)
            outs.append(wsh[c:] + ak[:c])
            new_states.append(decayed + ak[c:])
        else:
            outs.append(wsh[c:] + _dot(a_qk.astype(BF16), db))
            new_states.append(decayed + _dot_tn(kd.astype(BF16), db))
    return outs, new_states


def _gdn_kernel(cur_ref, hist_ref, sm_ref, z_ref, *refs, c, nb, cpb, zero_init):
    if zero_init:
        cw_ref, nw_ref, o_ref, sout_ref, xs_ref, s_ref = refs
    else:
        s0_ref, cw_ref, nw_ref, o_ref, sout_ref, xs_ref, s_ref = refs
    i = pl.program_id(1)
    rows = c * cpb
    wide = c % LANES == 0

    @pl.when(i == 0)
    def _():
        if zero_init:
            s_ref[...] = jnp.zeros_like(s_ref)
        else:
            s_ref[...] = s0_ref[...]

    hist = hist_ref[...]
    if zero_init:
        hist = jnp.where(i > 0, hist, 0.0)
    xs_ref[:, 0:SUBLANES, :] = hist
    xs_ref[:, SUBLANES:SUBLANES + rows, :] = cur_ref[...]
    base = SUBLANES - (GDN_CONV - 1)

    def conv_silu(b, r0, col):
        acc = xs_ref[b, base + r0:base + r0 + c, col:col + LANES] * cw_ref[0:1, col:col + LANES]
        for tap in range(1, GDN_CONV):
            acc = acc + (xs_ref[b, base + r0 + tap:base + r0 + tap + c, col:col + LANES]
                         * cw_ref[tap:tap + 1, col:col + LANES])
        return acc * _sigmoid(acc)

    incl = _tri(c)
    eye = incl & jnp.logical_not(_tri(c, strict=True))
    gens, g_lasts = [], []
    for b in range(nb):
        for ck in range(cpb):
            r0 = ck * c
            sm = sm_ref[b, r0:r0 + c, :]
            if wide:
                gs = _exact_left_c(incl.astype(BF16), sm)
                gst = gs.T
            else:
                gs = _exact_left(incl.astype(BF16), sm)
            for h in range(GDN_HEADS):
                g = gs[:, G_LANE + h:G_LANE + h + 1]
                if wide:
                    g_row = gst[G_LANE + h:G_LANE + h + 1, :]
                else:
                    g_row = _exact_left(jnp.ones((c, c), BF16), jnp.where(eye, g, 0.0))
                g_last = gs[c - 1:c, G_LANE + h:G_LANE + h + 1]
                g_lasts.append(g_last)
                gens.append(_gdn_local(
                    conv_silu(b, r0, h * GDN_DK), conv_silu(b, r0, GDN_QK + h * GDN_DK),
                    conv_silu(b, r0, 2 * GDN_QK + h * GDN_DV), g, g_row, g_last,
                    sm[:, BETA_LANE + h:BETA_LANE + h + 1], c))
    locals_ = _interleave(gens)

    states = [s_ref[b, h] for b in range(nb) for h in range(GDN_HEADS)]
    for ck in range(cpb):
        r0 = ck * c
        pick = [(b * cpb + ck) * GDN_HEADS + h for b in range(nb) for h in range(GDN_HEADS)]
        outs, states = _gdn_state_step([locals_[u] for u in pick], states, [g_lasts[u] for u in pick], c)
        for idx, o in enumerate(outs):
            b, h = divmod(idx, GDN_HEADS)
            on = o * lax.rsqrt(jnp.mean(o * o, axis=-1, keepdims=True) + EPS) * nw_ref[...]
            zh = z_ref[b, r0:r0 + c, h * GDN_DV:(h + 1) * GDN_DV]
            o_ref[b, r0:r0 + c, h * GDN_DV:(h + 1) * GDN_DV] = (on * (zh * _sigmoid(zh))).astype(BF16)
    for idx, st in enumerate(states):
        b, h = divmod(idx, GDN_HEADS)
        s_ref[b, h] = st

    @pl.when(i == pl.num_programs(1) - 1)
    def _():
        sout_ref[...] = s_ref[...]


def _gdn(gqkv, hist, sm, gz, s0, conv_w, norm_w, *, c, nb, cpb):
    b, t, _ = gqkv.shape
    rows = c * cpb
    assert t % rows == 0 and b % nb == 0 and c % SUBLANES == 0
    zero_init = hist is None
    blk = lambda w: pl.BlockSpec((nb, rows, w), lambda bi, i: (bi, i, 0))
    const = lambda a: pl.BlockSpec(a.shape, lambda bi, i: (0,) * a.ndim)
    state_spec = pl.BlockSpec((nb, GDN_HEADS, GDN_DK, GDN_DV), lambda bi, i: (bi, 0, 0, 0))
    if zero_init:
        assert nb == 1
        hist_arr = gqkv
        hist_spec = pl.BlockSpec((1, SUBLANES, GDN_CONV_CH),
                                 lambda bi, i: (bi, jnp.maximum(i * (rows // SUBLANES) - 1, 0), 0))
        extra, extra_specs = [], []
    else:
        assert t == rows
        hist_arr = hist
        hist_spec = pl.BlockSpec((nb, SUBLANES, GDN_CONV_CH), lambda bi, i: (bi, 0, 0))
        extra, extra_specs = [s0], [state_spec]
    return pl.pallas_call(
        functools.partial(_gdn_kernel, c=c, nb=nb, cpb=cpb, zero_init=zero_init),
        grid=(b // nb, t // rows),
        in_specs=[blk(GDN_CONV_CH), hist_spec, blk(LANES), blk(GDN_WIDTH)] + extra_specs
                 + [const(conv_w), const(norm_w)],
        out_specs=[blk(GDN_WIDTH), state_spec],
        out_shape=[jax.ShapeDtypeStruct((b, t, GDN_WIDTH), BF16),
                   jax.ShapeDtypeStruct((b, GDN_HEADS, GDN_DK, GDN_DV), F32)],
        scratch_shapes=[pltpu.VMEM((nb, SUBLANES + rows, GDN_CONV_CH), F32),
                        pltpu.VMEM((nb, GDN_HEADS, GDN_DK, GDN_DV), F32)],
        compiler_params=pltpu.CompilerParams(dimension_semantics=("arbitrary", "arbitrary"),
                                             vmem_limit_bytes=VMEM_LIMIT),
        name="gdn_prompt" if zero_init else "gdn_sample",
    )(gqkv, hist_arr, sm, gz, *extra, conv_w, norm_w)


FFN_COLS = 256


def _ffn_kernel(of_ref, og_ref, x_ref, wo_ref, nfw_ref, wup_ref, cw_ref, cb_ref, wdn_ref, nlw_ref, *refs,
                nseq, rows, tiles_per_seq, has_state):
    if has_state:
        st_ref, y_ref, nst_ref, wg_ref, wv_ref, act_ref = refs
    else:
        y_ref, nst_ref, wg_ref, wv_ref, act_ref, carry_ref = refs
    i = pl.program_id(0)
    hist0 = SUBLANES - (FFN_CONV - 1)

    h = (x_ref[...] + _dot(of_ref[...], wo_ref[0:FOX_WIDTH, :])
         + _dot(og_ref[...], wo_ref[FOX_WIDTH:FOX_WIDTH + GDN_WIDTH, :]))
    hn = (h * lax.rsqrt(jnp.mean(h * h, axis=-1, keepdims=True) + EPS) * nfw_ref[...]).astype(BF16)

    if not has_state:
        @pl.when(i % tiles_per_seq == 0)
        def _():
            carry_ref[...] = jnp.zeros_like(carry_ref)

    def conv_cols(buf_ref, col):
        up = _dot(hn, wup_ref[:, col:col + FFN_COLS])
        buf_ref[:, SUBLANES:SUBLANES + rows, :] = up.reshape(nseq, rows, FFN_COLS)
        if has_state:
            buf_ref[:, hist0:SUBLANES, :] = st_ref[:, :, col:col + FFN_COLS]
        else:
            buf_ref[:, hist0:SUBLANES, :] = carry_ref[:, hist0:SUBLANES, col:col + FFN_COLS]
        last = buf_ref[:, SUBLANES + rows - (FFN_CONV - 1):SUBLANES + rows, :]
        nst_ref[:, :, col:col + FFN_COLS] = last
        if not has_state:
            carry_ref[:, hist0:SUBLANES, col:col + FFN_COLS] = last
        uc = cb_ref[:, col:col + FFN_COLS][None]
        for tap in range(FFN_CONV):
            uc = uc + buf_ref[:, hist0 + tap:hist0 + tap + rows, :] * cw_ref[tap:tap + 1, col:col + FFN_COLS][None]
        return uc.reshape(nseq * rows, FFN_COLS)

    for ci in range(D_FF // FFN_COLS):
        col = ci * FFN_COLS
        gate = conv_cols(wg_ref, col)
        val = conv_cols(wv_ref, D_FF + col)
        act_ref[:, col:col + FFN_COLS] = (gate * _sigmoid(gate) * val).astype(BF16)

    h2 = h + _dot(act_ref[...], wdn_ref[...])
    y_ref[...] = h2 * lax.rsqrt(jnp.mean(h2 * h2, axis=-1, keepdims=True) + EPS) * nlw_ref[...]


def _ffn(ofox, ogdn, x2d, w_o, norm_ffn_w, w_up, conv_w, conv_b, w_down, norm_final_w, state, *,
         nseq, rows, seq_len):
    n = x2d.shape[0]
    tm = nseq * rows
    assert n % tm == 0 and seq_len % rows == 0 and rows % SUBLANES == 0 and D_FF % FFN_COLS == 0
    has_state = state is not None
    assert (rows == seq_len) if has_state else (nseq == 1)
    tps = seq_len // rows
    nbatch = n // seq_len
    row = lambda w: pl.BlockSpec((tm, w), lambda i: (i, 0))
    const = lambda a: pl.BlockSpec(a.shape, lambda i: (0,) * a.ndim, pipeline_mode=pl.Buffered(1))
    st_spec = pl.BlockSpec((nseq, FFN_CONV - 1, 2 * D_FF), lambda i: (i // tps, 0, 0))
    ins = [ofox, ogdn, x2d, w_o, norm_ffn_w, w_up, conv_w, conv_b, w_down, norm_final_w]
    in_specs = [row(FOX_WIDTH), row(GDN_WIDTH), row(D_MODEL)] + [const(a) for a in ins[3:]]
    scratch = [pltpu.VMEM((nseq, SUBLANES + rows, FFN_COLS), F32),
               pltpu.VMEM((nseq, SUBLANES + rows, FFN_COLS), F32),
               pltpu.VMEM((tm, D_FF), BF16)]
    if has_state:
        ins.append(state)
        in_specs.append(st_spec)
    else:
        scratch.append(pltpu.VMEM((nseq, SUBLANES, 2 * D_FF), F32))
    return pl.pallas_call(
        functools.partial(_ffn_kernel, nseq=nseq, rows=rows, tiles_per_seq=tps, has_state=has_state),
        grid=(n // tm,),
        in_specs=in_specs,
        out_specs=[row(D_MODEL), st_spec],
        out_shape=[jax.ShapeDtypeStruct((n, D_MODEL), F32),
                   jax.ShapeDtypeStruct((nbatch, FFN_CONV - 1, 2 * D_FF), F32)],
        scratch_shapes=scratch,
        compiler_params=pltpu.CompilerParams(dimension_semantics=("arbitrary",),
                                             vmem_limit_bytes=VMEM_LIMIT),
        name="ffn_sample" if has_state else "ffn_prompt",
    )(*ins)


ROW_TILE = 512
SAMPLE_KV_BLOCK = 2048
GDN_CHUNK = 128
GDN_CHUNKS_PER_STEP = 2
GDN_SAMPLE_SEQS_PER_STEP = 4


def _prep_weights(w_in, b_fox_f, gdn_a_log, gdn_dt_bias):
    c0 = 3 * FOX_WIDTH
    c1 = c0 + FOX_HEADS
    c2 = c1 + GDN_CONV_CH
    c3 = c2 + 2 * GDN_HEADS
    small = jnp.concatenate([w_in[:, c0:c1], w_in[:, c2:c3],
                             jnp.zeros((D_MODEL, LANES - FOX_HEADS - 2 * GDN_HEADS), w_in.dtype)], axis=1)
    w_cat = jnp.concatenate([w_in[:, :c0], w_in[:, c1:c2], w_in[:, c3:], small], axis=1).astype(BF16)
    pad = jnp.zeros((LANES - FOX_HEADS - GDN_HEADS,), F32)
    bias = jnp.concatenate([b_fox_f.astype(F32), gdn_dt_bias.astype(F32), pad])
    alog = jnp.concatenate([jnp.zeros((FOX_HEADS,), F32), gdn_a_log.astype(F32), pad])
    prm = jnp.zeros((SUBLANES, LANES), F32).at[0].set(bias).at[1].set(alog)
    return w_cat, prm


def _layer(xp, xs, fox_k, fox_v, fox_logf, st_gdn, st_gconv, st_fconv,
           w_in, b_fox_f, gdn_conv_w, gdn_a_log, gdn_dt_bias, gdn_norm_w, w_o,
           norm_mix_w, norm_ffn_w, w_up, ffn_conv_w, ffn_conv_b, w_down, norm_out_w):
    bp, sp, _ = xp.shape
    bs, ts, _ = xs.shape
    w_cat, prm = _prep_weights(w_in, b_fox_f, gdn_a_log, gdn_dt_bias)
    nmw = norm_mix_w.reshape(1, D_MODEL).astype(F32)
    nfw = norm_ffn_w.reshape(1, D_MODEL).astype(F32)
    now = norm_out_w.reshape(1, D_MODEL).astype(F32)
    gnw = gdn_norm_w.reshape(1, GDN_DV).astype(F32)
    w_o_b, w_up_b, w_dn_b = w_o.astype(BF16), w_up.astype(BF16), w_down.astype(BF16)
    cb = ffn_conv_b.reshape(1, 2 * D_FF).astype(F32)

    xp2 = xp.reshape(bp * sp, D_MODEL)
    q, k, v, kb, qa, ka, va, gqkv, gz, sm = _inproj(xp2, nmw, w_cat, prm, tm=ROW_TILE, seq_len=sp, prompt=True)
    r3 = lambda a: a.reshape(bp, sp, a.shape[-1])
    o_fox = _fox_prompt(r3(q), r3(qa), r3(kb), r3(ka), r3(va), t=ROW_TILE)
    gqkv3 = r3(gqkv)
    o_gdn, p_state = _gdn(gqkv3, None, r3(sm), r3(gz), None, gdn_conv_w, gnw,
                          c=GDN_CHUNK, nb=1, cpb=GDN_CHUNKS_PER_STEP)
    yp, p_fconv = _ffn(o_fox.reshape(bp * sp, FOX_WIDTH), o_gdn.reshape(bp * sp, GDN_WIDTH), xp2,
                       w_o_b, nfw, w_up_b, ffn_conv_w, cb, w_dn_b, now, None,
                       nseq=1, rows=ROW_TILE, seq_len=sp)
    p_out = (jnp.transpose(k, (0, 3, 1, 2)), jnp.transpose(v, (0, 3, 1, 2)),
             r3(sm)[:, :, LOGF_LANE:LOGF_LANE + FOX_HEADS], p_state,
             gqkv3[:, sp - (GDN_CONV - 1):], p_fconv)

    xs2 = xs.reshape(bs * ts, D_MODEL)
    q, k, v, gqkv, gz, sm = _inproj(xs2, nmw, w_cat, prm, tm=bs * ts, seq_len=ts, prompt=False)
    s3 = lambda a: a.reshape(bs, ts, a.shape[-1])
    past = fox_k.shape[1]
    lft = jnp.swapaxes(fox_logf.astype(F32), 1, 2)
    cache_t = lambda a: jnp.transpose(a.astype(F32), (0, 2, 3, 1))
    o_fox = _fox_sample(s3(q), s3(k), s3(v), s3(sm), lft, cache_t(fox_k), cache_t(fox_v),
                        kb=min(SAMPLE_KV_BLOCK, past))
    gqkv3 = s3(gqkv)
    hist = jnp.concatenate([jnp.zeros((bs, SUBLANES - (GDN_CONV - 1), GDN_CONV_CH), F32),
                            st_gconv.astype(F32)], axis=1)
    o_gdn, s_state = _gdn(gqkv3, hist, s3(sm), s3(gz), st_gdn.astype(F32), gdn_conv_w, gnw,
                          c=ts, nb=GDN_SAMPLE_SEQS_PER_STEP, cpb=1)
    ys, s_fconv = _ffn(o_fox.reshape(bs * ts, FOX_WIDTH), o_gdn.reshape(bs * ts, GDN_WIDTH), xs2,
                       w_o_b, nfw, w_up_b, ffn_conv_w, cb, w_dn_b, now, st_fconv.astype(F32),
                       nseq=bs, rows=ts, seq_len=ts)
    full = jnp.concatenate([st_gconv.astype(F32), gqkv3], axis=1)
    s_out = (k.reshape(bs, ts, FOX_HEADS, FOX_DIM), v.reshape(bs, ts, FOX_HEADS, FOX_DIM),
             s3(sm)[:, :, LOGF_LANE:LOGF_LANE + FOX_HEADS], s_state,
             full[:, ts:], s_fconv)
    return yp.reshape(bp, sp, D_MODEL), ys.reshape(bs, ts, D_MODEL), p_out, s_out


def kernel(x_prompt, x_sample, cache_fox_k, cache_fox_v, cache_fox_logf, state_gdn, state_gdn_conv,
           state_ffn_conv, w_in, b_fox_f, gdn_conv_w, gdn_a_log, gdn_dt_bias, gdn_norm_w, w_o, norm_mix_w,
           norm_ffn_w, w_up, ffn_conv_w, ffn_conv_b, w_down, norm_final_w):
    depth = w_in.shape[0]
    assert depth == 1, "the final RMSNorm is fused into the single layer's FFN kernel"
    yp, ys, p_out, s_out = _layer(
        x_prompt, x_sample, cache_fox_k[0], cache_fox_v[0], cache_fox_logf[0], state_gdn[0],
        state_gdn_conv[0], state_ffn_conv[0], w_in[0], b_fox_f[0], gdn_conv_w[0], gdn_a_log[0],
        gdn_dt_bias[0], gdn_norm_w[0], w_o[0], norm_mix_w[0], norm_ffn_w[0], w_up[0], ffn_conv_w[0],
        ffn_conv_b[0], w_down[0], norm_final_w)
    return (yp, ys) + tuple(a[None] for a in p_out) + tuple(a[None] for a in s_out)
```

```python
import functools
import math

import numpy as np
import jax
import jax.numpy as jnp
from jax import lax
from jax.experimental import pallas as pl
from jax.experimental.pallas import tpu as pltpu

F32 = jnp.float32
BF16 = jnp.bfloat16

D_MODEL = 1024
FOX_DIM = 64
FOX_HEADS = 8
FOX_WIDTH = FOX_HEADS * FOX_DIM
GDN_DK = 128
GDN_DV = 128
GDN_HEADS = 4
GDN_QK = GDN_HEADS * GDN_DK
GDN_WIDTH = GDN_HEADS * GDN_DV
GDN_CONV_CH = 2 * GDN_QK + GDN_WIDTH
GDN_CONV = 4
D_FF = 2816
FFN_CONV = 3
EPS = 1e-6

LANES = 128
SUBLANES = 8
VMEM_LIMIT = 56 * 1024 * 1024
MASKED = -1e30
LOG2E = math.log2(math.e)

LOGF_LANE = 0
G_LANE = FOX_HEADS
BETA_LANE = FOX_HEADS + GDN_HEADS

COL_FOX = 0
COL_GQKV = 3 * FOX_WIDTH
COL_GZ = COL_GQKV + GDN_CONV_CH
COL_SMALL = COL_GZ + GDN_WIDTH
N_COLS = COL_SMALL + LANES


def _dot(a, b):
    return jnp.dot(a, b, preferred_element_type=F32)


def _dot_nt(a, b):
    return lax.dot_general(a, b, (((1,), (1,)), ((), ())), preferred_element_type=F32)


def _dot_tn(a, b):
    return lax.dot_general(a, b, (((0,), (0,)), ((), ())), preferred_element_type=F32)


def _split3(x):
    hi = x.astype(BF16)
    r = x - hi.astype(F32)
    lo = r.astype(BF16)
    lo2 = (r - lo.astype(F32)).astype(BF16)
    return hi, lo, lo2


def _exact_left(mat, x):
    hi, lo, lo2 = _split3(x)
    return _dot(mat, hi) + _dot(mat, lo) + _dot(mat, lo2)


def _exact_right(x, mat):
    hi, lo, lo2 = _split3(x)
    return _dot(hi, mat) + _dot(lo, mat) + _dot(lo2, mat)


def _dot3(a, b):
    ah = a.astype(BF16)
    al = (a - ah.astype(F32)).astype(BF16)
    bh = b.astype(BF16)
    bl = (b - bh.astype(F32)).astype(BF16)
    return _dot(ah, bh) + (_dot(ah, bl) + _dot(al, bh))


def _exact_left_c(mat, x):
    return _dot(jnp.concatenate([mat, mat, mat], axis=1), jnp.concatenate(_split3(x), axis=0))


def _dot3c(a, b):
    ah = a.astype(BF16)
    al = (a - ah.astype(F32)).astype(BF16)
    bh = b.astype(BF16)
    bl = (b - bh.astype(F32)).astype(BF16)
    return _dot(jnp.concatenate([ah, ah, al], axis=1), jnp.concatenate([bh, bl, bh], axis=0))


def _sigmoid(x):
    return 1.0 / (1.0 + jnp.exp(-x))


def _tri(n, strict=False):
    r = lax.broadcasted_iota(jnp.int32, (n, n), 0)
    c = lax.broadcasted_iota(jnp.int32, (n, n), 1)
    return (r > c) if strict else (r >= c)


def _inproj_kernel(x_ref, nw_ref, w_ref, prm_ref, *refs, tiles_per_seq, prompt):
    if prompt:
        (aug_ref, augc_ref, q_ref, k_ref, v_ref, kb_ref, qa_ref, ka_ref, va_ref, g_ref, z_ref, sm_ref,
         carry_ref) = refs
    else:
        q_ref, k_ref, v_ref, g_ref, z_ref, sm_ref = refs
    x = x_ref[...]
    xn = x * lax.rsqrt(jnp.mean(x * x, axis=-1, keepdims=True) + EPS) * nw_ref[...]
    xb = xn.astype(BF16)

    q_scale = FOX_DIM ** -0.5 * (LOG2E if prompt else 1.0)
    q_ref[...] = (_dot(xb, w_ref[:, 0:FOX_WIDTH]) * q_scale).astype(BF16)
    kf = _dot(xb, w_ref[:, FOX_WIDTH:2 * FOX_WIDTH])
    vf = _dot(xb, w_ref[:, 2 * FOX_WIDTH:3 * FOX_WIDTH])
    if not prompt:
        k_ref[...] = kf
        v_ref[...] = vf
    else:
        tm = x.shape[0]
        vt = vf.T.reshape(FOX_HEADS, FOX_DIM, tm)
        k_ref[0] = kf.T.reshape(FOX_HEADS, FOX_DIM, tm)
        v_ref[0] = vt
        kb_ref[...] = kf.astype(BF16)
        va_ref[0, :, 0:FOX_DIM, :] = vt.astype(BF16)
        va_ref[0, :, FOX_DIM:, :] = jnp.ones((FOX_HEADS, ONES_ROWS, tm), BF16)
    g_ref[...] = _dot(xb, w_ref[:, COL_GQKV:COL_GZ])
    z_ref[...] = _dot(xb, w_ref[:, COL_GZ:COL_SMALL])

    z = _dot(xb, w_ref[:, COL_SMALL:N_COLS])
    lane = lax.broadcasted_iota(jnp.int32, z.shape, 1)
    zb = z + prm_ref[0:1, :]
    e = jnp.exp(-jnp.abs(zb))
    l1p = jnp.log(1.0 + e)
    logf = jnp.minimum(zb, 0.0) - l1p
    g = -jnp.exp(prm_ref[1:2, :]) * (jnp.maximum(zb, 0.0) + l1p)
    beta = _sigmoid(z)
    sm = jnp.where(lane < G_LANE, logf,
                   jnp.where(lane < BETA_LANE, g,
                             jnp.where(lane < BETA_LANE + GDN_HEADS, beta, 0.0)))
    sm_ref[...] = sm

    if prompt:
        i = pl.program_id(0)

        @pl.when(i % tiles_per_seq == 0)
        def _():
            carry_ref[...] = jnp.zeros_like(carry_ref)

        c = _exact_left(_tri(tm).astype(BF16), sm) + carry_ref[0:1, :]
        carry_ref[0:1, :] = c[tm - 1:tm, :]
        parts = jnp.concatenate(_split3(c * LOG2E), axis=1)
        aug = _dot(parts, aug_ref[...]) + augc_ref[...]
        qa_ref[...] = aug[:, :FOX_WIDTH].astype(BF16)
        ka_ref[...] = aug[:, FOX_WIDTH:].astype(BF16)


AUG_LANES = 6
ONES_ROWS = 16


def _aug_tables():
    place = np.zeros((3 * LANES, 2 * FOX_WIDTH), np.float32)
    ones = np.zeros((1, 2 * FOX_WIDTH), np.float32)
    for h in range(FOX_HEADS):
        q0 = (h // 2) * LANES + (h % 2) * AUG_LANES
        k0 = FOX_WIDTH + q0
        for part in range(3):
            place[part * LANES + LOGF_LANE + h, q0 + part] = 1.0
            place[part * LANES + LOGF_LANE + h, k0 + 3 + part] = -1.0
            ones[0, q0 + 3 + part] = 1.0
            ones[0, k0 + part] = 1.0
    return jnp.asarray(place, BF16), jnp.asarray(ones, F32)


def _inproj(x2d, norm_w, w_cat, prm, *, tm, seq_len, prompt):
    n = x2d.shape[0]
    assert n % tm == 0 and (not prompt or seq_len % tm == 0)
    tps = seq_len // tm if prompt else 1
    row = lambda w: pl.BlockSpec((tm, w), lambda i: (i, 0))
    const = lambda a: pl.BlockSpec(a.shape, lambda i: (0,) * a.ndim, pipeline_mode=pl.Buffered(1))
    ins = [x2d, norm_w, w_cat, prm]
    out_shape = [jax.ShapeDtypeStruct((n, FOX_WIDTH), BF16),
                 jax.ShapeDtypeStruct((n, FOX_WIDTH), F32),
                 jax.ShapeDtypeStruct((n, FOX_WIDTH), F32)]
    out_specs = [row(FOX_WIDTH), row(FOX_WIDTH), row(FOX_WIDTH)]
    scratch = []
    if prompt:
        kv_t = jax.ShapeDtypeStruct((n // seq_len, FOX_HEADS, FOX_DIM, seq_len), F32)
        out_shape[1:3] = [kv_t, kv_t]
        out_specs[1:3] = [pl.BlockSpec((1, FOX_HEADS, FOX_DIM, tm), lambda i: (i // tps, 0, 0, i % tps))] * 2
        ins += list(_aug_tables())
        out_shape += [jax.ShapeDtypeStruct((n, FOX_WIDTH), BF16)] * 3 + [
            jax.ShapeDtypeStruct((n // seq_len, FOX_HEADS, FOX_DIM + ONES_ROWS, seq_len), BF16)]
        out_specs += [row(FOX_WIDTH)] * 3 + [
            pl.BlockSpec((1, FOX_HEADS, FOX_DIM + ONES_ROWS, tm), lambda i: (i // tps, 0, 0, i % tps))]
        scratch = [pltpu.VMEM((SUBLANES, LANES), F32)]
    out_shape += [jax.ShapeDtypeStruct((n, GDN_CONV_CH), F32),
                  jax.ShapeDtypeStruct((n, GDN_WIDTH), F32),
                  jax.ShapeDtypeStruct((n, LANES), F32)]
    out_specs += [row(GDN_CONV_CH), row(GDN_WIDTH), row(LANES)]
    return pl.pallas_call(
        functools.partial(_inproj_kernel, tiles_per_seq=tps, prompt=prompt),
        grid=(n // tm,),
        in_specs=[row(D_MODEL)] + [const(a) for a in ins[1:]],
        out_specs=out_specs, out_shape=out_shape, scratch_shapes=scratch,
        compiler_params=pltpu.CompilerParams(dimension_semantics=("arbitrary",),
                                             vmem_limit_bytes=VMEM_LIMIT),
        name="inproj_prompt" if prompt else "inproj_sample",
    )(*ins)


def _fox_prompt_kernel(q_ref, qa_ref, k_ref, ka_ref, va_ref, o_ref, sa_ref, sb_ref, *, t):
    i = pl.program_id(2)
    q = q_ref[0]
    qa = qa_ref[0]
    lane = lax.broadcasted_iota(jnp.int32, (t, LANES), 1)
    low = lane < FOX_DIM
    zero = jnp.zeros_like(q)
    qcat = []
    for h in range(2):
        mine = (lane >= h * AUG_LANES) & (lane < (h + 1) * AUG_LANES)
        qcat.append(jnp.concatenate([jnp.where(low if h == 0 else jnp.logical_not(low), q, zero),
                                     jnp.where(mine, qa, zero)], axis=1))

    def logits(j, s_ref):
        start = pl.multiple_of(j * t, t)
        kcat = jnp.concatenate([k_ref[0, pl.ds(start, t), :], ka_ref[0, pl.ds(start, t), :]], axis=1)
        for h in range(2):
            s_ref[h] = _dot_nt(kcat, qcat[h])

    def consume(j, s_ref, carry, diag):
        start = pl.multiple_of(j * t, t)
        out = []
        for h in range(2):
            m, acc = carry[h]
            s = s_ref[h]
            if diag:
                key = lax.broadcasted_iota(jnp.int32, (t, t), 0)
                query = lax.broadcasted_iota(jnp.int32, (t, t), 1)
                s = jnp.where(key <= query, s, MASKED)
            m_new = jnp.maximum(m, jnp.max(s, axis=0, keepdims=True))
            p = jnp.exp2(s - m_new).astype(BF16)
            acc = jnp.exp2(m - m_new) * acc + _dot(va_ref[0, h, :, pl.ds(start, t)], p)
            out.append((m_new, acc))
        return tuple(out)

    init = tuple((jnp.full((1, t), MASKED, F32), jnp.zeros((FOX_DIM + ONES_ROWS, t), F32)) for _ in range(2))
    logits(0, sa_ref)

    def two_blocks(jj, carry):
        j = 2 * jj
        logits(j + 1, sb_ref)
        carry = consume(j, sa_ref, carry, False)
        logits(j + 2, sa_ref)
        return consume(j + 1, sb_ref, carry, False)

    carry = lax.fori_loop(0, i // 2, two_blocks, init)

    def odd_tail(carry):
        logits(i, sb_ref)
        return consume(i, sb_ref, consume(i - 1, sa_ref, carry, False), True)

    (_, a0), (_, a1) = lax.cond(i % 2 == 1, odd_tail, lambda c: consume(i, sa_ref, c, True), carry)
    o_t = jnp.concatenate([a[:FOX_DIM] / a[FOX_DIM:FOX_DIM + 1] for a in (a0, a1)], axis=0)
    o_ref[0] = o_t.T.astype(BF16)


def _fox_prompt(q, qa, kb, ka, va, *, t):
    b, s, _ = q.shape
    nblk = s // t
    assert s % t == 0
    npair = FOX_HEADS // 2
    qblk = pl.BlockSpec((1, t, LANES), lambda bi, p, i: (bi, i, p))
    kblk = pl.BlockSpec((1, s, LANES), lambda bi, p, i: (bi, 0, p))
    return pl.pallas_call(
        functools.partial(_fox_prompt_kernel, t=t),
        grid=(b, npair, nblk),
        in_specs=[qblk, qblk, kblk, kblk,
                  pl.BlockSpec((1, 2, FOX_DIM + ONES_ROWS, s), lambda bi, p, i: (bi, p, 0, 0))],
        out_specs=qblk,
        out_shape=jax.ShapeDtypeStruct((b, s, FOX_WIDTH), BF16),
        scratch_shapes=[pltpu.VMEM((2, t, t), F32), pltpu.VMEM((2, t, t), F32)],
        compiler_params=pltpu.CompilerParams(
            dimension_semantics=("arbitrary", "arbitrary", "arbitrary"), vmem_limit_bytes=VMEM_LIMIT),
        name="fox_prompt",
    )(q, qa, kb, ka, va)


def _fox_sample_kernel(q_ref, kn_ref, vn_ref, sm_ref, lft_ref, kt_ref, vt_ref, o_ref,
                       rk_ref, m_ref, l_ref, acc_ref, *, kb, t):
    j = pl.program_id(1)
    nkv = pl.num_programs(1)

    def heads(x):
        return [x[:, h * FOX_DIM:(h + 1) * FOX_DIM] for h in range(FOX_HEADS)]

    qh = [x.astype(BF16) for x in heads(q_ref[0].astype(F32))]

    lf_new = sm_ref[0]
    cn = _exact_left(_tri(t).astype(BF16), lf_new)

    @pl.when(j == 0)
    def _():
        p = lft_ref.shape[2]
        nb = p // LANES
        x = jnp.concatenate([lft_ref[0, :, i * LANES:(i + 1) * LANES] for i in range(nb)], axis=0)
        r = lax.broadcasted_iota(jnp.int32, (LANES, LANES), 0)
        cc = lax.broadcasted_iota(jnp.int32, (LANES, LANES), 1)
        y = _exact_right(x, (r > cc).astype(BF16))
        tot = jnp.sum(x, axis=1, keepdims=True)
        off = jnp.zeros((FOX_HEADS, 1), F32)
        for i in range(nb - 1, -1, -1):
            rk_ref[:, i * LANES:(i + 1) * LANES] = y[i * FOX_HEADS:(i + 1) * FOX_HEADS, :] + off
            off = off + tot[i * FOX_HEADS:(i + 1) * FOX_HEADS, :]
        m_ref[...] = jnp.full_like(m_ref, MASKED)
        l_ref[...] = jnp.zeros_like(l_ref)
        acc_ref[...] = jnp.zeros_like(acc_ref)

    def update(s, pv):
        m = m_ref[:, 0:1]
        m_new = jnp.maximum(m, jnp.max(s, axis=1, keepdims=True))
        alpha = jnp.exp(m - m_new)
        p = jnp.exp(s - m_new)
        l_ref[...] = jnp.broadcast_to(alpha * l_ref[:, 0:1] + jnp.sum(p, axis=1, keepdims=True), l_ref.shape)
        acc_ref[...] = alpha * acc_ref[...] + pv(p.astype(BF16))
        m_ref[...] = jnp.broadcast_to(m_new, m_ref.shape)

    start = pl.multiple_of(j * kb, kb)
    rk = rk_ref[:, pl.ds(start, kb)]
    s = jnp.concatenate([_dot(qh[h], kt_ref[0, h].astype(BF16)) + (cn[:, h:h + 1] + rk[h:h + 1, :])
                         for h in range(FOX_HEADS)], axis=0)
    update(s, lambda p: jnp.concatenate(
        [_dot_nt(p[h * t:(h + 1) * t], vt_ref[0, h].astype(BF16)) for h in range(FOX_HEADS)], axis=0))

    @pl.when(j == nkv - 1)
    def _():
        r = lax.broadcasted_iota(jnp.int32, (t, t), 0)
        cc = lax.broadcasted_iota(jnp.int32, (t, t), 1)
        after = (r > cc).astype(F32)
        tri = _tri(t).astype(BF16)
        knh = heads(kn_ref[0])
        vnh = [x.astype(BF16) for x in heads(vn_ref[0])]
        s_new = jnp.concatenate(
            [jnp.where(_tri(t), _dot_nt(qh[h], knh[h].astype(BF16))
                       + _exact_left(tri, lf_new[:, h:h + 1] * after), MASKED) for h in range(FOX_HEADS)], axis=0)
        update(s_new, lambda p: jnp.concatenate(
            [_dot(p[h * t:(h + 1) * t], vnh[h]) for h in range(FOX_HEADS)], axis=0))
        o_full = acc_ref[...] / l_ref[:, 0:1]
        o_ref[0] = jnp.concatenate([o_full[h * t:(h + 1) * t] for h in range(FOX_HEADS)], axis=1).astype(BF16)


def _fox_sample(q, kn, vn, sm, lft, kt, vt, *, kb):
    b, t, _ = q.shape
    p = kt.shape[3]
    assert p % kb == 0 and kb % LANES == 0
    rows = FOX_HEADS * t
    per_b = lambda shape: pl.BlockSpec((1,) + shape, lambda bi, j: (bi, 0, 0))
    cache = pl.BlockSpec((1, FOX_HEADS, FOX_DIM, kb), lambda bi, j: (bi, 0, 0, j))
    return pl.pallas_call(
        functools.partial(_fox_sample_kernel, kb=kb, t=t),
        grid=(b, p // kb),
        in_specs=[per_b((t, FOX_WIDTH)), per_b((t, FOX_WIDTH)), per_b((t, FOX_WIDTH)), per_b((t, LANES)),
                  per_b((FOX_HEADS, p)), cache, cache],
        out_specs=per_b((t, FOX_WIDTH)),
        out_shape=jax.ShapeDtypeStruct((b, t, FOX_WIDTH), BF16),
        scratch_shapes=[pltpu.VMEM((FOX_HEADS, p), F32),
                        pltpu.VMEM((rows, LANES), F32), pltpu.VMEM((rows, LANES), F32),
                        pltpu.VMEM((rows, FOX_DIM), F32)],
        compiler_params=pltpu.CompilerParams(dimension_semantics=("arbitrary", "arbitrary"),
                                             vmem_limit_bytes=VMEM_LIMIT),
        name="fox_sample",
    )(q, kn, vn, sm, lft, kt, vt)


def _interleave(gens):
    results = [None] * len(gens)
    alive = list(range(len(gens)))
    while alive:
        for idx in list(alive):
            try:
                next(gens[idx])
            except StopIteration as stop:
                results[idx] = stop.value
                alive.remove(idx)
    return results


def _gdn_local(qh, kh, vh, g, g_row, g_last, beta, c):
    wide = c % LANES == 0
    dot3 = _dot3c if wide else _dot3
    incl = _tri(c)
    strict = _tri(c, strict=True)
    qn = qh * (lax.rsqrt(jnp.sum(qh * qh, axis=-1, keepdims=True) + EPS) * (GDN_DK ** -0.5))
    kn = kh * lax.rsqrt(jnp.sum(kh * kh, axis=-1, keepdims=True) + EPS)
    eg = jnp.exp(g)
    decay = jnp.exp(jnp.where(incl, g - g_row, MASKED))
    kbeta = kn * beta
    kq = _dot_nt(jnp.concatenate([kbeta, qn], axis=0).astype(BF16), kn.astype(BF16))
    yield
    lmat = jnp.where(strict, kq[:c] * decay, 0.0)
    a_qk = kq[c:] * decay

    tinv = (incl & jnp.logical_not(strict)).astype(F32) - lmat
    power = dot3(lmat, lmat)
    yield
    span = 2
    while 2 * span < c:
        if wide:
            both = dot3(power, jnp.concatenate([tinv, power], axis=1))
            yield
            tinv = tinv + both[:, :c]
            power = both[:, c:]
        else:
            step = dot3(power, tinv)
            power = dot3(power, power)
            yield
            tinv = tinv + step
        span *= 2
    step = dot3(power, tinv)
    yield
    tinv = tinv + step

    sol = dot3(tinv, jnp.concatenate([vh * beta, kbeta * eg], axis=1))
    yield
    kd = kn * jnp.exp(g_last - g)
    return sol[:, :GDN_DV], sol[:, GDN_DV:], qn * eg, a_qk, (kd.T if wide else kd)


def _gdn_state_step(locals_, states, g_lasts, c):
    wide = c % LANES == 0
    ws = [_dot(jnp.concatenate([w, qeg], axis=0).astype(BF16), sh.astype(BF16))
          for (_, w, qeg, _, _), sh in zip(locals_, states)]
    outs, new_states = [], []
    for (u, _, _, a_qk, kd), wsh, sh, g_last in zip(locals_, ws, states, g_lasts):
        db = (u - wsh[:c]).astype(BF16)
        decayed = sh * jnp.exp(g_last)
        if wide:
            ak = _dot(jnp.concatenate([a_qk, kd], axis=0).astype(BF16), db)
            outs.append(wsh[c:] + ak[:c])
            new_states.append(decayed + ak[c:])
        else:
            outs.append(wsh[c:] + _dot(a_qk.astype(BF16), db))
            new_states.append(decayed + _dot_tn(kd.astype(BF16), db))
    return outs, new_states


def _gdn_kernel(cur_ref, hist_ref, sm_ref, z_ref, *refs, c, nb, cpb, zero_init):
    if zero_init:
        cw_ref, nw_ref, o_ref, sout_ref, xs_ref, s_ref = refs
    else:
        s0_ref, cw_ref, nw_ref, o_ref, sout_ref, xs_ref, s_ref = refs
    i = pl.program_id(1)
    rows = c * cpb
    wide = c % LANES == 0

    @pl.when(i == 0)
    def _():
        if zero_init:
            s_ref[...] = jnp.zeros_like(s_ref)
        else:
            s_ref[...] = s0_ref[...]

    hist = hist_ref[...]
    if zero_init:
        hist = jnp.where(i > 0, hist, 0.0)
    xs_ref[:, 0:SUBLANES, :] = hist
    xs_ref[:, SUBLANES:SUBLANES + rows, :] = cur_ref[...]
    base = SUBLANES - (GDN_CONV - 1)

    def conv_silu(b, r0, col):
        acc = xs_ref[b, base + r0:base + r0 + c, col:col + LANES] * cw_ref[0:1, col:col + LANES]
        for tap in range(1, GDN_CONV):
            acc = acc + (xs_ref[b, base + r0 + tap:base + r0 + tap + c, col:col + LANES]
                         * cw_ref[tap:tap + 1, col:col + LANES])
        return acc * _sigmoid(acc)

    incl = _tri(c)
    eye = incl & jnp.logical_not(_tri(c, strict=True))
    gens, g_lasts = [], []
    for b in range(nb):
        for ck in range(cpb):
            r0 = ck * c
            sm = sm_ref[b, r0:r0 + c, :]
            if wide:
                gs = _exact_left_c(incl.astype(BF16), sm)
                gst = gs.T
            else:
                gs = _exact_left(incl.astype(BF16), sm)
            for h in range(GDN_HEADS):
                g = gs[:, G_LANE + h:G_LANE + h + 1]
                if wide:
                    g_row = gst[G_LANE + h:G_LANE + h + 1, :]
                else:
                    g_row = _exact_left(jnp.ones((c, c), BF16), jnp.where(eye, g, 0.0))
                g_last = gs[c - 1:c, G_LANE + h:G_LANE + h + 1]
                g_lasts.append(g_last)
                gens.append(_gdn_local(
                    conv_silu(b, r0, h * GDN_DK), conv_silu(b, r0, GDN_QK + h * GDN_DK),
                    conv_silu(b, r0, 2 * GDN_QK + h * GDN_DV), g, g_row, g_last,
                    sm[:, BETA_LANE + h:BETA_LANE + h + 1], c))
    locals_ = _interleave(gens)

    states = [s_ref[b, h] for b in range(nb) for h in range(GDN_HEADS)]
    for ck in range(cpb):
        r0 = ck * c
        pick = [(b * cpb + ck) * GDN_HEADS + h for b in range(nb) for h in range(GDN_HEADS)]
        outs, states = _gdn_state_step([locals_[u] for u in pick], states, [g_lasts[u] for u in pick], c)
        for idx, o in enumerate(outs):
            b, h = divmod(idx, GDN_HEADS)
            on = o * lax.rsqrt(jnp.mean(o * o, axis=-1, keepdims=True) + EPS) * nw_ref[...]
            zh = z_ref[b, r0:r0 + c, h * GDN_DV:(h + 1) * GDN_DV]
            o_ref[b, r0:r0 + c, h * GDN_DV:(h + 1) * GDN_DV] = (on * (zh * _sigmoid(zh))).astype(BF16)
    for idx, st in enumerate(states):
        b, h = divmod(idx, GDN_HEADS)
        s_ref[b, h] = st

    @pl.when(i == pl.num_programs(1) - 1)
    def _():
        sout_ref[...] = s_ref[...]


def _gdn(gqkv, hist, sm, gz, s0, conv_w, norm_w, *, c, nb, cpb):
    b, t, _ = gqkv.shape
    rows = c * cpb
    assert t % rows == 0 and b % nb == 0 and c % SUBLANES == 0
    zero_init = hist is None
    blk = lambda w: pl.BlockSpec((nb, rows, w), lambda bi, i: (bi, i, 0))
    const = lambda a: pl.BlockSpec(a.shape, lambda bi, i: (0,) * a.ndim)
    state_spec = pl.BlockSpec((nb, GDN_HEADS, GDN_DK, GDN_DV), lambda bi, i: (bi, 0, 0, 0))
    if zero_init:
        assert nb == 1
        hist_arr = gqkv
        hist_spec = pl.BlockSpec((1, SUBLANES, GDN_CONV_CH),
                                 lambda bi, i: (bi, jnp.maximum(i * (rows // SUBLANES) - 1, 0), 0))
        extra, extra_specs = [], []
    else:
        assert t == rows
        hist_arr = hist
        hist_spec = pl.BlockSpec((nb, SUBLANES, GDN_CONV_CH), lambda bi, i: (bi, 0, 0))
        extra, extra_specs = [s0], [state_spec]
    return pl.pallas_call(
        functools.partial(_gdn_kernel, c=c, nb=nb, cpb=cpb, zero_init=zero_init),
        grid=(b // nb, t // rows),
        in_specs=[blk(GDN_CONV_CH), hist_spec, blk(LANES), blk(GDN_WIDTH)] + extra_specs
                 + [const(conv_w), const(norm_w)],
        out_specs=[blk(GDN_WIDTH), state_spec],
        out_shape=[jax.ShapeDtypeStruct((b, t, GDN_WIDTH), BF16),
                   jax.ShapeDtypeStruct((b, GDN_HEADS, GDN_DK, GDN_DV), F32)],
        scratch_shapes=[pltpu.VMEM((nb, SUBLANES + rows, GDN_CONV_CH), F32),
                        pltpu.VMEM((nb, GDN_HEADS, GDN_DK, GDN_DV), F32)],
        compiler_params=pltpu.CompilerParams(dimension_semantics=("arbitrary", "arbitrary"),
                                             vmem_limit_bytes=VMEM_LIMIT),
        name="gdn_prompt" if zero_init else "gdn_sample",
    )(gqkv, hist_arr, sm, gz, *extra, conv_w, norm_w)


FFN_COLS = 256


def _ffn_kernel(of_ref, og_ref, x_ref, wo_ref, nfw_ref, wup_ref, cw_ref, cb_ref, wdn_ref, nlw_ref, *refs,
                nseq, rows, tiles_per_seq, has_state):
    if has_state:
        st_ref, y_ref, nst_ref, wg_ref, wv_ref, act_ref = refs
    else:
        y_ref, nst_ref, wg_ref, wv_ref, act_ref, carry_ref = refs
    i = pl.program_id(0)
    hist0 = SUBLANES - (FFN_CONV - 1)

    h = (x_ref[...] + _dot(of_ref[...], wo_ref[0:FOX_WIDTH, :])
         + _dot(og_ref[...], wo_ref[FOX_WIDTH:FOX_WIDTH + GDN_WIDTH, :]))
    hn = (h * lax.rsqrt(jnp.mean(h * h, axis=-1, keepdims=True) + EPS) * nfw_ref[...]).astype(BF16)

    if not has_state:
        @pl.when(i % tiles_per_seq == 0)
        def _():
            carry_ref[...] = jnp.zeros_like(carry_ref)

    def conv_cols(buf_ref, col):
        up = _dot(hn, wup_ref[:, col:col + FFN_COLS])
        buf_ref[:, SUBLANES:SUBLANES + rows, :] = up.reshape(nseq, rows, FFN_COLS)
        if has_state:
            buf_ref[:, hist0:SUBLANES, :] = st_ref[:, :, col:col + FFN_COLS]
        else:
            buf_ref[:, hist0:SUBLANES, :] = carry_ref[:, hist0:SUBLANES, col:col + FFN_COLS]
        last = buf_ref[:, SUBLANES + rows - (FFN_CONV - 1):SUBLANES + rows, :]
        nst_ref[:, :, col:col + FFN_COLS] = last
        if not has_state:
            carry_ref[:, hist0:SUBLANES, col:col + FFN_COLS] = last
        uc = cb_ref[:, col:col + FFN_COLS][None]
        for tap in range(FFN_CONV):
            uc = uc + buf_ref[:, hist0 + tap:hist0 + tap + rows, :] * cw_ref[tap:tap + 1, col:col + FFN_COLS][None]
        return uc.reshape(nseq * rows, FFN_COLS)

    for ci in range(D_FF // FFN_COLS):
        col = ci * FFN_COLS
        gate = conv_cols(wg_ref, col)
        val = conv_cols(wv_ref, D_FF + col)
        act_ref[:, col:col + FFN_COLS] = (gate * _sigmoid(gate) * val).astype(BF16)

    h2 = h + _dot(act_ref[...], wdn_ref[...])
    y_ref[...] = h2 * lax.rsqrt(jnp.mean(h2 * h2, axis=-1, keepdims=True) + EPS) * nlw_ref[...]


def _ffn(ofox, ogdn, x2d, w_o, norm_ffn_w, w_up, conv_w, conv_b, w_down, norm_final_w, state, *,
         nseq, rows, seq_len):
    n = x2d.shape[0]
    tm = nseq * rows
    assert n % tm == 0 and seq_len % rows == 0 and rows % SUBLANES == 0 and D_FF % FFN_COLS == 0
    has_state = state is not None
    assert (rows == seq_len) if has_state else (nseq == 1)
    tps = seq_len // rows
    nbatch = n // seq_len
    row = lambda w: pl.BlockSpec((tm, w), lambda i: (i, 0))
    const = lambda a: pl.BlockSpec(a.shape, lambda i: (0,) * a.ndim, pipeline_mode=pl.Buffered(1))
    st_spec = pl.BlockSpec((nseq, FFN_CONV - 1, 2 * D_FF), lambda i: (i // tps, 0, 0))
    ins = [ofox, ogdn, x2d, w_o, norm_ffn_w, w_up, conv_w, conv_b, w_down, norm_final_w]
    in_specs = [row(FOX_WIDTH), row(GDN_WIDTH), row(D_MODEL)] + [const(a) for a in ins[3:]]
    scratch = [pltpu.VMEM((nseq, SUBLANES + rows, FFN_COLS), F32),
               pltpu.VMEM((nseq, SUBLANES + rows, FFN_COLS), F32),
               pltpu.VMEM((tm, D_FF), BF16)]
    if has_state:
        ins.append(state)
        in_specs.append(st_spec)
    else:
        scratch.append(pltpu.VMEM((nseq, SUBLANES, 2 * D_FF), F32))
    return pl.pallas_call(
        functools.partial(_ffn_kernel, nseq=nseq, rows=rows, tiles_per_seq=tps, has_state=has_state),
        grid=(n // tm,),
        in_specs=in_specs,
        out_specs=[row(D_MODEL), st_spec],
        out_shape=[jax.ShapeDtypeStruct((n, D_MODEL), F32),
                   jax.ShapeDtypeStruct((nbatch, FFN_CONV - 1, 2 * D_FF), F32)],
        scratch_shapes=scratch,
        compiler_params=pltpu.CompilerParams(dimension_semantics=("arbitrary",),
                                             vmem_limit_bytes=VMEM_LIMIT),
        name="ffn_sample" if has_state else "ffn_prompt",
    )(*ins)


ROW_TILE = 512
SAMPLE_KV_BLOCK = 2048
GDN_CHUNK = 128
GDN_CHUNKS_PER_STEP = 2
GDN_SAMPLE_SEQS_PER_STEP = 4


def _prep_weights(w_in, b_fox_f, gdn_a_log, gdn_dt_bias):
    c0 = 3 * FOX_WIDTH
    c1 = c0 + FOX_HEADS
    c2 = c1 + GDN_CONV_CH
    c3 = c2 + 2 * GDN_HEADS
    small = jnp.concatenate([w_in[:, c0:c1], w_in[:, c2:c3],
                             jnp.zeros((D_MODEL, LANES - FOX_HEADS - 2 * GDN_HEADS), w_in.dtype)], axis=1)
    w_cat = jnp.concatenate([w_in[:, :c0], w_in[:, c1:c2], w_in[:, c3:], small], axis=1).astype(BF16)
    pad = jnp.zeros((LANES - FOX_HEADS - GDN_HEADS,), F32)
    bias = jnp.concatenate([b_fox_f.astype(F32), gdn_dt_bias.astype(F32), pad])
    alog = jnp.concatenate([jnp.zeros((FOX_HEADS,), F32), gdn_a_log.astype(F32), pad])
    prm = jnp.zeros((SUBLANES, LANES), F32).at[0].set(bias).at[1].set(alog)
    return w_cat, prm


def _layer(xp, xs, fox_k, fox_v, fox_logf, st_gdn, st_gconv, st_fconv,
           w_in, b_fox_f, gdn_conv_w, gdn_a_log, gdn_dt_bias, gdn_norm_w, w_o,
           norm_mix_w, norm_ffn_w, w_up, ffn_conv_w, ffn_conv_b, w_down, norm_out_w):
    bp, sp, _ = xp.shape
    bs, ts, _ = xs.shape
    w_cat, prm = _prep_weights(w_in, b_fox_f, gdn_a_log, gdn_dt_bias)
    nmw = norm_mix_w.reshape(1, D_MODEL).astype(F32)
    nfw = norm_ffn_w.reshape(1, D_MODEL).astype(F32)
    now = norm_out_w.reshape(1, D_MODEL).astype(F32)
    gnw = gdn_norm_w.reshape(1, GDN_DV).astype(F32)
    w_o_b, w_up_b, w_dn_b = w_o.astype(BF16), w_up.astype(BF16), w_down.astype(BF16)
    cb = ffn_conv_b.reshape(1, 2 * D_FF).astype(F32)

    xp2 = xp.reshape(bp * sp, D_MODEL)
    q, k, v, kb, qa, ka, va, gqkv, gz, sm = _inproj(xp2, nmw, w_cat, prm, tm=ROW_TILE, seq_len=sp, prompt=True)
    r3 = lambda a: a.reshape(bp, sp, a.shape[-1])
    o_fox = _fox_prompt(r3(q), r3(qa), r3(kb), r3(ka), va, t=ROW_TILE)
    gqkv3 = r3(gqkv)
    o_gdn, p_state = _gdn(gqkv3, None, r3(sm), r3(gz), None, gdn_conv_w, gnw,
                          c=GDN_CHUNK, nb=1, cpb=GDN_CHUNKS_PER_STEP)
    yp, p_fconv = _ffn(o_fox.reshape(bp * sp, FOX_WIDTH), o_gdn.reshape(bp * sp, GDN_WIDTH), xp2,
                       w_o_b, nfw, w_up_b, ffn_conv_w, cb, w_dn_b, now, None,
                       nseq=1, rows=ROW_TILE, seq_len=sp)
    p_out = (jnp.transpose(k, (0, 3, 1, 2)), jnp.transpose(v, (0, 3, 1, 2)),
             r3(sm)[:, :, LOGF_LANE:LOGF_LANE + FOX_HEADS], p_state,
             gqkv3[:, sp - (GDN_CONV - 1):], p_fconv)

    xs2 = xs.reshape(bs * ts, D_MODEL)
    q, k, v, gqkv, gz, sm = _inproj(xs2, nmw, w_cat, prm, tm=bs * ts, seq_len=ts, prompt=False)
    s3 = lambda a: a.reshape(bs, ts, a.shape[-1])
    past = fox_k.shape[1]
    lft = jnp.swapaxes(fox_logf.astype(F32), 1, 2)
    cache_t = lambda a: jnp.transpose(a.astype(F32), (0, 2, 3, 1))
    o_fox = _fox_sample(s3(q), s3(k), s3(v), s3(sm), lft, cache_t(fox_k), cache_t(fox_v),
                        kb=min(SAMPLE_KV_BLOCK, past))
    gqkv3 = s3(gqkv)
    hist = jnp.concatenate([jnp.zeros((bs, SUBLANES - (GDN_CONV - 1), GDN_CONV_CH), F32),
                            st_gconv.astype(F32)], axis=1)
    o_gdn, s_state = _gdn(gqkv3, hist, s3(sm), s3(gz), st_gdn.astype(F32), gdn_conv_w, gnw,
                          c=ts, nb=GDN_SAMPLE_SEQS_PER_STEP, cpb=1)
    ys, s_fconv = _ffn(o_fox.reshape(bs * ts, FOX_WIDTH), o_gdn.reshape(bs * ts, GDN_WIDTH), xs2,
                       w_o_b, nfw, w_up_b, ffn_conv_w, cb, w_dn_b, now, st_fconv.astype(F32),
                       nseq=bs, rows=ts, seq_len=ts)
    full = jnp.concatenate([st_gconv.astype(F32), gqkv3], axis=1)
    s_out = (k.reshape(bs, ts, FOX_HEADS, FOX_DIM), v.reshape(bs, ts, FOX_HEADS, FOX_DIM),
             s3(sm)[:, :, LOGF_LANE:LOGF_LANE + FOX_HEADS], s_state,
             full[:, ts:], s_fconv)
    return yp.reshape(bp, sp, D_MODEL), ys.reshape(bs, ts, D_MODEL), p_out, s_out


def kernel(x_prompt, x_sample, cache_fox_k, cache_fox_v, cache_fox_logf, state_gdn, state_gdn_conv,
           state_ffn_conv, w_in, b_fox_f, gdn_conv_w, gdn_a_log, gdn_dt_bias, gdn_norm_w, w_o, norm_mix_w,
           norm_ffn_w, w_up, ffn_conv_w, ffn_conv_b, w_down, norm_final_w):
    depth = w_in.shape[0]
    assert depth == 1, "the final RMSNorm is fused into the single layer's FFN kernel"
    yp, ys, p_out, s_out = _layer(
        x_prompt, x_sample, cache_fox_k[0], cache_fox_v[0], cache_fox_logf[0], state_gdn[0],
        state_gdn_conv[0], state_ffn_conv[0], w_in[0], b_fox_f[0], gdn_conv_w[0], gdn_a_log[0],
        gdn_dt_bias[0], gdn_norm_w[0], w_o[0], norm_mix_w[0], norm_ffn_w[0], w_up[0], ffn_conv_w[0],
        ffn_conv_b[0], w_down[0], norm_final_w)
    return (yp, ys) + tuple(a[None] for a in p_out) + tuple(a[None] for a in s_out)
```

```python
import functools
import math

import numpy as np
import jax
import jax.numpy as jnp
from jax import lax
from jax.experimental import pallas as pl
from jax.experimental.pallas import tpu as pltpu

F32 = jnp.float32
BF16 = jnp.bfloat16

D_MODEL = 1024
FOX_DIM = 64
FOX_HEADS = 8
FOX_WIDTH = FOX_HEADS * FOX_DIM
GDN_DK = 128
GDN_DV = 128
GDN_HEADS = 4
GDN_QK = GDN_HEADS * GDN_DK
GDN_WIDTH = GDN_HEADS * GDN_DV
GDN_CONV_CH = 2 * GDN_QK + GDN_WIDTH
GDN_CONV = 4
D_FF = 2816
FFN_CONV = 3
EPS = 1e-6

LANES = 128
SUBLANES = 8
VMEM_LIMIT = 56 * 1024 * 1024
MASKED = -1e30
LOG2E = math.log2(math.e)

LOGF_LANE = 0
G_LANE = FOX_HEADS
BETA_LANE = FOX_HEADS + GDN_HEADS

COL_FOX = 0
COL_GQKV = 3 * FOX_WIDTH
COL_GZ = COL_GQKV + GDN_CONV_CH
COL_SMALL = COL_GZ + GDN_WIDTH
N_COLS = COL_SMALL + LANES


def _dot(a, b):
    return jnp.dot(a, b, preferred_element_type=F32)


def _dot_nt(a, b):
    return lax.dot_general(a, b, (((1,), (1,)), ((), ())), preferred_element_type=F32)


def _dot_tn(a, b):
    return lax.dot_general(a, b, (((0,), (0,)), ((), ())), preferred_element_type=F32)


def _split3(x):
    hi = x.astype(BF16)
    r = x - hi.astype(F32)
    lo = r.astype(BF16)
    lo2 = (r - lo.astype(F32)).astype(BF16)
    return hi, lo, lo2


def _exact_left(mat, x):
    hi, lo, lo2 = _split3(x)
    return _dot(mat, hi) + _dot(mat, lo) + _dot(mat, lo2)


def _exact_right(x, mat):
    hi, lo, lo2 = _split3(x)
    return _dot(hi, mat) + _dot(lo, mat) + _dot(lo2, mat)


def _dot3(a, b):
    ah = a.astype(BF16)
    al = (a - ah.astype(F32)).astype(BF16)
    bh = b.astype(BF16)
    bl = (b - bh.astype(F32)).astype(BF16)
    return _dot(ah, bh) + (_dot(ah, bl) + _dot(al, bh))


def _exact_left_c(mat, x):
    return _dot(jnp.concatenate([mat, mat, mat], axis=1), jnp.concatenate(_split3(x), axis=0))


def _split2(x):
    hi = x.astype(BF16)
    return hi, (x - hi.astype(F32)).astype(BF16)


def _dot3_parts(ah, al, bh, bl):
    return _dot(jnp.concatenate([ah, ah, al], axis=1), jnp.concatenate([bh, bl, bh], axis=0))


def _dot3c(a, b):
    return _dot3_parts(*_split2(a), *_split2(b))


def _sigmoid(x):
    return 1.0 / (1.0 + jnp.exp(-x))


def _tri(n, strict=False):
    r = lax.broadcasted_iota(jnp.int32, (n, n), 0)
    c = lax.broadcasted_iota(jnp.int32, (n, n), 1)
    return (r > c) if strict else (r >= c)


def _inproj_kernel(x_ref, nw_ref, w_ref, prm_ref, *refs, tiles_per_seq, prompt):
    if prompt:
        (aug_ref, augc_ref, q_ref, k_ref, v_ref, kb_ref, qa_ref, ka_ref, va_ref, g_ref, z_ref, sm_ref,
         carry_ref) = refs
    else:
        q_ref, k_ref, v_ref, g_ref, z_ref, sm_ref = refs
    x = x_ref[...]
    xn = x * lax.rsqrt(jnp.mean(x * x, axis=-1, keepdims=True) + EPS) * nw_ref[...]
    xb = xn.astype(BF16)

    q_scale = FOX_DIM ** -0.5 * (LOG2E if prompt else 1.0)
    q_ref[...] = (_dot(xb, w_ref[:, 0:FOX_WIDTH]) * q_scale).astype(BF16)
    kf = _dot(xb, w_ref[:, FOX_WIDTH:2 * FOX_WIDTH])
    vf = _dot(xb, w_ref[:, 2 * FOX_WIDTH:3 * FOX_WIDTH])
    if not prompt:
        k_ref[...] = kf
        v_ref[...] = vf
    else:
        tm = x.shape[0]
        vt = vf.T.reshape(FOX_HEADS, FOX_DIM, tm)
        k_ref[0] = kf.T.reshape(FOX_HEADS, FOX_DIM, tm)
        v_ref[0] = vt
        kb_ref[...] = kf.astype(BF16)
        va_ref[0, :, 0:FOX_DIM, :] = vt.astype(BF16)
        va_ref[0, :, FOX_DIM:, :] = jnp.ones((FOX_HEADS, ONES_ROWS, tm), BF16)
    g_ref[...] = _dot(xb, w_ref[:, COL_GQKV:COL_GZ])
    z_ref[...] = _dot(xb, w_ref[:, COL_GZ:COL_SMALL])

    z = _dot(xb, w_ref[:, COL_SMALL:N_COLS])
    lane = lax.broadcasted_iota(jnp.int32, z.shape, 1)
    zb = z + prm_ref[0:1, :]
    e = jnp.exp(-jnp.abs(zb))
    l1p = jnp.log(1.0 + e)
    logf = jnp.minimum(zb, 0.0) - l1p
    g = -jnp.exp(prm_ref[1:2, :]) * (jnp.maximum(zb, 0.0) + l1p)
    beta = _sigmoid(z)
    sm = jnp.where(lane < G_LANE, logf,
                   jnp.where(lane < BETA_LANE, g,
                             jnp.where(lane < BETA_LANE + GDN_HEADS, beta, 0.0)))
    sm_ref[...] = sm

    if prompt:
        i = pl.program_id(0)

        @pl.when(i % tiles_per_seq == 0)
        def _():
            carry_ref[...] = jnp.zeros_like(carry_ref)

        c = _exact_left(_tri(tm).astype(BF16), sm) + carry_ref[0:1, :]
        carry_ref[0:1, :] = c[tm - 1:tm, :]
        parts = jnp.concatenate(_split3(c * LOG2E), axis=1)
        aug = _dot(parts, aug_ref[...]) + augc_ref[...]
        qa_ref[...] = aug[:, :FOX_WIDTH].astype(BF16)
        ka_ref[...] = aug[:, FOX_WIDTH:].astype(BF16)


AUG_LANES = 6
ONES_ROWS = 16


def _aug_tables():
    place = np.zeros((3 * LANES, 2 * FOX_WIDTH), np.float32)
    ones = np.zeros((1, 2 * FOX_WIDTH), np.float32)
    for h in range(FOX_HEADS):
        q0 = (h // 2) * LANES + (h % 2) * AUG_LANES
        k0 = FOX_WIDTH + q0
        for part in range(3):
            place[part * LANES + LOGF_LANE + h, q0 + part] = 1.0
            place[part * LANES + LOGF_LANE + h, k0 + 3 + part] = -1.0
            ones[0, q0 + 3 + part] = 1.0
            ones[0, k0 + part] = 1.0
    return jnp.asarray(place, BF16), jnp.asarray(ones, F32)


def _inproj(x2d, norm_w, w_cat, prm, *, tm, seq_len, prompt):
    n = x2d.shape[0]
    assert n % tm == 0 and (not prompt or seq_len % tm == 0)
    tps = seq_len // tm if prompt else 1
    row = lambda w: pl.BlockSpec((tm, w), lambda i: (i, 0))
    const = lambda a: pl.BlockSpec(a.shape, lambda i: (0,) * a.ndim, pipeline_mode=pl.Buffered(1))
    ins = [x2d, norm_w, w_cat, prm]
    out_shape = [jax.ShapeDtypeStruct((n, FOX_WIDTH), BF16),
                 jax.ShapeDtypeStruct((n, FOX_WIDTH), F32),
                 jax.ShapeDtypeStruct((n, FOX_WIDTH), F32)]
    out_specs = [row(FOX_WIDTH), row(FOX_WIDTH), row(FOX_WIDTH)]
    scratch = []
    if prompt:
        kv_t = jax.ShapeDtypeStruct((n // seq_len, FOX_HEADS, FOX_DIM, seq_len), F32)
        out_shape[1:3] = [kv_t, kv_t]
        out_specs[1:3] = [pl.BlockSpec((1, FOX_HEADS, FOX_DIM, tm), lambda i: (i // tps, 0, 0, i % tps))] * 2
        ins += list(_aug_tables())
        out_shape += [jax.ShapeDtypeStruct((n, FOX_WIDTH), BF16)] * 3 + [
            jax.ShapeDtypeStruct((n // seq_len, FOX_HEADS, FOX_DIM + ONES_ROWS, seq_len), BF16)]
        out_specs += [row(FOX_WIDTH)] * 3 + [
            pl.BlockSpec((1, FOX_HEADS, FOX_DIM + ONES_ROWS, tm), lambda i: (i // tps, 0, 0, i % tps))]
        scratch = [pltpu.VMEM((SUBLANES, LANES), F32)]
    out_shape += [jax.ShapeDtypeStruct((n, GDN_CONV_CH), F32),
                  jax.ShapeDtypeStruct((n, GDN_WIDTH), F32),
                  jax.ShapeDtypeStruct((n, LANES), F32)]
    out_specs += [row(GDN_CONV_CH), row(GDN_WIDTH), row(LANES)]
    return pl.pallas_call(
        functools.partial(_inproj_kernel, tiles_per_seq=tps, prompt=prompt),
        grid=(n // tm,),
        in_specs=[row(D_MODEL)] + [const(a) for a in ins[1:]],
        out_specs=out_specs, out_shape=out_shape, scratch_shapes=scratch,
        compiler_params=pltpu.CompilerParams(dimension_semantics=("arbitrary",),
                                             vmem_limit_bytes=VMEM_LIMIT),
        name="inproj_prompt" if prompt else "inproj_sample",
    )(*ins)


def _fox_prompt_kernel(q_ref, qa_ref, k_ref, ka_ref, va_ref, o_ref, sa_ref, sb_ref, *, t):
    i = pl.program_id(2)
    q = q_ref[0]
    qa = qa_ref[0]
    lane = lax.broadcasted_iota(jnp.int32, (t, LANES), 1)
    low = lane < FOX_DIM
    zero = jnp.zeros_like(q)
    qcat = []
    for h in range(2):
        mine = (lane >= h * AUG_LANES) & (lane < (h + 1) * AUG_LANES)
        qcat.append(jnp.concatenate([jnp.where(low if h == 0 else jnp.logical_not(low), q, zero),
                                     jnp.where(mine, qa, zero)], axis=1))

    def logits(j, s_ref):
        start = pl.multiple_of(j * t, t)
        kcat = jnp.concatenate([k_ref[0, pl.ds(start, t), :], ka_ref[0, pl.ds(start, t), :]], axis=1)
        for h in range(2):
            s_ref[h] = _dot_nt(kcat, qcat[h])

    def consume(j, s_ref, carry, diag):
        start = pl.multiple_of(j * t, t)
        out = []
        for h in range(2):
            m, acc = carry[h]
            s = s_ref[h]
            if diag:
                key = lax.broadcasted_iota(jnp.int32, (t, t), 0)
                query = lax.broadcasted_iota(jnp.int32, (t, t), 1)
                s = jnp.where(key <= query, s, MASKED)
            m_new = jnp.maximum(m, jnp.max(s, axis=0, keepdims=True))
            p = jnp.exp2(s - m_new).astype(BF16)
            acc = jnp.exp2(m - m_new) * acc + _dot(va_ref[0, h, :, pl.ds(start, t)], p)
            out.append((m_new, acc))
        return tuple(out)

    init = tuple((jnp.full((1, t), MASKED, F32), jnp.zeros((FOX_DIM + ONES_ROWS, t), F32)) for _ in range(2))
    logits(0, sa_ref)

    def two_blocks(jj, carry):
        j = 2 * jj
        logits(j + 1, sb_ref)
        carry = consume(j, sa_ref, carry, False)
        logits(j + 2, sa_ref)
        return consume(j + 1, sb_ref, carry, False)

    carry = lax.fori_loop(0, i // 2, two_blocks, init)

    def odd_tail(carry):
        logits(i, sb_ref)
        return consume(i, sb_ref, consume(i - 1, sa_ref, carry, False), True)

    (_, a0), (_, a1) = lax.cond(i % 2 == 1, odd_tail, lambda c: consume(i, sa_ref, c, True), carry)
    o_t = jnp.concatenate([a[:FOX_DIM] / a[FOX_DIM:FOX_DIM + 1] for a in (a0, a1)], axis=0)
    o_ref[0] = o_t.T.astype(BF16)


def _fox_prompt(q, qa, kb, ka, va, *, t):
    b, s, _ = q.shape
    nblk = s // t
    assert s % t == 0
    npair = FOX_HEADS // 2
    qblk = pl.BlockSpec((1, t, LANES), lambda bi, p, i: (bi, i, p))
    kblk = pl.BlockSpec((1, s, LANES), lambda bi, p, i: (bi, 0, p))
    return pl.pallas_call(
        functools.partial(_fox_prompt_kernel, t=t),
        grid=(b, npair, nblk),
        in_specs=[qblk, qblk, kblk, kblk,
                  pl.BlockSpec((1, 2, FOX_DIM + ONES_ROWS, s), lambda bi, p, i: (bi, p, 0, 0))],
        out_specs=qblk,
        out_shape=jax.ShapeDtypeStruct((b, s, FOX_WIDTH), BF16),
        scratch_shapes=[pltpu.VMEM((2, t, t), F32), pltpu.VMEM((2, t, t), F32)],
        compiler_params=pltpu.CompilerParams(
            dimension_semantics=("arbitrary", "arbitrary", "arbitrary"), vmem_limit_bytes=VMEM_LIMIT),
        name="fox_prompt",
    )(q, qa, kb, ka, va)


def _fox_sample_kernel(q_ref, kn_ref, vn_ref, sm_ref, lft_ref, kt_ref, vt_ref, o_ref,
                       rk_ref, m_ref, l_ref, acc_ref, *, kb, t):
    j = pl.program_id(1)
    nkv = pl.num_programs(1)

    def heads(x):
        return [x[:, h * FOX_DIM:(h + 1) * FOX_DIM] for h in range(FOX_HEADS)]

    qh = [x.astype(BF16) for x in heads(q_ref[0].astype(F32))]

    lf_new = sm_ref[0]
    cn = _exact_left(_tri(t).astype(BF16), lf_new)

    @pl.when(j == 0)
    def _():
        p = lft_ref.shape[2]
        nb = p // LANES
        x = jnp.concatenate([lft_ref[0, :, i * LANES:(i + 1) * LANES] for i in range(nb)], axis=0)
        r = lax.broadcasted_iota(jnp.int32, (LANES, LANES), 0)
        cc = lax.broadcasted_iota(jnp.int32, (LANES, LANES), 1)
        y = _exact_right(x, (r > cc).astype(BF16))
        tot = jnp.sum(x, axis=1, keepdims=True)
        off = jnp.zeros((FOX_HEADS, 1), F32)
        for i in range(nb - 1, -1, -1):
            rk_ref[:, i * LANES:(i + 1) * LANES] = y[i * FOX_HEADS:(i + 1) * FOX_HEADS, :] + off
            off = off + tot[i * FOX_HEADS:(i + 1) * FOX_HEADS, :]
        m_ref[...] = jnp.full_like(m_ref, MASKED)
        l_ref[...] = jnp.zeros_like(l_ref)
        acc_ref[...] = jnp.zeros_like(acc_ref)

    def update(s, pv):
        m = m_ref[:, 0:1]
        m_new = jnp.maximum(m, jnp.max(s, axis=1, keepdims=True))
        alpha = jnp.exp(m - m_new)
        p = jnp.exp(s - m_new)
        l_ref[...] = jnp.broadcast_to(alpha * l_ref[:, 0:1] + jnp.sum(p, axis=1, keepdims=True), l_ref.shape)
        acc_ref[...] = alpha * acc_ref[...] + pv(p.astype(BF16))
        m_ref[...] = jnp.broadcast_to(m_new, m_ref.shape)

    start = pl.multiple_of(j * kb, kb)
    rk = rk_ref[:, pl.ds(start, kb)]
    s = jnp.concatenate([_dot(qh[h], kt_ref[0, h].astype(BF16)) + (cn[:, h:h + 1] + rk[h:h + 1, :])
                         for h in range(FOX_HEADS)], axis=0)
    update(s, lambda p: jnp.concatenate(
        [_dot_nt(p[h * t:(h + 1) * t], vt_ref[0, h].astype(BF16)) for h in range(FOX_HEADS)], axis=0))

    @pl.when(j == nkv - 1)
    def _():
        r = lax.broadcasted_iota(jnp.int32, (t, t), 0)
        cc = lax.broadcasted_iota(jnp.int32, (t, t), 1)
        after = (r > cc).astype(F32)
        tri = _tri(t).astype(BF16)
        knh = heads(kn_ref[0])
        vnh = [x.astype(BF16) for x in heads(vn_ref[0])]
        s_new = jnp.concatenate(
            [jnp.where(_tri(t), _dot_nt(qh[h], knh[h].astype(BF16))
                       + _exact_left(tri, lf_new[:, h:h + 1] * after), MASKED) for h in range(FOX_HEADS)], axis=0)
        update(s_new, lambda p: jnp.concatenate(
            [_dot(p[h * t:(h + 1) * t], vnh[h]) for h in range(FOX_HEADS)], axis=0))
        o_full = acc_ref[...] / l_ref[:, 0:1]
        o_ref[0] = jnp.concatenate([o_full[h * t:(h + 1) * t] for h in range(FOX_HEADS)], axis=1).astype(BF16)


def _fox_sample(q, kn, vn, sm, lft, kt, vt, *, kb):
    b, t, _ = q.shape
    p = kt.shape[3]
    assert p % kb == 0 and kb % LANES == 0
    rows = FOX_HEADS * t
    per_b = lambda shape: pl.BlockSpec((1,) + shape, lambda bi, j: (bi, 0, 0))
    cache = pl.BlockSpec((1, FOX_HEADS, FOX_DIM, kb), lambda bi, j: (bi, 0, 0, j))
    return pl.pallas_call(
        functools.partial(_fox_sample_kernel, kb=kb, t=t),
        grid=(b, p // kb),
        in_specs=[per_b((t, FOX_WIDTH)), per_b((t, FOX_WIDTH)), per_b((t, FOX_WIDTH)), per_b((t, LANES)),
                  per_b((FOX_HEADS, p)), cache, cache],
        out_specs=per_b((t, FOX_WIDTH)),
        out_shape=jax.ShapeDtypeStruct((b, t, FOX_WIDTH), BF16),
        scratch_shapes=[pltpu.VMEM((FOX_HEADS, p), F32),
                        pltpu.VMEM((rows, LANES), F32), pltpu.VMEM((rows, LANES), F32),
                        pltpu.VMEM((rows, FOX_DIM), F32)],
        compiler_params=pltpu.CompilerParams(dimension_semantics=("arbitrary", "arbitrary"),
                                             vmem_limit_bytes=VMEM_LIMIT),
        name="fox_sample",
    )(q, kn, vn, sm, lft, kt, vt)


def _interleave(gens):
    results = [None] * len(gens)
    alive = list(range(len(gens)))
    while alive:
        for idx in list(alive):
            try:
                next(gens[idx])
            except StopIteration as stop:
                results[idx] = stop.value
                alive.remove(idx)
    return results


def _gdn_local(qh, kh, vh, g, g_row, g_last, beta, c):
    wide = c % LANES == 0
    dot3 = _dot3c if wide else _dot3
    incl = _tri(c)
    strict = _tri(c, strict=True)
    qn = qh * (lax.rsqrt(jnp.sum(qh * qh, axis=-1, keepdims=True) + EPS) * (GDN_DK ** -0.5))
    kn = kh * lax.rsqrt(jnp.sum(kh * kh, axis=-1, keepdims=True) + EPS)
    eg = jnp.exp(g)
    decay = jnp.exp(jnp.where(incl, g - g_row, MASKED))
    kbeta = kn * beta
    kq = _dot_nt(jnp.concatenate([kbeta, qn], axis=0).astype(BF16), kn.astype(BF16))
    yield
    lmat = jnp.where(strict, kq[:c] * decay, 0.0)
    a_qk = kq[c:] * decay

    tinv = (incl & jnp.logical_not(strict)).astype(F32) - lmat
    if wide:
        lh, ll = _split2(lmat)
        power = _dot3_parts(lh, ll, lh, ll)
    else:
        power = dot3(lmat, lmat)
    yield
    span = 2
    while 2 * span < c:
        if wide:
            ph, pl_ = _split2(power)
            th, tl = _split2(tinv)
            both = _dot3_parts(ph, pl_, jnp.concatenate([th, ph], axis=1), jnp.concatenate([tl, pl_], axis=1))
            yield
            tinv = tinv + both[:, :c]
            power = both[:, c:]
        else:
            step = dot3(power, tinv)
            power = dot3(power, power)
            yield
            tinv = tinv + step
        span *= 2
    step = dot3(power, tinv)
    yield
    tinv = tinv + step

    sol = dot3(tinv, jnp.concatenate([vh * beta, kbeta * eg], axis=1))
    yield
    kd = kn * jnp.exp(g_last - g)
    return sol[:, :GDN_DV], sol[:, GDN_DV:], qn * eg, a_qk, (kd.T if wide else kd)


def _gdn_state_step(locals_, states, g_lasts, c):
    wide = c % LANES == 0
    ws = [_dot(jnp.concatenate([w, qeg], axis=0).astype(BF16), sh.astype(BF16))
          for (_, w, qeg, _, _), sh in zip(locals_, states)]
    outs, new_states = [], []
    for (u, _, _, a_qk, kd), wsh, sh, g_last in zip(locals_, ws, states, g_lasts):
        db = (u - wsh[:c]).astype(BF16)
        decayed = sh * jnp.exp(g_last)
        if wide:
            ak = _dot(jnp.concatenate([a_qk, kd], axis=0).astype(BF16), db)
            outs.append(wsh[c:] + ak[:c])
            new_states.append(decayed + ak[c:])
        else:
            outs.append(wsh[c:] + _dot(a_qk.astype(BF16), db))
            new_states.append(decayed + _dot_tn(kd.astype(BF16), db))
    return outs, new_states


def _gdn_kernel(cur_ref, hist_ref, sm_ref, z_ref, *refs, c, nb, cpb, zero_init):
    if zero_init:
        cw_ref, nw_ref, o_ref, sout_ref, xs_ref, s_ref = refs
    else:
        s0_ref, cw_ref, nw_ref, o_ref, sout_ref, xs_ref, s_ref = refs
    i = pl.program_id(1)
    rows = c * cpb
    wide = c % LANES == 0

    @pl.when(i == 0)
    def _():
        if zero_init:
            s_ref[...] = jnp.zeros_like(s_ref)
        else:
            s_ref[...] = s0_ref[...]

    hist = hist_ref[...]
    if zero_init:
        hist = jnp.where(i > 0, hist, 0.0)
    xs_ref[:, 0:SUBLANES, :] = hist
    xs_ref[:, SUBLANES:SUBLANES + rows, :] = cur_ref[...]
    base = SUBLANES - (GDN_CONV - 1)

    def conv_silu(b, r0, col):
        acc = xs_ref[b, base + r0:base + r0 + c, col:col + LANES] * cw_ref[0:1, col:col + LANES]
        for tap in range(1, GDN_CONV):
            acc = acc + (xs_ref[b, base + r0 + tap:base + r0 + tap + c, col:col + LANES]
                         * cw_ref[tap:tap + 1, col:col + LANES])
        return acc * _sigmoid(acc)

    incl = _tri(c)
    eye = incl & jnp.logical_not(_tri(c, strict=True))
    gens, g_lasts = [], []
    for b in range(nb):
        for ck in range(cpb):
            r0 = ck * c
            sm = sm_ref[b, r0:r0 + c, :]
            if wide:
                gs = _exact_left_c(incl.astype(BF16), sm)
                gst = gs.T
            else:
                gs = _exact_left(incl.astype(BF16), sm)
            for h in range(GDN_HEADS):
                g = gs[:, G_LANE + h:G_LANE + h + 1]
                if wide:
                    g_row = gst[G_LANE + h:G_LANE + h + 1, :]
                else:
                    g_row = _exact_left(jnp.ones((c, c), BF16), jnp.where(eye, g, 0.0))
                g_last = gs[c - 1:c, G_LANE + h:G_LANE + h + 1]
                g_lasts.append(g_last)
                gens.append(_gdn_local(
                    conv_silu(b, r0, h * GDN_DK), conv_silu(b, r0, GDN_QK + h * GDN_DK),
                    conv_silu(b, r0, 2 * GDN_QK + h * GDN_DV), g, g_row, g_last,
                    sm[:, BETA_LANE + h:BETA_LANE + h + 1], c))
    locals_ = _interleave(gens)

    states = [s_ref[b, h] for b in range(nb) for h in range(GDN_HEADS)]
    for ck in range(cpb):
        r0 = ck * c
        pick = [(b * cpb + ck) * GDN_HEADS + h for b in range(nb) for h in range(GDN_HEADS)]
        outs, states = _gdn_state_step([locals_[u] for u in pick], states, [g_lasts[u] for u in pick], c)
        for idx, o in enumerate(outs):
            b, h = divmod(idx, GDN_HEADS)
            on = o * lax.rsqrt(jnp.mean(o * o, axis=-1, keepdims=True) + EPS) * nw_ref[...]
            zh = z_ref[b, r0:r0 + c, h * GDN_DV:(h + 1) * GDN_DV]
            o_ref[b, r0:r0 + c, h * GDN_DV:(h + 1) * GDN_DV] = (on * (zh * _sigmoid(zh))).astype(BF16)
    for idx, st in enumerate(states):
        b, h = divmod(idx, GDN_HEADS)
        s_ref[b, h] = st

    @pl.when(i == pl.num_programs(1) - 1)
    def _():
        sout_ref[...] = s_ref[...]


def _gdn(gqkv, hist, sm, gz, s0, conv_w, norm_w, *, c, nb, cpb):
    b, t, _ = gqkv.shape
    rows = c * cpb
    assert t % rows == 0 and b % nb == 0 and c % SUBLANES == 0
    zero_init = hist is None
    blk = lambda w: pl.BlockSpec((nb, rows, w), lambda bi, i: (bi, i, 0))
    const = lambda a: pl.BlockSpec(a.shape, lambda bi, i: (0,) * a.ndim)
    state_spec = pl.BlockSpec((nb, GDN_HEADS, GDN_DK, GDN_DV), lambda bi, i: (bi, 0, 0, 0))
    if zero_init:
        assert nb == 1
        hist_arr = gqkv
        hist_spec = pl.BlockSpec((1, SUBLANES, GDN_CONV_CH),
                                 lambda bi, i: (bi, jnp.maximum(i * (rows // SUBLANES) - 1, 0), 0))
        extra, extra_specs = [], []
    else:
        assert t == rows
        hist_arr = hist
        hist_spec = pl.BlockSpec((nb, SUBLANES, GDN_CONV_CH), lambda bi, i: (bi, 0, 0))
        extra, extra_specs = [s0], [state_spec]
    return pl.pallas_call(
        functools.partial(_gdn_kernel, c=c, nb=nb, cpb=cpb, zero_init=zero_init),
        grid=(b // nb, t // rows),
        in_specs=[blk(GDN_CONV_CH), hist_spec, blk(LANES), blk(GDN_WIDTH)] + extra_specs
                 + [const(conv_w), const(norm_w)],
        out_specs=[blk(GDN_WIDTH), state_spec],
        out_shape=[jax.ShapeDtypeStruct((b, t, GDN_WIDTH), BF16),
                   jax.ShapeDtypeStruct((b, GDN_HEADS, GDN_DK, GDN_DV), F32)],
        scratch_shapes=[pltpu.VMEM((nb, SUBLANES + rows, GDN_CONV_CH), F32),
                        pltpu.VMEM((nb, GDN_HEADS, GDN_DK, GDN_DV), F32)],
        compiler_params=pltpu.CompilerParams(dimension_semantics=("arbitrary", "arbitrary"),
                                             vmem_limit_bytes=VMEM_LIMIT),
        name="gdn_prompt" if zero_init else "gdn_sample",
    )(gqkv, hist_arr, sm, gz, *extra, conv_w, norm_w)


FFN_COLS = 256


def _ffn_kernel(of_ref, og_ref, x_ref, wo_ref, nfw_ref, wup_ref, cw_ref, cb_ref, wdn_ref, nlw_ref, *refs,
                nseq, rows, tiles_per_seq, has_state):
    if has_state:
        st_ref, y_ref, nst_ref, wg_ref, wv_ref, act_ref = refs
    else:
        y_ref, nst_ref, wg_ref, wv_ref, act_ref, carry_ref = refs
    i = pl.program_id(0)
    hist0 = SUBLANES - (FFN_CONV - 1)

    h = (x_ref[...] + _dot(of_ref[...], wo_ref[0:FOX_WIDTH, :])
         + _dot(og_ref[...], wo_ref[FOX_WIDTH:FOX_WIDTH + GDN_WIDTH, :]))
    hn = (h * lax.rsqrt(jnp.mean(h * h, axis=-1, keepdims=True) + EPS) * nfw_ref[...]).astype(BF16)

    if not has_state:
        @pl.when(i % tiles_per_seq == 0)
        def _():
            carry_ref[...] = jnp.zeros_like(carry_ref)

    def conv_cols(buf_ref, col):
        up = _dot(hn, wup_ref[:, col:col + FFN_COLS])
        buf_ref[:, SUBLANES:SUBLANES + rows, :] = up.reshape(nseq, rows, FFN_COLS)
        if has_state:
            buf_ref[:, hist0:SUBLANES, :] = st_ref[:, :, col:col + FFN_COLS]
        else:
            buf_ref[:, hist0:SUBLANES, :] = carry_ref[:, hist0:SUBLANES, col:col + FFN_COLS]
        last = buf_ref[:, SUBLANES + rows - (FFN_CONV - 1):SUBLANES + rows, :]
        nst_ref[:, :, col:col + FFN_COLS] = last
        if not has_state:
            carry_ref[:, hist0:SUBLANES, col:col + FFN_COLS] = last
        uc = cb_ref[:, col:col + FFN_COLS][None]
        for tap in range(FFN_CONV):
            uc = uc + buf_ref[:, hist0 + tap:hist0 + tap + rows, :] * cw_ref[tap:tap + 1, col:col + FFN_COLS][None]
        return uc.reshape(nseq * rows, FFN_COLS)

    for ci in range(D_FF // FFN_COLS):
        col = ci * FFN_COLS
        gate = conv_cols(wg_ref, col)
        val = conv_cols(wv_ref, D_FF + col)
        act_ref[:, col:col + FFN_COLS] = (gate * _sigmoid(gate) * val).astype(BF16)

    h2 = h + _dot(act_ref[...], wdn_ref[...])
    y_ref[...] = h2 * lax.rsqrt(jnp.mean(h2 * h2, axis=-1, keepdims=True) + EPS) * nlw_ref[...]


def _ffn(ofox, ogdn, x2d, w_o, norm_ffn_w, w_up, conv_w, conv_b, w_down, norm_final_w, state, *,
         nseq, rows, seq_len):
    n = x2d.shape[0]
    tm = nseq * rows
    assert n % tm == 0 and seq_len % rows == 0 and rows % SUBLANES == 0 and D_FF % FFN_COLS == 0
    has_state = state is not None
    assert (rows == seq_len) if has_state else (nseq == 1)
    tps = seq_len // rows
    nbatch = n // seq_len
    row = lambda w: pl.BlockSpec((tm, w), lambda i: (i, 0))
    const = lambda a: pl.BlockSpec(a.shape, lambda i: (0,) * a.ndim, pipeline_mode=pl.Buffered(1))
    st_spec = pl.BlockSpec((nseq, FFN_CONV - 1, 2 * D_FF), lambda i: (i // tps, 0, 0))
    ins = [ofox, ogdn, x2d, w_o, norm_ffn_w, w_up, conv_w, conv_b, w_down, norm_final_w]
    in_specs = [row(FOX_WIDTH), row(GDN_WIDTH), row(D_MODEL)] + [const(a) for a in ins[3:]]
    scratch = [pltpu.VMEM((nseq, SUBLANES + rows, FFN_COLS), F32),
               pltpu.VMEM((nseq, SUBLANES + rows, FFN_COLS), F32),
               pltpu.VMEM((tm, D_FF), BF16)]
    if has_state:
        ins.append(state)
        in_specs.append(st_spec)
    else:
        scratch.append(pltpu.VMEM((nseq, SUBLANES, 2 * D_FF), F32))
    return pl.pallas_call(
        functools.partial(_ffn_kernel, nseq=nseq, rows=rows, tiles_per_seq=tps, has_state=has_state),
        grid=(n // tm,),
        in_specs=in_specs,
        out_specs=[row(D_MODEL), st_spec],
        out_shape=[jax.ShapeDtypeStruct((n, D_MODEL), F32),
                   jax.ShapeDtypeStruct((nbatch, FFN_CONV - 1, 2 * D_FF), F32)],
        scratch_shapes=scratch,
        compiler_params=pltpu.CompilerParams(dimension_semantics=("arbitrary",),
                                             vmem_limit_bytes=VMEM_LIMIT),
        name="ffn_sample" if has_state else "ffn_prompt",
    )(*ins)


ROW_TILE = 512
SAMPLE_KV_BLOCK = 4096
GDN_CHUNK = 128
GDN_CHUNKS_PER_STEP = 2
GDN_SAMPLE_SEQS_PER_STEP = 4


def _prep_weights(w_in, b_fox_f, gdn_a_log, gdn_dt_bias):
    c0 = 3 * FOX_WIDTH
    c1 = c0 + FOX_HEADS
    c2 = c1 + GDN_CONV_CH
    c3 = c2 + 2 * GDN_HEADS
    w_in = w_in.astype(BF16)
    small = jnp.concatenate([w_in[:, c0:c1], w_in[:, c2:c3],
                             jnp.zeros((D_MODEL, LANES - FOX_HEADS - 2 * GDN_HEADS), BF16)], axis=1)
    w_cat = jnp.concatenate([w_in[:, :c0], w_in[:, c1:c2], w_in[:, c3:], small], axis=1)
    pad = jnp.zeros((LANES - FOX_HEADS - GDN_HEADS,), F32)
    bias = jnp.concatenate([b_fox_f.astype(F32), gdn_dt_bias.astype(F32), pad])
    alog = jnp.concatenate([jnp.zeros((FOX_HEADS,), F32), gdn_a_log.astype(F32), pad])
    prm = jnp.zeros((SUBLANES, LANES), F32).at[0].set(bias).at[1].set(alog)
    return w_cat, prm


def _layer(xp, xs, fox_k, fox_v, fox_logf, st_gdn, st_gconv, st_fconv,
           w_in, b_fox_f, gdn_conv_w, gdn_a_log, gdn_dt_bias, gdn_norm_w, w_o,
           norm_mix_w, norm_ffn_w, w_up, ffn_conv_w, ffn_conv_b, w_down, norm_out_w):
    bp, sp, _ = xp.shape
    bs, ts, _ = xs.shape
    w_cat, prm = _prep_weights(w_in, b_fox_f, gdn_a_log, gdn_dt_bias)
    nmw = norm_mix_w.reshape(1, D_MODEL).astype(F32)
    nfw = norm_ffn_w.reshape(1, D_MODEL).astype(F32)
    now = norm_out_w.reshape(1, D_MODEL).astype(F32)
    gnw = gdn_norm_w.reshape(1, GDN_DV).astype(F32)
    w_o_b, w_up_b, w_dn_b = w_o.astype(BF16), w_up.astype(BF16), w_down.astype(BF16)
    cb = ffn_conv_b.reshape(1, 2 * D_FF).astype(F32)

    xp2 = xp.reshape(bp * sp, D_MODEL)
    q, k, v, kb, qa, ka, va, gqkv, gz, sm = _inproj(xp2, nmw, w_cat, prm, tm=ROW_TILE, seq_len=sp, prompt=True)
    r3 = lambda a: a.reshape(bp, sp, a.shape[-1])
    o_fox = _fox_prompt(r3(q), r3(qa), r3(kb), r3(ka), va, t=ROW_TILE)
    gqkv3 = r3(gqkv)
    o_gdn, p_state = _gdn(gqkv3, None, r3(sm), r3(gz), None, gdn_conv_w, gnw,
                          c=GDN_CHUNK, nb=1, cpb=GDN_CHUNKS_PER_STEP)
    yp, p_fconv = _ffn(o_fox.reshape(bp * sp, FOX_WIDTH), o_gdn.reshape(bp * sp, GDN_WIDTH), xp2,
                       w_o_b, nfw, w_up_b, ffn_conv_w, cb, w_dn_b, now, None,
                       nseq=1, rows=ROW_TILE, seq_len=sp)
    p_out = (jnp.transpose(k, (0, 3, 1, 2)), jnp.transpose(v, (0, 3, 1, 2)),
             r3(sm)[:, :, LOGF_LANE:LOGF_LANE + FOX_HEADS], p_state,
             gqkv3[:, sp - (GDN_CONV - 1):], p_fconv)

    xs2 = xs.reshape(bs * ts, D_MODEL)
    q, k, v, gqkv, gz, sm = _inproj(xs2, nmw, w_cat, prm, tm=bs * ts, seq_len=ts, prompt=False)
    s3 = lambda a: a.reshape(bs, ts, a.shape[-1])
    past = fox_k.shape[1]
    lft = jnp.swapaxes(fox_logf.astype(F32), 1, 2)
    cache_t = lambda a: jnp.transpose(a.astype(F32), (0, 2, 3, 1))
    o_fox = _fox_sample(s3(q), s3(k), s3(v), s3(sm), lft, cache_t(fox_k), cache_t(fox_v),
                        kb=min(SAMPLE_KV_BLOCK, past))
    gqkv3 = s3(gqkv)
    hist = jnp.concatenate([jnp.zeros((bs, SUBLANES - (GDN_CONV - 1), GDN_CONV_CH), F32),
                            st_gconv.astype(F32)], axis=1)
    o_gdn, s_state = _gdn(gqkv3, hist, s3(sm), s3(gz), st_gdn.astype(F32), gdn_conv_w, gnw,
                          c=ts, nb=GDN_SAMPLE_SEQS_PER_STEP, cpb=1)
    ys, s_fconv = _ffn(o_fox.reshape(bs * ts, FOX_WIDTH), o_gdn.reshape(bs * ts, GDN_WIDTH), xs2,
                       w_o_b, nfw, w_up_b, ffn_conv_w, cb, w_dn_b, now, st_fconv.astype(F32),
                       nseq=bs, rows=ts, seq_len=ts)
    full = jnp.concatenate([st_gconv.astype(F32), gqkv3], axis=1)
    s_out = (k.reshape(bs, ts, FOX_HEADS, FOX_DIM), v.reshape(bs, ts, FOX_HEADS, FOX_DIM),
             s3(sm)[:, :, LOGF_LANE:LOGF_LANE + FOX_HEADS], s_state,
             full[:, ts:], s_fconv)
    return yp.reshape(bp, sp, D_MODEL), ys.reshape(bs, ts, D_MODEL), p_out, s_out


def kernel(x_prompt, x_sample, cache_fox_k, cache_fox_v, cache_fox_logf, state_gdn, state_gdn_conv,
           state_ffn_conv, w_in, b_fox_f, gdn_conv_w, gdn_a_log, gdn_dt_bias, gdn_norm_w, w_o, norm_mix_w,
           norm_ffn_w, w_up, ffn_conv_w, ffn_conv_b, w_down, norm_final_w):
    depth = w_in.shape[0]
    assert depth == 1, "the final RMSNorm is fused into the single layer's FFN kernel"
    yp, ys, p_out, s_out = _layer(
        x_prompt, x_sample, cache_fox_k[0], cache_fox_v[0], cache_fox_logf[0], state_gdn[0],
        state_gdn_conv[0], state_ffn_conv[0], w_in[0], b_fox_f[0], gdn_conv_w[0], gdn_a_log[0],
        gdn_dt_bias[0], gdn_norm_w[0], w_o[0], norm_mix_w[0], norm_ffn_w[0], w_up[0], ffn_conv_w[0],
        ffn_conv_b[0], w_down[0], norm_final_w)
    return (yp, ys) + tuple(a[None] for a in p_out) + tuple(a[None] for a in s_out)
```

```python
import functools
import math

import numpy as np
import jax
import jax.numpy as jnp
from jax import lax
from jax.experimental import pallas as pl
from jax.experimental.pallas import tpu as pltpu

F32 = jnp.float32
BF16 = jnp.bfloat16

D_MODEL = 1024
FOX_DIM = 64
FOX_HEADS = 8
FOX_WIDTH = FOX_HEADS * FOX_DIM
GDN_DK = 128
GDN_DV = 128
GDN_HEADS = 4
GDN_QK = GDN_HEADS * GDN_DK
GDN_WIDTH = GDN_HEADS * GDN_DV
GDN_CONV_CH = 2 * GDN_QK + GDN_WIDTH
GDN_CONV = 4
D_FF = 2816
FFN_CONV = 3
EPS = 1e-6

LANES = 128
SUBLANES = 8
VMEM_LIMIT = 56 * 1024 * 1024
MASKED = -1e30
LOG2E = math.log2(math.e)

LOGF_LANE = 0
G_LANE = FOX_HEADS
BETA_LANE = FOX_HEADS + GDN_HEADS

COL_FOX = 0
COL_GQKV = 3 * FOX_WIDTH
COL_GZ = COL_GQKV + GDN_CONV_CH
COL_SMALL = COL_GZ + GDN_WIDTH
N_COLS = COL_SMALL + LANES


def _dot(a, b):
    return jnp.dot(a, b, preferred_element_type=F32)


def _dot_nt(a, b):
    return lax.dot_general(a, b, (((1,), (1,)), ((), ())), preferred_element_type=F32)


def _dot_tn(a, b):
    return lax.dot_general(a, b, (((0,), (0,)), ((), ())), preferred_element_type=F32)


def _split3(x):
    hi = x.astype(BF16)
    r = x - hi.astype(F32)
    lo = r.astype(BF16)
    lo2 = (r - lo.astype(F32)).astype(BF16)
    return hi, lo, lo2


def _exact_left(mat, x):
    hi, lo, lo2 = _split3(x)
    return _dot(mat, hi) + _dot(mat, lo) + _dot(mat, lo2)


def _exact_right(x, mat):
    hi, lo, lo2 = _split3(x)
    return _dot(hi, mat) + _dot(lo, mat) + _dot(lo2, mat)


def _dot3(a, b):
    ah = a.astype(BF16)
    al = (a - ah.astype(F32)).astype(BF16)
    bh = b.astype(BF16)
    bl = (b - bh.astype(F32)).astype(BF16)
    return _dot(ah, bh) + (_dot(ah, bl) + _dot(al, bh))


def _exact_left_c(mat, x):
    return _dot(jnp.concatenate([mat, mat, mat], axis=1), jnp.concatenate(_split3(x), axis=0))


def _split2(x):
    hi = x.astype(BF16)
    return hi, (x - hi.astype(F32)).astype(BF16)


def _dot3_parts(ah, al, bh, bl):
    return _dot(jnp.concatenate([ah, ah, al], axis=1), jnp.concatenate([bh, bl, bh], axis=0))


def _dot3c(a, b):
    return _dot3_parts(*_split2(a), *_split2(b))


def _sigmoid(x):
    return 1.0 / (1.0 + jnp.exp(-x))


def _tri(n, strict=False):
    r = lax.broadcasted_iota(jnp.int32, (n, n), 0)
    c = lax.broadcasted_iota(jnp.int32, (n, n), 1)
    return (r > c) if strict else (r >= c)


def _inproj_kernel(x_ref, nw_ref, w_ref, prm_ref, *refs, tiles_per_seq, prompt):
    if prompt:
        (aug_ref, augc_ref, cw_ref, q_ref, k_ref, v_ref, kb_ref, qa_ref, ka_ref, va_ref, tail_ref,
         g_ref, z_ref, sm_ref, carry_ref, win_ref) = refs
    else:
        q_ref, k_ref, v_ref, g_ref, z_ref, sm_ref = refs
    x = x_ref[...]
    xn = x * lax.rsqrt(jnp.mean(x * x, axis=-1, keepdims=True) + EPS) * nw_ref[...]
    xb = xn.astype(BF16)

    q_scale = FOX_DIM ** -0.5 * (LOG2E if prompt else 1.0)
    q_ref[...] = (_dot(xb, w_ref[:, 0:FOX_WIDTH]) * q_scale).astype(BF16)
    kf = _dot(xb, w_ref[:, FOX_WIDTH:2 * FOX_WIDTH])
    vf = _dot(xb, w_ref[:, 2 * FOX_WIDTH:3 * FOX_WIDTH])
    if not prompt:
        k_ref[...] = kf
        v_ref[...] = vf
    else:
        tm = x.shape[0]
        vt = vf.T.reshape(FOX_HEADS, FOX_DIM, tm)
        k_ref[0] = kf.T.reshape(FOX_HEADS, FOX_DIM, tm)
        v_ref[0] = vt
        kb_ref[...] = kf.astype(BF16)
        va_ref[0, :, 0:FOX_DIM, :] = vt.astype(BF16)
        va_ref[0, :, FOX_DIM:, :] = jnp.ones((FOX_HEADS, ONES_ROWS, tm), BF16)
    gq = _dot(xb, w_ref[:, COL_GQKV:COL_GZ])
    if not prompt:
        g_ref[...] = gq
    else:
        first = pl.program_id(0) % tiles_per_seq == 0

        @pl.when(first)
        def _():
            win_ref[0:SUBLANES, :] = jnp.zeros((SUBLANES, GDN_CONV_CH), F32)

        @pl.when(jnp.logical_not(first))
        def _():
            win_ref[0:SUBLANES, :] = win_ref[tm:tm + SUBLANES, :]

        win_ref[SUBLANES:SUBLANES + tm, :] = gq
        tail_ref[0] = gq[tm - SUBLANES:tm, :]
        for col in range(0, GDN_CONV_CH, LANES):
            g_ref[:, col:col + LANES] = _gdn_activate(
                _short_conv(win_ref, cw_ref, SUBLANES - (GDN_CONV - 1), tm, col), col)
    z_ref[...] = _dot(xb, w_ref[:, COL_GZ:COL_SMALL])

    z = _dot(xb, w_ref[:, COL_SMALL:N_COLS])
    lane = lax.broadcasted_iota(jnp.int32, z.shape, 1)
    zb = z + prm_ref[0:1, :]
    e = jnp.exp(-jnp.abs(zb))
    l1p = jnp.log(1.0 + e)
    logf = jnp.minimum(zb, 0.0) - l1p
    g = -jnp.exp(prm_ref[1:2, :]) * (jnp.maximum(zb, 0.0) + l1p)
    beta = _sigmoid(z)
    sm = jnp.where(lane < G_LANE, logf,
                   jnp.where(lane < BETA_LANE, g,
                             jnp.where(lane < BETA_LANE + GDN_HEADS, beta, 0.0)))
    sm_ref[...] = sm

    if prompt:
        i = pl.program_id(0)

        @pl.when(i % tiles_per_seq == 0)
        def _():
            carry_ref[...] = jnp.zeros_like(carry_ref)

        c = _exact_left(_tri(tm).astype(BF16), sm) + carry_ref[0:1, :]
        carry_ref[0:1, :] = c[tm - 1:tm, :]
        parts = jnp.concatenate(_split3(c * LOG2E), axis=1)
        aug = _dot(parts, aug_ref[...]) + augc_ref[...]
        qa_ref[...] = aug[:, :FOX_WIDTH].astype(BF16)
        ka_ref[...] = aug[:, FOX_WIDTH:].astype(BF16)


AUG_LANES = 6
ONES_ROWS = 16


def _aug_tables():
    place = np.zeros((3 * LANES, 2 * FOX_WIDTH), np.float32)
    ones = np.zeros((1, 2 * FOX_WIDTH), np.float32)
    for h in range(FOX_HEADS):
        q0 = (h // 2) * LANES + (h % 2) * AUG_LANES
        k0 = FOX_WIDTH + q0
        for part in range(3):
            place[part * LANES + LOGF_LANE + h, q0 + part] = 1.0
            place[part * LANES + LOGF_LANE + h, k0 + 3 + part] = -1.0
            ones[0, q0 + 3 + part] = 1.0
            ones[0, k0 + part] = 1.0
    return jnp.asarray(place, BF16), jnp.asarray(ones, F32)


def _inproj(x2d, norm_w, w_cat, prm, conv_w=None, *, tm, seq_len, prompt):
    n = x2d.shape[0]
    assert n % tm == 0 and (not prompt or seq_len % tm == 0)
    tps = seq_len // tm if prompt else 1
    row = lambda w: pl.BlockSpec((tm, w), lambda i: (i, 0))
    const = lambda a: pl.BlockSpec(a.shape, lambda i: (0,) * a.ndim, pipeline_mode=pl.Buffered(1))
    ins = [x2d, norm_w, w_cat, prm]
    out_shape = [jax.ShapeDtypeStruct((n, FOX_WIDTH), BF16),
                 jax.ShapeDtypeStruct((n, FOX_WIDTH), F32),
                 jax.ShapeDtypeStruct((n, FOX_WIDTH), F32)]
    out_specs = [row(FOX_WIDTH), row(FOX_WIDTH), row(FOX_WIDTH)]
    scratch = []
    if prompt:
        kv_t = jax.ShapeDtypeStruct((n // seq_len, FOX_HEADS, FOX_DIM, seq_len), F32)
        out_shape[1:3] = [kv_t, kv_t]
        out_specs[1:3] = [pl.BlockSpec((1, FOX_HEADS, FOX_DIM, tm), lambda i: (i // tps, 0, 0, i % tps))] * 2
        ins += list(_aug_tables()) + [conv_w]
        out_shape += [jax.ShapeDtypeStruct((n, FOX_WIDTH), BF16)] * 3 + [
            jax.ShapeDtypeStruct((n // seq_len, FOX_HEADS, FOX_DIM + ONES_ROWS, seq_len), BF16)]
        out_specs += [row(FOX_WIDTH)] * 3 + [
            pl.BlockSpec((1, FOX_HEADS, FOX_DIM + ONES_ROWS, tm), lambda i: (i // tps, 0, 0, i % tps))]
        out_shape += [jax.ShapeDtypeStruct((n // seq_len, SUBLANES, GDN_CONV_CH), F32)]
        out_specs += [pl.BlockSpec((1, SUBLANES, GDN_CONV_CH), lambda i: (i // tps, 0, 0))]
        scratch = [pltpu.VMEM((SUBLANES, LANES), F32), pltpu.VMEM((SUBLANES + tm, GDN_CONV_CH), F32)]
    out_shape += [jax.ShapeDtypeStruct((n, GDN_CONV_CH), F32),
                  jax.ShapeDtypeStruct((n, GDN_WIDTH), F32),
                  jax.ShapeDtypeStruct((n, LANES), F32)]
    out_specs += [row(GDN_CONV_CH), row(GDN_WIDTH), row(LANES)]
    return pl.pallas_call(
        functools.partial(_inproj_kernel, tiles_per_seq=tps, prompt=prompt),
        grid=(n // tm,),
        in_specs=[row(D_MODEL)] + [const(a) for a in ins[1:]],
        out_specs=out_specs, out_shape=out_shape, scratch_shapes=scratch,
        compiler_params=pltpu.CompilerParams(dimension_semantics=("arbitrary",),
                                             vmem_limit_bytes=VMEM_LIMIT),
        name="inproj_prompt" if prompt else "inproj_sample",
    )(*ins)


def _fox_prompt_kernel(q_ref, qa_ref, k_ref, ka_ref, va_ref, o_ref, sa_ref, sb_ref, *, t):
    i = pl.program_id(2)
    q = q_ref[0]
    qa = qa_ref[0]
    lane = lax.broadcasted_iota(jnp.int32, (t, LANES), 1)
    low = lane < FOX_DIM
    zero = jnp.zeros_like(q)
    qcat = []
    for h in range(2):
        mine = (lane >= h * AUG_LANES) & (lane < (h + 1) * AUG_LANES)
        qcat.append(jnp.concatenate([jnp.where(low if h == 0 else jnp.logical_not(low), q, zero),
                                     jnp.where(mine, qa, zero)], axis=1))

    def logits(j, s_ref):
        start = pl.multiple_of(j * t, t)
        kcat = jnp.concatenate([k_ref[0, pl.ds(start, t), :], ka_ref[0, pl.ds(start, t), :]], axis=1)
        for h in range(2):
            s_ref[h] = _dot_nt(kcat, qcat[h])

    def consume(j, s_ref, carry, diag):
        start = pl.multiple_of(j * t, t)
        out = []
        for h in range(2):
            m, acc = carry[h]
            s = s_ref[h]
            if diag:
                key = lax.broadcasted_iota(jnp.int32, (t, t), 0)
                query = lax.broadcasted_iota(jnp.int32, (t, t), 1)
                s = jnp.where(key <= query, s, MASKED)
            m_new = jnp.maximum(m, jnp.max(s, axis=0, keepdims=True))
            p = jnp.exp2(s - m_new).astype(BF16)
            acc = jnp.exp2(m - m_new) * acc + _dot(va_ref[0, h, :, pl.ds(start, t)], p)
            out.append((m_new, acc))
        return tuple(out)

    init = tuple((jnp.full((1, t), MASKED, F32), jnp.zeros((FOX_DIM + ONES_ROWS, t), F32)) for _ in range(2))
    logits(0, sa_ref)

    def two_blocks(jj, carry):
        j = 2 * jj
        logits(j + 1, sb_ref)
        carry = consume(j, sa_ref, carry, False)
        logits(j + 2, sa_ref)
        return consume(j + 1, sb_ref, carry, False)

    carry = lax.fori_loop(0, i // 2, two_blocks, init)

    def odd_tail(carry):
        logits(i, sb_ref)
        return consume(i, sb_ref, consume(i - 1, sa_ref, carry, False), True)

    (_, a0), (_, a1) = lax.cond(i % 2 == 1, odd_tail, lambda c: consume(i, sa_ref, c, True), carry)
    o_t = jnp.concatenate([a[:FOX_DIM] / a[FOX_DIM:FOX_DIM + 1] for a in (a0, a1)], axis=0)
    o_ref[0] = o_t.T.astype(BF16)


def _fox_prompt(q, qa, kb, ka, va, *, t):
    b, s, _ = q.shape
    nblk = s // t
    assert s % t == 0
    npair = FOX_HEADS // 2
    qblk = pl.BlockSpec((1, t, LANES), lambda bi, p, i: (bi, i, p))
    kblk = pl.BlockSpec((1, s, LANES), lambda bi, p, i: (bi, 0, p))
    return pl.pallas_call(
        functools.partial(_fox_prompt_kernel, t=t),
        grid=(b, npair, nblk),
        in_specs=[qblk, qblk, kblk, kblk,
                  pl.BlockSpec((1, 2, FOX_DIM + ONES_ROWS, s), lambda bi, p, i: (bi, p, 0, 0))],
        out_specs=qblk,
        out_shape=jax.ShapeDtypeStruct((b, s, FOX_WIDTH), BF16),
        scratch_shapes=[pltpu.VMEM((2, t, t), F32), pltpu.VMEM((2, t, t), F32)],
        compiler_params=pltpu.CompilerParams(
            dimension_semantics=("arbitrary", "arbitrary", "arbitrary"), vmem_limit_bytes=VMEM_LIMIT),
        name="fox_prompt",
    )(q, qa, kb, ka, va)


def _fox_sample_kernel(q_ref, kn_ref, vn_ref, sm_ref, lft_ref, kt_ref, vt_ref, o_ref,
                       rk_ref, m_ref, l_ref, acc_ref, *, kb, t):
    j = pl.program_id(1)
    nkv = pl.num_programs(1)

    def heads(x):
        return [x[:, h * FOX_DIM:(h + 1) * FOX_DIM] for h in range(FOX_HEADS)]

    qh = [x.astype(BF16) for x in heads(q_ref[0].astype(F32))]

    lf_new = sm_ref[0]
    cn = _exact_left(_tri(t).astype(BF16), lf_new)

    @pl.when(j == 0)
    def _():
        p = lft_ref.shape[2]
        nb = p // LANES
        x = jnp.concatenate([lft_ref[0, :, i * LANES:(i + 1) * LANES] for i in range(nb)], axis=0)
        r = lax.broadcasted_iota(jnp.int32, (LANES, LANES), 0)
        cc = lax.broadcasted_iota(jnp.int32, (LANES, LANES), 1)
        y = _exact_right(x, (r > cc).astype(BF16))
        tot = jnp.sum(x, axis=1, keepdims=True)
        off = jnp.zeros((FOX_HEADS, 1), F32)
        for i in range(nb - 1, -1, -1):
            rk_ref[:, i * LANES:(i + 1) * LANES] = y[i * FOX_HEADS:(i + 1) * FOX_HEADS, :] + off
            off = off + tot[i * FOX_HEADS:(i + 1) * FOX_HEADS, :]
        m_ref[...] = jnp.full_like(m_ref, MASKED)
        l_ref[...] = jnp.zeros_like(l_ref)
        acc_ref[...] = jnp.zeros_like(acc_ref)

    def update(s, pv):
        m = m_ref[:, 0:1]
        m_new = jnp.maximum(m, jnp.max(s, axis=1, keepdims=True))
        alpha = jnp.exp(m - m_new)
        p = jnp.exp(s - m_new)
        l_ref[...] = jnp.broadcast_to(alpha * l_ref[:, 0:1] + jnp.sum(p, axis=1, keepdims=True), l_ref.shape)
        acc_ref[...] = alpha * acc_ref[...] + pv(p.astype(BF16))
        m_ref[...] = jnp.broadcast_to(m_new, m_ref.shape)

    start = pl.multiple_of(j * kb, kb)
    rk = rk_ref[:, pl.ds(start, kb)]
    s = jnp.concatenate([_dot(qh[h], kt_ref[0, h].astype(BF16)) + (cn[:, h:h + 1] + rk[h:h + 1, :])
                         for h in range(FOX_HEADS)], axis=0)
    update(s, lambda p: jnp.concatenate(
        [_dot_nt(p[h * t:(h + 1) * t], vt_ref[0, h].astype(BF16)) for h in range(FOX_HEADS)], axis=0))

    @pl.when(j == nkv - 1)
    def _():
        r = lax.broadcasted_iota(jnp.int32, (t, t), 0)
        cc = lax.broadcasted_iota(jnp.int32, (t, t), 1)
        after = (r > cc).astype(F32)
        tri = _tri(t).astype(BF16)
        knh = heads(kn_ref[0])
        vnh = [x.astype(BF16) for x in heads(vn_ref[0])]
        s_new = jnp.concatenate(
            [jnp.where(_tri(t), _dot_nt(qh[h], knh[h].astype(BF16))
                       + _exact_left(tri, lf_new[:, h:h + 1] * after), MASKED) for h in range(FOX_HEADS)], axis=0)
        update(s_new, lambda p: jnp.concatenate(
            [_dot(p[h * t:(h + 1) * t], vnh[h]) for h in range(FOX_HEADS)], axis=0))
        o_full = acc_ref[...] / l_ref[:, 0:1]
        o_ref[0] = jnp.concatenate([o_full[h * t:(h + 1) * t] for h in range(FOX_HEADS)], axis=1).astype(BF16)


def _fox_sample(q, kn, vn, sm, lft, kt, vt, *, kb):
    b, t, _ = q.shape
    p = kt.shape[3]
    assert p % kb == 0 and kb % LANES == 0
    rows = FOX_HEADS * t
    per_b = lambda shape: pl.BlockSpec((1,) + shape, lambda bi, j: (bi, 0, 0))
    cache = pl.BlockSpec((1, FOX_HEADS, FOX_DIM, kb), lambda bi, j: (bi, 0, 0, j))
    return pl.pallas_call(
        functools.partial(_fox_sample_kernel, kb=kb, t=t),
        grid=(b, p // kb),
        in_specs=[per_b((t, FOX_WIDTH)), per_b((t, FOX_WIDTH)), per_b((t, FOX_WIDTH)), per_b((t, LANES)),
                  per_b((FOX_HEADS, p)), cache, cache],
        out_specs=per_b((t, FOX_WIDTH)),
        out_shape=jax.ShapeDtypeStruct((b, t, FOX_WIDTH), BF16),
        scratch_shapes=[pltpu.VMEM((FOX_HEADS, p), F32),
                        pltpu.VMEM((rows, LANES), F32), pltpu.VMEM((rows, LANES), F32),
                        pltpu.VMEM((rows, FOX_DIM), F32)],
        compiler_params=pltpu.CompilerParams(dimension_semantics=("arbitrary", "arbitrary"),
                                             vmem_limit_bytes=VMEM_LIMIT),
        name="fox_sample",
    )(q, kn, vn, sm, lft, kt, vt)


def _interleave(gens):
    results = [None] * len(gens)
    alive = list(range(len(gens)))
    while alive:
        for idx in list(alive):
            try:
                next(gens[idx])
            except StopIteration as stop:
                results[idx] = stop.value
                alive.remove(idx)
    return results


def _gdn_activate(acc, col):
    x = acc * _sigmoid(acc)
    if col >= 2 * GDN_QK:
        return x
    scale = GDN_DK ** -0.5 if col < GDN_QK else 1.0
    return x * (lax.rsqrt(jnp.sum(x * x, axis=-1, keepdims=True) + EPS) * scale)


def _short_conv(win_ref, cw_ref, row0, nrows, col, lead=()):
    acc = None
    for tap in range(GDN_CONV):
        term = (win_ref[lead + (slice(row0 + tap, row0 + tap + nrows), slice(col, col + LANES))]
                * cw_ref[tap:tap + 1, col:col + LANES])
        acc = term if acc is None else acc + term
    return acc


def _gdn_local(qn, kn, vh, g, g_row, g_last, beta, c):
    wide = c % LANES == 0
    dot3 = _dot3c if wide else _dot3
    incl = _tri(c)
    strict = _tri(c, strict=True)
    eg = jnp.exp(g)
    decay = jnp.exp(jnp.where(incl, g - g_row, MASKED))
    kbeta = kn * beta
    kq = _dot_nt(jnp.concatenate([kbeta, qn], axis=0).astype(BF16), kn.astype(BF16))
    yield
    lmat = jnp.where(strict, kq[:c] * decay, 0.0)
    a_qk = kq[c:] * decay

    tinv = (incl & jnp.logical_not(strict)).astype(F32) - lmat
    if wide:
        lh, ll = _split2(lmat)
        power = _dot3_parts(lh, ll, lh, ll)
    else:
        power = dot3(lmat, lmat)
    yield
    span = 2
    while 2 * span < c:
        if wide:
            ph, pl_ = _split2(power)
            th, tl = _split2(tinv)
            both = _dot3_parts(ph, pl_, jnp.concatenate([th, ph], axis=1), jnp.concatenate([tl, pl_], axis=1))
            yield
            tinv = tinv + both[:, :c]
            power = both[:, c:]
        else:
            step = dot3(power, tinv)
            power = dot3(power, power)
            yield
            tinv = tinv + step
        span *= 2
    step = dot3(power, tinv)
    yield
    tinv = tinv + step

    sol = dot3(tinv, jnp.concatenate([vh * beta, kbeta * eg], axis=1))
    yield
    kd = kn * jnp.exp(g_last - g)
    return sol[:, :GDN_DV], sol[:, GDN_DV:], qn * eg, a_qk, (kd.T if wide else kd)


def _gdn_state_step(locals_, states, g_lasts, c):
    wide = c % LANES == 0
    ws = [_dot(jnp.concatenate([w, qeg], axis=0).astype(BF16), sh.astype(BF16))
          for (_, w, qeg, _, _), sh in zip(locals_, states)]
    outs, new_states = [], []
    for (u, _, _, a_qk, kd), wsh, sh, g_last in zip(locals_, ws, states, g_lasts):
        db = (u - wsh[:c]).astype(BF16)
        decayed = sh * jnp.exp(g_last)
        if wide:
            ak = _dot(jnp.concatenate([a_qk, kd], axis=0).astype(BF16), db)
            outs.append(wsh[c:] + ak[:c])
            new_states.append(decayed + ak[c:])
        else:
            outs.append(wsh[c:] + _dot(a_qk.astype(BF16), db))
            new_states.append(decayed + _dot_tn(kd.astype(BF16), db))
    return outs, new_states


def _gdn_kernel(cur_ref, *refs, c, nb, cpb, activated):
    if activated:
        sm_ref, z_ref, nw_ref, o_ref, sout_ref, s_ref = refs
    else:
        hist_ref, sm_ref, z_ref, s0_ref, cw_ref, nw_ref, o_ref, sout_ref, xs_ref, s_ref = refs
    i = pl.program_id(1)
    rows = c * cpb
    wide = c % LANES == 0

    @pl.when(i == 0)
    def _():
        if activated:
            s_ref[...] = jnp.zeros_like(s_ref)
        else:
            s_ref[...] = s0_ref[...]

    if activated:
        def qkv(b, r0, col):
            return cur_ref[b, r0:r0 + c, col:col + LANES]
    else:
        xs_ref[:, 0:SUBLANES, :] = hist_ref[...]
        xs_ref[:, SUBLANES:SUBLANES + rows, :] = cur_ref[...]
        base = SUBLANES - (GDN_CONV - 1)

        def qkv(b, r0, col):
            return _gdn_activate(_short_conv(xs_ref, cw_ref, base + r0, c, col, lead=(b,)), col)

    incl = _tri(c)
    eye = incl & jnp.logical_not(_tri(c, strict=True))
    gens, g_lasts = [], []
    for b in range(nb):
        for ck in range(cpb):
            r0 = ck * c
            sm = sm_ref[b, r0:r0 + c, :]
            if wide:
                gs = _exact_left_c(incl.astype(BF16), sm)
                gst = gs.T
            else:
                gs = _exact_left(incl.astype(BF16), sm)
            for h in range(GDN_HEADS):
                g = gs[:, G_LANE + h:G_LANE + h + 1]
                if wide:
                    g_row = gst[G_LANE + h:G_LANE + h + 1, :]
                else:
                    g_row = _exact_left(jnp.ones((c, c), BF16), jnp.where(eye, g, 0.0))
                g_last = gs[c - 1:c, G_LANE + h:G_LANE + h + 1]
                g_lasts.append(g_last)
                gens.append(_gdn_local(
                    qkv(b, r0, h * GDN_DK), qkv(b, r0, GDN_QK + h * GDN_DK),
                    qkv(b, r0, 2 * GDN_QK + h * GDN_DV), g, g_row, g_last,
                    sm[:, BETA_LANE + h:BETA_LANE + h + 1], c))
    locals_ = _interleave(gens)

    states = [s_ref[b, h] for b in range(nb) for h in range(GDN_HEADS)]
    for ck in range(cpb):
        r0 = ck * c
        pick = [(b * cpb + ck) * GDN_HEADS + h for b in range(nb) for h in range(GDN_HEADS)]
        outs, states = _gdn_state_step([locals_[u] for u in pick], states, [g_lasts[u] for u in pick], c)
        for idx, o in enumerate(outs):
            b, h = divmod(idx, GDN_HEADS)
            on = o * lax.rsqrt(jnp.mean(o * o, axis=-1, keepdims=True) + EPS) * nw_ref[...]
            zh = z_ref[b, r0:r0 + c, h * GDN_DV:(h + 1) * GDN_DV]
            o_ref[b, r0:r0 + c, h * GDN_DV:(h + 1) * GDN_DV] = (on * (zh * _sigmoid(zh))).astype(BF16)
    for idx, st in enumerate(states):
        b, h = divmod(idx, GDN_HEADS)
        s_ref[b, h] = st

    @pl.when(i == pl.num_programs(1) - 1)
    def _():
        sout_ref[...] = s_ref[...]


def _gdn(gqkv, hist, sm, gz, s0, conv_w, norm_w, *, c, nb, cpb):
    b, t, _ = gqkv.shape
    rows = c * cpb
    assert t % rows == 0 and b % nb == 0 and c % SUBLANES == 0
    activated = hist is None
    blk = lambda w: pl.BlockSpec((nb, rows, w), lambda bi, i: (bi, i, 0))
    const = lambda a: pl.BlockSpec(a.shape, lambda bi, i: (0,) * a.ndim)
    state_spec = pl.BlockSpec((nb, GDN_HEADS, GDN_DK, GDN_DV), lambda bi, i: (bi, 0, 0, 0))
    scratch = [pltpu.VMEM((nb, GDN_HEADS, GDN_DK, GDN_DV), F32)]
    if activated:
        ins = [gqkv, sm, gz, norm_w]
        in_specs = [blk(GDN_CONV_CH), blk(LANES), blk(GDN_WIDTH), const(norm_w)]
    else:
        assert t == rows
        ins = [gqkv, hist, sm, gz, s0, conv_w, norm_w]
        in_specs = [blk(GDN_CONV_CH), pl.BlockSpec((nb, SUBLANES, GDN_CONV_CH), lambda bi, i: (bi, 0, 0)),
                    blk(LANES), blk(GDN_WIDTH), state_spec, const(conv_w), const(norm_w)]
        scratch = [pltpu.VMEM((nb, SUBLANES + rows, GDN_CONV_CH), F32)] + scratch
    return pl.pallas_call(
        functools.partial(_gdn_kernel, c=c, nb=nb, cpb=cpb, activated=activated),
        grid=(b // nb, t // rows),
        in_specs=in_specs,
        out_specs=[blk(GDN_WIDTH), state_spec],
        out_shape=[jax.ShapeDtypeStruct((b, t, GDN_WIDTH), BF16),
                   jax.ShapeDtypeStruct((b, GDN_HEADS, GDN_DK, GDN_DV), F32)],
        scratch_shapes=scratch,
        compiler_params=pltpu.CompilerParams(dimension_semantics=("arbitrary", "arbitrary"),
                                             vmem_limit_bytes=VMEM_LIMIT),
        name="gdn_prompt" if activated else "gdn_sample",
    )(*ins)


FFN_COLS = 256


def _ffn_kernel(of_ref, og_ref, x_ref, wo_ref, nfw_ref, wup_ref, cw_ref, cb_ref, wdn_ref, nlw_ref, *refs,
                nseq, rows, tiles_per_seq, has_state):
    if has_state:
        st_ref, y_ref, nst_ref, wg_ref, wv_ref, act_ref = refs
    else:
        y_ref, nst_ref, wg_ref, wv_ref, act_ref, carry_ref = refs
    i = pl.program_id(0)
    hist0 = SUBLANES - (FFN_CONV - 1)

    h = (x_ref[...] + _dot(of_ref[...], wo_ref[0:FOX_WIDTH, :])
         + _dot(og_ref[...], wo_ref[FOX_WIDTH:FOX_WIDTH + GDN_WIDTH, :]))
    hn = (h * lax.rsqrt(jnp.mean(h * h, axis=-1, keepdims=True) + EPS) * nfw_ref[...]).astype(BF16)

    if not has_state:
        @pl.when(i % tiles_per_seq == 0)
        def _():
            carry_ref[...] = jnp.zeros_like(carry_ref)

    def conv_cols(buf_ref, col):
        up = _dot(hn, wup_ref[:, col:col + FFN_COLS])
        buf_ref[:, SUBLANES:SUBLANES + rows, :] = up.reshape(nseq, rows, FFN_COLS)
        if has_state:
            buf_ref[:, hist0:SUBLANES, :] = st_ref[:, :, col:col + FFN_COLS]
        else:
            buf_ref[:, hist0:SUBLANES, :] = carry_ref[:, hist0:SUBLANES, col:col + FFN_COLS]
        last = buf_ref[:, SUBLANES + rows - (FFN_CONV - 1):SUBLANES + rows, :]
        nst_ref[:, :, col:col + FFN_COLS] = last
        if not has_state:
            carry_ref[:, hist0:SUBLANES, col:col + FFN_COLS] = last
        uc = cb_ref[:, col:col + FFN_COLS][None]
        for tap in range(FFN_CONV):
            uc = uc + buf_ref[:, hist0 + tap:hist0 + tap + rows, :] * cw_ref[tap:tap + 1, col:col + FFN_COLS][None]
        return uc.reshape(nseq * rows, FFN_COLS)

    for ci in range(D_FF // FFN_COLS):
        col = ci * FFN_COLS
        gate = conv_cols(wg_ref, col)
        val = conv_cols(wv_ref, D_FF + col)
        act_ref[:, col:col + FFN_COLS] = (gate * _sigmoid(gate) * val).astype(BF16)

    h2 = h + _dot(act_ref[...], wdn_ref[...])
    y_ref[...] = h2 * lax.rsqrt(jnp.mean(h2 * h2, axis=-1, keepdims=True) + EPS) * nlw_ref[...]


def _ffn(ofox, ogdn, x2d, w_o, norm_ffn_w, w_up, conv_w, conv_b, w_down, norm_final_w, state, *,
         nseq, rows, seq_len):
    n = x2d.shape[0]
    tm = nseq * rows
    assert n % tm == 0 and seq_len % rows == 0 and rows % SUBLANES == 0 and D_FF % FFN_COLS == 0
    has_state = state is not None
    assert (rows == seq_len) if has_state else (nseq == 1)
    tps = seq_len // rows
    nbatch = n // seq_len
    row = lambda w: pl.BlockSpec((tm, w), lambda i: (i, 0))
    const = lambda a: pl.BlockSpec(a.shape, lambda i: (0,) * a.ndim, pipeline_mode=pl.Buffered(1))
    st_spec = pl.BlockSpec((nseq, FFN_CONV - 1, 2 * D_FF), lambda i: (i // tps, 0, 0))
    ins = [ofox, ogdn, x2d, w_o, norm_ffn_w, w_up, conv_w, conv_b, w_down, norm_final_w]
    in_specs = [row(FOX_WIDTH), row(GDN_WIDTH), row(D_MODEL)] + [const(a) for a in ins[3:]]
    scratch = [pltpu.VMEM((nseq, SUBLANES + rows, FFN_COLS), F32),
               pltpu.VMEM((nseq, SUBLANES + rows, FFN_COLS), F32),
               pltpu.VMEM((tm, D_FF), BF16)]
    if has_state:
        ins.append(state)
        in_specs.append(st_spec)
    else:
        scratch.append(pltpu.VMEM((nseq, SUBLANES, 2 * D_FF), F32))
    return pl.pallas_call(
        functools.partial(_ffn_kernel, nseq=nseq, rows=rows, tiles_per_seq=tps, has_state=has_state),
        grid=(n // tm,),
        in_specs=in_specs,
        out_specs=[row(D_MODEL), st_spec],
        out_shape=[jax.ShapeDtypeStruct((n, D_MODEL), F32),
                   jax.ShapeDtypeStruct((nbatch, FFN_CONV - 1, 2 * D_FF), F32)],
        scratch_shapes=scratch,
        compiler_params=pltpu.CompilerParams(dimension_semantics=("arbitrary",),
                                             vmem_limit_bytes=VMEM_LIMIT),
        name="ffn_sample" if has_state else "ffn_prompt",
    )(*ins)


ROW_TILE = 512
SAMPLE_KV_BLOCK = 4096
GDN_CHUNK = 128
GDN_CHUNKS_PER_STEP = 2
GDN_SAMPLE_SEQS_PER_STEP = 4


def _prep_weights(w_in, b_fox_f, gdn_a_log, gdn_dt_bias):
    c0 = 3 * FOX_WIDTH
    c1 = c0 + FOX_HEADS
    c2 = c1 + GDN_CONV_CH
    c3 = c2 + 2 * GDN_HEADS
    w_in = w_in.astype(BF16)
    small = jnp.concatenate([w_in[:, c0:c1], w_in[:, c2:c3],
                             jnp.zeros((D_MODEL, LANES - FOX_HEADS - 2 * GDN_HEADS), BF16)], axis=1)
    w_cat = jnp.concatenate([w_in[:, :c0], w_in[:, c1:c2], w_in[:, c3:], small], axis=1)
    pad = jnp.zeros((LANES - FOX_HEADS - GDN_HEADS,), F32)
    bias = jnp.concatenate([b_fox_f.astype(F32), gdn_dt_bias.astype(F32), pad])
    alog = jnp.concatenate([jnp.zeros((FOX_HEADS,), F32), gdn_a_log.astype(F32), pad])
    prm = jnp.zeros((SUBLANES, LANES), F32).at[0].set(bias).at[1].set(alog)
    return w_cat, prm


def _layer(xp, xs, fox_k, fox_v, fox_logf, st_gdn, st_gconv, st_fconv,
           w_in, b_fox_f, gdn_conv_w, gdn_a_log, gdn_dt_bias, gdn_norm_w, w_o,
           norm_mix_w, norm_ffn_w, w_up, ffn_conv_w, ffn_conv_b, w_down, norm_out_w):
    bp, sp, _ = xp.shape
    bs, ts, _ = xs.shape
    w_cat, prm = _prep_weights(w_in, b_fox_f, gdn_a_log, gdn_dt_bias)
    nmw = norm_mix_w.reshape(1, D_MODEL).astype(F32)
    nfw = norm_ffn_w.reshape(1, D_MODEL).astype(F32)
    now = norm_out_w.reshape(1, D_MODEL).astype(F32)
    gnw = gdn_norm_w.reshape(1, GDN_DV).astype(F32)
    w_o_b, w_up_b, w_dn_b = w_o.astype(BF16), w_up.astype(BF16), w_down.astype(BF16)
    cb = ffn_conv_b.reshape(1, 2 * D_FF).astype(F32)

    xp2 = xp.reshape(bp * sp, D_MODEL)
    q, k, v, kb, qa, ka, va, gtail, gact, gz, sm = _inproj(xp2, nmw, w_cat, prm, gdn_conv_w.astype(F32),
                                                          tm=ROW_TILE, seq_len=sp, prompt=True)
    r3 = lambda a: a.reshape(bp, sp, a.shape[-1])
    o_fox = _fox_prompt(r3(q), r3(qa), r3(kb), r3(ka), va, t=ROW_TILE)
    o_gdn, p_state = _gdn(r3(gact), None, r3(sm), r3(gz), None, None, gnw,
                          c=GDN_CHUNK, nb=1, cpb=GDN_CHUNKS_PER_STEP)
    yp, p_fconv = _ffn(o_fox.reshape(bp * sp, FOX_WIDTH), o_gdn.reshape(bp * sp, GDN_WIDTH), xp2,
                       w_o_b, nfw, w_up_b, ffn_conv_w, cb, w_dn_b, now, None,
                       nseq=1, rows=ROW_TILE, seq_len=sp)
    p_out = (jnp.transpose(k, (0, 3, 1, 2)), jnp.transpose(v, (0, 3, 1, 2)),
             r3(sm)[:, :, LOGF_LANE:LOGF_LANE + FOX_HEADS], p_state,
             gtail[:, SUBLANES - (GDN_CONV - 1):], p_fconv)

    xs2 = xs.reshape(bs * ts, D_MODEL)
    q, k, v, gqkv, gz, sm = _inproj(xs2, nmw, w_cat, prm, tm=bs * ts, seq_len=ts, prompt=False)
    s3 = lambda a: a.reshape(bs, ts, a.shape[-1])
    past = fox_k.shape[1]
    lft = jnp.swapaxes(fox_logf.astype(F32), 1, 2)
    cache_t = lambda a: jnp.transpose(a.astype(F32), (0, 2, 3, 1))
    o_fox = _fox_sample(s3(q), s3(k), s3(v), s3(sm), lft, cache_t(fox_k), cache_t(fox_v),
                        kb=min(SAMPLE_KV_BLOCK, past))
    gqkv3 = s3(gqkv)
    hist = jnp.concatenate([jnp.zeros((bs, SUBLANES - (GDN_CONV - 1), GDN_CONV_CH), F32),
                            st_gconv.astype(F32)], axis=1)
    o_gdn, s_state = _gdn(gqkv3, hist, s3(sm), s3(gz), st_gdn.astype(F32), gdn_conv_w, gnw,
                          c=ts, nb=GDN_SAMPLE_SEQS_PER_STEP, cpb=1)
    ys, s_fconv = _ffn(o_fox.reshape(bs * ts, FOX_WIDTH), o_gdn.reshape(bs * ts, GDN_WIDTH), xs2,
                       w_o_b, nfw, w_up_b, ffn_conv_w, cb, w_dn_b, now, st_fconv.astype(F32),
                       nseq=bs, rows=ts, seq_len=ts)
    full = jnp.concatenate([st_gconv.astype(F32), gqkv3], axis=1)
    s_out = (k.reshape(bs, ts, FOX_HEADS, FOX_DIM), v.reshape(bs, ts, FOX_HEADS, FOX_DIM),
             s3(sm)[:, :, LOGF_LANE:LOGF_LANE + FOX_HEADS], s_state,
             full[:, ts:], s_fconv)
    return yp.reshape(bp, sp, D_MODEL), ys.reshape(bs, ts, D_MODEL), p_out, s_out


def kernel(x_prompt, x_sample, cache_fox_k, cache_fox_v, cache_fox_logf, state_gdn, state_gdn_conv,
           state_ffn_conv, w_in, b_fox_f, gdn_conv_w, gdn_a_log, gdn_dt_bias, gdn_norm_w, w_o, norm_mix_w,
           norm_ffn_w, w_up, ffn_conv_w, ffn_conv_b, w_down, norm_final_w):
    depth = w_in.shape[0]
    assert depth == 1, "the final RMSNorm is fused into the single layer's FFN kernel"
    yp, ys, p_out, s_out = _layer(
        x_prompt, x_sample, cache_fox_k[0], cache_fox_v[0], cache_fox_logf[0], state_gdn[0],
        state_gdn_conv[0], state_ffn_conv[0], w_in[0], b_fox_f[0], gdn_conv_w[0], gdn_a_log[0],
        gdn_dt_bias[0], gdn_norm_w[0], w_o[0], norm_mix_w[0], norm_ffn_w[0], w_up[0], ffn_conv_w[0],
        ffn_conv_b[0], w_down[0], norm_final_w)
    return (yp, ys) + tuple(a[None] for a in p_out) + tuple(a[None] for a in s_out)
```

```python
import functools
import math

import numpy as np
import jax
import jax.numpy as jnp
from jax import lax
from jax.experimental import pallas as pl
from jax.experimental.pallas import tpu as pltpu

F32 = jnp.float32
BF16 = jnp.bfloat16

D_MODEL = 1024
FOX_DIM = 64
FOX_HEADS = 8
FOX_WIDTH = FOX_HEADS * FOX_DIM
GDN_DK = 128
GDN_DV = 128
GDN_HEADS = 4
GDN_QK = GDN_HEADS * GDN_DK
GDN_WIDTH = GDN_HEADS * GDN_DV
GDN_CONV_CH = 2 * GDN_QK + GDN_WIDTH
GDN_CONV = 4
D_FF = 2816
FFN_CONV = 3
EPS = 1e-6

LANES = 128
SUBLANES = 8
VMEM_LIMIT = 56 * 1024 * 1024
MASKED = -1e30
LOG2E = math.log2(math.e)

LOGF_LANE = 0
G_LANE = FOX_HEADS
BETA_LANE = FOX_HEADS + GDN_HEADS

COL_FOX = 0
COL_GQKV = 3 * FOX_WIDTH
COL_GZ = COL_GQKV + GDN_CONV_CH
COL_SMALL = COL_GZ + GDN_WIDTH
N_COLS = COL_SMALL + LANES


def _dot(a, b):
    return jnp.dot(a, b, preferred_element_type=F32)


def _dot_nt(a, b):
    return lax.dot_general(a, b, (((1,), (1,)), ((), ())), preferred_element_type=F32)


def _dot_tn(a, b):
    return lax.dot_general(a, b, (((0,), (0,)), ((), ())), preferred_element_type=F32)


def _split3(x):
    hi = x.astype(BF16)
    r = x - hi.astype(F32)
    lo = r.astype(BF16)
    lo2 = (r - lo.astype(F32)).astype(BF16)
    return hi, lo, lo2


def _exact_left(mat, x):
    hi, lo, lo2 = _split3(x)
    return _dot(mat, hi) + _dot(mat, lo) + _dot(mat, lo2)


def _exact_right(x, mat):
    hi, lo, lo2 = _split3(x)
    return _dot(hi, mat) + _dot(lo, mat) + _dot(lo2, mat)


def _dot3(a, b):
    ah = a.astype(BF16)
    al = (a - ah.astype(F32)).astype(BF16)
    bh = b.astype(BF16)
    bl = (b - bh.astype(F32)).astype(BF16)
    return _dot(ah, bh) + (_dot(ah, bl) + _dot(al, bh))


def _exact_left_c(mat, x):
    return _dot(jnp.concatenate([mat, mat, mat], axis=1), jnp.concatenate(_split3(x), axis=0))


def _split2(x):
    hi = x.astype(BF16)
    return hi, (x - hi.astype(F32)).astype(BF16)


def _dot3_parts(ah, al, bh, bl):
    return _dot(jnp.concatenate([ah, ah, al], axis=1), jnp.concatenate([bh, bl, bh], axis=0))


def _dot3c(a, b):
    return _dot3_parts(*_split2(a), *_split2(b))


def _sigmoid(x):
    return 1.0 / (1.0 + jnp.exp(-x))


def _tri(n, strict=False):
    r = lax.broadcasted_iota(jnp.int32, (n, n), 0)
    c = lax.broadcasted_iota(jnp.int32, (n, n), 1)
    return (r > c) if strict else (r >= c)


def _inproj_kernel(x_ref, nw_ref, w_ref, prm_ref, *refs, tiles_per_seq, prompt):
    if prompt:
        (aug_ref, augc_ref, cw_ref, q_ref, k_ref, v_ref, kb_ref, qa_ref, ka_ref, va_ref, tail_ref,
         g_ref, z_ref, sm_ref, carry_ref, win_ref) = refs
    else:
        q_ref, k_ref, v_ref, g_ref, z_ref, sm_ref = refs
    x = x_ref[...]
    xn = x * lax.rsqrt(jnp.mean(x * x, axis=-1, keepdims=True) + EPS) * nw_ref[...]
    xb = xn.astype(BF16)

    tm = x.shape[0]
    gq = _dot(xb, w_ref[:, COL_GQKV:COL_GZ])
    conv_cols = []
    if not prompt:
        g_ref[...] = gq
    else:
        first = pl.program_id(0) % tiles_per_seq == 0

        @pl.when(first)
        def _():
            win_ref[0:SUBLANES, :] = jnp.zeros((SUBLANES, GDN_CONV_CH), F32)

        @pl.when(jnp.logical_not(first))
        def _():
            win_ref[0:SUBLANES, :] = win_ref[tm:tm + SUBLANES, :]

        win_ref[SUBLANES:SUBLANES + tm, :] = gq
        tail_ref[0] = gq[tm - SUBLANES:tm, :]
        conv_cols = list(range(0, GDN_CONV_CH, LANES))

    def conv_some(count):
        for _ in range(min(count, len(conv_cols))):
            col = conv_cols.pop(0)
            g_ref[:, col:col + LANES] = _gdn_activate(
                _short_conv(win_ref, cw_ref, SUBLANES - (GDN_CONV - 1), tm, col), col)

    q_scale = FOX_DIM ** -0.5 * (LOG2E if prompt else 1.0)
    q_ref[...] = (_dot(xb, w_ref[:, 0:FOX_WIDTH]) * q_scale).astype(BF16)
    conv_some(3)
    kf = _dot(xb, w_ref[:, FOX_WIDTH:2 * FOX_WIDTH])
    if not prompt:
        k_ref[...] = kf
    else:
        k_ref[0] = kf.T.reshape(FOX_HEADS, FOX_DIM, tm)
        kb_ref[...] = kf.astype(BF16)
    conv_some(3)
    vf = _dot(xb, w_ref[:, 2 * FOX_WIDTH:3 * FOX_WIDTH])
    if not prompt:
        v_ref[...] = vf
    else:
        vt = vf.T.reshape(FOX_HEADS, FOX_DIM, tm)
        v_ref[0] = vt
        va_ref[0, :, 0:FOX_DIM, :] = vt.astype(BF16)
        va_ref[0, :, FOX_DIM:, :] = jnp.ones((FOX_HEADS, ONES_ROWS, tm), BF16)
    conv_some(3)
    z_ref[...] = _dot(xb, w_ref[:, COL_GZ:COL_SMALL])
    conv_some(3)

    z = _dot(xb, w_ref[:, COL_SMALL:N_COLS])
    lane = lax.broadcasted_iota(jnp.int32, z.shape, 1)
    zb = z + prm_ref[0:1, :]
    e = jnp.exp(-jnp.abs(zb))
    l1p = jnp.log(1.0 + e)
    logf = jnp.minimum(zb, 0.0) - l1p
    g = -jnp.exp(prm_ref[1:2, :]) * (jnp.maximum(zb, 0.0) + l1p)
    beta = _sigmoid(z)
    sm = jnp.where(lane < G_LANE, logf,
                   jnp.where(lane < BETA_LANE, g,
                             jnp.where(lane < BETA_LANE + GDN_HEADS, beta, 0.0)))
    sm_ref[...] = sm

    if prompt:
        i = pl.program_id(0)

        @pl.when(i % tiles_per_seq == 0)
        def _():
            carry_ref[...] = jnp.zeros_like(carry_ref)

        c = _exact_left(_tri(tm).astype(BF16), sm) + carry_ref[0:1, :]
        carry_ref[0:1, :] = c[tm - 1:tm, :]
        parts = jnp.concatenate(_split3(c * LOG2E), axis=1)
        aug = _dot(parts, aug_ref[...]) + augc_ref[...]
        qa_ref[...] = aug[:, :FOX_WIDTH].astype(BF16)
        ka_ref[...] = aug[:, FOX_WIDTH:].astype(BF16)


AUG_LANES = 6
ONES_ROWS = 16


def _aug_tables():
    place = np.zeros((3 * LANES, 2 * FOX_WIDTH), np.float32)
    ones = np.zeros((1, 2 * FOX_WIDTH), np.float32)
    for h in range(FOX_HEADS):
        q0 = (h // 2) * LANES + (h % 2) * AUG_LANES
        k0 = FOX_WIDTH + q0
        for part in range(3):
            place[part * LANES + LOGF_LANE + h, q0 + part] = 1.0
            place[part * LANES + LOGF_LANE + h, k0 + 3 + part] = -1.0
            ones[0, q0 + 3 + part] = 1.0
            ones[0, k0 + part] = 1.0
    return jnp.asarray(place, BF16), jnp.asarray(ones, F32)


def _inproj(x2d, norm_w, w_cat, prm, conv_w=None, *, tm, seq_len, prompt):
    n = x2d.shape[0]
    assert n % tm == 0 and (not prompt or seq_len % tm == 0)
    tps = seq_len // tm if prompt else 1
    row = lambda w: pl.BlockSpec((tm, w), lambda i: (i, 0))
    const = lambda a: pl.BlockSpec(a.shape, lambda i: (0,) * a.ndim, pipeline_mode=pl.Buffered(1))
    ins = [x2d, norm_w, w_cat, prm]
    out_shape = [jax.ShapeDtypeStruct((n, FOX_WIDTH), BF16),
                 jax.ShapeDtypeStruct((n, FOX_WIDTH), F32),
                 jax.ShapeDtypeStruct((n, FOX_WIDTH), F32)]
    out_specs = [row(FOX_WIDTH), row(FOX_WIDTH), row(FOX_WIDTH)]
    scratch = []
    if prompt:
        kv_t = jax.ShapeDtypeStruct((n // seq_len, FOX_HEADS, FOX_DIM, seq_len), F32)
        out_shape[1:3] = [kv_t, kv_t]
        out_specs[1:3] = [pl.BlockSpec((1, FOX_HEADS, FOX_DIM, tm), lambda i: (i // tps, 0, 0, i % tps))] * 2
        ins += list(_aug_tables()) + [conv_w]
        out_shape += [jax.ShapeDtypeStruct((n, FOX_WIDTH), BF16)] * 3 + [
            jax.ShapeDtypeStruct((n // seq_len, FOX_HEADS, FOX_DIM + ONES_ROWS, seq_len), BF16)]
        out_specs += [row(FOX_WIDTH)] * 3 + [
            pl.BlockSpec((1, FOX_HEADS, FOX_DIM + ONES_ROWS, tm), lambda i: (i // tps, 0, 0, i % tps))]
        out_shape += [jax.ShapeDtypeStruct((n // seq_len, SUBLANES, GDN_CONV_CH), F32)]
        out_specs += [pl.BlockSpec((1, SUBLANES, GDN_CONV_CH), lambda i: (i // tps, 0, 0))]
        scratch = [pltpu.VMEM((SUBLANES, LANES), F32), pltpu.VMEM((SUBLANES + tm, GDN_CONV_CH), F32)]
    out_shape += [jax.ShapeDtypeStruct((n, GDN_CONV_CH), F32),
                  jax.ShapeDtypeStruct((n, GDN_WIDTH), F32),
                  jax.ShapeDtypeStruct((n, LANES), F32)]
    out_specs += [row(GDN_CONV_CH), row(GDN_WIDTH), row(LANES)]
    return pl.pallas_call(
        functools.partial(_inproj_kernel, tiles_per_seq=tps, prompt=prompt),
        grid=(n // tm,),
        in_specs=[row(D_MODEL)] + [const(a) for a in ins[1:]],
        out_specs=out_specs, out_shape=out_shape, scratch_shapes=scratch,
        compiler_params=pltpu.CompilerParams(dimension_semantics=("arbitrary",),
                                             vmem_limit_bytes=VMEM_LIMIT),
        name="inproj_prompt" if prompt else "inproj_sample",
    )(*ins)


def _fox_prompt_kernel(q_ref, qa_ref, k_ref, ka_ref, va_ref, o_ref, sa_ref, sb_ref, *, t):
    i = pl.program_id(2)
    q = q_ref[0]
    qa = qa_ref[0]
    lane = lax.broadcasted_iota(jnp.int32, (t, LANES), 1)
    low = lane < FOX_DIM
    zero = jnp.zeros_like(q)
    qcat = []
    for h in range(2):
        mine = (lane >= h * AUG_LANES) & (lane < (h + 1) * AUG_LANES)
        qcat.append(jnp.concatenate([jnp.where(low if h == 0 else jnp.logical_not(low), q, zero),
                                     jnp.where(mine, qa, zero)], axis=1))

    def logits(j, s_ref):
        start = pl.multiple_of(j * t, t)
        kcat = jnp.concatenate([k_ref[0, pl.ds(start, t), :], ka_ref[0, pl.ds(start, t), :]], axis=1)
        for h in range(2):
            s_ref[h] = _dot_nt(kcat, qcat[h])

    def consume(j, s_ref, carry, diag):
        start = pl.multiple_of(j * t, t)
        out = []
        for h in range(2):
            m, acc = carry[h]
            s = s_ref[h]
            if diag:
                key = lax.broadcasted_iota(jnp.int32, (t, t), 0)
                query = lax.broadcasted_iota(jnp.int32, (t, t), 1)
                s = jnp.where(key <= query, s, MASKED)
            m_new = jnp.maximum(m, jnp.max(s, axis=0, keepdims=True))
            p = jnp.exp2(s - m_new).astype(BF16)
            acc = jnp.exp2(m - m_new) * acc + _dot(va_ref[0, h, :, pl.ds(start, t)], p)
            out.append((m_new, acc))
        return tuple(out)

    init = tuple((jnp.full((1, t), MASKED, F32), jnp.zeros((FOX_DIM + ONES_ROWS, t), F32)) for _ in range(2))
    logits(0, sa_ref)

    def two_blocks(jj, carry):
        j = 2 * jj
        logits(j + 1, sb_ref)
        carry = consume(j, sa_ref, carry, False)
        logits(j + 2, sa_ref)
        return consume(j + 1, sb_ref, carry, False)

    carry = lax.fori_loop(0, i // 2, two_blocks, init)

    def odd_tail(carry):
        logits(i, sb_ref)
        return consume(i, sb_ref, consume(i - 1, sa_ref, carry, False), True)

    (_, a0), (_, a1) = lax.cond(i % 2 == 1, odd_tail, lambda c: consume(i, sa_ref, c, True), carry)
    o_t = jnp.concatenate([a[:FOX_DIM] / a[FOX_DIM:FOX_DIM + 1] for a in (a0, a1)], axis=0)
    o_ref[0] = o_t.T.astype(BF16)


def _fox_prompt(q, qa, kb, ka, va, *, t):
    b, s, _ = q.shape
    nblk = s // t
    assert s % t == 0
    npair = FOX_HEADS // 2
    qblk = pl.BlockSpec((1, t, LANES), lambda bi, p, i: (bi, i, p))
    kblk = pl.BlockSpec((1, s, LANES), lambda bi, p, i: (bi, 0, p))
    return pl.pallas_call(
        functools.partial(_fox_prompt_kernel, t=t),
        grid=(b, npair, nblk),
        in_specs=[qblk, qblk, kblk, kblk,
                  pl.BlockSpec((1, 2, FOX_DIM + ONES_ROWS, s), lambda bi, p, i: (bi, p, 0, 0))],
        out_specs=qblk,
        out_shape=jax.ShapeDtypeStruct((b, s, FOX_WIDTH), BF16),
        scratch_shapes=[pltpu.VMEM((2, t, t), F32), pltpu.VMEM((2, t, t), F32)],
        compiler_params=pltpu.CompilerParams(
            dimension_semantics=("arbitrary", "arbitrary", "arbitrary"), vmem_limit_bytes=VMEM_LIMIT),
        name="fox_prompt",
    )(q, qa, kb, ka, va)


def _fox_sample_kernel(q_ref, kn_ref, vn_ref, sm_ref, lft_ref, kt_ref, vt_ref, o_ref,
                       rk_ref, m_ref, l_ref, acc_ref, *, kb, t):
    j = pl.program_id(1)
    nkv = pl.num_programs(1)

    def heads(x):
        return [x[:, h * FOX_DIM:(h + 1) * FOX_DIM] for h in range(FOX_HEADS)]

    qh = [x.astype(BF16) for x in heads(q_ref[0].astype(F32))]

    lf_new = sm_ref[0]
    cn = _exact_left(_tri(t).astype(BF16), lf_new)

    @pl.when(j == 0)
    def _():
        p = lft_ref.shape[2]
        nb = p // LANES
        x = jnp.concatenate([lft_ref[0, :, i * LANES:(i + 1) * LANES] for i in range(nb)], axis=0)
        r = lax.broadcasted_iota(jnp.int32, (LANES, LANES), 0)
        cc = lax.broadcasted_iota(jnp.int32, (LANES, LANES), 1)
        y = _exact_right(x, (r > cc).astype(BF16))
        tot = jnp.sum(x, axis=1, keepdims=True)
        off = jnp.zeros((FOX_HEADS, 1), F32)
        for i in range(nb - 1, -1, -1):
            rk_ref[:, i * LANES:(i + 1) * LANES] = y[i * FOX_HEADS:(i + 1) * FOX_HEADS, :] + off
            off = off + tot[i * FOX_HEADS:(i + 1) * FOX_HEADS, :]
        m_ref[...] = jnp.full_like(m_ref, MASKED)
        l_ref[...] = jnp.zeros_like(l_ref)
        acc_ref[...] = jnp.zeros_like(acc_ref)

    def update(s, pv):
        m = m_ref[:, 0:1]
        m_new = jnp.maximum(m, jnp.max(s, axis=1, keepdims=True))
        alpha = jnp.exp(m - m_new)
        p = jnp.exp(s - m_new)
        l_ref[...] = jnp.broadcast_to(alpha * l_ref[:, 0:1] + jnp.sum(p, axis=1, keepdims=True), l_ref.shape)
        acc_ref[...] = alpha * acc_ref[...] + pv(p.astype(BF16))
        m_ref[...] = jnp.broadcast_to(m_new, m_ref.shape)

    start = pl.multiple_of(j * kb, kb)
    rk = rk_ref[:, pl.ds(start, kb)]
    s = jnp.concatenate([_dot(qh[h], kt_ref[0, h].astype(BF16)) + (cn[:, h:h + 1] + rk[h:h + 1, :])
                         for h in range(FOX_HEADS)], axis=0)
    update(s, lambda p: jnp.concatenate(
        [_dot_nt(p[h * t:(h + 1) * t], vt_ref[0, h].astype(BF16)) for h in range(FOX_HEADS)], axis=0))

    @pl.when(j == nkv - 1)
    def _():
        r = lax.broadcasted_iota(jnp.int32, (t, t), 0)
        cc = lax.broadcasted_iota(jnp.int32, (t, t), 1)
        after = (r > cc).astype(F32)
        tri = _tri(t).astype(BF16)
        knh = heads(kn_ref[0])
        vnh = [x.astype(BF16) for x in heads(vn_ref[0])]
        s_new = jnp.concatenate(
            [jnp.where(_tri(t), _dot_nt(qh[h], knh[h].astype(BF16))
                       + _exact_left(tri, lf_new[:, h:h + 1] * after), MASKED) for h in range(FOX_HEADS)], axis=0)
        update(s_new, lambda p: jnp.concatenate(
            [_dot(p[h * t:(h + 1) * t], vnh[h]) for h in range(FOX_HEADS)], axis=0))
        o_full = acc_ref[...] / l_ref[:, 0:1]
        o_ref[0] = jnp.concatenate([o_full[h * t:(h + 1) * t] for h in range(FOX_HEADS)], axis=1).astype(BF16)


def _fox_sample(q, kn, vn, sm, lft, kt, vt, *, kb):
    b, t, _ = q.shape
    p = kt.shape[3]
    assert p % kb == 0 and kb % LANES == 0
    rows = FOX_HEADS * t
    per_b = lambda shape: pl.BlockSpec((1,) + shape, lambda bi, j: (bi, 0, 0))
    cache = pl.BlockSpec((1, FOX_HEADS, FOX_DIM, kb), lambda bi, j: (bi, 0, 0, j))
    return pl.pallas_call(
        functools.partial(_fox_sample_kernel, kb=kb, t=t),
        grid=(b, p // kb),
        in_specs=[per_b((t, FOX_WIDTH)), per_b((t, FOX_WIDTH)), per_b((t, FOX_WIDTH)), per_b((t, LANES)),
                  per_b((FOX_HEADS, p)), cache, cache],
        out_specs=per_b((t, FOX_WIDTH)),
        out_shape=jax.ShapeDtypeStruct((b, t, FOX_WIDTH), BF16),
        scratch_shapes=[pltpu.VMEM((FOX_HEADS, p), F32),
                        pltpu.VMEM((rows, LANES), F32), pltpu.VMEM((rows, LANES), F32),
                        pltpu.VMEM((rows, FOX_DIM), F32)],
        compiler_params=pltpu.CompilerParams(dimension_semantics=("arbitrary", "arbitrary"),
                                             vmem_limit_bytes=VMEM_LIMIT),
        name="fox_sample",
    )(q, kn, vn, sm, lft, kt, vt)


def _interleave(gens):
    results = [None] * len(gens)
    alive = list(range(len(gens)))
    while alive:
        for idx in list(alive):
            try:
                next(gens[idx])
            except StopIteration as stop:
                results[idx] = stop.value
                alive.remove(idx)
    return results


def _gdn_activate(acc, col):
    x = acc * _sigmoid(acc)
    if col >= 2 * GDN_QK:
        return x
    scale = GDN_DK ** -0.5 if col < GDN_QK else 1.0
    return x * (lax.rsqrt(jnp.sum(x * x, axis=-1, keepdims=True) + EPS) * scale)


def _short_conv(win_ref, cw_ref, row0, nrows, col, lead=()):
    acc = None
    for tap in range(GDN_CONV):
        term = (win_ref[lead + (slice(row0 + tap, row0 + tap + nrows), slice(col, col + LANES))]
                * cw_ref[tap:tap + 1, col:col + LANES])
        acc = term if acc is None else acc + term
    return acc


def _gdn_local(qn, kn, vh, g, g_row, g_last, beta, c):
    wide = c % LANES == 0
    dot3 = _dot3c if wide else _dot3
    incl = _tri(c)
    strict = _tri(c, strict=True)
    eg = jnp.exp(g)
    decay = jnp.exp(jnp.where(incl, g - g_row, MASKED))
    kbeta = kn * beta
    kq = _dot_nt(jnp.concatenate([kbeta, qn], axis=0).astype(BF16), kn.astype(BF16))
    yield
    lmat = jnp.where(strict, kq[:c] * decay, 0.0)
    a_qk = kq[c:] * decay

    tinv = (incl & jnp.logical_not(strict)).astype(F32) - lmat
    if wide:
        lh, ll = _split2(lmat)
        power = _dot3_parts(lh, ll, lh, ll)
    else:
        power = dot3(lmat, lmat)
    yield
    span = 2
    while 2 * span < c:
        if wide:
            ph, pl_ = _split2(power)
            th, tl = _split2(tinv)
            both = _dot3_parts(ph, pl_, jnp.concatenate([th, ph], axis=1), jnp.concatenate([tl, pl_], axis=1))
            yield
            tinv = tinv + both[:, :c]
            power = both[:, c:]
        else:
            step = dot3(power, tinv)
            power = dot3(power, power)
            yield
            tinv = tinv + step
        span *= 2
    step = dot3(power, tinv)
    yield
    tinv = tinv + step

    sol = dot3(tinv, jnp.concatenate([vh * beta, kbeta * eg], axis=1))
    yield
    kd = kn * jnp.exp(g_last - g)
    return sol[:, :GDN_DV], sol[:, GDN_DV:], qn * eg, a_qk, (kd.T if wide else kd)


def _gdn_state_step(locals_, states, g_lasts, c):
    wide = c % LANES == 0
    ws = [_dot(jnp.concatenate([w, qeg], axis=0).astype(BF16), sh.astype(BF16))
          for (_, w, qeg, _, _), sh in zip(locals_, states)]
    outs, new_states = [], []
    for (u, _, _, a_qk, kd), wsh, sh, g_last in zip(locals_, ws, states, g_lasts):
        db = (u - wsh[:c]).astype(BF16)
        decayed = sh * jnp.exp(g_last)
        if wide:
            ak = _dot(jnp.concatenate([a_qk, kd], axis=0).astype(BF16), db)
            outs.append(wsh[c:] + ak[:c])
            new_states.append(decayed + ak[c:])
        else:
            outs.append(wsh[c:] + _dot(a_qk.astype(BF16), db))
            new_states.append(decayed + _dot_tn(kd.astype(BF16), db))
    return outs, new_states


def _gdn_kernel(cur_ref, *refs, c, nb, cpb, activated):
    if activated:
        sm_ref, z_ref, nw_ref, o_ref, sout_ref, s_ref = refs
    else:
        hist_ref, sm_ref, z_ref, s0_ref, cw_ref, nw_ref, o_ref, sout_ref, xs_ref, s_ref = refs
    i = pl.program_id(1)
    rows = c * cpb
    wide = c % LANES == 0

    @pl.when(i == 0)
    def _():
        if activated:
            s_ref[...] = jnp.zeros_like(s_ref)
        else:
            s_ref[...] = s0_ref[...]

    if activated:
        def qkv(b, r0, col):
            return cur_ref[b, r0:r0 + c, col:col + LANES]
    else:
        xs_ref[:, 0:SUBLANES, :] = hist_ref[...]
        xs_ref[:, SUBLANES:SUBLANES + rows, :] = cur_ref[...]
        base = SUBLANES - (GDN_CONV - 1)

        def qkv(b, r0, col):
            return _gdn_activate(_short_conv(xs_ref, cw_ref, base + r0, c, col, lead=(b,)), col)

    incl = _tri(c)
    eye = incl & jnp.logical_not(_tri(c, strict=True))
    gens, g_lasts = [], []
    for b in range(nb):
        for ck in range(cpb):
            r0 = ck * c
            sm = sm_ref[b, r0:r0 + c, :]
            if wide:
                gs = _exact_left_c(incl.astype(BF16), sm)
                gst = gs.T
            else:
                gs = _exact_left(incl.astype(BF16), sm)
            for h in range(GDN_HEADS):
                g = gs[:, G_LANE + h:G_LANE + h + 1]
                if wide:
                    g_row = gst[G_LANE + h:G_LANE + h + 1, :]
                else:
                    g_row = _exact_left(jnp.ones((c, c), BF16), jnp.where(eye, g, 0.0))
                g_last = gs[c - 1:c, G_LANE + h:G_LANE + h + 1]
                g_lasts.append(g_last)
                gens.append(_gdn_local(
                    qkv(b, r0, h * GDN_DK), qkv(b, r0, GDN_QK + h * GDN_DK),
                    qkv(b, r0, 2 * GDN_QK + h * GDN_DV), g, g_row, g_last,
                    sm[:, BETA_LANE + h:BETA_LANE + h + 1], c))
    locals_ = _interleave(gens)

    states = [s_ref[b, h] for b in range(nb) for h in range(GDN_HEADS)]
    for ck in range(cpb):
        r0 = ck * c
        pick = [(b * cpb + ck) * GDN_HEADS + h for b in range(nb) for h in range(GDN_HEADS)]
        outs, states = _gdn_state_step([locals_[u] for u in pick], states, [g_lasts[u] for u in pick], c)
        for idx, o in enumerate(outs):
            b, h = divmod(idx, GDN_HEADS)
            on = o * lax.rsqrt(jnp.mean(o * o, axis=-1, keepdims=True) + EPS) * nw_ref[...]
            zh = z_ref[b, r0:r0 + c, h * GDN_DV:(h + 1) * GDN_DV]
            o_ref[b, r0:r0 + c, h * GDN_DV:(h + 1) * GDN_DV] = (on * (zh * _sigmoid(zh))).astype(BF16)
    for idx, st in enumerate(states):
        b, h = divmod(idx, GDN_HEADS)
        s_ref[b, h] = st

    @pl.when(i == pl.num_programs(1) - 1)
    def _():
        sout_ref[...] = s_ref[...]


def _gdn(gqkv, hist, sm, gz, s0, conv_w, norm_w, *, c, nb, cpb):
    b, t, _ = gqkv.shape
    rows = c * cpb
    assert t % rows == 0 and b % nb == 0 and c % SUBLANES == 0
    activated = hist is None
    blk = lambda w: pl.BlockSpec((nb, rows, w), lambda bi, i: (bi, i, 0))
    const = lambda a: pl.BlockSpec(a.shape, lambda bi, i: (0,) * a.ndim)
    state_spec = pl.BlockSpec((nb, GDN_HEADS, GDN_DK, GDN_DV), lambda bi, i: (bi, 0, 0, 0))
    scratch = [pltpu.VMEM((nb, GDN_HEADS, GDN_DK, GDN_DV), F32)]
    if activated:
        ins = [gqkv, sm, gz, norm_w]
        in_specs = [blk(GDN_CONV_CH), blk(LANES), blk(GDN_WIDTH), const(norm_w)]
    else:
        assert t == rows
        ins = [gqkv, hist, sm, gz, s0, conv_w, norm_w]
        in_specs = [blk(GDN_CONV_CH), pl.BlockSpec((nb, SUBLANES, GDN_CONV_CH), lambda bi, i: (bi, 0, 0)),
                    blk(LANES), blk(GDN_WIDTH), state_spec, const(conv_w), const(norm_w)]
        scratch = [pltpu.VMEM((nb, SUBLANES + rows, GDN_CONV_CH), F32)] + scratch
    return pl.pallas_call(
        functools.partial(_gdn_kernel, c=c, nb=nb, cpb=cpb, activated=activated),
        grid=(b // nb, t // rows),
        in_specs=in_specs,
        out_specs=[blk(GDN_WIDTH), state_spec],
        out_shape=[jax.ShapeDtypeStruct((b, t, GDN_WIDTH), BF16),
                   jax.ShapeDtypeStruct((b, GDN_HEADS, GDN_DK, GDN_DV), F32)],
        scratch_shapes=scratch,
        compiler_params=pltpu.CompilerParams(dimension_semantics=("arbitrary", "arbitrary"),
                                             vmem_limit_bytes=VMEM_LIMIT),
        name="gdn_prompt" if activated else "gdn_sample",
    )(*ins)


FFN_COLS = 256


def _ffn_kernel(of_ref, og_ref, x_ref, wo_ref, nfw_ref, wup_ref, cw_ref, cb_ref, wdn_ref, nlw_ref, *refs,
                nseq, rows, tiles_per_seq, has_state):
    if has_state:
        st_ref, y_ref, nst_ref, wg_ref, wv_ref, act_ref = refs
    else:
        y_ref, nst_ref, wg_ref, wv_ref, act_ref, carry_ref = refs
    i = pl.program_id(0)
    hist0 = SUBLANES - (FFN_CONV - 1)

    h = (x_ref[...] + _dot(of_ref[...], wo_ref[0:FOX_WIDTH, :])
         + _dot(og_ref[...], wo_ref[FOX_WIDTH:FOX_WIDTH + GDN_WIDTH, :]))
    hn = (h * lax.rsqrt(jnp.mean(h * h, axis=-1, keepdims=True) + EPS) * nfw_ref[...]).astype(BF16)

    if not has_state:
        @pl.when(i % tiles_per_seq == 0)
        def _():
            carry_ref[...] = jnp.zeros_like(carry_ref)

    def conv_cols(buf_ref, col):
        up = _dot(hn, wup_ref[:, col:col + FFN_COLS])
        buf_ref[:, SUBLANES:SUBLANES + rows, :] = up.reshape(nseq, rows, FFN_COLS)
        if has_state:
            buf_ref[:, hist0:SUBLANES, :] = st_ref[:, :, col:col + FFN_COLS]
        else:
            buf_ref[:, hist0:SUBLANES, :] = carry_ref[:, hist0:SUBLANES, col:col + FFN_COLS]
        last = buf_ref[:, SUBLANES + rows - (FFN_CONV - 1):SUBLANES + rows, :]
        nst_ref[:, :, col:col + FFN_COLS] = last
        if not has_state:
            carry_ref[:, hist0:SUBLANES, col:col + FFN_COLS] = last
        uc = cb_ref[:, col:col + FFN_COLS][None]
        for tap in range(FFN_CONV):
            uc = uc + buf_ref[:, hist0 + tap:hist0 + tap + rows, :] * cw_ref[tap:tap + 1, col:col + FFN_COLS][None]
        return uc.reshape(nseq * rows, FFN_COLS)

    for ci in range(D_FF // FFN_COLS):
        col = ci * FFN_COLS
        gate = conv_cols(wg_ref, col)
        val = conv_cols(wv_ref, D_FF + col)
        act_ref[:, col:col + FFN_COLS] = (gate * _sigmoid(gate) * val).astype(BF16)

    h2 = h + _dot(act_ref[...], wdn_ref[...])
    y_ref[...] = h2 * lax.rsqrt(jnp.mean(h2 * h2, axis=-1, keepdims=True) + EPS) * nlw_ref[...]


def _ffn(ofox, ogdn, x2d, w_o, norm_ffn_w, w_up, conv_w, conv_b, w_down, norm_final_w, state, *,
         nseq, rows, seq_len):
    n = x2d.shape[0]
    tm = nseq * rows
    assert n % tm == 0 and seq_len % rows == 0 and rows % SUBLANES == 0 and D_FF % FFN_COLS == 0
    has_state = state is not None
    assert (rows == seq_len) if has_state else (nseq == 1)
    tps = seq_len // rows
    nbatch = n // seq_len
    row = lambda w: pl.BlockSpec((tm, w), lambda i: (i, 0))
    const = lambda a: pl.BlockSpec(a.shape, lambda i: (0,) * a.ndim, pipeline_mode=pl.Buffered(1))
    st_spec = pl.BlockSpec((nseq, FFN_CONV - 1, 2 * D_FF), lambda i: (i // tps, 0, 0))
    ins = [ofox, ogdn, x2d, w_o, norm_ffn_w, w_up, conv_w, conv_b, w_down, norm_final_w]
    in_specs = [row(FOX_WIDTH), row(GDN_WIDTH), row(D_MODEL)] + [const(a) for a in ins[3:]]
    scratch = [pltpu.VMEM((nseq, SUBLANES + rows, FFN_COLS), F32),
               pltpu.VMEM((nseq, SUBLANES + rows, FFN_COLS), F32),
               pltpu.VMEM((tm, D_FF), BF16)]
    if has_state:
        ins.append(state)
        in_specs.append(st_spec)
    else:
        scratch.append(pltpu.VMEM((nseq, SUBLANES, 2 * D_FF), F32))
    return pl.pallas_call(
        functools.partial(_ffn_kernel, nseq=nseq, rows=rows, tiles_per_seq=tps, has_state=has_state),
        grid=(n // tm,),
        in_specs=in_specs,
        out_specs=[row(D_MODEL), st_spec],
        out_shape=[jax.ShapeDtypeStruct((n, D_MODEL), F32),
                   jax.ShapeDtypeStruct((nbatch, FFN_CONV - 1, 2 * D_FF), F32)],
        scratch_shapes=scratch,
        compiler_params=pltpu.CompilerParams(dimension_semantics=("arbitrary",),
                                             vmem_limit_bytes=VMEM_LIMIT),
        name="ffn_sample" if has_state else "ffn_prompt",
    )(*ins)


ROW_TILE = 512
SAMPLE_KV_BLOCK = 4096
GDN_CHUNK = 128
GDN_CHUNKS_PER_STEP = 2
GDN_SAMPLE_SEQS_PER_STEP = 4


def _prep_weights(w_in, b_fox_f, gdn_a_log, gdn_dt_bias):
    c0 = 3 * FOX_WIDTH
    c1 = c0 + FOX_HEADS
    c2 = c1 + GDN_CONV_CH
    c3 = c2 + 2 * GDN_HEADS
    w_in = w_in.astype(BF16)
    small = jnp.concatenate([w_in[:, c0:c1], w_in[:, c2:c3],
                             jnp.zeros((D_MODEL, LANES - FOX_HEADS - 2 * GDN_HEADS), BF16)], axis=1)
    w_cat = jnp.concatenate([w_in[:, :c0], w_in[:, c1:c2], w_in[:, c3:], small], axis=1)
    pad = jnp.zeros((LANES - FOX_HEADS - GDN_HEADS,), F32)
    bias = jnp.concatenate([b_fox_f.astype(F32), gdn_dt_bias.astype(F32), pad])
    alog = jnp.concatenate([jnp.zeros((FOX_HEADS,), F32), gdn_a_log.astype(F32), pad])
    prm = jnp.zeros((SUBLANES, LANES), F32).at[0].set(bias).at[1].set(alog)
    return w_cat, prm


def _layer(xp, xs, fox_k, fox_v, fox_logf, st_gdn, st_gconv, st_fconv,
           w_in, b_fox_f, gdn_conv_w, gdn_a_log, gdn_dt_bias, gdn_norm_w, w_o,
           norm_mix_w, norm_ffn_w, w_up, ffn_conv_w, ffn_conv_b, w_down, norm_out_w):
    bp, sp, _ = xp.shape
    bs, ts, _ = xs.shape
    w_cat, prm = _prep_weights(w_in, b_fox_f, gdn_a_log, gdn_dt_bias)
    nmw = norm_mix_w.reshape(1, D_MODEL).astype(F32)
    nfw = norm_ffn_w.reshape(1, D_MODEL).astype(F32)
    now = norm_out_w.reshape(1, D_MODEL).astype(F32)
    gnw = gdn_norm_w.reshape(1, GDN_DV).astype(F32)
    w_o_b, w_up_b, w_dn_b = w_o.astype(BF16), w_up.astype(BF16), w_down.astype(BF16)
    cb = ffn_conv_b.reshape(1, 2 * D_FF).astype(F32)

    xp2 = xp.reshape(bp * sp, D_MODEL)
    q, k, v, kb, qa, ka, va, gtail, gact, gz, sm = _inproj(xp2, nmw, w_cat, prm, gdn_conv_w.astype(F32),
                                                          tm=ROW_TILE, seq_len=sp, prompt=True)
    r3 = lambda a: a.reshape(bp, sp, a.shape[-1])
    o_fox = _fox_prompt(r3(q), r3(qa), r3(kb), r3(ka), va, t=ROW_TILE)
    o_gdn, p_state = _gdn(r3(gact), None, r3(sm), r3(gz), None, None, gnw,
                          c=GDN_CHUNK, nb=1, cpb=GDN_CHUNKS_PER_STEP)
    yp, p_fconv = _ffn(o_fox.reshape(bp * sp, FOX_WIDTH), o_gdn.reshape(bp * sp, GDN_WIDTH), xp2,
                       w_o_b, nfw, w_up_b, ffn_conv_w, cb, w_dn_b, now, None,
                       nseq=1, rows=ROW_TILE, seq_len=sp)
    p_out = (jnp.transpose(k, (0, 3, 1, 2)), jnp.transpose(v, (0, 3, 1, 2)),
             r3(sm)[:, :, LOGF_LANE:LOGF_LANE + FOX_HEADS], p_state,
             gtail[:, SUBLANES - (GDN_CONV - 1):], p_fconv)

    xs2 = xs.reshape(bs * ts, D_MODEL)
    q, k, v, gqkv, gz, sm = _inproj(xs2, nmw, w_cat, prm, tm=bs * ts, seq_len=ts, prompt=False)
    s3 = lambda a: a.reshape(bs, ts, a.shape[-1])
    past = fox_k.shape[1]
    lft = jnp.swapaxes(fox_logf.astype(F32), 1, 2)
    cache_t = lambda a: jnp.transpose(a.astype(F32), (0, 2, 3, 1))
    o_fox = _fox_sample(s3(q), s3(k), s3(v), s3(sm), lft, cache_t(fox_k), cache_t(fox_v),
                        kb=min(SAMPLE_KV_BLOCK, past))
    gqkv3 = s3(gqkv)
    hist = jnp.concatenate([jnp.zeros((bs, SUBLANES - (GDN_CONV - 1), GDN_CONV_CH), F32),
                            st_gconv.astype(F32)], axis=1)
    o_gdn, s_state = _gdn(gqkv3, hist, s3(sm), s3(gz), st_gdn.astype(F32), gdn_conv_w, gnw,
                          c=ts, nb=GDN_SAMPLE_SEQS_PER_STEP, cpb=1)
    ys, s_fconv = _ffn(o_fox.reshape(bs * ts, FOX_WIDTH), o_gdn.reshape(bs * ts, GDN_WIDTH), xs2,
                       w_o_b, nfw, w_up_b, ffn_conv_w, cb, w_dn_b, now, st_fconv.astype(F32),
                       nseq=bs, rows=ts, seq_len=ts)
    full = jnp.concatenate([st_gconv.astype(F32), gqkv3], axis=1)
    s_out = (k.reshape(bs, ts, FOX_HEADS, FOX_DIM), v.reshape(bs, ts, FOX_HEADS, FOX_DIM),
             s3(sm)[:, :, LOGF_LANE:LOGF_LANE + FOX_HEADS], s_state,
             full[:, ts:], s_fconv)
    return yp.reshape(bp, sp, D_MODEL), ys.reshape(bs, ts, D_MODEL), p_out, s_out


def kernel(x_prompt, x_sample, cache_fox_k, cache_fox_v, cache_fox_logf, state_gdn, state_gdn_conv,
           state_ffn_conv, w_in, b_fox_f, gdn_conv_w, gdn_a_log, gdn_dt_bias, gdn_norm_w, w_o, norm_mix_w,
           norm_ffn_w, w_up, ffn_conv_w, ffn_conv_b, w_down, norm_final_w):
    depth = w_in.shape[0]
    assert depth == 1, "the final RMSNorm is fused into the single layer's FFN kernel"
    yp, ys, p_out, s_out = _layer(
        x_prompt, x_sample, cache_fox_k[0], cache_fox_v[0], cache_fox_logf[0], state_gdn[0],
        state_gdn_conv[0], state_ffn_conv[0], w_in[0], b_fox_f[0], gdn_conv_w[0], gdn_a_log[0],
        gdn_dt_bias[0], gdn_norm_w[0], w_o[0], norm_mix_w[0], norm_ffn_w[0], w_up[0], ffn_conv_w[0],
        ffn_conv_b[0], w_down[0], norm_final_w)
    return (yp, ys) + tuple(a[None] for a in p_out) + tuple(a[None] for a in s_out)
```

```python
import functools
import math

import numpy as np
import jax
import jax.numpy as jnp
from jax import lax
from jax.experimental import pallas as pl
from jax.experimental.pallas import tpu as pltpu

F32 = jnp.float32
BF16 = jnp.bfloat16

D_MODEL = 1024
FOX_DIM = 64
FOX_HEADS = 8
FOX_WIDTH = FOX_HEADS * FOX_DIM
GDN_DK = 128
GDN_DV = 128
GDN_HEADS = 4
GDN_QK = GDN_HEADS * GDN_DK
GDN_WIDTH = GDN_HEADS * GDN_DV
GDN_CONV_CH = 2 * GDN_QK + GDN_WIDTH
GDN_CONV = 4
D_FF = 2816
FFN_CONV = 3
EPS = 1e-6

LANES = 128
SUBLANES = 8
VMEM_LIMIT = 56 * 1024 * 1024
MASKED = -1e30
LOG2E = math.log2(math.e)

LOGF_LANE = 0
G_LANE = FOX_HEADS
BETA_LANE = FOX_HEADS + GDN_HEADS

COL_FOX = 0
COL_GQKV = 3 * FOX_WIDTH
COL_GZ = COL_GQKV + GDN_CONV_CH
COL_SMALL = COL_GZ + GDN_WIDTH
N_COLS = COL_SMALL + LANES


def _dot(a, b):
    return jnp.dot(a, b, preferred_element_type=F32)


def _dot_nt(a, b):
    return lax.dot_general(a, b, (((1,), (1,)), ((), ())), preferred_element_type=F32)


def _dot_tn(a, b):
    return lax.dot_general(a, b, (((0,), (0,)), ((), ())), preferred_element_type=F32)


def _split3(x):
    hi = x.astype(BF16)
    r = x - hi.astype(F32)
    lo = r.astype(BF16)
    lo2 = (r - lo.astype(F32)).astype(BF16)
    return hi, lo, lo2


def _exact_left(mat, x):
    hi, lo, lo2 = _split3(x)
    return _dot(mat, hi) + _dot(mat, lo) + _dot(mat, lo2)


def _exact_right(x, mat):
    hi, lo, lo2 = _split3(x)
    return _dot(hi, mat) + _dot(lo, mat) + _dot(lo2, mat)


def _dot3(a, b):
    ah = a.astype(BF16)
    al = (a - ah.astype(F32)).astype(BF16)
    bh = b.astype(BF16)
    bl = (b - bh.astype(F32)).astype(BF16)
    return _dot(ah, bh) + (_dot(ah, bl) + _dot(al, bh))


def _exact_left_c(mat, x):
    return _dot(jnp.concatenate([mat, mat, mat], axis=1), jnp.concatenate(_split3(x), axis=0))


def _split2(x):
    hi = x.astype(BF16)
    return hi, (x - hi.astype(F32)).astype(BF16)


def _dot3_parts(ah, al, bh, bl):
    return _dot(jnp.concatenate([ah, ah, al], axis=1), jnp.concatenate([bh, bl, bh], axis=0))


def _dot3c(a, b):
    return _dot3_parts(*_split2(a), *_split2(b))


def _sigmoid(x):
    return 1.0 / (1.0 + jnp.exp(-x))


def _tri(n, strict=False):
    r = lax.broadcasted_iota(jnp.int32, (n, n), 0)
    c = lax.broadcasted_iota(jnp.int32, (n, n), 1)
    return (r > c) if strict else (r >= c)


def _inproj_kernel(x_ref, nw_ref, w_ref, prm_ref, *refs, tiles_per_seq, prompt):
    if prompt:
        (aug_ref, augc_ref, cw_ref, q_ref, k_ref, v_ref, kb_ref, qa_ref, ka_ref, va_ref, tail_ref,
         g_ref, z_ref, sm_ref, carry_ref, win_ref) = refs
    else:
        q_ref, k_ref, v_ref, g_ref, z_ref, sm_ref = refs
    x = x_ref[...]
    xn = x * lax.rsqrt(jnp.mean(x * x, axis=-1, keepdims=True) + EPS) * nw_ref[...]
    xb = xn.astype(BF16)

    tm = x.shape[0]
    gq = _dot(xb, w_ref[:, COL_GQKV:COL_GZ])
    conv_cols = []
    if not prompt:
        g_ref[...] = gq
    else:
        first = pl.program_id(0) % tiles_per_seq == 0

        @pl.when(first)
        def _():
            win_ref[0:SUBLANES, :] = jnp.zeros((SUBLANES, GDN_CONV_CH), F32)

        @pl.when(jnp.logical_not(first))
        def _():
            win_ref[0:SUBLANES, :] = win_ref[tm:tm + SUBLANES, :]

        win_ref[SUBLANES:SUBLANES + tm, :] = gq
        tail_ref[0] = gq[tm - SUBLANES:tm, :]
        conv_cols = list(range(0, GDN_CONV_CH, LANES))

    def conv_some(count):
        for _ in range(min(count, len(conv_cols))):
            col = conv_cols.pop(0)
            g_ref[:, col:col + LANES] = _gdn_activate(
                _short_conv(win_ref, cw_ref, SUBLANES - (GDN_CONV - 1), tm, col), col)

    q_scale = FOX_DIM ** -0.5 * (LOG2E if prompt else 1.0)
    q_ref[...] = (_dot(xb, w_ref[:, 0:FOX_WIDTH]) * q_scale).astype(BF16)
    conv_some(3)
    kf = _dot(xb, w_ref[:, FOX_WIDTH:2 * FOX_WIDTH])
    if not prompt:
        k_ref[...] = kf
    else:
        k_ref[0] = kf.T.reshape(FOX_HEADS, FOX_DIM, tm)
        kb_ref[...] = kf.astype(BF16)
    conv_some(3)
    vf = _dot(xb, w_ref[:, 2 * FOX_WIDTH:3 * FOX_WIDTH])
    if not prompt:
        v_ref[...] = vf
    else:
        vt = vf.T.reshape(FOX_HEADS, FOX_DIM, tm)
        v_ref[0] = vt
        va_ref[0, :, 0:FOX_DIM, :] = vt.astype(BF16)
        va_ref[0, :, FOX_DIM:, :] = jnp.ones((FOX_HEADS, ONES_ROWS, tm), BF16)
    conv_some(3)
    z_ref[...] = _dot(xb, w_ref[:, COL_GZ:COL_SMALL])
    conv_some(3)

    z = _dot(xb, w_ref[:, COL_SMALL:N_COLS])
    lane = lax.broadcasted_iota(jnp.int32, z.shape, 1)
    zb = z + prm_ref[0:1, :]
    e = jnp.exp(-jnp.abs(zb))
    l1p = jnp.log(1.0 + e)
    logf = jnp.minimum(zb, 0.0) - l1p
    g = -jnp.exp(prm_ref[1:2, :]) * (jnp.maximum(zb, 0.0) + l1p)
    beta = _sigmoid(z)
    sm = jnp.where(lane < G_LANE, logf,
                   jnp.where(lane < BETA_LANE, g,
                             jnp.where(lane < BETA_LANE + GDN_HEADS, beta, 0.0)))
    sm_ref[...] = sm

    if prompt:
        i = pl.program_id(0)

        @pl.when(i % tiles_per_seq == 0)
        def _():
            carry_ref[...] = jnp.zeros_like(carry_ref)

        c = _exact_left(_tri(tm).astype(BF16), sm) + carry_ref[0:1, :]
        carry_ref[0:1, :] = c[tm - 1:tm, :]
        parts = jnp.concatenate(_split3(c * LOG2E), axis=1)
        aug = _dot(parts, aug_ref[...]) + augc_ref[...]
        qa_ref[...] = aug[:, :FOX_WIDTH].astype(BF16)
        ka_ref[...] = aug[:, FOX_WIDTH:].astype(BF16)


AUG_LANES = 6
ONES_ROWS = 16
FOX_QUERY_SPLITS = 2


def _aug_tables():
    place = np.zeros((3 * LANES, 2 * FOX_WIDTH), np.float32)
    ones = np.zeros((1, 2 * FOX_WIDTH), np.float32)
    for h in range(FOX_HEADS):
        q0 = (h // 2) * LANES + (h % 2) * AUG_LANES
        k0 = FOX_WIDTH + q0
        for part in range(3):
            place[part * LANES + LOGF_LANE + h, q0 + part] = 1.0
            place[part * LANES + LOGF_LANE + h, k0 + 3 + part] = -1.0
            ones[0, q0 + 3 + part] = 1.0
            ones[0, k0 + part] = 1.0
    return jnp.asarray(place, BF16), jnp.asarray(ones, F32)


def _inproj(x2d, norm_w, w_cat, prm, conv_w=None, *, tm, seq_len, prompt):
    n = x2d.shape[0]
    assert n % tm == 0 and (not prompt or seq_len % tm == 0)
    tps = seq_len // tm if prompt else 1
    row = lambda w: pl.BlockSpec((tm, w), lambda i: (i, 0))
    const = lambda a: pl.BlockSpec(a.shape, lambda i: (0,) * a.ndim, pipeline_mode=pl.Buffered(1))
    ins = [x2d, norm_w, w_cat, prm]
    out_shape = [jax.ShapeDtypeStruct((n, FOX_WIDTH), BF16),
                 jax.ShapeDtypeStruct((n, FOX_WIDTH), F32),
                 jax.ShapeDtypeStruct((n, FOX_WIDTH), F32)]
    out_specs = [row(FOX_WIDTH), row(FOX_WIDTH), row(FOX_WIDTH)]
    scratch = []
    if prompt:
        kv_t = jax.ShapeDtypeStruct((n // seq_len, FOX_HEADS, FOX_DIM, seq_len), F32)
        out_shape[1:3] = [kv_t, kv_t]
        out_specs[1:3] = [pl.BlockSpec((1, FOX_HEADS, FOX_DIM, tm), lambda i: (i // tps, 0, 0, i % tps))] * 2
        ins += list(_aug_tables()) + [conv_w]
        out_shape += [jax.ShapeDtypeStruct((n, FOX_WIDTH), BF16)] * 3 + [
            jax.ShapeDtypeStruct((n // seq_len, FOX_HEADS, FOX_DIM + ONES_ROWS, seq_len), BF16)]
        out_specs += [row(FOX_WIDTH)] * 3 + [
            pl.BlockSpec((1, FOX_HEADS, FOX_DIM + ONES_ROWS, tm), lambda i: (i // tps, 0, 0, i % tps))]
        out_shape += [jax.ShapeDtypeStruct((n // seq_len, SUBLANES, GDN_CONV_CH), F32)]
        out_specs += [pl.BlockSpec((1, SUBLANES, GDN_CONV_CH), lambda i: (i // tps, 0, 0))]
        scratch = [pltpu.VMEM((SUBLANES, LANES), F32), pltpu.VMEM((SUBLANES + tm, GDN_CONV_CH), F32)]
    out_shape += [jax.ShapeDtypeStruct((n, GDN_CONV_CH), F32),
                  jax.ShapeDtypeStruct((n, GDN_WIDTH), F32),
                  jax.ShapeDtypeStruct((n, LANES), F32)]
    out_specs += [row(GDN_CONV_CH), row(GDN_WIDTH), row(LANES)]
    return pl.pallas_call(
        functools.partial(_inproj_kernel, tiles_per_seq=tps, prompt=prompt),
        grid=(n // tm,),
        in_specs=[row(D_MODEL)] + [const(a) for a in ins[1:]],
        out_specs=out_specs, out_shape=out_shape, scratch_shapes=scratch,
        compiler_params=pltpu.CompilerParams(dimension_semantics=("arbitrary",),
                                             vmem_limit_bytes=VMEM_LIMIT),
        name="inproj_prompt" if prompt else "inproj_sample",
    )(*ins)


def _fox_prompt_kernel(q_ref, qa_ref, k_ref, ka_ref, va_ref, o_ref, sa_ref, sb_ref, *, t):
    i = pl.program_id(2)
    q = q_ref[0]
    qa = qa_ref[0]
    lane = lax.broadcasted_iota(jnp.int32, (t, LANES), 1)
    low = lane < FOX_DIM
    zero = jnp.zeros_like(q)
    qcat = []
    for h in range(2):
        mine = (lane >= h * AUG_LANES) & (lane < (h + 1) * AUG_LANES)
        qcat.append(jnp.concatenate([jnp.where(low if h == 0 else jnp.logical_not(low), q, zero),
                                     jnp.where(mine, qa, zero)], axis=1))

    tq = t // FOX_QUERY_SPLITS
    units = [(h, u) for h in range(2) for u in range(FOX_QUERY_SPLITS)]
    qunit = [qcat[h][u * tq:(u + 1) * tq] for h, u in units]

    def logits(j, s_ref):
        start = pl.multiple_of(j * t, t)
        kcat = jnp.concatenate([k_ref[0, pl.ds(start, t), :], ka_ref[0, pl.ds(start, t), :]], axis=1)
        for n in range(len(units)):
            s_ref[n] = _dot_nt(kcat, qunit[n])

    def consume(j, s_ref, carry, diag):
        start = pl.multiple_of(j * t, t)
        out = []
        for n, (h, u) in enumerate(units):
            m, acc = carry[n]
            s = s_ref[n]
            if diag:
                key = lax.broadcasted_iota(jnp.int32, (t, tq), 0)
                query = lax.broadcasted_iota(jnp.int32, (t, tq), 1) + u * tq
                s = jnp.where(key <= query, s, MASKED)
            m_new = jnp.maximum(m, jnp.max(s, axis=0, keepdims=True))
            p = jnp.exp2(s - m_new).astype(BF16)
            acc = jnp.exp2(m - m_new) * acc + _dot(va_ref[0, h, :, pl.ds(start, t)], p)
            out.append((m_new, acc))
        return tuple(out)

    init = tuple((jnp.full((1, tq), MASKED, F32), jnp.zeros((FOX_DIM + ONES_ROWS, tq), F32)) for _ in units)
    logits(0, sa_ref)

    def two_blocks(jj, carry):
        j = 2 * jj
        logits(j + 1, sb_ref)
        carry = consume(j, sa_ref, carry, False)
        logits(j + 2, sa_ref)
        return consume(j + 1, sb_ref, carry, False)

    carry = lax.fori_loop(0, i // 2, two_blocks, init)

    def odd_tail(carry):
        logits(i, sb_ref)
        return consume(i, sb_ref, consume(i - 1, sa_ref, carry, False), True)

    carry = lax.cond(i % 2 == 1, odd_tail, lambda c: consume(i, sa_ref, c, True), carry)
    heads = [jnp.concatenate([carry[h * FOX_QUERY_SPLITS + u][1] for u in range(FOX_QUERY_SPLITS)], axis=1)
             for h in range(2)]
    o_t = jnp.concatenate([a[:FOX_DIM] / a[FOX_DIM:FOX_DIM + 1] for a in heads], axis=0)
    o_ref[0] = o_t.T.astype(BF16)


def _fox_prompt(q, qa, kb, ka, va, *, t):
    b, s, _ = q.shape
    nblk = s // t
    assert s % t == 0
    npair = FOX_HEADS // 2
    qblk = pl.BlockSpec((1, t, LANES), lambda bi, p, i: (bi, i, p))
    kblk = pl.BlockSpec((1, s, LANES), lambda bi, p, i: (bi, 0, p))
    return pl.pallas_call(
        functools.partial(_fox_prompt_kernel, t=t),
        grid=(b, npair, nblk),
        in_specs=[qblk, qblk, kblk, kblk,
                  pl.BlockSpec((1, 2, FOX_DIM + ONES_ROWS, s), lambda bi, p, i: (bi, p, 0, 0))],
        out_specs=qblk,
        out_shape=jax.ShapeDtypeStruct((b, s, FOX_WIDTH), BF16),
        scratch_shapes=[pltpu.VMEM((2 * FOX_QUERY_SPLITS, t, t // FOX_QUERY_SPLITS), F32)] * 2,
        compiler_params=pltpu.CompilerParams(
            dimension_semantics=("arbitrary", "arbitrary", "arbitrary"), vmem_limit_bytes=VMEM_LIMIT),
        name="fox_prompt",
    )(q, qa, kb, ka, va)


def _fox_sample_kernel(q_ref, kn_ref, vn_ref, sm_ref, lft_ref, kt_ref, vt_ref, o_ref,
                       rk_ref, m_ref, l_ref, acc_ref, *, kb, t):
    j = pl.program_id(1)
    nkv = pl.num_programs(1)

    def heads(x):
        return [x[:, h * FOX_DIM:(h + 1) * FOX_DIM] for h in range(FOX_HEADS)]

    qh = [x.astype(BF16) for x in heads(q_ref[0].astype(F32))]

    lf_new = sm_ref[0]
    cn = _exact_left(_tri(t).astype(BF16), lf_new)

    @pl.when(j == 0)
    def _():
        p = lft_ref.shape[2]
        nb = p // LANES
        x = jnp.concatenate([lft_ref[0, :, i * LANES:(i + 1) * LANES] for i in range(nb)], axis=0)
        r = lax.broadcasted_iota(jnp.int32, (LANES, LANES), 0)
        cc = lax.broadcasted_iota(jnp.int32, (LANES, LANES), 1)
        y = _exact_right(x, (r > cc).astype(BF16))
        tot = jnp.sum(x, axis=1, keepdims=True)
        off = jnp.zeros((FOX_HEADS, 1), F32)
        for i in range(nb - 1, -1, -1):
            rk_ref[:, i * LANES:(i + 1) * LANES] = y[i * FOX_HEADS:(i + 1) * FOX_HEADS, :] + off
            off = off + tot[i * FOX_HEADS:(i + 1) * FOX_HEADS, :]
        m_ref[...] = jnp.full_like(m_ref, MASKED)
        l_ref[...] = jnp.zeros_like(l_ref)
        acc_ref[...] = jnp.zeros_like(acc_ref)

    def update(s, pv):
        m = m_ref[:, 0:1]
        m_new = jnp.maximum(m, jnp.max(s, axis=1, keepdims=True))
        alpha = jnp.exp(m - m_new)
        p = jnp.exp(s - m_new)
        l_ref[...] = jnp.broadcast_to(alpha * l_ref[:, 0:1] + jnp.sum(p, axis=1, keepdims=True), l_ref.shape)
        acc_ref[...] = alpha * acc_ref[...] + pv(p.astype(BF16))
        m_ref[...] = jnp.broadcast_to(m_new, m_ref.shape)

    start = pl.multiple_of(j * kb, kb)
    rk = rk_ref[:, pl.ds(start, kb)]
    s = jnp.concatenate([_dot(qh[h], kt_ref[0, h].astype(BF16)) + (cn[:, h:h + 1] + rk[h:h + 1, :])
                         for h in range(FOX_HEADS)], axis=0)
    update(s, lambda p: jnp.concatenate(
        [_dot_nt(p[h * t:(h + 1) * t], vt_ref[0, h].astype(BF16)) for h in range(FOX_HEADS)], axis=0))

    @pl.when(j == nkv - 1)
    def _():
        r = lax.broadcasted_iota(jnp.int32, (t, t), 0)
        cc = lax.broadcasted_iota(jnp.int32, (t, t), 1)
        after = (r > cc).astype(F32)
        tri = _tri(t).astype(BF16)
        knh = heads(kn_ref[0])
        vnh = [x.astype(BF16) for x in heads(vn_ref[0])]
        s_new = jnp.concatenate(
            [jnp.where(_tri(t), _dot_nt(qh[h], knh[h].astype(BF16))
                       + _exact_left(tri, lf_new[:, h:h + 1] * after), MASKED) for h in range(FOX_HEADS)], axis=0)
        update(s_new, lambda p: jnp.concatenate(
            [_dot(p[h * t:(h + 1) * t], vnh[h]) for h in range(FOX_HEADS)], axis=0))
        o_full = acc_ref[...] / l_ref[:, 0:1]
        o_ref[0] = jnp.concatenate([o_full[h * t:(h + 1) * t] for h in range(FOX_HEADS)], axis=1).astype(BF16)


def _fox_sample(q, kn, vn, sm, lft, kt, vt, *, kb):
    b, t, _ = q.shape
    p = kt.shape[3]
    assert p % kb == 0 and kb % LANES == 0
    rows = FOX_HEADS * t
    per_b = lambda shape: pl.BlockSpec((1,) + shape, lambda bi, j: (bi, 0, 0))
    cache = pl.BlockSpec((1, FOX_HEADS, FOX_DIM, kb), lambda bi, j: (bi, 0, 0, j))
    return pl.pallas_call(
        functools.partial(_fox_sample_kernel, kb=kb, t=t),
        grid=(b, p // kb),
        in_specs=[per_b((t, FOX_WIDTH)), per_b((t, FOX_WIDTH)), per_b((t, FOX_WIDTH)), per_b((t, LANES)),
                  per_b((FOX_HEADS, p)), cache, cache],
        out_specs=per_b((t, FOX_WIDTH)),
        out_shape=jax.ShapeDtypeStruct((b, t, FOX_WIDTH), BF16),
        scratch_shapes=[pltpu.VMEM((FOX_HEADS, p), F32),
                        pltpu.VMEM((rows, LANES), F32), pltpu.VMEM((rows, LANES), F32),
                        pltpu.VMEM((rows, FOX_DIM), F32)],
        compiler_params=pltpu.CompilerParams(dimension_semantics=("arbitrary", "arbitrary"),
                                             vmem_limit_bytes=VMEM_LIMIT),
        name="fox_sample",
    )(q, kn, vn, sm, lft, kt, vt)


def _interleave(gens):
    results = [None] * len(gens)
    alive = list(range(len(gens)))
    while alive:
        for idx in list(alive):
            try:
                next(gens[idx])
            except StopIteration as stop:
                results[idx] = stop.value
                alive.remove(idx)
    return results


def _gdn_activate(acc, col):
    x = acc * _sigmoid(acc)
    if col >= 2 * GDN_QK:
        return x
    scale = GDN_DK ** -0.5 if col < GDN_QK else 1.0
    return x * (lax.rsqrt(jnp.sum(x * x, axis=-1, keepdims=True) + EPS) * scale)


def _short_conv(win_ref, cw_ref, row0, nrows, col, lead=()):
    acc = None
    for tap in range(GDN_CONV):
        term = (win_ref[lead + (slice(row0 + tap, row0 + tap + nrows), slice(col, col + LANES))]
                * cw_ref[tap:tap + 1, col:col + LANES])
        acc = term if acc is None else acc + term
    return acc


def _gdn_local(qn, kn, vh, g, g_row, g_last, beta, c):
    wide = c % LANES == 0
    dot3 = _dot3c if wide else _dot3
    incl = _tri(c)
    strict = _tri(c, strict=True)
    eg = jnp.exp(g)
    decay = jnp.exp(jnp.where(incl, g - g_row, MASKED))
    kbeta = kn * beta
    kq = _dot_nt(jnp.concatenate([kbeta, qn], axis=0).astype(BF16), kn.astype(BF16))
    yield
    lmat = jnp.where(strict, kq[:c] * decay, 0.0)
    a_qk = kq[c:] * decay

    tinv = (incl & jnp.logical_not(strict)).astype(F32) - lmat
    if wide:
        lh, ll = _split2(lmat)
        power = _dot3_parts(lh, ll, lh, ll)
    else:
        power = dot3(lmat, lmat)
    yield
    span = 2
    while 2 * span < c:
        if wide:
            ph, pl_ = _split2(power)
            th, tl = _split2(tinv)
            both = _dot3_parts(ph, pl_, jnp.concatenate([th, ph], axis=1), jnp.concatenate([tl, pl_], axis=1))
            yield
            tinv = tinv + both[:, :c]
            power = both[:, c:]
        else:
            step = dot3(power, tinv)
            power = dot3(power, power)
            yield
            tinv = tinv + step
        span *= 2
    step = dot3(power, tinv)
    yield
    tinv = tinv + step

    sol = dot3(tinv, jnp.concatenate([vh * beta, kbeta * eg], axis=1))
    yield
    kd = kn * jnp.exp(g_last - g)
    return sol[:, :GDN_DV], sol[:, GDN_DV:], qn * eg, a_qk, (kd.T if wide else kd)


def _gdn_state_step(locals_, states, g_lasts, c):
    wide = c % LANES == 0
    ws = [_dot(jnp.concatenate([w, qeg], axis=0).astype(BF16), sh.astype(BF16))
          for (_, w, qeg, _, _), sh in zip(locals_, states)]
    outs, new_states = [], []
    for (u, _, _, a_qk, kd), wsh, sh, g_last in zip(locals_, ws, states, g_lasts):
        db = (u - wsh[:c]).astype(BF16)
        decayed = sh * jnp.exp(g_last)
        if wide:
            ak = _dot(jnp.concatenate([a_qk, kd], axis=0).astype(BF16), db)
            outs.append(wsh[c:] + ak[:c])
            new_states.append(decayed + ak[c:])
        else:
            outs.append(wsh[c:] + _dot(a_qk.astype(BF16), db))
            new_states.append(decayed + _dot_tn(kd.astype(BF16), db))
    return outs, new_states


def _gdn_kernel(cur_ref, *refs, c, nb, cpb, activated):
    if activated:
        sm_ref, z_ref, nw_ref, o_ref, sout_ref, s_ref = refs
    else:
        hist_ref, sm_ref, z_ref, s0_ref, cw_ref, nw_ref, o_ref, sout_ref, xs_ref, s_ref = refs
    i = pl.program_id(1)
    rows = c * cpb
    wide = c % LANES == 0

    @pl.when(i == 0)
    def _():
        if activated:
            s_ref[...] = jnp.zeros_like(s_ref)
        else:
            s_ref[...] = s0_ref[...]

    if activated:
        def qkv(b, r0, col):
            return cur_ref[b, r0:r0 + c, col:col + LANES]
    else:
        xs_ref[:, 0:SUBLANES, :] = hist_ref[...]
        xs_ref[:, SUBLANES:SUBLANES + rows, :] = cur_ref[...]
        base = SUBLANES - (GDN_CONV - 1)

        def qkv(b, r0, col):
            return _gdn_activate(_short_conv(xs_ref, cw_ref, base + r0, c, col, lead=(b,)), col)

    incl = _tri(c)
    eye = incl & jnp.logical_not(_tri(c, strict=True))
    gens, g_lasts = [], []
    for b in range(nb):
        for ck in range(cpb):
            r0 = ck * c
            sm = sm_ref[b, r0:r0 + c, :]
            if wide:
                gs = _exact_left_c(incl.astype(BF16), sm)
                gst = gs.T
            else:
                gs = _exact_left(incl.astype(BF16), sm)
            for h in range(GDN_HEADS):
                g = gs[:, G_LANE + h:G_LANE + h + 1]
                if wide:
                    g_row = gst[G_LANE + h:G_LANE + h + 1, :]
                else:
                    g_row = _exact_left(jnp.ones((c, c), BF16), jnp.where(eye, g, 0.0))
                g_last = gs[c - 1:c, G_LANE + h:G_LANE + h + 1]
                g_lasts.append(g_last)
                gens.append(_gdn_local(
                    qkv(b, r0, h * GDN_DK), qkv(b, r0, GDN_QK + h * GDN_DK),
                    qkv(b, r0, 2 * GDN_QK + h * GDN_DV), g, g_row, g_last,
                    sm[:, BETA_LANE + h:BETA_LANE + h + 1], c))
    locals_ = _interleave(gens)

    states = [s_ref[b, h] for b in range(nb) for h in range(GDN_HEADS)]
    for ck in range(cpb):
        r0 = ck * c
        pick = [(b * cpb + ck) * GDN_HEADS + h for b in range(nb) for h in range(GDN_HEADS)]
        outs, states = _gdn_state_step([locals_[u] for u in pick], states, [g_lasts[u] for u in pick], c)
        for idx, o in enumerate(outs):
            b, h = divmod(idx, GDN_HEADS)
            on = o * lax.rsqrt(jnp.mean(o * o, axis=-1, keepdims=True) + EPS) * nw_ref[...]
            zh = z_ref[b, r0:r0 + c, h * GDN_DV:(h + 1) * GDN_DV]
            o_ref[b, r0:r0 + c, h * GDN_DV:(h + 1) * GDN_DV] = (on * (zh * _sigmoid(zh))).astype(BF16)
    for idx, st in enumerate(states):
        b, h = divmod(idx, GDN_HEADS)
        s_ref[b, h] = st

    @pl.when(i == pl.num_programs(1) - 1)
    def _():
        sout_ref[...] = s_ref[...]


def _gdn(gqkv, hist, sm, gz, s0, conv_w, norm_w, *, c, nb, cpb):
    b, t, _ = gqkv.shape
    rows = c * cpb
    assert t % rows == 0 and b % nb == 0 and c % SUBLANES == 0
    activated = hist is None
    blk = lambda w: pl.BlockSpec((nb, rows, w), lambda bi, i: (bi, i, 0))
    const = lambda a: pl.BlockSpec(a.shape, lambda bi, i: (0,) * a.ndim)
    state_spec = pl.BlockSpec((nb, GDN_HEADS, GDN_DK, GDN_DV), lambda bi, i: (bi, 0, 0, 0))
    scratch = [pltpu.VMEM((nb, GDN_HEADS, GDN_DK, GDN_DV), F32)]
    if activated:
        ins = [gqkv, sm, gz, norm_w]
        in_specs = [blk(GDN_CONV_CH), blk(LANES), blk(GDN_WIDTH), const(norm_w)]
    else:
        assert t == rows
        ins = [gqkv, hist, sm, gz, s0, conv_w, norm_w]
        in_specs = [blk(GDN_CONV_CH), pl.BlockSpec((nb, SUBLANES, GDN_CONV_CH), lambda bi, i: (bi, 0, 0)),
                    blk(LANES), blk(GDN_WIDTH), state_spec, const(conv_w), const(norm_w)]
        scratch = [pltpu.VMEM((nb, SUBLANES + rows, GDN_CONV_CH), F32)] + scratch
    return pl.pallas_call(
        functools.partial(_gdn_kernel, c=c, nb=nb, cpb=cpb, activated=activated),
        grid=(b // nb, t // rows),
        in_specs=in_specs,
        out_specs=[blk(GDN_WIDTH), state_spec],
        out_shape=[jax.ShapeDtypeStruct((b, t, GDN_WIDTH), BF16),
                   jax.ShapeDtypeStruct((b, GDN_HEADS, GDN_DK, GDN_DV), F32)],
        scratch_shapes=scratch,
        compiler_params=pltpu.CompilerParams(dimension_semantics=("arbitrary", "arbitrary"),
                                             vmem_limit_bytes=VMEM_LIMIT),
        name="gdn_prompt" if activated else "gdn_sample",
    )(*ins)


FFN_COLS = 256


def _ffn_kernel(of_ref, og_ref, x_ref, wo_ref, nfw_ref, wup_ref, cw_ref, cb_ref, wdn_ref, nlw_ref, *refs,
                nseq, rows, tiles_per_seq, has_state):
    if has_state:
        st_ref, y_ref, nst_ref, wg_ref, wv_ref, act_ref = refs
    else:
        y_ref, nst_ref, wg_ref, wv_ref, act_ref, carry_ref = refs
    i = pl.program_id(0)
    hist0 = SUBLANES - (FFN_CONV - 1)

    h = (x_ref[...] + _dot(of_ref[...], wo_ref[0:FOX_WIDTH, :])
         + _dot(og_ref[...], wo_ref[FOX_WIDTH:FOX_WIDTH + GDN_WIDTH, :]))
    hn = (h * lax.rsqrt(jnp.mean(h * h, axis=-1, keepdims=True) + EPS) * nfw_ref[...]).astype(BF16)

    if not has_state:
        @pl.when(i % tiles_per_seq == 0)
        def _():
            carry_ref[...] = jnp.zeros_like(carry_ref)

    def conv_cols(buf_ref, col):
        up = _dot(hn, wup_ref[:, col:col + FFN_COLS])
        buf_ref[:, SUBLANES:SUBLANES + rows, :] = up.reshape(nseq, rows, FFN_COLS)
        if has_state:
            buf_ref[:, hist0:SUBLANES, :] = st_ref[:, :, col:col + FFN_COLS]
        else:
            buf_ref[:, hist0:SUBLANES, :] = carry_ref[:, hist0:SUBLANES, col:col + FFN_COLS]
        last = buf_ref[:, SUBLANES + rows - (FFN_CONV - 1):SUBLANES + rows, :]
        nst_ref[:, :, col:col + FFN_COLS] = last
        if not has_state:
            carry_ref[:, hist0:SUBLANES, col:col + FFN_COLS] = last
        uc = cb_ref[:, col:col + FFN_COLS][None]
        for tap in range(FFN_CONV):
            uc = uc + buf_ref[:, hist0 + tap:hist0 + tap + rows, :] * cw_ref[tap:tap + 1, col:col + FFN_COLS][None]
        return uc.reshape(nseq * rows, FFN_COLS)

    for ci in range(D_FF // FFN_COLS):
        col = ci * FFN_COLS
        gate = conv_cols(wg_ref, col)
        val = conv_cols(wv_ref, D_FF + col)
        act_ref[:, col:col + FFN_COLS] = (gate * _sigmoid(gate) * val).astype(BF16)

    h2 = h + _dot(act_ref[...], wdn_ref[...])
    y_ref[...] = h2 * lax.rsqrt(jnp.mean(h2 * h2, axis=-1, keepdims=True) + EPS) * nlw_ref[...]


def _ffn(ofox, ogdn, x2d, w_o, norm_ffn_w, w_up, conv_w, conv_b, w_down, norm_final_w, state, *,
         nseq, rows, seq_len):
    n = x2d.shape[0]
    tm = nseq * rows
    assert n % tm == 0 and seq_len % rows == 0 and rows % SUBLANES == 0 and D_FF % FFN_COLS == 0
    has_state = state is not None
    assert (rows == seq_len) if has_state else (nseq == 1)
    tps = seq_len // rows
    nbatch = n // seq_len
    row = lambda w: pl.BlockSpec((tm, w), lambda i: (i, 0))
    const = lambda a: pl.BlockSpec(a.shape, lambda i: (0,) * a.ndim, pipeline_mode=pl.Buffered(1))
    st_spec = pl.BlockSpec((nseq, FFN_CONV - 1, 2 * D_FF), lambda i: (i // tps, 0, 0))
    ins = [ofox, ogdn, x2d, w_o, norm_ffn_w, w_up, conv_w, conv_b, w_down, norm_final_w]
    in_specs = [row(FOX_WIDTH), row(GDN_WIDTH), row(D_MODEL)] + [const(a) for a in ins[3:]]
    scratch = [pltpu.VMEM((nseq, SUBLANES + rows, FFN_COLS), F32),
               pltpu.VMEM((nseq, SUBLANES + rows, FFN_COLS), F32),
               pltpu.VMEM((tm, D_FF), BF16)]
    if has_state:
        ins.append(state)
        in_specs.append(st_spec)
    else:
        scratch.append(pltpu.VMEM((nseq, SUBLANES, 2 * D_FF), F32))
    return pl.pallas_call(
        functools.partial(_ffn_kernel, nseq=nseq, rows=rows, tiles_per_seq=tps, has_state=has_state),
        grid=(n // tm,),
        in_specs=in_specs,
        out_specs=[row(D_MODEL), st_spec],
        out_shape=[jax.ShapeDtypeStruct((n, D_MODEL), F32),
                   jax.ShapeDtypeStruct((nbatch, FFN_CONV - 1, 2 * D_FF), F32)],
        scratch_shapes=scratch,
        compiler_params=pltpu.CompilerParams(dimension_semantics=("arbitrary",),
                                             vmem_limit_bytes=VMEM_LIMIT),
        name="ffn_sample" if has_state else "ffn_prompt",
    )(*ins)


ROW_TILE = 512
SAMPLE_KV_BLOCK = 4096
GDN_CHUNK = 128
GDN_CHUNKS_PER_STEP = 4
GDN_SAMPLE_SEQS_PER_STEP = 4


def _prep_weights(w_in, b_fox_f, gdn_a_log, gdn_dt_bias):
    c0 = 3 * FOX_WIDTH
    c1 = c0 + FOX_HEADS
    c2 = c1 + GDN_CONV_CH
    c3 = c2 + 2 * GDN_HEADS
    w_in = w_in.astype(BF16)
    small = jnp.concatenate([w_in[:, c0:c1], w_in[:, c2:c3],
                             jnp.zeros((D_MODEL, LANES - FOX_HEADS - 2 * GDN_HEADS), BF16)], axis=1)
    w_cat = jnp.concatenate([w_in[:, :c0], w_in[:, c1:c2], w_in[:, c3:], small], axis=1)
    pad = jnp.zeros((LANES - FOX_HEADS - GDN_HEADS,), F32)
    bias = jnp.concatenate([b_fox_f.astype(F32), gdn_dt_bias.astype(F32), pad])
    alog = jnp.concatenate([jnp.zeros((FOX_HEADS,), F32), gdn_a_log.astype(F32), pad])
    prm = jnp.zeros((SUBLANES, LANES), F32).at[0].set(bias).at[1].set(alog)
    return w_cat, prm


def _layer(xp, xs, fox_k, fox_v, fox_logf, st_gdn, st_gconv, st_fconv,
           w_in, b_fox_f, gdn_conv_w, gdn_a_log, gdn_dt_bias, gdn_norm_w, w_o,
           norm_mix_w, norm_ffn_w, w_up, ffn_conv_w, ffn_conv_b, w_down, norm_out_w):
    bp, sp, _ = xp.shape
    bs, ts, _ = xs.shape
    w_cat, prm = _prep_weights(w_in, b_fox_f, gdn_a_log, gdn_dt_bias)
    nmw = norm_mix_w.reshape(1, D_MODEL).astype(F32)
    nfw = norm_ffn_w.reshape(1, D_MODEL).astype(F32)
    now = norm_out_w.reshape(1, D_MODEL).astype(F32)
    gnw = gdn_norm_w.reshape(1, GDN_DV).astype(F32)
    w_o_b, w_up_b, w_dn_b = w_o.astype(BF16), w_up.astype(BF16), w_down.astype(BF16)
    cb = ffn_conv_b.reshape(1, 2 * D_FF).astype(F32)

    xp2 = xp.reshape(bp * sp, D_MODEL)
    q, k, v, kb, qa, ka, va, gtail, gact, gz, sm = _inproj(xp2, nmw, w_cat, prm, gdn_conv_w.astype(F32),
                                                          tm=ROW_TILE, seq_len=sp, prompt=True)
    r3 = lambda a: a.reshape(bp, sp, a.shape[-1])
    o_fox = _fox_prompt(r3(q), r3(qa), r3(kb), r3(ka), va, t=ROW_TILE)
    o_gdn, p_state = _gdn(r3(gact), None, r3(sm), r3(gz), None, None, gnw,
                          c=GDN_CHUNK, nb=1, cpb=GDN_CHUNKS_PER_STEP)
    yp, p_fconv = _ffn(o_fox.reshape(bp * sp, FOX_WIDTH), o_gdn.reshape(bp * sp, GDN_WIDTH), xp2,
                       w_o_b, nfw, w_up_b, ffn_conv_w, cb, w_dn_b, now, None,
                       nseq=1, rows=ROW_TILE, seq_len=sp)
    p_out = (jnp.transpose(k, (0, 3, 1, 2)), jnp.transpose(v, (0, 3, 1, 2)),
             r3(sm)[:, :, LOGF_LANE:LOGF_LANE + FOX_HEADS], p_state,
             gtail[:, SUBLANES - (GDN_CONV - 1):], p_fconv)

    xs2 = xs.reshape(bs * ts, D_MODEL)
    q, k, v, gqkv, gz, sm = _inproj(xs2, nmw, w_cat, prm, tm=bs * ts, seq_len=ts, prompt=False)
    s3 = lambda a: a.reshape(bs, ts, a.shape[-1])
    past = fox_k.shape[1]
    lft = jnp.swapaxes(fox_logf.astype(F32), 1, 2)
    cache_t = lambda a: jnp.transpose(a.astype(F32), (0, 2, 3, 1))
    o_fox = _fox_sample(s3(q), s3(k), s3(v), s3(sm), lft, cache_t(fox_k), cache_t(fox_v),
                        kb=min(SAMPLE_KV_BLOCK, past))
    gqkv3 = s3(gqkv)
    hist = jnp.concatenate([jnp.zeros((bs, SUBLANES - (GDN_CONV - 1), GDN_CONV_CH), F32),
                            st_gconv.astype(F32)], axis=1)
    o_gdn, s_state = _gdn(gqkv3, hist, s3(sm), s3(gz), st_gdn.astype(F32), gdn_conv_w, gnw,
                          c=ts, nb=GDN_SAMPLE_SEQS_PER_STEP, cpb=1)
    ys, s_fconv = _ffn(o_fox.reshape(bs * ts, FOX_WIDTH), o_gdn.reshape(bs * ts, GDN_WIDTH), xs2,
                       w_o_b, nfw, w_up_b, ffn_conv_w, cb, w_dn_b, now, st_fconv.astype(F32),
                       nseq=bs, rows=ts, seq_len=ts)
    full = jnp.concatenate([st_gconv.astype(F32), gqkv3], axis=1)
    s_out = (k.reshape(bs, ts, FOX_HEADS, FOX_DIM), v.reshape(bs, ts, FOX_HEADS, FOX_DIM),
             s3(sm)[:, :, LOGF_LANE:LOGF_LANE + FOX_HEADS], s_state,
             full[:, ts:], s_fconv)
    return yp.reshape(bp, sp, D_MODEL), ys.reshape(bs, ts, D_MODEL), p_out, s_out


def kernel(x_prompt, x_sample, cache_fox_k, cache_fox_v, cache_fox_logf, state_gdn, state_gdn_conv,
           state_ffn_conv, w_in, b_fox_f, gdn_conv_w, gdn_a_log, gdn_dt_bias, gdn_norm_w, w_o, norm_mix_w,
           norm_ffn_w, w_up, ffn_conv_w, ffn_conv_b, w_down, norm_final_w):
    depth = w_in.shape[0]
    assert depth == 1, "the final RMSNorm is fused into the single layer's FFN kernel"
    yp, ys, p_out, s_out = _layer(
        x_prompt, x_sample, cache_fox_k[0], cache_fox_v[0], cache_fox_logf[0], state_gdn[0],
        state_gdn_conv[0], state_ffn_conv[0], w_in[0], b_fox_f[0], gdn_conv_w[0], gdn_a_log[0],
        gdn_dt_bias[0], gdn_norm_w[0], w_o[0], norm_mix_w[0], norm_ffn_w[0], w_up[0], ffn_conv_w[0],
        ffn_conv_b[0], w_down[0], norm_final_w)
    return (yp, ys) + tuple(a[None] for a in p_out) + tuple(a[None] for a in s_out)
```

```python
import functools
import math

import numpy as np
import jax
import jax.numpy as jnp
from jax import lax
from jax.experimental import pallas as pl
from jax.experimental.pallas import tpu as pltpu

F32 = jnp.float32
BF16 = jnp.bfloat16

D_MODEL = 1024
FOX_DIM = 64
FOX_HEADS = 8
FOX_WIDTH = FOX_HEADS * FOX_DIM
GDN_DK = 128
GDN_DV = 128
GDN_HEADS = 4
GDN_QK = GDN_HEADS * GDN_DK
GDN_WIDTH = GDN_HEADS * GDN_DV
GDN_CONV_CH = 2 * GDN_QK + GDN_WIDTH
GDN_CONV = 4
D_FF = 2816
FFN_CONV = 3
EPS = 1e-6

LANES = 128
SUBLANES = 8
VMEM_LIMIT = 56 * 1024 * 1024
MASKED = -1e30
LOG2E = math.log2(math.e)

LOGF_LANE = 0
G_LANE = FOX_HEADS
BETA_LANE = FOX_HEADS + GDN_HEADS

COL_FOX = 0
COL_GQKV = 3 * FOX_WIDTH
COL_GZ = COL_GQKV + GDN_CONV_CH
COL_SMALL = COL_GZ + GDN_WIDTH
N_COLS = COL_SMALL + LANES


def _dot(a, b):
    return jnp.dot(a, b, preferred_element_type=F32)


def _dot_nt(a, b):
    return lax.dot_general(a, b, (((1,), (1,)), ((), ())), preferred_element_type=F32)


def _dot_tn(a, b):
    return lax.dot_general(a, b, (((0,), (0,)), ((), ())), preferred_element_type=F32)


def _split3(x):
    hi = x.astype(BF16)
    r = x - hi.astype(F32)
    lo = r.astype(BF16)
    lo2 = (r - lo.astype(F32)).astype(BF16)
    return hi, lo, lo2


def _exact_left(mat, x):
    hi, lo, lo2 = _split3(x)
    return _dot(mat, hi) + _dot(mat, lo) + _dot(mat, lo2)


def _exact_right(x, mat):
    hi, lo, lo2 = _split3(x)
    return _dot(hi, mat) + _dot(lo, mat) + _dot(lo2, mat)


def _dot3(a, b):
    ah = a.astype(BF16)
    al = (a - ah.astype(F32)).astype(BF16)
    bh = b.astype(BF16)
    bl = (b - bh.astype(F32)).astype(BF16)
    return _dot(ah, bh) + (_dot(ah, bl) + _dot(al, bh))


def _exact_left_c(mat, x):
    return _dot(jnp.concatenate([mat, mat, mat], axis=1), jnp.concatenate(_split3(x), axis=0))


def _split2(x):
    hi = x.astype(BF16)
    return hi, (x - hi.astype(F32)).astype(BF16)


def _dot3_parts(ah, al, bh, bl):
    return _dot(jnp.concatenate([ah, ah, al], axis=1), jnp.concatenate([bh, bl, bh], axis=0))


def _dot3c(a, b):
    return _dot3_parts(*_split2(a), *_split2(b))


def _sigmoid(x):
    return 1.0 / (1.0 + jnp.exp(-x))


def _tri(n, strict=False):
    r = lax.broadcasted_iota(jnp.int32, (n, n), 0)
    c = lax.broadcasted_iota(jnp.int32, (n, n), 1)
    return (r > c) if strict else (r >= c)


def _inproj_kernel(x_ref, nw_ref, w_ref, prm_ref, *refs, tiles_per_seq, prompt):
    if prompt:
        (aug_ref, augc_ref, cw_ref, q_ref, k_ref, v_ref, kb_ref, qa_ref, ka_ref, va_ref, tail_ref,
         g_ref, z_ref, sm_ref, carry_ref, win_ref) = refs
    else:
        q_ref, k_ref, v_ref, g_ref, z_ref, sm_ref = refs
    x = x_ref[...]
    xn = x * lax.rsqrt(jnp.mean(x * x, axis=-1, keepdims=True) + EPS) * nw_ref[...]
    xb = xn.astype(BF16)

    tm = x.shape[0]
    gq = _dot(xb, w_ref[:, COL_GQKV:COL_GZ])
    conv_cols = []
    if not prompt:
        g_ref[...] = gq
    else:
        first = pl.program_id(0) % tiles_per_seq == 0

        @pl.when(first)
        def _():
            win_ref[0:SUBLANES, :] = jnp.zeros((SUBLANES, GDN_CONV_CH), F32)

        @pl.when(jnp.logical_not(first))
        def _():
            win_ref[0:SUBLANES, :] = win_ref[tm:tm + SUBLANES, :]

        win_ref[SUBLANES:SUBLANES + tm, :] = gq
        tail_ref[0] = gq[tm - SUBLANES:tm, :]
        conv_cols = list(range(0, GDN_CONV_CH, LANES))

    def conv_some(count):
        for _ in range(min(count, len(conv_cols))):
            col = conv_cols.pop(0)
            g_ref[:, col:col + LANES] = _gdn_activate(
                _short_conv(win_ref, cw_ref, SUBLANES - (GDN_CONV - 1), tm, col), col)

    q_scale = FOX_DIM ** -0.5 * (LOG2E if prompt else 1.0)
    q_ref[...] = (_dot(xb, w_ref[:, 0:FOX_WIDTH]) * q_scale).astype(BF16)
    conv_some(3)
    kf = _dot(xb, w_ref[:, FOX_WIDTH:2 * FOX_WIDTH])
    if not prompt:
        k_ref[...] = kf
    else:
        k_ref[0] = kf.T.reshape(FOX_HEADS, FOX_DIM, tm)
        kb_ref[...] = kf.astype(BF16)
    conv_some(3)
    vf = _dot(xb, w_ref[:, 2 * FOX_WIDTH:3 * FOX_WIDTH])
    if not prompt:
        v_ref[...] = vf
    else:
        vt = vf.T.reshape(FOX_HEADS, FOX_DIM, tm)
        v_ref[0] = vt
        va_ref[0, :, 0:FOX_DIM, :] = vt.astype(BF16)
        va_ref[0, :, FOX_DIM:, :] = jnp.ones((FOX_HEADS, ONES_ROWS, tm), BF16)
    conv_some(3)
    z_ref[...] = _dot(xb, w_ref[:, COL_GZ:COL_SMALL])
    conv_some(3)

    z = _dot(xb, w_ref[:, COL_SMALL:N_COLS])
    lane = lax.broadcasted_iota(jnp.int32, z.shape, 1)
    zb = z + prm_ref[0:1, :]
    e = jnp.exp(-jnp.abs(zb))
    l1p = jnp.log(1.0 + e)
    logf = jnp.minimum(zb, 0.0) - l1p
    g = -jnp.exp(prm_ref[1:2, :]) * (jnp.maximum(zb, 0.0) + l1p)
    beta = _sigmoid(z)
    sm = jnp.where(lane < G_LANE, logf,
                   jnp.where(lane < BETA_LANE, g,
                             jnp.where(lane < BETA_LANE + GDN_HEADS, beta, 0.0)))
    sm_ref[...] = sm

    if prompt:
        i = pl.program_id(0)

        @pl.when(i % tiles_per_seq == 0)
        def _():
            carry_ref[...] = jnp.zeros_like(carry_ref)

        c = _exact_left(_tri(tm).astype(BF16), sm) + carry_ref[0:1, :]
        carry_ref[0:1, :] = c[tm - 1:tm, :]
        parts = jnp.concatenate(_split3(c * LOG2E), axis=1)
        aug = _dot(parts, aug_ref[...]) + augc_ref[...]
        qa_ref[...] = aug[:, :FOX_WIDTH].astype(BF16)
        ka_ref[...] = aug[:, FOX_WIDTH:].astype(BF16)


AUG_LANES = 6
ONES_ROWS = 16
FOX_QUERY_SPLITS = 2


def _aug_tables():
    place = np.zeros((3 * LANES, 2 * FOX_WIDTH), np.float32)
    ones = np.zeros((1, 2 * FOX_WIDTH), np.float32)
    for h in range(FOX_HEADS):
        q0 = (h // 2) * LANES + (h % 2) * AUG_LANES
        k0 = FOX_WIDTH + q0
        for part in range(3):
            place[part * LANES + LOGF_LANE + h, q0 + part] = 1.0
            place[part * LANES + LOGF_LANE + h, k0 + 3 + part] = -1.0
            ones[0, q0 + 3 + part] = 1.0
            ones[0, k0 + part] = 1.0
    return jnp.asarray(place, BF16), jnp.asarray(ones, F32)


def _inproj(x2d, norm_w, w_cat, prm, conv_w=None, *, tm, seq_len, prompt):
    n = x2d.shape[0]
    assert n % tm == 0 and (not prompt or seq_len % tm == 0)
    tps = seq_len // tm if prompt else 1
    row = lambda w: pl.BlockSpec((tm, w), lambda i: (i, 0))
    const = lambda a: pl.BlockSpec(a.shape, lambda i: (0,) * a.ndim, pipeline_mode=pl.Buffered(1))
    ins = [x2d, norm_w, w_cat, prm]
    out_shape = [jax.ShapeDtypeStruct((n, FOX_WIDTH), BF16),
                 jax.ShapeDtypeStruct((n, FOX_WIDTH), F32),
                 jax.ShapeDtypeStruct((n, FOX_WIDTH), F32)]
    out_specs = [row(FOX_WIDTH), row(FOX_WIDTH), row(FOX_WIDTH)]
    scratch = []
    if prompt:
        kv_t = jax.ShapeDtypeStruct((n // seq_len, FOX_HEADS, FOX_DIM, seq_len), F32)
        out_shape[1:3] = [kv_t, kv_t]
        out_specs[1:3] = [pl.BlockSpec((1, FOX_HEADS, FOX_DIM, tm), lambda i: (i // tps, 0, 0, i % tps))] * 2
        ins += list(_aug_tables()) + [conv_w]
        out_shape += [jax.ShapeDtypeStruct((n, FOX_WIDTH), BF16)] * 3 + [
            jax.ShapeDtypeStruct((n // seq_len, FOX_HEADS, FOX_DIM + ONES_ROWS, seq_len), BF16)]
        out_specs += [row(FOX_WIDTH)] * 3 + [
            pl.BlockSpec((1, FOX_HEADS, FOX_DIM + ONES_ROWS, tm), lambda i: (i // tps, 0, 0, i % tps))]
        out_shape += [jax.ShapeDtypeStruct((n // seq_len, SUBLANES, GDN_CONV_CH), F32)]
        out_specs += [pl.BlockSpec((1, SUBLANES, GDN_CONV_CH), lambda i: (i // tps, 0, 0))]
        scratch = [pltpu.VMEM((SUBLANES, LANES), F32), pltpu.VMEM((SUBLANES + tm, GDN_CONV_CH), F32)]
    out_shape += [jax.ShapeDtypeStruct((n, GDN_CONV_CH), F32),
                  jax.ShapeDtypeStruct((n, GDN_WIDTH), F32),
                  jax.ShapeDtypeStruct((n, LANES), F32)]
    out_specs += [row(GDN_CONV_CH), row(GDN_WIDTH), row(LANES)]
    return pl.pallas_call(
        functools.partial(_inproj_kernel, tiles_per_seq=tps, prompt=prompt),
        grid=(n // tm,),
        in_specs=[row(D_MODEL)] + [const(a) for a in ins[1:]],
        out_specs=out_specs, out_shape=out_shape, scratch_shapes=scratch,
        compiler_params=pltpu.CompilerParams(dimension_semantics=("arbitrary",),
                                             vmem_limit_bytes=VMEM_LIMIT),
        name="inproj_prompt" if prompt else "inproj_sample",
    )(*ins)


def _fox_prompt_kernel(q_ref, qa_ref, k_ref, ka_ref, va_ref, o_ref, sa_ref, sb_ref, *, t):
    i = pl.program_id(2)
    q = q_ref[0]
    qa = qa_ref[0]
    lane = lax.broadcasted_iota(jnp.int32, (t, LANES), 1)
    low = lane < FOX_DIM
    zero = jnp.zeros_like(q)
    qcat = []
    for h in range(2):
        mine = (lane >= h * AUG_LANES) & (lane < (h + 1) * AUG_LANES)
        qcat.append(jnp.concatenate([jnp.where(low if h == 0 else jnp.logical_not(low), q, zero),
                                     jnp.where(mine, qa, zero)], axis=1))

    tq = t // FOX_QUERY_SPLITS
    units = [(h, u) for h in range(2) for u in range(FOX_QUERY_SPLITS)]
    qunit = [qcat[h][u * tq:(u + 1) * tq] for h, u in units]

    def logits(j, s_ref):
        start = pl.multiple_of(j * t, t)
        kcat = jnp.concatenate([k_ref[0, pl.ds(start, t), :], ka_ref[0, pl.ds(start, t), :]], axis=1)
        for n in range(len(units)):
            s_ref[n] = _dot_nt(kcat, qunit[n])

    def consume(j, s_ref, carry, diag):
        start = pl.multiple_of(j * t, t)
        out = []
        for n, (h, u) in enumerate(units):
            m, acc = carry[n]
            nk = (u + 1) * tq if diag else t
            s = s_ref[n, 0:nk, :]
            if diag:
                key = lax.broadcasted_iota(jnp.int32, (nk, tq), 0)
                query = lax.broadcasted_iota(jnp.int32, (nk, tq), 1) + u * tq
                s = jnp.where(key <= query, s, MASKED)
            m_new = jnp.maximum(m, jnp.max(s, axis=0, keepdims=True))
            p = jnp.exp2(s - m_new).astype(BF16)
            acc = jnp.exp2(m - m_new) * acc + _dot(va_ref[0, h, :, pl.ds(start, nk)], p)
            out.append((m_new, acc))
        return tuple(out)

    init = tuple((jnp.full((1, tq), MASKED, F32), jnp.zeros((FOX_DIM + ONES_ROWS, tq), F32)) for _ in units)
    logits(0, sa_ref)

    def two_blocks(jj, carry):
        j = 2 * jj
        logits(j + 1, sb_ref)
        carry = consume(j, sa_ref, carry, False)
        logits(j + 2, sa_ref)
        return consume(j + 1, sb_ref, carry, False)

    carry = lax.fori_loop(0, i // 2, two_blocks, init)

    def odd_tail(carry):
        logits(i, sb_ref)
        return consume(i, sb_ref, consume(i - 1, sa_ref, carry, False), True)

    carry = lax.cond(i % 2 == 1, odd_tail, lambda c: consume(i, sa_ref, c, True), carry)
    heads = [jnp.concatenate([carry[h * FOX_QUERY_SPLITS + u][1] for u in range(FOX_QUERY_SPLITS)], axis=1)
             for h in range(2)]
    o_t = jnp.concatenate([a[:FOX_DIM] / a[FOX_DIM:FOX_DIM + 1] for a in heads], axis=0)
    o_ref[0] = o_t.T.astype(BF16)


def _fox_prompt(q, qa, kb, ka, va, *, t):
    b, s, _ = q.shape
    nblk = s // t
    assert s % t == 0
    npair = FOX_HEADS // 2
    qblk = pl.BlockSpec((1, t, LANES), lambda bi, p, i: (bi, i, p))
    kblk = pl.BlockSpec((1, s, LANES), lambda bi, p, i: (bi, 0, p))
    return pl.pallas_call(
        functools.partial(_fox_prompt_kernel, t=t),
        grid=(b, npair, nblk),
        in_specs=[qblk, qblk, kblk, kblk,
                  pl.BlockSpec((1, 2, FOX_DIM + ONES_ROWS, s), lambda bi, p, i: (bi, p, 0, 0))],
        out_specs=qblk,
        out_shape=jax.ShapeDtypeStruct((b, s, FOX_WIDTH), BF16),
        scratch_shapes=[pltpu.VMEM((2 * FOX_QUERY_SPLITS, t, t // FOX_QUERY_SPLITS), F32)] * 2,
        compiler_params=pltpu.CompilerParams(
            dimension_semantics=("arbitrary", "arbitrary", "arbitrary"), vmem_limit_bytes=VMEM_LIMIT),
        name="fox_prompt",
    )(q, qa, kb, ka, va)


def _fox_sample_kernel(q_ref, kn_ref, vn_ref, sm_ref, lft_ref, kt_ref, vt_ref, o_ref,
                       rk_ref, m_ref, l_ref, acc_ref, *, kb, t):
    j = pl.program_id(1)
    nkv = pl.num_programs(1)

    def heads(x):
        return [x[:, h * FOX_DIM:(h + 1) * FOX_DIM] for h in range(FOX_HEADS)]

    qh = [x.astype(BF16) for x in heads(q_ref[0].astype(F32))]

    lf_new = sm_ref[0]
    cn = _exact_left(_tri(t).astype(BF16), lf_new)

    @pl.when(j == 0)
    def _():
        p = lft_ref.shape[2]
        nb = p // LANES
        x = jnp.concatenate([lft_ref[0, :, i * LANES:(i + 1) * LANES] for i in range(nb)], axis=0)
        r = lax.broadcasted_iota(jnp.int32, (LANES, LANES), 0)
        cc = lax.broadcasted_iota(jnp.int32, (LANES, LANES), 1)
        y = _exact_right(x, (r > cc).astype(BF16))
        tot = jnp.sum(x, axis=1, keepdims=True)
        off = jnp.zeros((FOX_HEADS, 1), F32)
        for i in range(nb - 1, -1, -1):
            rk_ref[:, i * LANES:(i + 1) * LANES] = y[i * FOX_HEADS:(i + 1) * FOX_HEADS, :] + off
            off = off + tot[i * FOX_HEADS:(i + 1) * FOX_HEADS, :]
        m_ref[...] = jnp.full_like(m_ref, MASKED)
        l_ref[...] = jnp.zeros_like(l_ref)
        acc_ref[...] = jnp.zeros_like(acc_ref)

    def update(s, pv):
        m = m_ref[:, 0:1]
        m_new = jnp.maximum(m, jnp.max(s, axis=1, keepdims=True))
        alpha = jnp.exp(m - m_new)
        p = jnp.exp(s - m_new)
        l_ref[...] = jnp.broadcast_to(alpha * l_ref[:, 0:1] + jnp.sum(p, axis=1, keepdims=True), l_ref.shape)
        acc_ref[...] = alpha * acc_ref[...] + pv(p.astype(BF16))
        m_ref[...] = jnp.broadcast_to(m_new, m_ref.shape)

    start = pl.multiple_of(j * kb, kb)
    rk = rk_ref[:, pl.ds(start, kb)]
    s = jnp.concatenate([_dot(qh[h], kt_ref[0, h].astype(BF16)) + (cn[:, h:h + 1] + rk[h:h + 1, :])
                         for h in range(FOX_HEADS)], axis=0)
    update(s, lambda p: jnp.concatenate(
        [_dot_nt(p[h * t:(h + 1) * t], vt_ref[0, h].astype(BF16)) for h in range(FOX_HEADS)], axis=0))

    @pl.when(j == nkv - 1)
    def _():
        r = lax.broadcasted_iota(jnp.int32, (t, t), 0)
        cc = lax.broadcasted_iota(jnp.int32, (t, t), 1)
        after = (r > cc).astype(F32)
        tri = _tri(t).astype(BF16)
        knh = heads(kn_ref[0])
        vnh = [x.astype(BF16) for x in heads(vn_ref[0])]
        s_new = jnp.concatenate(
            [jnp.where(_tri(t), _dot_nt(qh[h], knh[h].astype(BF16))
                       + _exact_left(tri, lf_new[:, h:h + 1] * after), MASKED) for h in range(FOX_HEADS)], axis=0)
        update(s_new, lambda p: jnp.concatenate(
            [_dot(p[h * t:(h + 1) * t], vnh[h]) for h in range(FOX_HEADS)], axis=0))
        o_full = acc_ref[...] / l_ref[:, 0:1]
        o_ref[0] = jnp.concatenate([o_full[h * t:(h + 1) * t] for h in range(FOX_HEADS)], axis=1).astype(BF16)


def _fox_sample(q, kn, vn, sm, lft, kt, vt, *, kb):
    b, t, _ = q.shape
    p = kt.shape[3]
    assert p % kb == 0 and kb % LANES == 0
    rows = FOX_HEADS * t
    per_b = lambda shape: pl.BlockSpec((1,) + shape, lambda bi, j: (bi, 0, 0))
    cache = pl.BlockSpec((1, FOX_HEADS, FOX_DIM, kb), lambda bi, j: (bi, 0, 0, j))
    return pl.pallas_call(
        functools.partial(_fox_sample_kernel, kb=kb, t=t),
        grid=(b, p // kb),
        in_specs=[per_b((t, FOX_WIDTH)), per_b((t, FOX_WIDTH)), per_b((t, FOX_WIDTH)), per_b((t, LANES)),
                  per_b((FOX_HEADS, p)), cache, cache],
        out_specs=per_b((t, FOX_WIDTH)),
        out_shape=jax.ShapeDtypeStruct((b, t, FOX_WIDTH), BF16),
        scratch_shapes=[pltpu.VMEM((FOX_HEADS, p), F32),
                        pltpu.VMEM((rows, LANES), F32), pltpu.VMEM((rows, LANES), F32),
                        pltpu.VMEM((rows, FOX_DIM), F32)],
        compiler_params=pltpu.CompilerParams(dimension_semantics=("arbitrary", "arbitrary"),
                                             vmem_limit_bytes=VMEM_LIMIT),
        name="fox_sample",
    )(q, kn, vn, sm, lft, kt, vt)


def _interleave(gens):
    results = [None] * len(gens)
    alive = list(range(len(gens)))
    while alive:
        for idx in list(alive):
            try:
                next(gens[idx])
            except StopIteration as stop:
                results[idx] = stop.value
                alive.remove(idx)
    return results


def _gdn_activate(acc, col):
    x = acc * _sigmoid(acc)
    if col >= 2 * GDN_QK:
        return x
    scale = GDN_DK ** -0.5 if col < GDN_QK else 1.0
    return x * (lax.rsqrt(jnp.sum(x * x, axis=-1, keepdims=True) + EPS) * scale)


def _short_conv(win_ref, cw_ref, row0, nrows, col, lead=()):
    acc = None
    for tap in range(GDN_CONV):
        term = (win_ref[lead + (slice(row0 + tap, row0 + tap + nrows), slice(col, col + LANES))]
                * cw_ref[tap:tap + 1, col:col + LANES])
        acc = term if acc is None else acc + term
    return acc


def _gdn_local(qn, kn, vh, g, g_row, g_last, beta, c):
    wide = c % LANES == 0
    dot3 = _dot3c if wide else _dot3
    incl = _tri(c)
    strict = _tri(c, strict=True)
    eg = jnp.exp(g)
    decay = jnp.exp(jnp.where(incl, g - g_row, MASKED))
    kbeta = kn * beta
    kq = _dot_nt(jnp.concatenate([kbeta, qn], axis=0).astype(BF16), kn.astype(BF16))
    yield
    lmat = jnp.where(strict, kq[:c] * decay, 0.0)
    a_qk = kq[c:] * decay

    tinv = (incl & jnp.logical_not(strict)).astype(F32) - lmat
    if wide:
        lh, ll = _split2(lmat)
        power = _dot3_parts(lh, ll, lh, ll)
    else:
        power = dot3(lmat, lmat)
    yield
    span = 2
    while 2 * span < c:
        if wide:
            ph, pl_ = _split2(power)
            th, tl = _split2(tinv)
            both = _dot3_parts(ph, pl_, jnp.concatenate([th, ph], axis=1), jnp.concatenate([tl, pl_], axis=1))
            yield
            tinv = tinv + both[:, :c]
            power = both[:, c:]
        else:
            step = dot3(power, tinv)
            power = dot3(power, power)
            yield
            tinv = tinv + step
        span *= 2
    step = dot3(power, tinv)
    yield
    tinv = tinv + step

    sol = dot3(tinv, jnp.concatenate([vh * beta, kbeta * eg], axis=1))
    yield
    kd = kn * jnp.exp(g_last - g)
    return sol[:, :GDN_DV], sol[:, GDN_DV:], qn * eg, a_qk, (kd.T if wide else kd)


def _gdn_state_step(locals_, states, g_lasts, c):
    wide = c % LANES == 0
    ws = [_dot(jnp.concatenate([w, qeg], axis=0).astype(BF16), sh.astype(BF16))
          for (_, w, qeg, _, _), sh in zip(locals_, states)]
    outs, new_states = [], []
    for (u, _, _, a_qk, kd), wsh, sh, g_last in zip(locals_, ws, states, g_lasts):
        db = (u - wsh[:c]).astype(BF16)
        decayed = sh * jnp.exp(g_last)
        if wide:
            ak = _dot(jnp.concatenate([a_qk, kd], axis=0).astype(BF16), db)
            outs.append(wsh[c:] + ak[:c])
            new_states.append(decayed + ak[c:])
        else:
            outs.append(wsh[c:] + _dot(a_qk.astype(BF16), db))
            new_states.append(decayed + _dot_tn(kd.astype(BF16), db))
    return outs, new_states


def _gdn_kernel(cur_ref, *refs, c, nb, cpb, activated):
    if activated:
        sm_ref, z_ref, nw_ref, o_ref, sout_ref, s_ref = refs
    else:
        hist_ref, sm_ref, z_ref, s0_ref, cw_ref, nw_ref, o_ref, sout_ref, xs_ref, s_ref = refs
    i = pl.program_id(1)
    rows = c * cpb
    wide = c % LANES == 0

    @pl.when(i == 0)
    def _():
        if activated:
            s_ref[...] = jnp.zeros_like(s_ref)
        else:
            s_ref[...] = s0_ref[...]

    if activated:
        def qkv(b, r0, col):
            return cur_ref[b, r0:r0 + c, col:col + LANES]
    else:
        xs_ref[:, 0:SUBLANES, :] = hist_ref[...]
        xs_ref[:, SUBLANES:SUBLANES + rows, :] = cur_ref[...]
        base = SUBLANES - (GDN_CONV - 1)

        def qkv(b, r0, col):
            return _gdn_activate(_short_conv(xs_ref, cw_ref, base + r0, c, col, lead=(b,)), col)

    incl = _tri(c)
    eye = incl & jnp.logical_not(_tri(c, strict=True))
    gens, g_lasts = [], []
    for b in range(nb):
        for ck in range(cpb):
            r0 = ck * c
            sm = sm_ref[b, r0:r0 + c, :]
            if wide:
                gs = _exact_left_c(incl.astype(BF16), sm)
                gst = gs.T
            else:
                gs = _exact_left(incl.astype(BF16), sm)
            for h in range(GDN_HEADS):
                g = gs[:, G_LANE + h:G_LANE + h + 1]
                if wide:
                    g_row = gst[G_LANE + h:G_LANE + h + 1, :]
                else:
                    g_row = _exact_left(jnp.ones((c, c), BF16), jnp.where(eye, g, 0.0))
                g_last = gs[c - 1:c, G_LANE + h:G_LANE + h + 1]
                g_lasts.append(g_last)
                gens.append(_gdn_local(
                    qkv(b, r0, h * GDN_DK), qkv(b, r0, GDN_QK + h * GDN_DK),
                    qkv(b, r0, 2 * GDN_QK + h * GDN_DV), g, g_row, g_last,
                    sm[:, BETA_LANE + h:BETA_LANE + h + 1], c))
    locals_ = _interleave(gens)

    states = [s_ref[b, h] for b in range(nb) for h in range(GDN_HEADS)]
    for ck in range(cpb):
        r0 = ck * c
        pick = [(b * cpb + ck) * GDN_HEADS + h for b in range(nb) for h in range(GDN_HEADS)]
        outs, states = _gdn_state_step([locals_[u] for u in pick], states, [g_lasts[u] for u in pick], c)
        for idx, o in enumerate(outs):
            b, h = divmod(idx, GDN_HEADS)
            on = o * lax.rsqrt(jnp.mean(o * o, axis=-1, keepdims=True) + EPS) * nw_ref[...]
            zh = z_ref[b, r0:r0 + c, h * GDN_DV:(h + 1) * GDN_DV]
            o_ref[b, r0:r0 + c, h * GDN_DV:(h + 1) * GDN_DV] = (on * (zh * _sigmoid(zh))).astype(BF16)
    for idx, st in enumerate(states):
        b, h = divmod(idx, GDN_HEADS)
        s_ref[b, h] = st

    @pl.when(i == pl.num_programs(1) - 1)
    def _():
        sout_ref[...] = s_ref[...]


def _gdn(gqkv, hist, sm, gz, s0, conv_w, norm_w, *, c, nb, cpb):
    b, t, _ = gqkv.shape
    rows = c * cpb
    assert t % rows == 0 and b % nb == 0 and c % SUBLANES == 0
    activated = hist is None
    blk = lambda w: pl.BlockSpec((nb, rows, w), lambda bi, i: (bi, i, 0))
    const = lambda a: pl.BlockSpec(a.shape, lambda bi, i: (0,) * a.ndim)
    state_spec = pl.BlockSpec((nb, GDN_HEADS, GDN_DK, GDN_DV), lambda bi, i: (bi, 0, 0, 0))
    scratch = [pltpu.VMEM((nb, GDN_HEADS, GDN_DK, GDN_DV), F32)]
    if activated:
        ins = [gqkv, sm, gz, norm_w]
        in_specs = [blk(GDN_CONV_CH), blk(LANES), blk(GDN_WIDTH), const(norm_w)]
    else:
        assert t == rows
        ins = [gqkv, hist, sm, gz, s0, conv_w, norm_w]
        in_specs = [blk(GDN_CONV_CH), pl.BlockSpec((nb, SUBLANES, GDN_CONV_CH), lambda bi, i: (bi, 0, 0)),
                    blk(LANES), blk(GDN_WIDTH), state_spec, const(conv_w), const(norm_w)]
        scratch = [pltpu.VMEM((nb, SUBLANES + rows, GDN_CONV_CH), F32)] + scratch
    return pl.pallas_call(
        functools.partial(_gdn_kernel, c=c, nb=nb, cpb=cpb, activated=activated),
        grid=(b // nb, t // rows),
        in_specs=in_specs,
        out_specs=[blk(GDN_WIDTH), state_spec],
        out_shape=[jax.ShapeDtypeStruct((b, t, GDN_WIDTH), BF16),
                   jax.ShapeDtypeStruct((b, GDN_HEADS, GDN_DK, GDN_DV), F32)],
        scratch_shapes=scratch,
        compiler_params=pltpu.CompilerParams(dimension_semantics=("arbitrary", "arbitrary"),
                                             vmem_limit_bytes=VMEM_LIMIT),
        name="gdn_prompt" if activated else "gdn_sample",
    )(*ins)


FFN_COLS = 256


def _ffn_kernel(of_ref, og_ref, x_ref, wo_ref, nfw_ref, wup_ref, cw_ref, cb_ref, wdn_ref, nlw_ref, *refs,
                nseq, rows, tiles_per_seq, has_state):
    if has_state:
        st_ref, y_ref, nst_ref, wg_ref, wv_ref, act_ref = refs
    else:
        y_ref, nst_ref, wg_ref, wv_ref, act_ref, carry_ref = refs
    i = pl.program_id(0)
    hist0 = SUBLANES - (FFN_CONV - 1)

    h = (x_ref[...] + _dot(of_ref[...], wo_ref[0:FOX_WIDTH, :])
         + _dot(og_ref[...], wo_ref[FOX_WIDTH:FOX_WIDTH + GDN_WIDTH, :]))
    hn = (h * lax.rsqrt(jnp.mean(h * h, axis=-1, keepdims=True) + EPS) * nfw_ref[...]).astype(BF16)

    if not has_state:
        @pl.when(i % tiles_per_seq == 0)
        def _():
            carry_ref[...] = jnp.zeros_like(carry_ref)

    def conv_cols(buf_ref, col):
        up = _dot(hn, wup_ref[:, col:col + FFN_COLS])
        buf_ref[:, SUBLANES:SUBLANES + rows, :] = up.reshape(nseq, rows, FFN_COLS)
        if has_state:
            buf_ref[:, hist0:SUBLANES, :] = st_ref[:, :, col:col + FFN_COLS]
        else:
            buf_ref[:, hist0:SUBLANES, :] = carry_ref[:, hist0:SUBLANES, col:col + FFN_COLS]
        last = buf_ref[:, SUBLANES + rows - (FFN_CONV - 1):SUBLANES + rows, :]
        nst_ref[:, :, col:col + FFN_COLS] = last
        if not has_state:
            carry_ref[:, hist0:SUBLANES, col:col + FFN_COLS] = last
        uc = cb_ref[:, col:col + FFN_COLS][None]
        for tap in range(FFN_CONV):
            uc = uc + buf_ref[:, hist0 + tap:hist0 + tap + rows, :] * cw_ref[tap:tap + 1, col:col + FFN_COLS][None]
        return uc.reshape(nseq * rows, FFN_COLS)

    for ci in range(D_FF // FFN_COLS):
        col = ci * FFN_COLS
        gate = conv_cols(wg_ref, col)
        val = conv_cols(wv_ref, D_FF + col)
        act_ref[:, col:col + FFN_COLS] = (gate * _sigmoid(gate) * val).astype(BF16)

    h2 = h + _dot(act_ref[...], wdn_ref[...])
    y_ref[...] = h2 * lax.rsqrt(jnp.mean(h2 * h2, axis=-1, keepdims=True) + EPS) * nlw_ref[...]


def _ffn(ofox, ogdn, x2d, w_o, norm_ffn_w, w_up, conv_w, conv_b, w_down, norm_final_w, state, *,
         nseq, rows, seq_len):
    n = x2d.shape[0]
    tm = nseq * rows
    assert n % tm == 0 and seq_len % rows == 0 and rows % SUBLANES == 0 and D_FF % FFN_COLS == 0
    has_state = state is not None
    assert (rows == seq_len) if has_state else (nseq == 1)
    tps = seq_len // rows
    nbatch = n // seq_len
    row = lambda w: pl.BlockSpec((tm, w), lambda i: (i, 0))
    const = lambda a: pl.BlockSpec(a.shape, lambda i: (0,) * a.ndim, pipeline_mode=pl.Buffered(1))
    st_spec = pl.BlockSpec((nseq, FFN_CONV - 1, 2 * D_FF), lambda i: (i // tps, 0, 0))
    ins = [ofox, ogdn, x2d, w_o, norm_ffn_w, w_up, conv_w, conv_b, w_down, norm_final_w]
    in_specs = [row(FOX_WIDTH), row(GDN_WIDTH), row(D_MODEL)] + [const(a) for a in ins[3:]]
    scratch = [pltpu.VMEM((nseq, SUBLANES + rows, FFN_COLS), F32),
               pltpu.VMEM((nseq, SUBLANES + rows, FFN_COLS), F32),
               pltpu.VMEM((tm, D_FF), BF16)]
    if has_state:
        ins.append(state)
        in_specs.append(st_spec)
    else:
        scratch.append(pltpu.VMEM((nseq, SUBLANES, 2 * D_FF), F32))
    return pl.pallas_call(
        functools.partial(_ffn_kernel, nseq=nseq, rows=rows, tiles_per_seq=tps, has_state=has_state),
        grid=(n // tm,),
        in_specs=in_specs,
        out_specs=[row(D_MODEL), st_spec],
        out_shape=[jax.ShapeDtypeStruct((n, D_MODEL), F32),
                   jax.ShapeDtypeStruct((nbatch, FFN_CONV - 1, 2 * D_FF), F32)],
        scratch_shapes=scratch,
        compiler_params=pltpu.CompilerParams(dimension_semantics=("arbitrary",),
                                             vmem_limit_bytes=VMEM_LIMIT),
        name="ffn_sample" if has_state else "ffn_prompt",
    )(*ins)


ROW_TILE = 512
SAMPLE_KV_BLOCK = 4096
GDN_CHUNK = 128
GDN_CHUNKS_PER_STEP = 4
GDN_SAMPLE_SEQS_PER_STEP = 4


def _prep_weights(w_in, b_fox_f, gdn_a_log, gdn_dt_bias):
    c0 = 3 * FOX_WIDTH
    c1 = c0 + FOX_HEADS
    c2 = c1 + GDN_CONV_CH
    c3 = c2 + 2 * GDN_HEADS
    wt = jnp.swapaxes(w_in, 0, 1)
    small = jnp.concatenate([wt[c0:c1], wt[c2:c3],
                             jnp.zeros((LANES - FOX_HEADS - 2 * GDN_HEADS, D_MODEL), wt.dtype)], axis=0)
    w_cat = jnp.swapaxes(jnp.concatenate([wt[:c0], wt[c1:c2], wt[c3:], small], axis=0), 0, 1).astype(BF16)
    pad = jnp.zeros((LANES - FOX_HEADS - GDN_HEADS,), F32)
    bias = jnp.concatenate([b_fox_f.astype(F32), gdn_dt_bias.astype(F32), pad])
    alog = jnp.concatenate([jnp.zeros((FOX_HEADS,), F32), gdn_a_log.astype(F32), pad])
    prm = jnp.zeros((SUBLANES, LANES), F32).at[0].set(bias).at[1].set(alog)
    return w_cat, prm


def _layer(xp, xs, fox_k, fox_v, fox_logf, st_gdn, st_gconv, st_fconv,
           w_in, b_fox_f, gdn_conv_w, gdn_a_log, gdn_dt_bias, gdn_norm_w, w_o,
           norm_mix_w, norm_ffn_w, w_up, ffn_conv_w, ffn_conv_b, w_down, norm_out_w):
    bp, sp, _ = xp.shape
    bs, ts, _ = xs.shape
    w_cat, prm = _prep_weights(w_in, b_fox_f, gdn_a_log, gdn_dt_bias)
    nmw = norm_mix_w.reshape(1, D_MODEL).astype(F32)
    nfw = norm_ffn_w.reshape(1, D_MODEL).astype(F32)
    now = norm_out_w.reshape(1, D_MODEL).astype(F32)
    gnw = gdn_norm_w.reshape(1, GDN_DV).astype(F32)
    w_o_b, w_up_b, w_dn_b = w_o.astype(BF16), w_up.astype(BF16), w_down.astype(BF16)
    cb = ffn_conv_b.reshape(1, 2 * D_FF).astype(F32)

    xp2 = xp.reshape(bp * sp, D_MODEL)
    q, k, v, kb, qa, ka, va, gtail, gact, gz, sm = _inproj(xp2, nmw, w_cat, prm, gdn_conv_w.astype(F32),
                                                          tm=ROW_TILE, seq_len=sp, prompt=True)
    r3 = lambda a: a.reshape(bp, sp, a.shape[-1])
    o_fox = _fox_prompt(r3(q), r3(qa), r3(kb), r3(ka), va, t=ROW_TILE)
    o_gdn, p_state = _gdn(r3(gact), None, r3(sm), r3(gz), None, None, gnw,
                          c=GDN_CHUNK, nb=1, cpb=GDN_CHUNKS_PER_STEP)
    yp, p_fconv = _ffn(o_fox.reshape(bp * sp, FOX_WIDTH), o_gdn.reshape(bp * sp, GDN_WIDTH), xp2,
                       w_o_b, nfw, w_up_b, ffn_conv_w, cb, w_dn_b, now, None,
                       nseq=1, rows=ROW_TILE, seq_len=sp)
    p_out = (jnp.transpose(k, (0, 3, 1, 2)), jnp.transpose(v, (0, 3, 1, 2)),
             r3(sm)[:, :, LOGF_LANE:LOGF_LANE + FOX_HEADS], p_state,
             gtail[:, SUBLANES - (GDN_CONV - 1):], p_fconv)

    xs2 = xs.reshape(bs * ts, D_MODEL)
    q, k, v, gqkv, gz, sm = _inproj(xs2, nmw, w_cat, prm, tm=bs * ts, seq_len=ts, prompt=False)
    s3 = lambda a: a.reshape(bs, ts, a.shape[-1])
    past = fox_k.shape[1]
    lft = jnp.swapaxes(fox_logf.astype(F32), 1, 2)
    cache_t = lambda a: jnp.transpose(a.astype(F32), (0, 2, 3, 1))
    o_fox = _fox_sample(s3(q), s3(k), s3(v), s3(sm), lft, cache_t(fox_k), cache_t(fox_v),
                        kb=min(SAMPLE_KV_BLOCK, past))
    gqkv3 = s3(gqkv)
    hist = jnp.concatenate([jnp.zeros((bs, SUBLANES - (GDN_CONV - 1), GDN_CONV_CH), F32),
                            st_gconv.astype(F32)], axis=1)
    o_gdn, s_state = _gdn(gqkv3, hist, s3(sm), s3(gz), st_gdn.astype(F32), gdn_conv_w, gnw,
                          c=ts, nb=GDN_SAMPLE_SEQS_PER_STEP, cpb=1)
    ys, s_fconv = _ffn(o_fox.reshape(bs * ts, FOX_WIDTH), o_gdn.reshape(bs * ts, GDN_WIDTH), xs2,
                       w_o_b, nfw, w_up_b, ffn_conv_w, cb, w_dn_b, now, st_fconv.astype(F32),
                       nseq=bs, rows=ts, seq_len=ts)
    full = jnp.concatenate([st_gconv.astype(F32), gqkv3], axis=1)
    s_out = (k.reshape(bs, ts, FOX_HEADS, FOX_DIM), v.reshape(bs, ts, FOX_HEADS, FOX_DIM),
             s3(sm)[:, :, LOGF_LANE:LOGF_LANE + FOX_HEADS], s_state,
             full[:, ts:], s_fconv)
    return yp.reshape(bp, sp, D_MODEL), ys.reshape(bs, ts, D_MODEL), p_out, s_out


def kernel(x_prompt, x_sample, cache_fox_k, cache_fox_v, cache_fox_logf, state_gdn, state_gdn_conv,
           state_ffn_conv, w_in, b_fox_f, gdn_conv_w, gdn_a_log, gdn_dt_bias, gdn_norm_w, w_o, norm_mix_w,
           norm_ffn_w, w_up, ffn_conv_w, ffn_conv_b, w_down, norm_final_w):
    depth = w_in.shape[0]
    assert depth == 1, "the final RMSNorm is fused into the single layer's FFN kernel"
    yp, ys, p_out, s_out = _layer(
        x_prompt, x_sample, cache_fox_k[0], cache_fox_v[0], cache_fox_logf[0], state_gdn[0],
        state_gdn_conv[0], state_ffn_conv[0], w_in[0], b_fox_f[0], gdn_conv_w[0], gdn_a_log[0],
        gdn_dt_bias[0], gdn_norm_w[0], w_o[0], norm_mix_w[0], norm_ffn_w[0], w_up[0], ffn_conv_w[0],
        ffn_conv_b[0], w_down[0], norm_final_w)
    return (yp, ys) + tuple(a[None] for a in p_out) + tuple(a[None] for a in s_out)
```

```python
import functools
import math

import numpy as np
import jax
import jax.numpy as jnp
from jax import lax
from jax.experimental import pallas as pl
from jax.experimental.pallas import tpu as pltpu

F32 = jnp.float32
BF16 = jnp.bfloat16

D_MODEL = 1024
FOX_DIM = 64
FOX_HEADS = 8
FOX_WIDTH = FOX_HEADS * FOX_DIM
GDN_DK = 128
GDN_DV = 128
GDN_HEADS = 4
GDN_QK = GDN_HEADS * GDN_DK
GDN_WIDTH = GDN_HEADS * GDN_DV
GDN_CONV_CH = 2 * GDN_QK + GDN_WIDTH
GDN_CONV = 4
D_FF = 2816
FFN_CONV = 3
EPS = 1e-6

LANES = 128
SUBLANES = 8
VMEM_LIMIT = 56 * 1024 * 1024
MASKED = -1e30
LOG2E = math.log2(math.e)

LOGF_LANE = 0
G_LANE = FOX_HEADS
BETA_LANE = FOX_HEADS + GDN_HEADS

COL_FOX = 0
COL_GQKV = 3 * FOX_WIDTH
COL_GZ = COL_GQKV + GDN_CONV_CH
COL_SMALL = COL_GZ + GDN_WIDTH
N_COLS = COL_SMALL + LANES


def _dot(a, b):
    return jnp.dot(a, b, preferred_element_type=F32)


def _dot_nt(a, b):
    return lax.dot_general(a, b, (((1,), (1,)), ((), ())), preferred_element_type=F32)


def _dot_tn(a, b):
    return lax.dot_general(a, b, (((0,), (0,)), ((), ())), preferred_element_type=F32)


def _split3(x):
    hi = x.astype(BF16)
    r = x - hi.astype(F32)
    lo = r.astype(BF16)
    lo2 = (r - lo.astype(F32)).astype(BF16)
    return hi, lo, lo2


def _exact_left(mat, x):
    hi, lo, lo2 = _split3(x)
    return _dot(mat, hi) + _dot(mat, lo) + _dot(mat, lo2)


def _exact_right(x, mat):
    hi, lo, lo2 = _split3(x)
    return _dot(hi, mat) + _dot(lo, mat) + _dot(lo2, mat)


def _dot3(a, b):
    ah = a.astype(BF16)
    al = (a - ah.astype(F32)).astype(BF16)
    bh = b.astype(BF16)
    bl = (b - bh.astype(F32)).astype(BF16)
    return _dot(ah, bh) + (_dot(ah, bl) + _dot(al, bh))


def _exact_left_c(mat, x):
    return _dot(jnp.concatenate([mat, mat, mat], axis=1), jnp.concatenate(_split3(x), axis=0))


def _split2(x):
    hi = x.astype(BF16)
    return hi, (x - hi.astype(F32)).astype(BF16)


def _dot3_parts(ah, al, bh, bl):
    return _dot(jnp.concatenate([ah, ah, al], axis=1), jnp.concatenate([bh, bl, bh], axis=0))


def _dot3c(a, b):
    return _dot3_parts(*_split2(a), *_split2(b))


def _sigmoid(x):
    return 1.0 / (1.0 + jnp.exp(-x))


def _tri(n, strict=False):
    r = lax.broadcasted_iota(jnp.int32, (n, n), 0)
    c = lax.broadcasted_iota(jnp.int32, (n, n), 1)
    return (r > c) if strict else (r >= c)


def _inproj_kernel(x_ref, nw_ref, w_ref, prm_ref, *refs, tiles_per_seq, prompt):
    if prompt:
        (aug_ref, augc_ref, cw_ref, q_ref, k_ref, v_ref, kb_ref, qa_ref, ka_ref, va_ref, tail_ref,
         g_ref, z_ref, sm_ref, carry_ref, win_ref) = refs
    else:
        q_ref, k_ref, v_ref, g_ref, z_ref, sm_ref = refs
    x = x_ref[...]
    xn = x * lax.rsqrt(jnp.mean(x * x, axis=-1, keepdims=True) + EPS) * nw_ref[...]
    xb = xn.astype(BF16)

    tm = x.shape[0]
    gq = _dot(xb, w_ref[:, COL_GQKV:COL_GZ])
    conv_cols = []
    if not prompt:
        g_ref[...] = gq
    else:
        first = pl.program_id(0) % tiles_per_seq == 0

        @pl.when(first)
        def _():
            win_ref[0:SUBLANES, :] = jnp.zeros((SUBLANES, GDN_CONV_CH), F32)

        @pl.when(jnp.logical_not(first))
        def _():
            win_ref[0:SUBLANES, :] = win_ref[tm:tm + SUBLANES, :]

        win_ref[SUBLANES:SUBLANES + tm, :] = gq
        tail_ref[0] = gq[tm - SUBLANES:tm, :]
        conv_cols = list(range(0, GDN_CONV_CH, LANES))

    def conv_some(count):
        for _ in range(min(count, len(conv_cols))):
            col = conv_cols.pop(0)
            g_ref[:, col:col + LANES] = _gdn_activate(
                _short_conv(win_ref, cw_ref, SUBLANES - (GDN_CONV - 1), tm, col), col)

    q_scale = FOX_DIM ** -0.5 * (LOG2E if prompt else 1.0)
    q_ref[...] = (_dot(xb, w_ref[:, 0:FOX_WIDTH]) * q_scale).astype(BF16)
    conv_some(3)
    kf = _dot(xb, w_ref[:, FOX_WIDTH:2 * FOX_WIDTH])
    if not prompt:
        k_ref[...] = kf
    else:
        k_ref[0] = kf.T.reshape(FOX_HEADS, FOX_DIM, tm)
        kb_ref[...] = kf.astype(BF16)
    conv_some(3)
    vf = _dot(xb, w_ref[:, 2 * FOX_WIDTH:3 * FOX_WIDTH])
    if not prompt:
        v_ref[...] = vf
    else:
        vt = vf.T.reshape(FOX_HEADS, FOX_DIM, tm)
        v_ref[0] = vt
        va_ref[0, :, 0:FOX_DIM, :] = vt.astype(BF16)
        va_ref[0, :, FOX_DIM:, :] = jnp.ones((FOX_HEADS, ONES_ROWS, tm), BF16)
    conv_some(3)
    z_ref[...] = _dot(xb, w_ref[:, COL_GZ:COL_SMALL])
    conv_some(3)

    z = _dot(xb, w_ref[:, COL_SMALL:N_COLS])
    lane = lax.broadcasted_iota(jnp.int32, z.shape, 1)
    zb = z + prm_ref[0:1, :]
    e = jnp.exp(-jnp.abs(zb))
    l1p = jnp.log(1.0 + e)
    logf = jnp.minimum(zb, 0.0) - l1p
    g = -jnp.exp(prm_ref[1:2, :]) * (jnp.maximum(zb, 0.0) + l1p)
    beta = _sigmoid(z)
    sm = jnp.where(lane < G_LANE, logf,
                   jnp.where(lane < BETA_LANE, g,
                             jnp.where(lane < BETA_LANE + GDN_HEADS, beta, 0.0)))
    sm_ref[...] = sm

    if prompt:
        i = pl.program_id(0)

        @pl.when(i % tiles_per_seq == 0)
        def _():
            carry_ref[...] = jnp.zeros_like(carry_ref)

        c = _exact_left(_tri(tm).astype(BF16), sm) + carry_ref[0:1, :]
        carry_ref[0:1, :] = c[tm - 1:tm, :]
        parts = jnp.concatenate(_split3(c * LOG2E), axis=1)
        aug = _dot(parts, aug_ref[...]) + augc_ref[...]
        qa_ref[...] = aug[:, :FOX_WIDTH].astype(BF16)
        ka_ref[...] = aug[:, FOX_WIDTH:].astype(BF16)


AUG_LANES = 6
ONES_ROWS = 16
FOX_QUERY_SPLITS = 2


def _aug_tables():
    place = np.zeros((3 * LANES, 2 * FOX_WIDTH), np.float32)
    ones = np.zeros((1, 2 * FOX_WIDTH), np.float32)
    for h in range(FOX_HEADS):
        q0 = (h // 2) * LANES + (h % 2) * AUG_LANES
        k0 = FOX_WIDTH + q0
        for part in range(3):
            place[part * LANES + LOGF_LANE + h, q0 + part] = 1.0
            place[part * LANES + LOGF_LANE + h, k0 + 3 + part] = -1.0
            ones[0, q0 + 3 + part] = 1.0
            ones[0, k0 + part] = 1.0
    return jnp.asarray(place, BF16), jnp.asarray(ones, F32)


def _inproj(x2d, norm_w, w_cat, prm, conv_w=None, *, tm, seq_len, prompt):
    n = x2d.shape[0]
    assert n % tm == 0 and (not prompt or seq_len % tm == 0)
    tps = seq_len // tm if prompt else 1
    row = lambda w: pl.BlockSpec((tm, w), lambda i: (i, 0))
    const = lambda a: pl.BlockSpec(a.shape, lambda i: (0,) * a.ndim, pipeline_mode=pl.Buffered(1))
    ins = [x2d, norm_w, w_cat, prm]
    out_shape = [jax.ShapeDtypeStruct((n, FOX_WIDTH), BF16),
                 jax.ShapeDtypeStruct((n, FOX_WIDTH), F32),
                 jax.ShapeDtypeStruct((n, FOX_WIDTH), F32)]
    out_specs = [row(FOX_WIDTH), row(FOX_WIDTH), row(FOX_WIDTH)]
    scratch = []
    if prompt:
        kv_t = jax.ShapeDtypeStruct((n // seq_len, FOX_HEADS, FOX_DIM, seq_len), F32)
        out_shape[1:3] = [kv_t, kv_t]
        out_specs[1:3] = [pl.BlockSpec((1, FOX_HEADS, FOX_DIM, tm), lambda i: (i // tps, 0, 0, i % tps))] * 2
        ins += list(_aug_tables()) + [conv_w]
        out_shape += [jax.ShapeDtypeStruct((n, FOX_WIDTH), BF16)] * 3 + [
            jax.ShapeDtypeStruct((n // seq_len, FOX_HEADS, FOX_DIM + ONES_ROWS, seq_len), BF16)]
        out_specs += [row(FOX_WIDTH)] * 3 + [
            pl.BlockSpec((1, FOX_HEADS, FOX_DIM + ONES_ROWS, tm), lambda i: (i // tps, 0, 0, i % tps))]
        out_shape += [jax.ShapeDtypeStruct((n // seq_len, SUBLANES, GDN_CONV_CH), F32)]
        out_specs += [pl.BlockSpec((1, SUBLANES, GDN_CONV_CH), lambda i: (i // tps, 0, 0))]
        scratch = [pltpu.VMEM((SUBLANES, LANES), F32), pltpu.VMEM((SUBLANES + tm, GDN_CONV_CH), F32)]
    out_shape += [jax.ShapeDtypeStruct((n, GDN_CONV_CH), F32),
                  jax.ShapeDtypeStruct((n, GDN_WIDTH), F32),
                  jax.ShapeDtypeStruct((n, LANES), F32)]
    out_specs += [row(GDN_CONV_CH), row(GDN_WIDTH), row(LANES)]
    return pl.pallas_call(
        functools.partial(_inproj_kernel, tiles_per_seq=tps, prompt=prompt),
        grid=(n // tm,),
        in_specs=[row(D_MODEL)] + [const(a) for a in ins[1:]],
        out_specs=out_specs, out_shape=out_shape, scratch_shapes=scratch,
        compiler_params=pltpu.CompilerParams(dimension_semantics=("arbitrary",),
                                             vmem_limit_bytes=VMEM_LIMIT),
        name="inproj_prompt" if prompt else "inproj_sample",
    )(*ins)


def _fox_prompt_kernel(q_ref, qa_ref, k_ref, ka_ref, va_ref, o_ref, sa_ref, sb_ref, *, t):
    i = pl.program_id(2)
    q = q_ref[0]
    qa = qa_ref[0]
    lane = lax.broadcasted_iota(jnp.int32, (t, LANES), 1)
    low = lane < FOX_DIM
    zero = jnp.zeros_like(q)
    qcat = []
    for h in range(2):
        mine = (lane >= h * AUG_LANES) & (lane < (h + 1) * AUG_LANES)
        qcat.append(jnp.concatenate([jnp.where(low if h == 0 else jnp.logical_not(low), q, zero),
                                     jnp.where(mine, qa, zero)], axis=1))

    tq = t // FOX_QUERY_SPLITS
    units = [(h, u) for h in range(2) for u in range(FOX_QUERY_SPLITS)]
    qunit = [qcat[h][u * tq:(u + 1) * tq] for h, u in units]

    def logits(j, s_ref):
        start = pl.multiple_of(j * t, t)
        kcat = jnp.concatenate([k_ref[0, pl.ds(start, t), :], ka_ref[0, pl.ds(start, t), :]], axis=1)
        for n in range(len(units)):
            s_ref[n] = _dot_nt(kcat, qunit[n])

    def consume(j, s_ref, carry, diag):
        start = pl.multiple_of(j * t, t)
        out = []
        for n, (h, u) in enumerate(units):
            m, acc = carry[n]
            nk = (u + 1) * tq if diag else t
            s = s_ref[n, 0:nk, :]
            if diag:
                key = lax.broadcasted_iota(jnp.int32, (nk, tq), 0)
                query = lax.broadcasted_iota(jnp.int32, (nk, tq), 1) + u * tq
                s = jnp.where(key <= query, s, MASKED)
            m_new = jnp.maximum(m, jnp.max(s, axis=0, keepdims=True))
            p = jnp.exp2(s - m_new).astype(BF16)
            acc = jnp.exp2(m - m_new) * acc + _dot(va_ref[0, h, :, pl.ds(start, nk)], p)
            out.append((m_new, acc))
        return tuple(out)

    init = tuple((jnp.full((1, tq), MASKED, F32), jnp.zeros((FOX_DIM + ONES_ROWS, tq), F32)) for _ in units)
    logits(i, sa_ref)
    logits(0, sb_ref)
    carry = consume(i, sa_ref, init, True)

    def two_blocks(jj, carry):
        j = 2 * jj
        logits(j + 1, sa_ref)
        carry = consume(j, sb_ref, carry, False)
        logits(j + 2, sb_ref)
        return consume(j + 1, sa_ref, carry, False)

    carry = lax.fori_loop(0, i // 2, two_blocks, carry)
    carry = lax.cond(i % 2 == 1, lambda c: consume(i - 1, sb_ref, c, False), lambda c: c, carry)
    heads = [jnp.concatenate([carry[h * FOX_QUERY_SPLITS + u][1] for u in range(FOX_QUERY_SPLITS)], axis=1)
             for h in range(2)]
    o_t = jnp.concatenate([a[:FOX_DIM] / a[FOX_DIM:FOX_DIM + 1] for a in heads], axis=0)
    o_ref[0] = o_t.T.astype(BF16)


def _fox_prompt(q, qa, kb, ka, va, *, t):
    b, s, _ = q.shape
    nblk = s // t
    assert s % t == 0
    npair = FOX_HEADS // 2
    qblk = pl.BlockSpec((1, t, LANES), lambda bi, p, i: (bi, i, p))
    kblk = pl.BlockSpec((1, s, LANES), lambda bi, p, i: (bi, 0, p))
    return pl.pallas_call(
        functools.partial(_fox_prompt_kernel, t=t),
        grid=(b, npair, nblk),
        in_specs=[qblk, qblk, kblk, kblk,
                  pl.BlockSpec((1, 2, FOX_DIM + ONES_ROWS, s), lambda bi, p, i: (bi, p, 0, 0))],
        out_specs=qblk,
        out_shape=jax.ShapeDtypeStruct((b, s, FOX_WIDTH), BF16),
        scratch_shapes=[pltpu.VMEM((2 * FOX_QUERY_SPLITS, t, t // FOX_QUERY_SPLITS), F32)] * 2,
        compiler_params=pltpu.CompilerParams(
            dimension_semantics=("arbitrary", "arbitrary", "arbitrary"), vmem_limit_bytes=VMEM_LIMIT),
        name="fox_prompt",
    )(q, qa, kb, ka, va)


def _fox_sample_kernel(q_ref, kn_ref, vn_ref, sm_ref, lft_ref, kt_ref, vt_ref, o_ref,
                       rk_ref, m_ref, l_ref, acc_ref, *, kb, t):
    j = pl.program_id(1)
    nkv = pl.num_programs(1)

    def heads(x):
        return [x[:, h * FOX_DIM:(h + 1) * FOX_DIM] for h in range(FOX_HEADS)]

    qh = [x.astype(BF16) for x in heads(q_ref[0].astype(F32))]

    lf_new = sm_ref[0]
    cn = _exact_left(_tri(t).astype(BF16), lf_new)

    @pl.when(j == 0)
    def _():
        p = lft_ref.shape[2]
        nb = p // LANES
        x = jnp.concatenate([lft_ref[0, :, i * LANES:(i + 1) * LANES] for i in range(nb)], axis=0)
        r = lax.broadcasted_iota(jnp.int32, (LANES, LANES), 0)
        cc = lax.broadcasted_iota(jnp.int32, (LANES, LANES), 1)
        y = _exact_right(x, (r > cc).astype(BF16))
        tot = jnp.sum(x, axis=1, keepdims=True)
        off = jnp.zeros((FOX_HEADS, 1), F32)
        for i in range(nb - 1, -1, -1):
            rk_ref[:, i * LANES:(i + 1) * LANES] = y[i * FOX_HEADS:(i + 1) * FOX_HEADS, :] + off
            off = off + tot[i * FOX_HEADS:(i + 1) * FOX_HEADS, :]
        m_ref[...] = jnp.full_like(m_ref, MASKED)
        l_ref[...] = jnp.zeros_like(l_ref)
        acc_ref[...] = jnp.zeros_like(acc_ref)

    def update(s, pv):
        m = m_ref[:, 0:1]
        m_new = jnp.maximum(m, jnp.max(s, axis=1, keepdims=True))
        alpha = jnp.exp(m - m_new)
        p = jnp.exp(s - m_new)
        l_ref[...] = jnp.broadcast_to(alpha * l_ref[:, 0:1] + jnp.sum(p, axis=1, keepdims=True), l_ref.shape)
        acc_ref[...] = alpha * acc_ref[...] + pv(p.astype(BF16))
        m_ref[...] = jnp.broadcast_to(m_new, m_ref.shape)

    start = pl.multiple_of(j * kb, kb)
    rk = rk_ref[:, pl.ds(start, kb)]
    s = jnp.concatenate([_dot(qh[h], kt_ref[0, h].astype(BF16)) + (cn[:, h:h + 1] + rk[h:h + 1, :])
                         for h in range(FOX_HEADS)], axis=0)
    update(s, lambda p: jnp.concatenate(
        [_dot_nt(p[h * t:(h + 1) * t], vt_ref[0, h].astype(BF16)) for h in range(FOX_HEADS)], axis=0))

    @pl.when(j == nkv - 1)
    def _():
        r = lax.broadcasted_iota(jnp.int32, (t, t), 0)
        cc = lax.broadcasted_iota(jnp.int32, (t, t), 1)
        after = (r > cc).astype(F32)
        tri = _tri(t).astype(BF16)
        knh = heads(kn_ref[0])
        vnh = [x.astype(BF16) for x in heads(vn_ref[0])]
        s_new = jnp.concatenate(
            [jnp.where(_tri(t), _dot_nt(qh[h], knh[h].astype(BF16))
                       + _exact_left(tri, lf_new[:, h:h + 1] * after), MASKED) for h in range(FOX_HEADS)], axis=0)
        update(s_new, lambda p: jnp.concatenate(
            [_dot(p[h * t:(h + 1) * t], vnh[h]) for h in range(FOX_HEADS)], axis=0))
        o_full = acc_ref[...] / l_ref[:, 0:1]
        o_ref[0] = jnp.concatenate([o_full[h * t:(h + 1) * t] for h in range(FOX_HEADS)], axis=1).astype(BF16)


def _fox_sample(q, kn, vn, sm, lft, kt, vt, *, kb):
    b, t, _ = q.shape
    p = kt.shape[3]
    assert p % kb == 0 and kb % LANES == 0
    rows = FOX_HEADS * t
    per_b = lambda shape: pl.BlockSpec((1,) + shape, lambda bi, j: (bi, 0, 0))
    cache = pl.BlockSpec((1, FOX_HEADS, FOX_DIM, kb), lambda bi, j: (bi, 0, 0, j))
    return pl.pallas_call(
        functools.partial(_fox_sample_kernel, kb=kb, t=t),
        grid=(b, p // kb),
        in_specs=[per_b((t, FOX_WIDTH)), per_b((t, FOX_WIDTH)), per_b((t, FOX_WIDTH)), per_b((t, LANES)),
                  per_b((FOX_HEADS, p)), cache, cache],
        out_specs=per_b((t, FOX_WIDTH)),
        out_shape=jax.ShapeDtypeStruct((b, t, FOX_WIDTH), BF16),
        scratch_shapes=[pltpu.VMEM((FOX_HEADS, p), F32),
                        pltpu.VMEM((rows, LANES), F32), pltpu.VMEM((rows, LANES), F32),
                        pltpu.VMEM((rows, FOX_DIM), F32)],
        compiler_params=pltpu.CompilerParams(dimension_semantics=("arbitrary", "arbitrary"),
                                             vmem_limit_bytes=VMEM_LIMIT),
        name="fox_sample",
    )(q, kn, vn, sm, lft, kt, vt)


def _interleave(gens):
    results = [None] * len(gens)
    alive = list(range(len(gens)))
    while alive:
        for idx in list(alive):
            try:
                next(gens[idx])
            except StopIteration as stop:
                results[idx] = stop.value
                alive.remove(idx)
    return results


def _gdn_activate(acc, col):
    x = acc * _sigmoid(acc)
    if col >= 2 * GDN_QK:
        return x
    scale = GDN_DK ** -0.5 if col < GDN_QK else 1.0
    return x * (lax.rsqrt(jnp.sum(x * x, axis=-1, keepdims=True) + EPS) * scale)


def _short_conv(win_ref, cw_ref, row0, nrows, col, lead=()):
    acc = None
    for tap in range(GDN_CONV):
        term = (win_ref[lead + (slice(row0 + tap, row0 + tap + nrows), slice(col, col + LANES))]
                * cw_ref[tap:tap + 1, col:col + LANES])
        acc = term if acc is None else acc + term
    return acc


def _gdn_local(qn, kn, vh, g, g_row, g_last, beta, c):
    wide = c % LANES == 0
    dot3 = _dot3c if wide else _dot3
    incl = _tri(c)
    strict = _tri(c, strict=True)
    eg = jnp.exp(g)
    decay = jnp.exp(jnp.where(incl, g - g_row, MASKED))
    kbeta = kn * beta
    kq = _dot_nt(jnp.concatenate([kbeta, qn], axis=0).astype(BF16), kn.astype(BF16))
    yield
    lmat = jnp.where(strict, kq[:c] * decay, 0.0)
    a_qk = kq[c:] * decay

    tinv = (incl & jnp.logical_not(strict)).astype(F32) - lmat
    if wide:
        lh, ll = _split2(lmat)
        power = _dot3_parts(lh, ll, lh, ll)
    else:
        power = dot3(lmat, lmat)
    yield
    span = 2
    while 2 * span < c:
        if wide:
            ph, pl_ = _split2(power)
            th, tl = _split2(tinv)
            both = _dot3_parts(ph, pl_, jnp.concatenate([th, ph], axis=1), jnp.concatenate([tl, pl_], axis=1))
            yield
            tinv = tinv + both[:, :c]
            power = both[:, c:]
        else:
            step = dot3(power, tinv)
            power = dot3(power, power)
            yield
            tinv = tinv + step
        span *= 2
    step = dot3(power, tinv)
    yield
    tinv = tinv + step

    sol = dot3(tinv, jnp.concatenate([vh * beta, kbeta * eg], axis=1))
    yield
    kd = kn * jnp.exp(g_last - g)
    return sol[:, :GDN_DV], sol[:, GDN_DV:], qn * eg, a_qk, (kd.T if wide else kd)


def _gdn_state_step(locals_, states, g_lasts, c):
    wide = c % LANES == 0
    ws = [_dot(jnp.concatenate([w, qeg], axis=0).astype(BF16), sh.astype(BF16))
          for (_, w, qeg, _, _), sh in zip(locals_, states)]
    outs, new_states = [], []
    for (u, _, _, a_qk, kd), wsh, sh, g_last in zip(locals_, ws, states, g_lasts):
        db = (u - wsh[:c]).astype(BF16)
        decayed = sh * jnp.exp(g_last)
        if wide:
            ak = _dot(jnp.concatenate([a_qk, kd], axis=0).astype(BF16), db)
            outs.append(wsh[c:] + ak[:c])
            new_states.append(decayed + ak[c:])
        else:
            outs.append(wsh[c:] + _dot(a_qk.astype(BF16), db))
            new_states.append(decayed + _dot_tn(kd.astype(BF16), db))
    return outs, new_states


def _gdn_kernel(cur_ref, *refs, c, nb, cpb, activated):
    if activated:
        sm_ref, z_ref, nw_ref, o_ref, sout_ref, s_ref = refs
    else:
        hist_ref, sm_ref, z_ref, s0_ref, cw_ref, nw_ref, o_ref, sout_ref, xs_ref, s_ref = refs
    i = pl.program_id(1)
    rows = c * cpb
    wide = c % LANES == 0

    @pl.when(i == 0)
    def _():
        if activated:
            s_ref[...] = jnp.zeros_like(s_ref)
        else:
            s_ref[...] = s0_ref[...]

    if activated:
        def qkv(b, r0, col):
            return cur_ref[b, r0:r0 + c, col:col + LANES]
    else:
        xs_ref[:, 0:SUBLANES, :] = hist_ref[...]
        xs_ref[:, SUBLANES:SUBLANES + rows, :] = cur_ref[...]
        base = SUBLANES - (GDN_CONV - 1)

        def qkv(b, r0, col):
            return _gdn_activate(_short_conv(xs_ref, cw_ref, base + r0, c, col, lead=(b,)), col)

    incl = _tri(c)
    eye = incl & jnp.logical_not(_tri(c, strict=True))
    gens, g_lasts = [], []
    for b in range(nb):
        for ck in range(cpb):
            r0 = ck * c
            sm = sm_ref[b, r0:r0 + c, :]
            if wide:
                gs = _exact_left_c(incl.astype(BF16), sm)
                gst = gs.T
            else:
                gs = _exact_left(incl.astype(BF16), sm)
            for h in range(GDN_HEADS):
                g = gs[:, G_LANE + h:G_LANE + h + 1]
                if wide:
                    g_row = gst[G_LANE + h:G_LANE + h + 1, :]
                else:
                    g_row = _exact_left(jnp.ones((c, c), BF16), jnp.where(eye, g, 0.0))
                g_last = gs[c - 1:c, G_LANE + h:G_LANE + h + 1]
                g_lasts.append(g_last)
                gens.append(_gdn_local(
                    qkv(b, r0, h * GDN_DK), qkv(b, r0, GDN_QK + h * GDN_DK),
                    qkv(b, r0, 2 * GDN_QK + h * GDN_DV), g, g_row, g_last,
                    sm[:, BETA_LANE + h:BETA_LANE + h + 1], c))
    locals_ = _interleave(gens)

    states = [s_ref[b, h] for b in range(nb) for h in range(GDN_HEADS)]
    for ck in range(cpb):
        r0 = ck * c
        pick = [(b * cpb + ck) * GDN_HEADS + h for b in range(nb) for h in range(GDN_HEADS)]
        outs, states = _gdn_state_step([locals_[u] for u in pick], states, [g_lasts[u] for u in pick], c)
        for idx, o in enumerate(outs):
            b, h = divmod(idx, GDN_HEADS)
            on = o * lax.rsqrt(jnp.mean(o * o, axis=-1, keepdims=True) + EPS) * nw_ref[...]
            zh = z_ref[b, r0:r0 + c, h * GDN_DV:(h + 1) * GDN_DV]
            o_ref[b, r0:r0 + c, h * GDN_DV:(h + 1) * GDN_DV] = (on * (zh * _sigmoid(zh))).astype(BF16)
    for idx, st in enumerate(states):
        b, h = divmod(idx, GDN_HEADS)
        s_ref[b, h] = st

    @pl.when(i == pl.num_programs(1) - 1)
    def _():
        sout_ref[...] = s_ref[...]


def _gdn(gqkv, hist, sm, gz, s0, conv_w, norm_w, *, c, nb, cpb):
    b, t, _ = gqkv.shape
    rows = c * cpb
    assert t % rows == 0 and b % nb == 0 and c % SUBLANES == 0
    activated = hist is None
    blk = lambda w: pl.BlockSpec((nb, rows, w), lambda bi, i: (bi, i, 0))
    const = lambda a: pl.BlockSpec(a.shape, lambda bi, i: (0,) * a.ndim)
    state_spec = pl.BlockSpec((nb, GDN_HEADS, GDN_DK, GDN_DV), lambda bi, i: (bi, 0, 0, 0))
    scratch = [pltpu.VMEM((nb, GDN_HEADS, GDN_DK, GDN_DV), F32)]
    if activated:
        ins = [gqkv, sm, gz, norm_w]
        in_specs = [blk(GDN_CONV_CH), blk(LANES), blk(GDN_WIDTH), const(norm_w)]
    else:
        assert t == rows
        ins = [gqkv, hist, sm, gz, s0, conv_w, norm_w]
        in_specs = [blk(GDN_CONV_CH), pl.BlockSpec((nb, SUBLANES, GDN_CONV_CH), lambda bi, i: (bi, 0, 0)),
                    blk(LANES), blk(GDN_WIDTH), state_spec, const(conv_w), const(norm_w)]
        scratch = [pltpu.VMEM((nb, SUBLANES + rows, GDN_CONV_CH), F32)] + scratch
    return pl.pallas_call(
        functools.partial(_gdn_kernel, c=c, nb=nb, cpb=cpb, activated=activated),
        grid=(b // nb, t // rows),
        in_specs=in_specs,
        out_specs=[blk(GDN_WIDTH), state_spec],
        out_shape=[jax.ShapeDtypeStruct((b, t, GDN_WIDTH), BF16),
                   jax.ShapeDtypeStruct((b, GDN_HEADS, GDN_DK, GDN_DV), F32)],
        scratch_shapes=scratch,
        compiler_params=pltpu.CompilerParams(dimension_semantics=("arbitrary", "arbitrary"),
                                             vmem_limit_bytes=VMEM_LIMIT),
        name="gdn_prompt" if activated else "gdn_sample",
    )(*ins)


FFN_COLS = 256


def _ffn_kernel(of_ref, og_ref, x_ref, wo_ref, nfw_ref, wup_ref, cw_ref, cb_ref, wdn_ref, nlw_ref, *refs,
                nseq, rows, tiles_per_seq, has_state):
    if has_state:
        st_ref, y_ref, nst_ref, wg_ref, wv_ref, act_ref = refs
    else:
        y_ref, nst_ref, wg_ref, wv_ref, act_ref, carry_ref = refs
    i = pl.program_id(0)
    hist0 = SUBLANES - (FFN_CONV - 1)

    h = (x_ref[...] + _dot(of_ref[...], wo_ref[0:FOX_WIDTH, :])
         + _dot(og_ref[...], wo_ref[FOX_WIDTH:FOX_WIDTH + GDN_WIDTH, :]))
    hn = (h * lax.rsqrt(jnp.mean(h * h, axis=-1, keepdims=True) + EPS) * nfw_ref[...]).astype(BF16)

    if not has_state:
        @pl.when(i % tiles_per_seq == 0)
        def _():
            carry_ref[...] = jnp.zeros_like(carry_ref)

    def conv_cols(buf_ref, col):
        up = _dot(hn, wup_ref[:, col:col + FFN_COLS])
        buf_ref[:, SUBLANES:SUBLANES + rows, :] = up.reshape(nseq, rows, FFN_COLS)
        if has_state:
            buf_ref[:, hist0:SUBLANES, :] = st_ref[:, :, col:col + FFN_COLS]
        else:
            buf_ref[:, hist0:SUBLANES, :] = carry_ref[:, hist0:SUBLANES, col:col + FFN_COLS]
        last = buf_ref[:, SUBLANES + rows - (FFN_CONV - 1):SUBLANES + rows, :]
        nst_ref[:, :, col:col + FFN_COLS] = last
        if not has_state:
            carry_ref[:, hist0:SUBLANES, col:col + FFN_COLS] = last
        uc = cb_ref[:, col:col + FFN_COLS][None]
        for tap in range(FFN_CONV):
            uc = uc + buf_ref[:, hist0 + tap:hist0 + tap + rows, :] * cw_ref[tap:tap + 1, col:col + FFN_COLS][None]
        return uc.reshape(nseq * rows, FFN_COLS)

    for ci in range(D_FF // FFN_COLS):
        col = ci * FFN_COLS
        gate = conv_cols(wg_ref, col)
        val = conv_cols(wv_ref, D_FF + col)
        act_ref[:, col:col + FFN_COLS] = (gate * _sigmoid(gate) * val).astype(BF16)

    h2 = h + _dot(act_ref[...], wdn_ref[...])
    y_ref[...] = h2 * lax.rsqrt(jnp.mean(h2 * h2, axis=-1, keepdims=True) + EPS) * nlw_ref[...]


def _ffn(ofox, ogdn, x2d, w_o, norm_ffn_w, w_up, conv_w, conv_b, w_down, norm_final_w, state, *,
         nseq, rows, seq_len):
    n = x2d.shape[0]
    tm = nseq * rows
    assert n % tm == 0 and seq_len % rows == 0 and rows % SUBLANES == 0 and D_FF % FFN_COLS == 0
    has_state = state is not None
    assert (rows == seq_len) if has_state else (nseq == 1)
    tps = seq_len // rows
    nbatch = n // seq_len
    row = lambda w: pl.BlockSpec((tm, w), lambda i: (i, 0))
    const = lambda a: pl.BlockSpec(a.shape, lambda i: (0,) * a.ndim, pipeline_mode=pl.Buffered(1))
    st_spec = pl.BlockSpec((nseq, FFN_CONV - 1, 2 * D_FF), lambda i: (i // tps, 0, 0))
    ins = [ofox, ogdn, x2d, w_o, norm_ffn_w, w_up, conv_w, conv_b, w_down, norm_final_w]
    in_specs = [row(FOX_WIDTH), row(GDN_WIDTH), row(D_MODEL)] + [const(a) for a in ins[3:]]
    scratch = [pltpu.VMEM((nseq, SUBLANES + rows, FFN_COLS), F32),
               pltpu.VMEM((nseq, SUBLANES + rows, FFN_COLS), F32),
               pltpu.VMEM((tm, D_FF), BF16)]
    if has_state:
        ins.append(state)
        in_specs.append(st_spec)
    else:
        scratch.append(pltpu.VMEM((nseq, SUBLANES, 2 * D_FF), F32))
    return pl.pallas_call(
        functools.partial(_ffn_kernel, nseq=nseq, rows=rows, tiles_per_seq=tps, has_state=has_state),
        grid=(n // tm,),
        in_specs=in_specs,
        out_specs=[row(D_MODEL), st_spec],
        out_shape=[jax.ShapeDtypeStruct((n, D_MODEL), F32),
                   jax.ShapeDtypeStruct((nbatch, FFN_CONV - 1, 2 * D_FF), F32)],
        scratch_shapes=scratch,
        compiler_params=pltpu.CompilerParams(dimension_semantics=("arbitrary",),
                                             vmem_limit_bytes=VMEM_LIMIT),
        name="ffn_sample" if has_state else "ffn_prompt",
    )(*ins)


ROW_TILE = 512
SAMPLE_KV_BLOCK = 4096
GDN_CHUNK = 128
GDN_CHUNKS_PER_STEP = 4
GDN_SAMPLE_SEQS_PER_STEP = 4


def _prep_weights(w_in, b_fox_f, gdn_a_log, gdn_dt_bias):
    c0 = 3 * FOX_WIDTH
    c1 = c0 + FOX_HEADS
    c2 = c1 + GDN_CONV_CH
    c3 = c2 + 2 * GDN_HEADS
    wt = jnp.swapaxes(w_in, 0, 1)
    small = jnp.concatenate([wt[c0:c1], wt[c2:c3],
                             jnp.zeros((LANES - FOX_HEADS - 2 * GDN_HEADS, D_MODEL), wt.dtype)], axis=0)
    w_cat = jnp.swapaxes(jnp.concatenate([wt[:c0], wt[c1:c2], wt[c3:], small], axis=0), 0, 1).astype(BF16)
    pad = jnp.zeros((LANES - FOX_HEADS - GDN_HEADS,), F32)
    bias = jnp.concatenate([b_fox_f.astype(F32), gdn_dt_bias.astype(F32), pad])
    alog = jnp.concatenate([jnp.zeros((FOX_HEADS,), F32), gdn_a_log.astype(F32), pad])
    prm = jnp.zeros((SUBLANES, LANES), F32).at[0].set(bias).at[1].set(alog)
    return w_cat, prm


def _layer(xp, xs, fox_k, fox_v, fox_logf, st_gdn, st_gconv, st_fconv,
           w_in, b_fox_f, gdn_conv_w, gdn_a_log, gdn_dt_bias, gdn_norm_w, w_o,
           norm_mix_w, norm_ffn_w, w_up, ffn_conv_w, ffn_conv_b, w_down, norm_out_w):
    bp, sp, _ = xp.shape
    bs, ts, _ = xs.shape
    w_cat, prm = _prep_weights(w_in, b_fox_f, gdn_a_log, gdn_dt_bias)
    nmw = norm_mix_w.reshape(1, D_MODEL).astype(F32)
    nfw = norm_ffn_w.reshape(1, D_MODEL).astype(F32)
    now = norm_out_w.reshape(1, D_MODEL).astype(F32)
    gnw = gdn_norm_w.reshape(1, GDN_DV).astype(F32)
    w_o_b, w_up_b, w_dn_b = w_o.astype(BF16), w_up.astype(BF16), w_down.astype(BF16)
    cb = ffn_conv_b.reshape(1, 2 * D_FF).astype(F32)

    xp2 = xp.reshape(bp * sp, D_MODEL)
    q, k, v, kb, qa, ka, va, gtail, gact, gz, sm = _inproj(xp2, nmw, w_cat, prm, gdn_conv_w.astype(F32),
                                                          tm=ROW_TILE, seq_len=sp, prompt=True)
    r3 = lambda a: a.reshape(bp, sp, a.shape[-1])
    o_fox = _fox_prompt(r3(q), r3(qa), r3(kb), r3(ka), va, t=ROW_TILE)
    o_gdn, p_state = _gdn(r3(gact), None, r3(sm), r3(gz), None, None, gnw,
                          c=GDN_CHUNK, nb=1, cpb=GDN_CHUNKS_PER_STEP)
    yp, p_fconv = _ffn(o_fox.reshape(bp * sp, FOX_WIDTH), o_gdn.reshape(bp * sp, GDN_WIDTH), xp2,
                       w_o_b, nfw, w_up_b, ffn_conv_w, cb, w_dn_b, now, None,
                       nseq=1, rows=ROW_TILE, seq_len=sp)
    p_out = (jnp.transpose(k, (0, 3, 1, 2)), jnp.transpose(v, (0, 3, 1, 2)),
             r3(sm)[:, :, LOGF_LANE:LOGF_LANE + FOX_HEADS], p_state,
             gtail[:, SUBLANES - (GDN_CONV - 1):], p_fconv)

    xs2 = xs.reshape(bs * ts, D_MODEL)
    q, k, v, gqkv, gz, sm = _inproj(xs2, nmw, w_cat, prm, tm=bs * ts, seq_len=ts, prompt=False)
    s3 = lambda a: a.reshape(bs, ts, a.shape[-1])
    past = fox_k.shape[1]
    lft = jnp.swapaxes(fox_logf.astype(F32), 1, 2)
    cache_t = lambda a: jnp.transpose(a.astype(F32), (0, 2, 3, 1))
    o_fox = _fox_sample(s3(q), s3(k), s3(v), s3(sm), lft, cache_t(fox_k), cache_t(fox_v),
                        kb=min(SAMPLE_KV_BLOCK, past))
    gqkv3 = s3(gqkv)
    hist = jnp.concatenate([jnp.zeros((bs, SUBLANES - (GDN_CONV - 1), GDN_CONV_CH), F32),
                            st_gconv.astype(F32)], axis=1)
    o_gdn, s_state = _gdn(gqkv3, hist, s3(sm), s3(gz), st_gdn.astype(F32), gdn_conv_w, gnw,
                          c=ts, nb=GDN_SAMPLE_SEQS_PER_STEP, cpb=1)
    ys, s_fconv = _ffn(o_fox.reshape(bs * ts, FOX_WIDTH), o_gdn.reshape(bs * ts, GDN_WIDTH), xs2,
                       w_o_b, nfw, w_up_b, ffn_conv_w, cb, w_dn_b, now, st_fconv.astype(F32),
                       nseq=bs, rows=ts, seq_len=ts)
    full = jnp.concatenate([st_gconv.astype(F32), gqkv3], axis=1)
    s_out = (k.reshape(bs, ts, FOX_HEADS, FOX_DIM), v.reshape(bs, ts, FOX_HEADS, FOX_DIM),
             s3(sm)[:, :, LOGF_LANE:LOGF_LANE + FOX_HEADS], s_state,
             full[:, ts:], s_fconv)
    return yp.reshape(bp, sp, D_MODEL), ys.reshape(bs, ts, D_MODEL), p_out, s_out


def kernel(x_prompt, x_sample, cache_fox_k, cache_fox_v, cache_fox_logf, state_gdn, state_gdn_conv,
           state_ffn_conv, w_in, b_fox_f, gdn_conv_w, gdn_a_log, gdn_dt_bias, gdn_norm_w, w_o, norm_mix_w,
           norm_ffn_w, w_up, ffn_conv_w, ffn_conv_b, w_down, norm_final_w):
    depth = w_in.shape[0]
    assert depth == 1, "the final RMSNorm is fused into the single layer's FFN kernel"
    yp, ys, p_out, s_out = _layer(
        x_prompt, x_sample, cache_fox_k[0], cache_fox_v[0], cache_fox_logf[0], state_gdn[0],
        state_gdn_conv[0], state_ffn_conv[0], w_in[0], b_fox_f[0], gdn_conv_w[0], gdn_a_log[0],
        gdn_dt_bias[0], gdn_norm_w[0], w_o[0], norm_mix_w[0], norm_ffn_w[0], w_up[0], ffn_conv_w[0],
        ffn_conv_b[0], w_down[0], norm_final_w)
    return (yp, ys) + tuple(a[None] for a in p_out) + tuple(a[None] for a in s_out)
```

```python
import functools
import math

import numpy as np
import jax
import jax.numpy as jnp
from jax import lax
from jax.experimental import pallas as pl
from jax.experimental.pallas import tpu as pltpu

F32 = jnp.float32
BF16 = jnp.bfloat16

D_MODEL = 1024
FOX_DIM = 64
FOX_HEADS = 8
FOX_WIDTH = FOX_HEADS * FOX_DIM
GDN_DK = 128
GDN_DV = 128
GDN_HEADS = 4
GDN_QK = GDN_HEADS * GDN_DK
GDN_WIDTH = GDN_HEADS * GDN_DV
GDN_CONV_CH = 2 * GDN_QK + GDN_WIDTH
GDN_CONV = 4
D_FF = 2816
FFN_CONV = 3
EPS = 1e-6

LANES = 128
SUBLANES = 8
VMEM_LIMIT = 56 * 1024 * 1024
MASKED = -1e30
LOG2E = math.log2(math.e)

LOGF_LANE = 0
G_LANE = FOX_HEADS
BETA_LANE = FOX_HEADS + GDN_HEADS

COL_FOX = 0
COL_GQKV = 3 * FOX_WIDTH
COL_GZ = COL_GQKV + GDN_CONV_CH
COL_SMALL = COL_GZ + GDN_WIDTH
N_COLS = COL_SMALL + LANES


def _dot(a, b):
    return jnp.dot(a, b, preferred_element_type=F32)


def _dot_nt(a, b):
    return lax.dot_general(a, b, (((1,), (1,)), ((), ())), preferred_element_type=F32)


def _dot_tn(a, b):
    return lax.dot_general(a, b, (((0,), (0,)), ((), ())), preferred_element_type=F32)


def _split3(x):
    hi = x.astype(BF16)
    r = x - hi.astype(F32)
    lo = r.astype(BF16)
    lo2 = (r - lo.astype(F32)).astype(BF16)
    return hi, lo, lo2


def _exact_left(mat, x):
    hi, lo, lo2 = _split3(x)
    return _dot(mat, hi) + _dot(mat, lo) + _dot(mat, lo2)


def _exact_right(x, mat):
    hi, lo, lo2 = _split3(x)
    return _dot(hi, mat) + _dot(lo, mat) + _dot(lo2, mat)


def _dot3(a, b):
    ah = a.astype(BF16)
    al = (a - ah.astype(F32)).astype(BF16)
    bh = b.astype(BF16)
    bl = (b - bh.astype(F32)).astype(BF16)
    return _dot(ah, bh) + (_dot(ah, bl) + _dot(al, bh))


def _exact_left_c(mat, x):
    return _dot(jnp.concatenate([mat, mat, mat], axis=1), jnp.concatenate(_split3(x), axis=0))


def _split2(x):
    hi = x.astype(BF16)
    return hi, (x - hi.astype(F32)).astype(BF16)


def _dot3_parts(ah, al, bh, bl):
    return _dot(jnp.concatenate([ah, ah, al], axis=1), jnp.concatenate([bh, bl, bh], axis=0))


def _dot3c(a, b):
    return _dot3_parts(*_split2(a), *_split2(b))


def _sigmoid(x):
    return 1.0 / (1.0 + jnp.exp(-x))


def _tri(n, strict=False):
    r = lax.broadcasted_iota(jnp.int32, (n, n), 0)
    c = lax.broadcasted_iota(jnp.int32, (n, n), 1)
    return (r > c) if strict else (r >= c)


def _inproj_kernel(x_ref, nw_ref, w_ref, prm_ref, *refs, tiles_per_seq, prompt):
    if prompt:
        (aug_ref, augc_ref, cw_ref, q_ref, k_ref, v_ref, kb_ref, qa_ref, ka_ref, va_ref, tail_ref,
         g_ref, z_ref, sm_ref, carry_ref, win_ref) = refs
    else:
        q_ref, k_ref, v_ref, g_ref, z_ref, sm_ref = refs
    x = x_ref[...]
    xn = x * lax.rsqrt(jnp.mean(x * x, axis=-1, keepdims=True) + EPS) * nw_ref[...]
    xb = xn.astype(BF16)

    tm = x.shape[0]
    gq = _dot(xb, w_ref[:, COL_GQKV:COL_GZ])
    conv_cols = []
    if not prompt:
        g_ref[...] = gq
    else:
        first = pl.program_id(0) % tiles_per_seq == 0

        @pl.when(first)
        def _():
            win_ref[0:SUBLANES, :] = jnp.zeros((SUBLANES, GDN_CONV_CH), F32)

        @pl.when(jnp.logical_not(first))
        def _():
            win_ref[0:SUBLANES, :] = win_ref[tm:tm + SUBLANES, :]

        win_ref[SUBLANES:SUBLANES + tm, :] = gq
        tail_ref[0] = gq[tm - SUBLANES:tm, :]
        conv_cols = list(range(0, GDN_CONV_CH, LANES))

    def conv_some(count):
        for _ in range(min(count, len(conv_cols))):
            col = conv_cols.pop(0)
            g_ref[:, col:col + LANES] = _gdn_activate(
                _short_conv(win_ref, cw_ref, SUBLANES - (GDN_CONV - 1), tm, col), col)

    q_scale = FOX_DIM ** -0.5 * (LOG2E if prompt else 1.0)
    q_ref[...] = (_dot(xb, w_ref[:, 0:FOX_WIDTH]) * q_scale).astype(BF16)
    conv_some(3)
    kf = _dot(xb, w_ref[:, FOX_WIDTH:2 * FOX_WIDTH])
    if not prompt:
        k_ref[...] = kf
    else:
        k_ref[0] = kf.T.reshape(FOX_HEADS, FOX_DIM, tm)
        kb_ref[...] = kf.astype(BF16)
    conv_some(3)
    vf = _dot(xb, w_ref[:, 2 * FOX_WIDTH:3 * FOX_WIDTH])
    if not prompt:
        v_ref[...] = vf
    else:
        vt = vf.T.reshape(FOX_HEADS, FOX_DIM, tm)
        v_ref[0] = vt
        va_ref[0, :, 0:FOX_DIM, :] = vt.astype(BF16)
        va_ref[0, :, FOX_DIM:, :] = jnp.ones((FOX_HEADS, ONES_ROWS, tm), BF16)
    conv_some(3)
    z_ref[...] = _dot(xb, w_ref[:, COL_GZ:COL_SMALL])
    conv_some(3)

    z = _dot(xb, w_ref[:, COL_SMALL:N_COLS])
    lane = lax.broadcasted_iota(jnp.int32, z.shape, 1)
    zb = z + prm_ref[0:1, :]
    e = jnp.exp(-jnp.abs(zb))
    l1p = jnp.log(1.0 + e)
    logf = jnp.minimum(zb, 0.0) - l1p
    g = -jnp.exp(prm_ref[1:2, :]) * (jnp.maximum(zb, 0.0) + l1p)
    beta = _sigmoid(z)
    sm = jnp.where(lane < G_LANE, logf,
                   jnp.where(lane < BETA_LANE, g,
                             jnp.where(lane < BETA_LANE + GDN_HEADS, beta, 0.0)))
    sm_ref[...] = sm

    if prompt:
        i = pl.program_id(0)

        @pl.when(i % tiles_per_seq == 0)
        def _():
            carry_ref[...] = jnp.zeros_like(carry_ref)

        c = _exact_left(_tri(tm).astype(BF16), sm) + carry_ref[0:1, :]
        carry_ref[0:1, :] = c[tm - 1:tm, :]
        parts = jnp.concatenate(_split3(c * LOG2E), axis=1)
        aug = _dot(parts, aug_ref[...]) + augc_ref[...]
        qa_ref[...] = aug[:, :FOX_WIDTH].astype(BF16)
        ka_ref[...] = aug[:, FOX_WIDTH:].astype(BF16)


AUG_LANES = 6
ONES_ROWS = 16
FOX_BLOCK = 1024
FOX_QUERY_SPLITS = 4


def _aug_tables():
    place = np.zeros((3 * LANES, 2 * FOX_WIDTH), np.float32)
    ones = np.zeros((1, 2 * FOX_WIDTH), np.float32)
    for h in range(FOX_HEADS):
        q0 = (h // 2) * LANES + (h % 2) * AUG_LANES
        k0 = FOX_WIDTH + q0
        for part in range(3):
            place[part * LANES + LOGF_LANE + h, q0 + part] = 1.0
            place[part * LANES + LOGF_LANE + h, k0 + 3 + part] = -1.0
            ones[0, q0 + 3 + part] = 1.0
            ones[0, k0 + part] = 1.0
    return jnp.asarray(place, BF16), jnp.asarray(ones, F32)


def _inproj(x2d, norm_w, w_cat, prm, conv_w=None, *, tm, seq_len, prompt):
    n = x2d.shape[0]
    assert n % tm == 0 and (not prompt or seq_len % tm == 0)
    tps = seq_len // tm if prompt else 1
    row = lambda w: pl.BlockSpec((tm, w), lambda i: (i, 0))
    const = lambda a: pl.BlockSpec(a.shape, lambda i: (0,) * a.ndim, pipeline_mode=pl.Buffered(1))
    ins = [x2d, norm_w, w_cat, prm]
    out_shape = [jax.ShapeDtypeStruct((n, FOX_WIDTH), BF16),
                 jax.ShapeDtypeStruct((n, FOX_WIDTH), F32),
                 jax.ShapeDtypeStruct((n, FOX_WIDTH), F32)]
    out_specs = [row(FOX_WIDTH), row(FOX_WIDTH), row(FOX_WIDTH)]
    scratch = []
    if prompt:
        kv_t = jax.ShapeDtypeStruct((n // seq_len, FOX_HEADS, FOX_DIM, seq_len), F32)
        out_shape[1:3] = [kv_t, kv_t]
        out_specs[1:3] = [pl.BlockSpec((1, FOX_HEADS, FOX_DIM, tm), lambda i: (i // tps, 0, 0, i % tps))] * 2
        ins += list(_aug_tables()) + [conv_w]
        out_shape += [jax.ShapeDtypeStruct((n, FOX_WIDTH), BF16)] * 3 + [
            jax.ShapeDtypeStruct((n // seq_len, FOX_HEADS, FOX_DIM + ONES_ROWS, seq_len), BF16)]
        out_specs += [row(FOX_WIDTH)] * 3 + [
            pl.BlockSpec((1, FOX_HEADS, FOX_DIM + ONES_ROWS, tm), lambda i: (i // tps, 0, 0, i % tps))]
        out_shape += [jax.ShapeDtypeStruct((n // seq_len, SUBLANES, GDN_CONV_CH), F32)]
        out_specs += [pl.BlockSpec((1, SUBLANES, GDN_CONV_CH), lambda i: (i // tps, 0, 0))]
        scratch = [pltpu.VMEM((SUBLANES, LANES), F32), pltpu.VMEM((SUBLANES + tm, GDN_CONV_CH), F32)]
    out_shape += [jax.ShapeDtypeStruct((n, GDN_CONV_CH), F32),
                  jax.ShapeDtypeStruct((n, GDN_WIDTH), F32),
                  jax.ShapeDtypeStruct((n, LANES), F32)]
    out_specs += [row(GDN_CONV_CH), row(GDN_WIDTH), row(LANES)]
    return pl.pallas_call(
        functools.partial(_inproj_kernel, tiles_per_seq=tps, prompt=prompt),
        grid=(n // tm,),
        in_specs=[row(D_MODEL)] + [const(a) for a in ins[1:]],
        out_specs=out_specs, out_shape=out_shape, scratch_shapes=scratch,
        compiler_params=pltpu.CompilerParams(dimension_semantics=("arbitrary",),
                                             vmem_limit_bytes=VMEM_LIMIT),
        name="inproj_prompt" if prompt else "inproj_sample",
    )(*ins)


def _fox_prompt_kernel(q_ref, qa_ref, k_ref, ka_ref, va_ref, o_ref, sa_ref, sb_ref, *, t):
    i = pl.program_id(2)
    q = q_ref[0]
    qa = qa_ref[0]
    lane = lax.broadcasted_iota(jnp.int32, (t, LANES), 1)
    low = lane < FOX_DIM
    zero = jnp.zeros_like(q)
    qcat = []
    for h in range(2):
        mine = (lane >= h * AUG_LANES) & (lane < (h + 1) * AUG_LANES)
        qcat.append(jnp.concatenate([jnp.where(low if h == 0 else jnp.logical_not(low), q, zero),
                                     jnp.where(mine, qa, zero)], axis=1))

    tq = t // FOX_QUERY_SPLITS
    units = [(h, u) for h in range(2) for u in range(FOX_QUERY_SPLITS)]
    qunit = [qcat[h][u * tq:(u + 1) * tq] for h, u in units]

    def logits(j, s_ref):
        start = pl.multiple_of(j * t, t)
        kcat = jnp.concatenate([k_ref[0, pl.ds(start, t), :], ka_ref[0, pl.ds(start, t), :]], axis=1)
        for n in range(len(units)):
            s_ref[n] = _dot_nt(kcat, qunit[n])

    def consume(j, s_ref, carry, diag):
        start = pl.multiple_of(j * t, t)
        out = []
        for n, (h, u) in enumerate(units):
            m, acc = carry[n]
            nk = (u + 1) * tq if diag else t
            s = s_ref[n, 0:nk, :]
            if diag:
                key = lax.broadcasted_iota(jnp.int32, (nk, tq), 0)
                query = lax.broadcasted_iota(jnp.int32, (nk, tq), 1) + u * tq
                s = jnp.where(key <= query, s, MASKED)
            m_new = jnp.maximum(m, jnp.max(s, axis=0, keepdims=True))
            p = jnp.exp2(s - m_new).astype(BF16)
            acc = jnp.exp2(m - m_new) * acc + _dot(va_ref[0, h, :, pl.ds(start, nk)], p)
            out.append((m_new, acc))
        return tuple(out)

    init = tuple((jnp.full((1, tq), MASKED, F32), jnp.zeros((FOX_DIM + ONES_ROWS, tq), F32)) for _ in units)
    logits(i, sa_ref)
    logits(0, sb_ref)
    carry = consume(i, sa_ref, init, True)

    def two_blocks(jj, carry):
        j = 2 * jj
        logits(j + 1, sa_ref)
        carry = consume(j, sb_ref, carry, False)
        logits(j + 2, sb_ref)
        return consume(j + 1, sa_ref, carry, False)

    carry = lax.fori_loop(0, i // 2, two_blocks, carry)
    carry = lax.cond(i % 2 == 1, lambda c: consume(i - 1, sb_ref, c, False), lambda c: c, carry)
    heads = [jnp.concatenate([carry[h * FOX_QUERY_SPLITS + u][1] for u in range(FOX_QUERY_SPLITS)], axis=1)
             for h in range(2)]
    o_t = jnp.concatenate([a[:FOX_DIM] / a[FOX_DIM:FOX_DIM + 1] for a in heads], axis=0)
    o_ref[0] = o_t.T.astype(BF16)


def _fox_prompt(q, qa, kb, ka, va, *, t):
    b, s, _ = q.shape
    nblk = s // t
    assert s % t == 0
    npair = FOX_HEADS // 2
    qblk = pl.BlockSpec((1, t, LANES), lambda bi, p, i: (bi, i, p))
    kblk = pl.BlockSpec((1, s, LANES), lambda bi, p, i: (bi, 0, p))
    return pl.pallas_call(
        functools.partial(_fox_prompt_kernel, t=t),
        grid=(b, npair, nblk),
        in_specs=[qblk, qblk, kblk, kblk,
                  pl.BlockSpec((1, 2, FOX_DIM + ONES_ROWS, s), lambda bi, p, i: (bi, p, 0, 0))],
        out_specs=qblk,
        out_shape=jax.ShapeDtypeStruct((b, s, FOX_WIDTH), BF16),
        scratch_shapes=[pltpu.VMEM((2 * FOX_QUERY_SPLITS, t, t // FOX_QUERY_SPLITS), F32)] * 2,
        compiler_params=pltpu.CompilerParams(
            dimension_semantics=("arbitrary", "arbitrary", "arbitrary"), vmem_limit_bytes=VMEM_LIMIT),
        name="fox_prompt",
    )(q, qa, kb, ka, va)


def _fox_sample_kernel(q_ref, kn_ref, vn_ref, sm_ref, lft_ref, kt_ref, vt_ref, o_ref,
                       rk_ref, m_ref, l_ref, acc_ref, *, kb, t):
    j = pl.program_id(1)
    nkv = pl.num_programs(1)

    def heads(x):
        return [x[:, h * FOX_DIM:(h + 1) * FOX_DIM] for h in range(FOX_HEADS)]

    qh = [x.astype(BF16) for x in heads(q_ref[0].astype(F32))]

    lf_new = sm_ref[0]
    cn = _exact_left(_tri(t).astype(BF16), lf_new)

    @pl.when(j == 0)
    def _():
        p = lft_ref.shape[2]
        nb = p // LANES
        x = jnp.concatenate([lft_ref[0, :, i * LANES:(i + 1) * LANES] for i in range(nb)], axis=0)
        r = lax.broadcasted_iota(jnp.int32, (LANES, LANES), 0)
        cc = lax.broadcasted_iota(jnp.int32, (LANES, LANES), 1)
        y = _exact_right(x, (r > cc).astype(BF16))
        tot = jnp.sum(x, axis=1, keepdims=True)
        off = jnp.zeros((FOX_HEADS, 1), F32)
        for i in range(nb - 1, -1, -1):
            rk_ref[:, i * LANES:(i + 1) * LANES] = y[i * FOX_HEADS:(i + 1) * FOX_HEADS, :] + off
            off = off + tot[i * FOX_HEADS:(i + 1) * FOX_HEADS, :]
        m_ref[...] = jnp.full_like(m_ref, MASKED)
        l_ref[...] = jnp.zeros_like(l_ref)
        acc_ref[...] = jnp.zeros_like(acc_ref)

    def update(s, pv):
        m = m_ref[:, 0:1]
        m_new = jnp.maximum(m, jnp.max(s, axis=1, keepdims=True))
        alpha = jnp.exp(m - m_new)
        p = jnp.exp(s - m_new)
        l_ref[...] = jnp.broadcast_to(alpha * l_ref[:, 0:1] + jnp.sum(p, axis=1, keepdims=True), l_ref.shape)
        acc_ref[...] = alpha * acc_ref[...] + pv(p.astype(BF16))
        m_ref[...] = jnp.broadcast_to(m_new, m_ref.shape)

    start = pl.multiple_of(j * kb, kb)
    rk = rk_ref[:, pl.ds(start, kb)]
    s = jnp.concatenate([_dot(qh[h], kt_ref[0, h].astype(BF16)) + (cn[:, h:h + 1] + rk[h:h + 1, :])
                         for h in range(FOX_HEADS)], axis=0)
    update(s, lambda p: jnp.concatenate(
        [_dot_nt(p[h * t:(h + 1) * t], vt_ref[0, h].astype(BF16)) for h in range(FOX_HEADS)], axis=0))

    @pl.when(j == nkv - 1)
    def _():
        r = lax.broadcasted_iota(jnp.int32, (t, t), 0)
        cc = lax.broadcasted_iota(jnp.int32, (t, t), 1)
        after = (r > cc).astype(F32)
        tri = _tri(t).astype(BF16)
        knh = heads(kn_ref[0])
        vnh = [x.astype(BF16) for x in heads(vn_ref[0])]
        s_new = jnp.concatenate(
            [jnp.where(_tri(t), _dot_nt(qh[h], knh[h].astype(BF16))
                       + _exact_left(tri, lf_new[:, h:h + 1] * after), MASKED) for h in range(FOX_HEADS)], axis=0)
        update(s_new, lambda p: jnp.concatenate(
            [_dot(p[h * t:(h + 1) * t], vnh[h]) for h in range(FOX_HEADS)], axis=0))
        o_full = acc_ref[...] / l_ref[:, 0:1]
        o_ref[0] = jnp.concatenate([o_full[h * t:(h + 1) * t] for h in range(FOX_HEADS)], axis=1).astype(BF16)


def _fox_sample(q, kn, vn, sm, lft, kt, vt, *, kb):
    b, t, _ = q.shape
    p = kt.shape[3]
    assert p % kb == 0 and kb % LANES == 0
    rows = FOX_HEADS * t
    per_b = lambda shape: pl.BlockSpec((1,) + shape, lambda bi, j: (bi, 0, 0))
    cache = pl.BlockSpec((1, FOX_HEADS, FOX_DIM, kb), lambda bi, j: (bi, 0, 0, j))
    return pl.pallas_call(
        functools.partial(_fox_sample_kernel, kb=kb, t=t),
        grid=(b, p // kb),
        in_specs=[per_b((t, FOX_WIDTH)), per_b((t, FOX_WIDTH)), per_b((t, FOX_WIDTH)), per_b((t, LANES)),
                  per_b((FOX_HEADS, p)), cache, cache],
        out_specs=per_b((t, FOX_WIDTH)),
        out_shape=jax.ShapeDtypeStruct((b, t, FOX_WIDTH), BF16),
        scratch_shapes=[pltpu.VMEM((FOX_HEADS, p), F32),
                        pltpu.VMEM((rows, LANES), F32), pltpu.VMEM((rows, LANES), F32),
                        pltpu.VMEM((rows, FOX_DIM), F32)],
        compiler_params=pltpu.CompilerParams(dimension_semantics=("arbitrary", "arbitrary"),
                                             vmem_limit_bytes=VMEM_LIMIT),
        name="fox_sample",
    )(q, kn, vn, sm, lft, kt, vt)


def _interleave(gens):
    results = [None] * len(gens)
    alive = list(range(len(gens)))
    while alive:
        for idx in list(alive):
            try:
                next(gens[idx])
            except StopIteration as stop:
                results[idx] = stop.value
                alive.remove(idx)
    return results


def _gdn_activate(acc, col):
    x = acc * _sigmoid(acc)
    if col >= 2 * GDN_QK:
        return x
    scale = GDN_DK ** -0.5 if col < GDN_QK else 1.0
    return x * (lax.rsqrt(jnp.sum(x * x, axis=-1, keepdims=True) + EPS) * scale)


def _short_conv(win_ref, cw_ref, row0, nrows, col, lead=()):
    acc = None
    for tap in range(GDN_CONV):
        term = (win_ref[lead + (slice(row0 + tap, row0 + tap + nrows), slice(col, col + LANES))]
                * cw_ref[tap:tap + 1, col:col + LANES])
        acc = term if acc is None else acc + term
    return acc


def _gdn_local(qn, kn, vh, g, g_row, g_last, beta, c):
    wide = c % LANES == 0
    dot3 = _dot3c if wide else _dot3
    incl = _tri(c)
    strict = _tri(c, strict=True)
    eg = jnp.exp(g)
    decay = jnp.exp(jnp.where(incl, g - g_row, MASKED))
    kbeta = kn * beta
    kq = _dot_nt(jnp.concatenate([kbeta, qn], axis=0).astype(BF16), kn.astype(BF16))
    yield
    lmat = jnp.where(strict, kq[:c] * decay, 0.0)
    a_qk = kq[c:] * decay

    tinv = (incl & jnp.logical_not(strict)).astype(F32) - lmat
    if wide:
        lh, ll = _split2(lmat)
        power = _dot3_parts(lh, ll, lh, ll)
    else:
        power = dot3(lmat, lmat)
    yield
    span = 2
    while 2 * span < c:
        if wide:
            ph, pl_ = _split2(power)
            th, tl = _split2(tinv)
            both = _dot3_parts(ph, pl_, jnp.concatenate([th, ph], axis=1), jnp.concatenate([tl, pl_], axis=1))
            yield
            tinv = tinv + both[:, :c]
            power = both[:, c:]
        else:
            step = dot3(power, tinv)
            power = dot3(power, power)
            yield
            tinv = tinv + step
        span *= 2
    step = dot3(power, tinv)
    yield
    tinv = tinv + step

    sol = dot3(tinv, jnp.concatenate([vh * beta, kbeta * eg], axis=1))
    yield
    kd = kn * jnp.exp(g_last - g)
    return sol[:, :GDN_DV], sol[:, GDN_DV:], qn * eg, a_qk, (kd.T if wide else kd)


def _gdn_state_step(locals_, states, g_lasts, c):
    wide = c % LANES == 0
    ws = [_dot(jnp.concatenate([w, qeg], axis=0).astype(BF16), sh.astype(BF16))
          for (_, w, qeg, _, _), sh in zip(locals_, states)]
    outs, new_states = [], []
    for (u, _, _, a_qk, kd), wsh, sh, g_last in zip(locals_, ws, states, g_lasts):
        db = (u - wsh[:c]).astype(BF16)
        decayed = sh * jnp.exp(g_last)
        if wide:
            ak = _dot(jnp.concatenate([a_qk, kd], axis=0).astype(BF16), db)
            outs.append(wsh[c:] + ak[:c])
            new_states.append(decayed + ak[c:])
        else:
            outs.append(wsh[c:] + _dot(a_qk.astype(BF16), db))
            new_states.append(decayed + _dot_tn(kd.astype(BF16), db))
    return outs, new_states


def _gdn_kernel(cur_ref, *refs, c, nb, cpb, activated):
    if activated:
        sm_ref, z_ref, nw_ref, o_ref, sout_ref, s_ref = refs
    else:
        hist_ref, sm_ref, z_ref, s0_ref, cw_ref, nw_ref, o_ref, sout_ref, xs_ref, s_ref = refs
    i = pl.program_id(1)
    rows = c * cpb
    wide = c % LANES == 0

    @pl.when(i == 0)
    def _():
        if activated:
            s_ref[...] = jnp.zeros_like(s_ref)
        else:
            s_ref[...] = s0_ref[...]

    if activated:
        def qkv(b, r0, col):
            return cur_ref[b, r0:r0 + c, col:col + LANES]
    else:
        xs_ref[:, 0:SUBLANES, :] = hist_ref[...]
        xs_ref[:, SUBLANES:SUBLANES + rows, :] = cur_ref[...]
        base = SUBLANES - (GDN_CONV - 1)

        def qkv(b, r0, col):
            return _gdn_activate(_short_conv(xs_ref, cw_ref, base + r0, c, col, lead=(b,)), col)

    incl = _tri(c)
    eye = incl & jnp.logical_not(_tri(c, strict=True))
    gens, g_lasts = [], []
    for b in range(nb):
        for ck in range(cpb):
            r0 = ck * c
            sm = sm_ref[b, r0:r0 + c, :]
            if wide:
                gs = _exact_left_c(incl.astype(BF16), sm)
                gst = gs.T
            else:
                gs = _exact_left(incl.astype(BF16), sm)
            for h in range(GDN_HEADS):
                g = gs[:, G_LANE + h:G_LANE + h + 1]
                if wide:
                    g_row = gst[G_LANE + h:G_LANE + h + 1, :]
                else:
                    g_row = _exact_left(jnp.ones((c, c), BF16), jnp.where(eye, g, 0.0))
                g_last = gs[c - 1:c, G_LANE + h:G_LANE + h + 1]
                g_lasts.append(g_last)
                gens.append(_gdn_local(
                    qkv(b, r0, h * GDN_DK), qkv(b, r0, GDN_QK + h * GDN_DK),
                    qkv(b, r0, 2 * GDN_QK + h * GDN_DV), g, g_row, g_last,
                    sm[:, BETA_LANE + h:BETA_LANE + h + 1], c))
    locals_ = _interleave(gens)

    states = [s_ref[b, h] for b in range(nb) for h in range(GDN_HEADS)]
    for ck in range(cpb):
        r0 = ck * c
        pick = [(b * cpb + ck) * GDN_HEADS + h for b in range(nb) for h in range(GDN_HEADS)]
        outs, states = _gdn_state_step([locals_[u] for u in pick], states, [g_lasts[u] for u in pick], c)
        for idx, o in enumerate(outs):
            b, h = divmod(idx, GDN_HEADS)
            on = o * lax.rsqrt(jnp.mean(o * o, axis=-1, keepdims=True) + EPS) * nw_ref[...]
            zh = z_ref[b, r0:r0 + c, h * GDN_DV:(h + 1) * GDN_DV]
            o_ref[b, r0:r0 + c, h * GDN_DV:(h + 1) * GDN_DV] = (on * (zh * _sigmoid(zh))).astype(BF16)
    for idx, st in enumerate(states):
        b, h = divmod(idx, GDN_HEADS)
        s_ref[b, h] = st

    @pl.when(i == pl.num_programs(1) - 1)
    def _():
        sout_ref[...] = s_ref[...]


def _gdn(gqkv, hist, sm, gz, s0, conv_w, norm_w, *, c, nb, cpb):
    b, t, _ = gqkv.shape
    rows = c * cpb
    assert t % rows == 0 and b % nb == 0 and c % SUBLANES == 0
    activated = hist is None
    blk = lambda w: pl.BlockSpec((nb, rows, w), lambda bi, i: (bi, i, 0))
    const = lambda a: pl.BlockSpec(a.shape, lambda bi, i: (0,) * a.ndim)
    state_spec = pl.BlockSpec((nb, GDN_HEADS, GDN_DK, GDN_DV), lambda bi, i: (bi, 0, 0, 0))
    scratch = [pltpu.VMEM((nb, GDN_HEADS, GDN_DK, GDN_DV), F32)]
    if activated:
        ins = [gqkv, sm, gz, norm_w]
        in_specs = [blk(GDN_CONV_CH), blk(LANES), blk(GDN_WIDTH), const(norm_w)]
    else:
        assert t == rows
        ins = [gqkv, hist, sm, gz, s0, conv_w, norm_w]
        in_specs = [blk(GDN_CONV_CH), pl.BlockSpec((nb, SUBLANES, GDN_CONV_CH), lambda bi, i: (bi, 0, 0)),
                    blk(LANES), blk(GDN_WIDTH), state_spec, const(conv_w), const(norm_w)]
        scratch = [pltpu.VMEM((nb, SUBLANES + rows, GDN_CONV_CH), F32)] + scratch
    return pl.pallas_call(
        functools.partial(_gdn_kernel, c=c, nb=nb, cpb=cpb, activated=activated),
        grid=(b // nb, t // rows),
        in_specs=in_specs,
        out_specs=[blk(GDN_WIDTH), state_spec],
        out_shape=[jax.ShapeDtypeStruct((b, t, GDN_WIDTH), BF16),
                   jax.ShapeDtypeStruct((b, GDN_HEADS, GDN_DK, GDN_DV), F32)],
        scratch_shapes=scratch,
        compiler_params=pltpu.CompilerParams(dimension_semantics=("arbitrary", "arbitrary"),
                                             vmem_limit_bytes=VMEM_LIMIT),
        name="gdn_prompt" if activated else "gdn_sample",
    )(*ins)


FFN_COLS = 256


def _ffn_kernel(of_ref, og_ref, x_ref, wo_ref, nfw_ref, wup_ref, cw_ref, cb_ref, wdn_ref, nlw_ref, *refs,
                nseq, rows, tiles_per_seq, has_state):
    if has_state:
        st_ref, y_ref, nst_ref, wg_ref, wv_ref, act_ref = refs
    else:
        y_ref, nst_ref, wg_ref, wv_ref, act_ref, carry_ref = refs
    i = pl.program_id(0)
    hist0 = SUBLANES - (FFN_CONV - 1)

    h = (x_ref[...] + _dot(of_ref[...], wo_ref[0:FOX_WIDTH, :])
         + _dot(og_ref[...], wo_ref[FOX_WIDTH:FOX_WIDTH + GDN_WIDTH, :]))
    hn = (h * lax.rsqrt(jnp.mean(h * h, axis=-1, keepdims=True) + EPS) * nfw_ref[...]).astype(BF16)

    if not has_state:
        @pl.when(i % tiles_per_seq == 0)
        def _():
            carry_ref[...] = jnp.zeros_like(carry_ref)

    def conv_cols(buf_ref, col):
        up = _dot(hn, wup_ref[:, col:col + FFN_COLS])
        buf_ref[:, SUBLANES:SUBLANES + rows, :] = up.reshape(nseq, rows, FFN_COLS)
        if has_state:
            buf_ref[:, hist0:SUBLANES, :] = st_ref[:, :, col:col + FFN_COLS]
        else:
            buf_ref[:, hist0:SUBLANES, :] = carry_ref[:, hist0:SUBLANES, col:col + FFN_COLS]
        last = buf_ref[:, SUBLANES + rows - (FFN_CONV - 1):SUBLANES + rows, :]
        nst_ref[:, :, col:col + FFN_COLS] = last
        if not has_state:
            carry_ref[:, hist0:SUBLANES, col:col + FFN_COLS] = last
        uc = cb_ref[:, col:col + FFN_COLS][None]
        for tap in range(FFN_CONV):
            uc = uc + buf_ref[:, hist0 + tap:hist0 + tap + rows, :] * cw_ref[tap:tap + 1, col:col + FFN_COLS][None]
        return uc.reshape(nseq * rows, FFN_COLS)

    for ci in range(D_FF // FFN_COLS):
        col = ci * FFN_COLS
        gate = conv_cols(wg_ref, col)
        val = conv_cols(wv_ref, D_FF + col)
        act_ref[:, col:col + FFN_COLS] = (gate * _sigmoid(gate) * val).astype(BF16)

    h2 = h + _dot(act_ref[...], wdn_ref[...])
    y_ref[...] = h2 * lax.rsqrt(jnp.mean(h2 * h2, axis=-1, keepdims=True) + EPS) * nlw_ref[...]


def _ffn(ofox, ogdn, x2d, w_o, norm_ffn_w, w_up, conv_w, conv_b, w_down, norm_final_w, state, *,
         nseq, rows, seq_len):
    n = x2d.shape[0]
    tm = nseq * rows
    assert n % tm == 0 and seq_len % rows == 0 and rows % SUBLANES == 0 and D_FF % FFN_COLS == 0
    has_state = state is not None
    assert (rows == seq_len) if has_state else (nseq == 1)
    tps = seq_len // rows
    nbatch = n // seq_len
    row = lambda w: pl.BlockSpec((tm, w), lambda i: (i, 0))
    const = lambda a: pl.BlockSpec(a.shape, lambda i: (0,) * a.ndim, pipeline_mode=pl.Buffered(1))
    st_spec = pl.BlockSpec((nseq, FFN_CONV - 1, 2 * D_FF), lambda i: (i // tps, 0, 0))
    ins = [ofox, ogdn, x2d, w_o, norm_ffn_w, w_up, conv_w, conv_b, w_down, norm_final_w]
    in_specs = [row(FOX_WIDTH), row(GDN_WIDTH), row(D_MODEL)] + [const(a) for a in ins[3:]]
    scratch = [pltpu.VMEM((nseq, SUBLANES + rows, FFN_COLS), F32),
               pltpu.VMEM((nseq, SUBLANES + rows, FFN_COLS), F32),
               pltpu.VMEM((tm, D_FF), BF16)]
    if has_state:
        ins.append(state)
        in_specs.append(st_spec)
    else:
        scratch.append(pltpu.VMEM((nseq, SUBLANES, 2 * D_FF), F32))
    return pl.pallas_call(
        functools.partial(_ffn_kernel, nseq=nseq, rows=rows, tiles_per_seq=tps, has_state=has_state),
        grid=(n // tm,),
        in_specs=in_specs,
        out_specs=[row(D_MODEL), st_spec],
        out_shape=[jax.ShapeDtypeStruct((n, D_MODEL), F32),
                   jax.ShapeDtypeStruct((nbatch, FFN_CONV - 1, 2 * D_FF), F32)],
        scratch_shapes=scratch,
        compiler_params=pltpu.CompilerParams(dimension_semantics=("arbitrary",),
                                             vmem_limit_bytes=VMEM_LIMIT),
        name="ffn_sample" if has_state else "ffn_prompt",
    )(*ins)


ROW_TILE = 512
SAMPLE_KV_BLOCK = 4096
GDN_CHUNK = 128
GDN_CHUNKS_PER_STEP = 8
GDN_SAMPLE_SEQS_PER_STEP = 8


def _prep_weights(w_in, b_fox_f, gdn_a_log, gdn_dt_bias):
    c0 = 3 * FOX_WIDTH
    c1 = c0 + FOX_HEADS
    c2 = c1 + GDN_CONV_CH
    c3 = c2 + 2 * GDN_HEADS
    wt = jnp.swapaxes(w_in, 0, 1)
    small = jnp.concatenate([wt[c0:c1], wt[c2:c3],
                             jnp.zeros((LANES - FOX_HEADS - 2 * GDN_HEADS, D_MODEL), wt.dtype)], axis=0)
    w_cat = jnp.swapaxes(jnp.concatenate([wt[:c0], wt[c1:c2], wt[c3:], small], axis=0), 0, 1).astype(BF16)
    pad = jnp.zeros((LANES - FOX_HEADS - GDN_HEADS,), F32)
    bias = jnp.concatenate([b_fox_f.astype(F32), gdn_dt_bias.astype(F32), pad])
    alog = jnp.concatenate([jnp.zeros((FOX_HEADS,), F32), gdn_a_log.astype(F32), pad])
    prm = jnp.zeros((SUBLANES, LANES), F32).at[0].set(bias).at[1].set(alog)
    return w_cat, prm


def _layer(xp, xs, fox_k, fox_v, fox_logf, st_gdn, st_gconv, st_fconv,
           w_in, b_fox_f, gdn_conv_w, gdn_a_log, gdn_dt_bias, gdn_norm_w, w_o,
           norm_mix_w, norm_ffn_w, w_up, ffn_conv_w, ffn_conv_b, w_down, norm_out_w):
    bp, sp, _ = xp.shape
    bs, ts, _ = xs.shape
    w_cat, prm = _prep_weights(w_in, b_fox_f, gdn_a_log, gdn_dt_bias)
    nmw = norm_mix_w.reshape(1, D_MODEL).astype(F32)
    nfw = norm_ffn_w.reshape(1, D_MODEL).astype(F32)
    now = norm_out_w.reshape(1, D_MODEL).astype(F32)
    gnw = gdn_norm_w.reshape(1, GDN_DV).astype(F32)
    w_o_b, w_up_b, w_dn_b = w_o.astype(BF16), w_up.astype(BF16), w_down.astype(BF16)
    cb = ffn_conv_b.reshape(1, 2 * D_FF).astype(F32)

    xp2 = xp.reshape(bp * sp, D_MODEL)
    q, k, v, kb, qa, ka, va, gtail, gact, gz, sm = _inproj(xp2, nmw, w_cat, prm, gdn_conv_w.astype(F32),
                                                          tm=ROW_TILE, seq_len=sp, prompt=True)
    r3 = lambda a: a.reshape(bp, sp, a.shape[-1])
    o_fox = _fox_prompt(r3(q), r3(qa), r3(kb), r3(ka), va, t=min(FOX_BLOCK, sp))
    o_gdn, p_state = _gdn(r3(gact), None, r3(sm), r3(gz), None, None, gnw,
                          c=GDN_CHUNK, nb=1, cpb=GDN_CHUNKS_PER_STEP)
    yp, p_fconv = _ffn(o_fox.reshape(bp * sp, FOX_WIDTH), o_gdn.reshape(bp * sp, GDN_WIDTH), xp2,
                       w_o_b, nfw, w_up_b, ffn_conv_w, cb, w_dn_b, now, None,
                       nseq=1, rows=ROW_TILE, seq_len=sp)
    p_out = (jnp.transpose(k, (0, 3, 1, 2)), jnp.transpose(v, (0, 3, 1, 2)),
             r3(sm)[:, :, LOGF_LANE:LOGF_LANE + FOX_HEADS], p_state,
             gtail[:, SUBLANES - (GDN_CONV - 1):], p_fconv)

    xs2 = xs.reshape(bs * ts, D_MODEL)
    q, k, v, gqkv, gz, sm = _inproj(xs2, nmw, w_cat, prm, tm=bs * ts, seq_len=ts, prompt=False)
    s3 = lambda a: a.reshape(bs, ts, a.shape[-1])
    past = fox_k.shape[1]
    lft = jnp.swapaxes(fox_logf.astype(F32), 1, 2)
    cache_t = lambda a: jnp.transpose(a.astype(F32), (0, 2, 3, 1))
    o_fox = _fox_sample(s3(q), s3(k), s3(v), s3(sm), lft, cache_t(fox_k), cache_t(fox_v),
                        kb=min(SAMPLE_KV_BLOCK, past))
    gqkv3 = s3(gqkv)
    hist = jnp.concatenate([jnp.zeros((bs, SUBLANES - (GDN_CONV - 1), GDN_CONV_CH), F32),
                            st_gconv.astype(F32)], axis=1)
    o_gdn, s_state = _gdn(gqkv3, hist, s3(sm), s3(gz), st_gdn.astype(F32), gdn_conv_w, gnw,
                          c=ts, nb=GDN_SAMPLE_SEQS_PER_STEP, cpb=1)
    ys, s_fconv = _ffn(o_fox.reshape(bs * ts, FOX_WIDTH), o_gdn.reshape(bs * ts, GDN_WIDTH), xs2,
                       w_o_b, nfw, w_up_b, ffn_conv_w, cb, w_dn_b, now, st_fconv.astype(F32),
                       nseq=bs, rows=ts, seq_len=ts)
    full = jnp.concatenate([st_gconv.astype(F32), gqkv3], axis=1)
    s_out = (k.reshape(bs, ts, FOX_HEADS, FOX_DIM), v.reshape(bs, ts, FOX_HEADS, FOX_DIM),
             s3(sm)[:, :, LOGF_LANE:LOGF_LANE + FOX_HEADS], s_state,
             full[:, ts:], s_fconv)
    return yp.reshape(bp, sp, D_MODEL), ys.reshape(bs, ts, D_MODEL), p_out, s_out


def kernel(x_prompt, x_sample, cache_fox_k, cache_fox_v, cache_fox_logf, state_gdn, state_gdn_conv,
           state_ffn_conv, w_in, b_fox_f, gdn_conv_w, gdn_a_log, gdn_dt_bias, gdn_norm_w, w_o, norm_mix_w,
           norm_ffn_w, w_up, ffn_conv_w, ffn_conv_b, w_down, norm_final_w):
    depth = w_in.shape[0]
    assert depth == 1, "the final RMSNorm is fused into the single layer's FFN kernel"
    yp, ys, p_out, s_out = _layer(
        x_prompt, x_sample, cache_fox_k[0], cache_fox_v[0], cache_fox_logf[0], state_gdn[0],
        state_gdn_conv[0], state_ffn_conv[0], w_in[0], b_fox_f[0], gdn_conv_w[0], gdn_a_log[0],
        gdn_dt_bias[0], gdn_norm_w[0], w_o[0], norm_mix_w[0], norm_ffn_w[0], w_up[0], ffn_conv_w[0],
        ffn_conv_b[0], w_down[0], norm_final_w)
    return (yp, ys) + tuple(a[None] for a in p_out) + tuple(a[None] for a in s_out)
```

```python
import functools
import math

import numpy as np
import jax
import jax.numpy as jnp
from jax import lax
from jax.experimental import pallas as pl
from jax.experimental.pallas import tpu as pltpu

F32 = jnp.float32
BF16 = jnp.bfloat16

D_MODEL = 1024
FOX_DIM = 64
FOX_HEADS = 8
FOX_WIDTH = FOX_HEADS * FOX_DIM
GDN_DK = 128
GDN_DV = 128
GDN_HEADS = 4
GDN_QK = GDN_HEADS * GDN_DK
GDN_WIDTH = GDN_HEADS * GDN_DV
GDN_CONV_CH = 2 * GDN_QK + GDN_WIDTH
GDN_CONV = 4
D_FF = 2816
FFN_CONV = 3
EPS = 1e-6

LANES = 128
SUBLANES = 8
VMEM_LIMIT = 56 * 1024 * 1024
MASKED = -1e30
LOG2E = math.log2(math.e)

LOGF_LANE = 0
G_LANE = FOX_HEADS
BETA_LANE = FOX_HEADS + GDN_HEADS

COL_FOX = 0
COL_GQKV = 3 * FOX_WIDTH
COL_GZ = COL_GQKV + GDN_CONV_CH
COL_SMALL = COL_GZ + GDN_WIDTH
N_COLS = COL_SMALL + LANES


def _dot(a, b):
    return jnp.dot(a, b, preferred_element_type=F32)


def _dot_nt(a, b):
    return lax.dot_general(a, b, (((1,), (1,)), ((), ())), preferred_element_type=F32)


def _dot_tn(a, b):
    return lax.dot_general(a, b, (((0,), (0,)), ((), ())), preferred_element_type=F32)


def _split3(x):
    hi = x.astype(BF16)
    r = x - hi.astype(F32)
    lo = r.astype(BF16)
    lo2 = (r - lo.astype(F32)).astype(BF16)
    return hi, lo, lo2


def _exact_left(mat, x):
    hi, lo, lo2 = _split3(x)
    return _dot(mat, hi) + _dot(mat, lo) + _dot(mat, lo2)


def _exact_right(x, mat):
    hi, lo, lo2 = _split3(x)
    return _dot(hi, mat) + _dot(lo, mat) + _dot(lo2, mat)


def _dot3(a, b):
    ah = a.astype(BF16)
    al = (a - ah.astype(F32)).astype(BF16)
    bh = b.astype(BF16)
    bl = (b - bh.astype(F32)).astype(BF16)
    return _dot(ah, bh) + (_dot(ah, bl) + _dot(al, bh))


def _exact_left_c(mat, x):
    return _dot(jnp.concatenate([mat, mat, mat], axis=1), jnp.concatenate(_split3(x), axis=0))


def _split2(x):
    hi = x.astype(BF16)
    return hi, (x - hi.astype(F32)).astype(BF16)


def _dot3_parts(ah, al, bh, bl):
    return _dot(jnp.concatenate([ah, ah, al], axis=1), jnp.concatenate([bh, bl, bh], axis=0))


def _dot3c(a, b):
    return _dot3_parts(*_split2(a), *_split2(b))


def _sigmoid(x):
    return 1.0 / (1.0 + jnp.exp(-x))


def _tri(n, strict=False):
    r = lax.broadcasted_iota(jnp.int32, (n, n), 0)
    c = lax.broadcasted_iota(jnp.int32, (n, n), 1)
    return (r > c) if strict else (r >= c)


def _inproj_kernel(x_ref, nw_ref, w_ref, prm_ref, *refs, tiles_per_seq, prompt):
    if prompt:
        (aug_ref, augc_ref, cw_ref, q_ref, k_ref, v_ref, kb_ref, qa_ref, ka_ref, va_ref, tail_ref,
         g_ref, z_ref, sm_ref, carry_ref, win_ref) = refs
    else:
        q_ref, k_ref, v_ref, g_ref, z_ref, sm_ref = refs
    x = x_ref[...]
    xn = x * lax.rsqrt(jnp.mean(x * x, axis=-1, keepdims=True) + EPS) * nw_ref[...]
    xb = xn.astype(BF16)

    tm = x.shape[0]
    gq = _dot(xb, w_ref[:, COL_GQKV:COL_GZ])
    conv_cols = []
    if not prompt:
        g_ref[...] = gq
    else:
        first = pl.program_id(0) % tiles_per_seq == 0

        @pl.when(first)
        def _():
            win_ref[0:SUBLANES, :] = jnp.zeros((SUBLANES, GDN_CONV_CH), F32)

        @pl.when(jnp.logical_not(first))
        def _():
            win_ref[0:SUBLANES, :] = win_ref[tm:tm + SUBLANES, :]

        win_ref[SUBLANES:SUBLANES + tm, :] = gq
        tail_ref[0] = gq[tm - SUBLANES:tm, :]
        conv_cols = list(range(0, GDN_CONV_CH, LANES))

    def conv_some(count):
        for _ in range(min(count, len(conv_cols))):
            col = conv_cols.pop(0)
            g_ref[:, col:col + LANES] = _gdn_activate(
                _short_conv(win_ref, cw_ref, SUBLANES - (GDN_CONV - 1), tm, col), col)

    q_scale = FOX_DIM ** -0.5 * (LOG2E if prompt else 1.0)
    q_ref[...] = (_dot(xb, w_ref[:, 0:FOX_WIDTH]) * q_scale).astype(BF16)
    conv_some(3)
    kf = _dot(xb, w_ref[:, FOX_WIDTH:2 * FOX_WIDTH])
    if not prompt:
        k_ref[...] = kf
    else:
        k_ref[0] = kf.T.reshape(FOX_HEADS, FOX_DIM, tm)
        kb_ref[...] = kf.astype(BF16)
    conv_some(3)
    vf = _dot(xb, w_ref[:, 2 * FOX_WIDTH:3 * FOX_WIDTH])
    if not prompt:
        v_ref[...] = vf
    else:
        vt = vf.T.reshape(FOX_HEADS, FOX_DIM, tm)
        v_ref[0] = vt
        va_ref[0, :, 0:FOX_DIM, :] = vt.astype(BF16)
        va_ref[0, :, FOX_DIM:, :] = jnp.ones((FOX_HEADS, ONES_ROWS, tm), BF16)
    conv_some(3)
    z_ref[...] = _dot(xb, w_ref[:, COL_GZ:COL_SMALL])
    conv_some(3)

    z = _dot(xb, w_ref[:, COL_SMALL:N_COLS])
    lane = lax.broadcasted_iota(jnp.int32, z.shape, 1)
    zb = z + prm_ref[0:1, :]
    e = jnp.exp(-jnp.abs(zb))
    l1p = jnp.log(1.0 + e)
    logf = jnp.minimum(zb, 0.0) - l1p
    g = -jnp.exp(prm_ref[1:2, :]) * (jnp.maximum(zb, 0.0) + l1p)
    beta = _sigmoid(z)
    sm = jnp.where(lane < G_LANE, logf,
                   jnp.where(lane < BETA_LANE, g,
                             jnp.where(lane < BETA_LANE + GDN_HEADS, beta, 0.0)))
    sm_ref[...] = sm

    if prompt:
        i = pl.program_id(0)

        @pl.when(i % tiles_per_seq == 0)
        def _():
            carry_ref[...] = jnp.zeros_like(carry_ref)

        c = _exact_left(_tri(tm).astype(BF16), sm) + carry_ref[0:1, :]
        carry_ref[0:1, :] = c[tm - 1:tm, :]
        parts = jnp.concatenate(_split3(c * LOG2E), axis=1)
        aug = _dot(parts, aug_ref[...]) + augc_ref[...]
        qa_ref[...] = aug[:, :FOX_WIDTH].astype(BF16)
        ka_ref[...] = aug[:, FOX_WIDTH:].astype(BF16)


AUG_LANES = 6
ONES_ROWS = 16
FOX_BLOCK = 1024
FOX_QUERY_SPLITS = 4


def _aug_tables():
    place = np.zeros((3 * LANES, 2 * FOX_WIDTH), np.float32)
    ones = np.zeros((1, 2 * FOX_WIDTH), np.float32)
    for h in range(FOX_HEADS):
        q0 = (h // 2) * LANES + (h % 2) * AUG_LANES
        k0 = FOX_WIDTH + q0
        for part in range(3):
            place[part * LANES + LOGF_LANE + h, q0 + part] = 1.0
            place[part * LANES + LOGF_LANE + h, k0 + 3 + part] = -1.0
            ones[0, q0 + 3 + part] = 1.0
            ones[0, k0 + part] = 1.0
    return jnp.asarray(place, BF16), jnp.asarray(ones, F32)


def _inproj(x2d, norm_w, w_cat, prm, conv_w=None, *, tm, seq_len, prompt):
    n = x2d.shape[0]
    assert n % tm == 0 and (not prompt or seq_len % tm == 0)
    tps = seq_len // tm if prompt else 1
    row = lambda w: pl.BlockSpec((tm, w), lambda i: (i, 0))
    const = lambda a: pl.BlockSpec(a.shape, lambda i: (0,) * a.ndim, pipeline_mode=pl.Buffered(1))
    ins = [x2d, norm_w, w_cat, prm]
    out_shape = [jax.ShapeDtypeStruct((n, FOX_WIDTH), BF16),
                 jax.ShapeDtypeStruct((n, FOX_WIDTH), F32),
                 jax.ShapeDtypeStruct((n, FOX_WIDTH), F32)]
    out_specs = [row(FOX_WIDTH), row(FOX_WIDTH), row(FOX_WIDTH)]
    scratch = []
    if prompt:
        kv_t = jax.ShapeDtypeStruct((n // seq_len, FOX_HEADS, FOX_DIM, seq_len), F32)
        out_shape[1:3] = [kv_t, kv_t]
        out_specs[1:3] = [pl.BlockSpec((1, FOX_HEADS, FOX_DIM, tm), lambda i: (i // tps, 0, 0, i % tps))] * 2
        ins += list(_aug_tables()) + [conv_w]
        out_shape += [jax.ShapeDtypeStruct((n, FOX_WIDTH), BF16)] * 3 + [
            jax.ShapeDtypeStruct((n // seq_len, FOX_HEADS, FOX_DIM + ONES_ROWS, seq_len), BF16)]
        out_specs += [row(FOX_WIDTH)] * 3 + [
            pl.BlockSpec((1, FOX_HEADS, FOX_DIM + ONES_ROWS, tm), lambda i: (i // tps, 0, 0, i % tps))]
        out_shape += [jax.ShapeDtypeStruct((n // seq_len, SUBLANES, GDN_CONV_CH), F32)]
        out_specs += [pl.BlockSpec((1, SUBLANES, GDN_CONV_CH), lambda i: (i // tps, 0, 0))]
        scratch = [pltpu.VMEM((SUBLANES, LANES), F32), pltpu.VMEM((SUBLANES + tm, GDN_CONV_CH), F32)]
    out_shape += [jax.ShapeDtypeStruct((n, GDN_CONV_CH), F32),
                  jax.ShapeDtypeStruct((n, GDN_WIDTH), F32),
                  jax.ShapeDtypeStruct((n, LANES), F32)]
    out_specs += [row(GDN_CONV_CH), row(GDN_WIDTH), row(LANES)]
    return pl.pallas_call(
        functools.partial(_inproj_kernel, tiles_per_seq=tps, prompt=prompt),
        grid=(n // tm,),
        in_specs=[row(D_MODEL)] + [const(a) for a in ins[1:]],
        out_specs=out_specs, out_shape=out_shape, scratch_shapes=scratch,
        compiler_params=pltpu.CompilerParams(dimension_semantics=("arbitrary",),
                                             vmem_limit_bytes=VMEM_LIMIT),
        name="inproj_prompt" if prompt else "inproj_sample",
    )(*ins)


def _fox_prompt_kernel(q_ref, qa_ref, k_ref, ka_ref, va_ref, o_ref, sa_ref, sb_ref, *, t):
    i = pl.program_id(2)
    q = q_ref[0]
    qa = qa_ref[0]
    lane = lax.broadcasted_iota(jnp.int32, (t, LANES), 1)
    low = lane < FOX_DIM
    zero = jnp.zeros_like(q)
    qcat = []
    for h in range(2):
        mine = (lane >= h * AUG_LANES) & (lane < (h + 1) * AUG_LANES)
        qcat.append(jnp.concatenate([jnp.where(low if h == 0 else jnp.logical_not(low), q, zero),
                                     jnp.where(mine, qa, zero)], axis=1))

    tq = t // FOX_QUERY_SPLITS
    units = [(h, u) for h in range(2) for u in range(FOX_QUERY_SPLITS)]
    qunit = [qcat[h][u * tq:(u + 1) * tq] for h, u in units]

    def logits(j, s_ref):
        start = pl.multiple_of(j * t, t)
        kcat = jnp.concatenate([k_ref[0, pl.ds(start, t), :], ka_ref[0, pl.ds(start, t), :]], axis=1)
        for n in range(len(units)):
            s_ref[n] = _dot_nt(kcat, qunit[n])

    def consume(j, s_ref, carry, diag):
        start = pl.multiple_of(j * t, t)
        out = []
        for n, (h, u) in enumerate(units):
            m, acc = carry[n]
            nk = (u + 1) * tq if diag else t
            s = s_ref[n, 0:nk, :]
            if diag:
                key = lax.broadcasted_iota(jnp.int32, (nk, tq), 0)
                query = lax.broadcasted_iota(jnp.int32, (nk, tq), 1) + u * tq
                s = jnp.where(key <= query, s, MASKED)
            m_new = jnp.maximum(m, jnp.max(s, axis=0, keepdims=True))
            p = jnp.exp2(s - m_new).astype(BF16)
            acc = jnp.exp2(m - m_new) * acc + _dot(va_ref[0, h, :, pl.ds(start, nk)], p)
            out.append((m_new, acc))
        return tuple(out)

    init = tuple((jnp.full((1, tq), MASKED, F32), jnp.zeros((FOX_DIM + ONES_ROWS, tq), F32)) for _ in units)
    logits(i, sa_ref)
    logits(0, sb_ref)
    carry = consume(i, sa_ref, init, True)

    def two_blocks(jj, carry):
        j = 2 * jj
        logits(j + 1, sa_ref)
        carry = consume(j, sb_ref, carry, False)
        logits(j + 2, sb_ref)
        return consume(j + 1, sa_ref, carry, False)

    carry = lax.fori_loop(0, i // 2, two_blocks, carry)
    carry = lax.cond(i % 2 == 1, lambda c: consume(i - 1, sb_ref, c, False), lambda c: c, carry)
    heads = [jnp.concatenate([carry[h * FOX_QUERY_SPLITS + u][1] for u in range(FOX_QUERY_SPLITS)], axis=1)
             for h in range(2)]
    o_t = jnp.concatenate([a[:FOX_DIM] / a[FOX_DIM:FOX_DIM + 1] for a in heads], axis=0)
    o_ref[0] = o_t.T.astype(BF16)


def _fox_prompt(q, qa, kb, ka, va, *, t):
    b, s, _ = q.shape
    nblk = s // t
    assert s % t == 0
    npair = FOX_HEADS // 2
    qblk = pl.BlockSpec((1, t, LANES), lambda bi, p, i: (bi, i, p))
    kblk = pl.BlockSpec((1, s, LANES), lambda bi, p, i: (bi, 0, p))
    return pl.pallas_call(
        functools.partial(_fox_prompt_kernel, t=t),
        grid=(b, npair, nblk),
        in_specs=[qblk, qblk, kblk, kblk,
                  pl.BlockSpec((1, 2, FOX_DIM + ONES_ROWS, s), lambda bi, p, i: (bi, p, 0, 0))],
        out_specs=qblk,
        out_shape=jax.ShapeDtypeStruct((b, s, FOX_WIDTH), BF16),
        scratch_shapes=[pltpu.VMEM((2 * FOX_QUERY_SPLITS, t, t // FOX_QUERY_SPLITS), F32)] * 2,
        compiler_params=pltpu.CompilerParams(
            dimension_semantics=("arbitrary", "arbitrary", "arbitrary"), vmem_limit_bytes=VMEM_LIMIT),
        name="fox_prompt",
    )(q, qa, kb, ka, va)


def _fox_sample_kernel(q_ref, kn_ref, vn_ref, sm_ref, lft_ref, kt_ref, vt_ref, o_ref,
                       rk_ref, m_ref, l_ref, acc_ref, *, kb, t):
    j = pl.program_id(1)
    nkv = pl.num_programs(1)

    def heads(x):
        return [x[:, h * FOX_DIM:(h + 1) * FOX_DIM] for h in range(FOX_HEADS)]

    qh = [x.astype(BF16) for x in heads(q_ref[0].astype(F32))]

    lf_new = sm_ref[0]
    cn = _exact_left(_tri(t).astype(BF16), lf_new)

    @pl.when(j == 0)
    def _():
        p = lft_ref.shape[2]
        nb = p // LANES
        x = jnp.concatenate([lft_ref[0, :, i * LANES:(i + 1) * LANES] for i in range(nb)], axis=0)
        r = lax.broadcasted_iota(jnp.int32, (LANES, LANES), 0)
        cc = lax.broadcasted_iota(jnp.int32, (LANES, LANES), 1)
        y = _exact_right(x, (r > cc).astype(BF16))
        tot = jnp.sum(x, axis=1, keepdims=True)
        off = jnp.zeros((FOX_HEADS, 1), F32)
        for i in range(nb - 1, -1, -1):
            rk_ref[:, i * LANES:(i + 1) * LANES] = y[i * FOX_HEADS:(i + 1) * FOX_HEADS, :] + off
            off = off + tot[i * FOX_HEADS:(i + 1) * FOX_HEADS, :]
        m_ref[...] = jnp.full_like(m_ref, MASKED)
        l_ref[...] = jnp.zeros_like(l_ref)
        acc_ref[...] = jnp.zeros_like(acc_ref)

    def update(s, pv):
        m = m_ref[:, 0:1]
        m_new = jnp.maximum(m, jnp.max(s, axis=1, keepdims=True))
        alpha = jnp.exp(m - m_new)
        p = jnp.exp(s - m_new)
        l_ref[...] = jnp.broadcast_to(alpha * l_ref[:, 0:1] + jnp.sum(p, axis=1, keepdims=True), l_ref.shape)
        acc_ref[...] = alpha * acc_ref[...] + pv(p.astype(BF16))
        m_ref[...] = jnp.broadcast_to(m_new, m_ref.shape)

    start = pl.multiple_of(j * kb, kb)
    rk = rk_ref[:, pl.ds(start, kb)]
    s = jnp.concatenate([_dot(qh[h], kt_ref[0, h].astype(BF16)) + (cn[:, h:h + 1] + rk[h:h + 1, :])
                         for h in range(FOX_HEADS)], axis=0)
    update(s, lambda p: jnp.concatenate(
        [_dot_nt(p[h * t:(h + 1) * t], vt_ref[0, h].astype(BF16)) for h in range(FOX_HEADS)], axis=0))

    @pl.when(j == nkv - 1)
    def _():
        r = lax.broadcasted_iota(jnp.int32, (t, t), 0)
        cc = lax.broadcasted_iota(jnp.int32, (t, t), 1)
        after = (r > cc).astype(F32)
        tri = _tri(t).astype(BF16)
        knh = heads(kn_ref[0])
        vnh = [x.astype(BF16) for x in heads(vn_ref[0])]
        s_new = jnp.concatenate(
            [jnp.where(_tri(t), _dot_nt(qh[h], knh[h].astype(BF16))
                       + _exact_left(tri, lf_new[:, h:h + 1] * after), MASKED) for h in range(FOX_HEADS)], axis=0)
        update(s_new, lambda p: jnp.concatenate(
            [_dot(p[h * t:(h + 1) * t], vnh[h]) for h in range(FOX_HEADS)], axis=0))
        o_full = acc_ref[...] / l_ref[:, 0:1]
        o_ref[0] = jnp.concatenate([o_full[h * t:(h + 1) * t] for h in range(FOX_HEADS)], axis=1).astype(BF16)


def _fox_sample(q, kn, vn, sm, lft, kt, vt, *, kb):
    b, t, _ = q.shape
    p = kt.shape[3]
    assert p % kb == 0 and kb % LANES == 0
    rows = FOX_HEADS * t
    per_b = lambda shape: pl.BlockSpec((1,) + shape, lambda bi, j: (bi, 0, 0))
    cache = pl.BlockSpec((1, FOX_HEADS, FOX_DIM, kb), lambda bi, j: (bi, 0, 0, j))
    return pl.pallas_call(
        functools.partial(_fox_sample_kernel, kb=kb, t=t),
        grid=(b, p // kb),
        in_specs=[per_b((t, FOX_WIDTH)), per_b((t, FOX_WIDTH)), per_b((t, FOX_WIDTH)), per_b((t, LANES)),
                  per_b((FOX_HEADS, p)), cache, cache],
        out_specs=per_b((t, FOX_WIDTH)),
        out_shape=jax.ShapeDtypeStruct((b, t, FOX_WIDTH), BF16),
        scratch_shapes=[pltpu.VMEM((FOX_HEADS, p), F32),
                        pltpu.VMEM((rows, LANES), F32), pltpu.VMEM((rows, LANES), F32),
                        pltpu.VMEM((rows, FOX_DIM), F32)],
        compiler_params=pltpu.CompilerParams(dimension_semantics=("arbitrary", "arbitrary"),
                                             vmem_limit_bytes=VMEM_LIMIT),
        name="fox_sample",
    )(q, kn, vn, sm, lft, kt, vt)


def _interleave(gens):
    results = [None] * len(gens)
    alive = list(range(len(gens)))
    while alive:
        for idx in list(alive):
            try:
                next(gens[idx])
            except StopIteration as stop:
                results[idx] = stop.value
                alive.remove(idx)
    return results


def _gdn_activate(acc, col):
    x = acc * _sigmoid(acc)
    if col >= 2 * GDN_QK:
        return x
    scale = GDN_DK ** -0.5 if col < GDN_QK else 1.0
    return x * (lax.rsqrt(jnp.sum(x * x, axis=-1, keepdims=True) + EPS) * scale)


def _short_conv(win_ref, cw_ref, row0, nrows, col, lead=()):
    acc = None
    for tap in range(GDN_CONV):
        term = (win_ref[lead + (slice(row0 + tap, row0 + tap + nrows), slice(col, col + LANES))]
                * cw_ref[tap:tap + 1, col:col + LANES])
        acc = term if acc is None else acc + term
    return acc


def _gdn_local(qn, kn, vh, g, g_row, g_last, beta, c):
    wide = c % LANES == 0
    dot3 = _dot3c if wide else _dot3
    incl = _tri(c)
    strict = _tri(c, strict=True)
    eg = jnp.exp(g)
    decay = jnp.exp(jnp.where(incl, g - g_row, MASKED))
    kbeta = kn * beta
    kq = _dot_nt(jnp.concatenate([kbeta, qn], axis=0).astype(BF16), kn.astype(BF16))
    yield
    lmat = jnp.where(strict, kq[:c] * decay, 0.0)
    a_qk = kq[c:] * decay

    tinv = (incl & jnp.logical_not(strict)).astype(F32) - lmat
    if wide:
        lh, ll = _split2(lmat)
        power = _dot3_parts(lh, ll, lh, ll)
    else:
        power = dot3(lmat, lmat)
    yield
    span = 2
    while 2 * span < c:
        if wide:
            ph, pl_ = _split2(power)
            th, tl = _split2(tinv)
            both = _dot3_parts(ph, pl_, jnp.concatenate([th, ph], axis=1), jnp.concatenate([tl, pl_], axis=1))
            yield
            tinv = tinv + both[:, :c]
            power = both[:, c:]
        else:
            step = dot3(power, tinv)
            power = dot3(power, power)
            yield
            tinv = tinv + step
        span *= 2
    step = dot3(power, tinv)
    yield
    tinv = tinv + step

    sol = dot3(tinv, jnp.concatenate([vh * beta, kbeta * eg], axis=1))
    yield
    kd = kn * jnp.exp(g_last - g)
    return sol[:, :GDN_DV], sol[:, GDN_DV:], qn * eg, a_qk, (kd.T if wide else kd)


def _gdn_state_step(locals_, states, g_lasts, c):
    wide = c % LANES == 0
    ws = [_dot(jnp.concatenate([w, qeg], axis=0).astype(BF16), sh.astype(BF16))
          for (_, w, qeg, _, _), sh in zip(locals_, states)]
    outs, new_states = [], []
    for (u, _, _, a_qk, kd), wsh, sh, g_last in zip(locals_, ws, states, g_lasts):
        db = (u - wsh[:c]).astype(BF16)
        decayed = sh * jnp.exp(g_last)
        if wide:
            ak = _dot(jnp.concatenate([a_qk, kd], axis=0).astype(BF16), db)
            outs.append(wsh[c:] + ak[:c])
            new_states.append(decayed + ak[c:])
        else:
            outs.append(wsh[c:] + _dot(a_qk.astype(BF16), db))
            new_states.append(decayed + _dot_tn(kd.astype(BF16), db))
    return outs, new_states


def _gdn_kernel(cur_ref, *refs, c, nb, cpb, activated):
    if activated:
        sm_ref, z_ref, nw_ref, o_ref, sout_ref, s_ref = refs
    else:
        hist_ref, sm_ref, z_ref, s0_ref, cw_ref, nw_ref, o_ref, sout_ref, xs_ref, s_ref = refs
    i = pl.program_id(1)
    rows = c * cpb
    wide = c % LANES == 0

    @pl.when(i == 0)
    def _():
        if activated:
            s_ref[...] = jnp.zeros_like(s_ref)
        else:
            s_ref[...] = s0_ref[...]

    if activated:
        def qkv(b, r0, col):
            return cur_ref[b, r0:r0 + c, col:col + LANES]
    else:
        xs_ref[:, 0:SUBLANES, :] = hist_ref[...]
        xs_ref[:, SUBLANES:SUBLANES + rows, :] = cur_ref[...]
        base = SUBLANES - (GDN_CONV - 1)

        def qkv(b, r0, col):
            return _gdn_activate(_short_conv(xs_ref, cw_ref, base + r0, c, col, lead=(b,)), col)

    incl = _tri(c)
    eye = incl & jnp.logical_not(_tri(c, strict=True))
    gens, g_lasts = [], []
    for b in range(nb):
        for ck in range(cpb):
            r0 = ck * c
            sm = sm_ref[b, r0:r0 + c, :]
            if wide:
                gs = _exact_left_c(incl.astype(BF16), sm)
                gst = gs.T
            else:
                gs = _exact_left(incl.astype(BF16), sm)
            for h in range(GDN_HEADS):
                g = gs[:, G_LANE + h:G_LANE + h + 1]
                if wide:
                    g_row = gst[G_LANE + h:G_LANE + h + 1, :]
                else:
                    g_row = _exact_left(jnp.ones((c, c), BF16), jnp.where(eye, g, 0.0))
                g_last = gs[c - 1:c, G_LANE + h:G_LANE + h + 1]
                g_lasts.append(g_last)
                gens.append(_gdn_local(
                    qkv(b, r0, h * GDN_DK), qkv(b, r0, GDN_QK + h * GDN_DK),
                    qkv(b, r0, 2 * GDN_QK + h * GDN_DV), g, g_row, g_last,
                    sm[:, BETA_LANE + h:BETA_LANE + h + 1], c))
    locals_ = _interleave(gens)

    states = [s_ref[b, h] for b in range(nb) for h in range(GDN_HEADS)]
    for ck in range(cpb):
        r0 = ck * c
        pick = [(b * cpb + ck) * GDN_HEADS + h for b in range(nb) for h in range(GDN_HEADS)]
        outs, states = _gdn_state_step([locals_[u] for u in pick], states, [g_lasts[u] for u in pick], c)
        for idx, o in enumerate(outs):
            b, h = divmod(idx, GDN_HEADS)
            on = o * lax.rsqrt(jnp.mean(o * o, axis=-1, keepdims=True) + EPS) * nw_ref[...]
            zh = z_ref[b, r0:r0 + c, h * GDN_DV:(h + 1) * GDN_DV]
            o_ref[b, r0:r0 + c, h * GDN_DV:(h + 1) * GDN_DV] = (on * (zh * _sigmoid(zh))).astype(BF16)
    for idx, st in enumerate(states):
        b, h = divmod(idx, GDN_HEADS)
        s_ref[b, h] = st

    @pl.when(i == pl.num_programs(1) - 1)
    def _():
        sout_ref[...] = s_ref[...]


def _gdn(gqkv, hist, sm, gz, s0, conv_w, norm_w, *, c, nb, cpb):
    b, t, _ = gqkv.shape
    rows = c * cpb
    assert t % rows == 0 and b % nb == 0 and c % SUBLANES == 0
    activated = hist is None
    blk = lambda w: pl.BlockSpec((nb, rows, w), lambda bi, i: (bi, i, 0))
    const = lambda a: pl.BlockSpec(a.shape, lambda bi, i: (0,) * a.ndim)
    state_spec = pl.BlockSpec((nb, GDN_HEADS, GDN_DK, GDN_DV), lambda bi, i: (bi, 0, 0, 0))
    scratch = [pltpu.VMEM((nb, GDN_HEADS, GDN_DK, GDN_DV), F32)]
    if activated:
        ins = [gqkv, sm, gz, norm_w]
        in_specs = [blk(GDN_CONV_CH), blk(LANES), blk(GDN_WIDTH), const(norm_w)]
    else:
        assert t == rows
        ins = [gqkv, hist, sm, gz, s0, conv_w, norm_w]
        in_specs = [blk(GDN_CONV_CH), pl.BlockSpec((nb, SUBLANES, GDN_CONV_CH), lambda bi, i: (bi, 0, 0)),
                    blk(LANES), blk(GDN_WIDTH), state_spec, const(conv_w), const(norm_w)]
        scratch = [pltpu.VMEM((nb, SUBLANES + rows, GDN_CONV_CH), F32)] + scratch
    return pl.pallas_call(
        functools.partial(_gdn_kernel, c=c, nb=nb, cpb=cpb, activated=activated),
        grid=(b // nb, t // rows),
        in_specs=in_specs,
        out_specs=[blk(GDN_WIDTH), state_spec],
        out_shape=[jax.ShapeDtypeStruct((b, t, GDN_WIDTH), BF16),
                   jax.ShapeDtypeStruct((b, GDN_HEADS, GDN_DK, GDN_DV), F32)],
        scratch_shapes=scratch,
        compiler_params=pltpu.CompilerParams(dimension_semantics=("arbitrary", "arbitrary"),
                                             vmem_limit_bytes=VMEM_LIMIT),
        name="gdn_prompt" if activated else "gdn_sample",
    )(*ins)


FFN_COLS = 256


def _ffn_kernel(of_ref, og_ref, x_ref, wo_ref, nfw_ref, wup_ref, cw_ref, cb_ref, wdn_ref, nlw_ref, *refs,
                nseq, rows, tiles_per_seq, has_state):
    if has_state:
        st_ref, y_ref, nst_ref, w0_ref, w1_ref, w2_ref, w3_ref, act_ref = refs
    else:
        y_ref, nst_ref, w0_ref, w1_ref, w2_ref, w3_ref, act_ref, carry_ref = refs
    windows = [(w0_ref, w1_ref), (w2_ref, w3_ref)]
    i = pl.program_id(0)
    hist0 = SUBLANES - (FFN_CONV - 1)

    h = (x_ref[...] + _dot(of_ref[...], wo_ref[0:FOX_WIDTH, :])
         + _dot(og_ref[...], wo_ref[FOX_WIDTH:FOX_WIDTH + GDN_WIDTH, :]))
    hn = (h * lax.rsqrt(jnp.mean(h * h, axis=-1, keepdims=True) + EPS) * nfw_ref[...]).astype(BF16)

    if not has_state:
        @pl.when(i % tiles_per_seq == 0)
        def _():
            carry_ref[...] = jnp.zeros_like(carry_ref)

    def project(buf_ref, col):
        up = _dot(hn, wup_ref[:, col:col + FFN_COLS])
        buf_ref[:, SUBLANES:SUBLANES + rows, :] = up.reshape(nseq, rows, FFN_COLS)
        if has_state:
            buf_ref[:, hist0:SUBLANES, :] = st_ref[:, :, col:col + FFN_COLS]
        else:
            buf_ref[:, hist0:SUBLANES, :] = carry_ref[:, hist0:SUBLANES, col:col + FFN_COLS]
        last = buf_ref[:, SUBLANES + rows - (FFN_CONV - 1):SUBLANES + rows, :]
        nst_ref[:, :, col:col + FFN_COLS] = last
        if not has_state:
            carry_ref[:, hist0:SUBLANES, col:col + FFN_COLS] = last

    def conv(buf_ref, col):
        uc = cb_ref[:, col:col + FFN_COLS][None]
        for tap in range(FFN_CONV):
            uc = uc + buf_ref[:, hist0 + tap:hist0 + tap + rows, :] * cw_ref[tap:tap + 1, col:col + FFN_COLS][None]
        return uc.reshape(nseq * rows, FFN_COLS)

    def project_slab(ci):
        gate_buf, val_buf = windows[ci % 2]
        project(gate_buf, ci * FFN_COLS)
        project(val_buf, D_FF + ci * FFN_COLS)

    n_slabs = D_FF // FFN_COLS
    project_slab(0)
    for ci in range(n_slabs):
        if ci + 1 < n_slabs:
            project_slab(ci + 1)
        col = ci * FFN_COLS
        gate_buf, val_buf = windows[ci % 2]
        gate = conv(gate_buf, col)
        val = conv(val_buf, D_FF + col)
        act_ref[:, col:col + FFN_COLS] = (gate * _sigmoid(gate) * val).astype(BF16)

    h2 = h + _dot(act_ref[...], wdn_ref[...])
    y_ref[...] = h2 * lax.rsqrt(jnp.mean(h2 * h2, axis=-1, keepdims=True) + EPS) * nlw_ref[...]


def _ffn(ofox, ogdn, x2d, w_o, norm_ffn_w, w_up, conv_w, conv_b, w_down, norm_final_w, state, *,
         nseq, rows, seq_len):
    n = x2d.shape[0]
    tm = nseq * rows
    assert n % tm == 0 and seq_len % rows == 0 and rows % SUBLANES == 0 and D_FF % FFN_COLS == 0
    has_state = state is not None
    assert (rows == seq_len) if has_state else (nseq == 1)
    tps = seq_len // rows
    nbatch = n // seq_len
    row = lambda w: pl.BlockSpec((tm, w), lambda i: (i, 0))
    const = lambda a: pl.BlockSpec(a.shape, lambda i: (0,) * a.ndim, pipeline_mode=pl.Buffered(1))
    st_spec = pl.BlockSpec((nseq, FFN_CONV - 1, 2 * D_FF), lambda i: (i // tps, 0, 0))
    ins = [ofox, ogdn, x2d, w_o, norm_ffn_w, w_up, conv_w, conv_b, w_down, norm_final_w]
    in_specs = [row(FOX_WIDTH), row(GDN_WIDTH), row(D_MODEL)] + [const(a) for a in ins[3:]]
    scratch = [pltpu.VMEM((nseq, SUBLANES + rows, FFN_COLS), F32)] * 4 + [pltpu.VMEM((tm, D_FF), BF16)]
    if has_state:
        ins.append(state)
        in_specs.append(st_spec)
    else:
        scratch.append(pltpu.VMEM((nseq, SUBLANES, 2 * D_FF), F32))
    return pl.pallas_call(
        functools.partial(_ffn_kernel, nseq=nseq, rows=rows, tiles_per_seq=tps, has_state=has_state),
        grid=(n // tm,),
        in_specs=in_specs,
        out_specs=[row(D_MODEL), st_spec],
        out_shape=[jax.ShapeDtypeStruct((n, D_MODEL), F32),
                   jax.ShapeDtypeStruct((nbatch, FFN_CONV - 1, 2 * D_FF), F32)],
        scratch_shapes=scratch,
        compiler_params=pltpu.CompilerParams(dimension_semantics=("arbitrary",),
                                             vmem_limit_bytes=VMEM_LIMIT),
        name="ffn_sample" if has_state else "ffn_prompt",
    )(*ins)


ROW_TILE = 512
SAMPLE_KV_BLOCK = 4096
GDN_CHUNK = 128
GDN_CHUNKS_PER_STEP = 8
GDN_SAMPLE_SEQS_PER_STEP = 8


def _prep_weights(w_in, b_fox_f, gdn_a_log, gdn_dt_bias):
    c0 = 3 * FOX_WIDTH
    c1 = c0 + FOX_HEADS
    c2 = c1 + GDN_CONV_CH
    c3 = c2 + 2 * GDN_HEADS
    wt = jnp.swapaxes(w_in, 0, 1)
    small = jnp.concatenate([wt[c0:c1], wt[c2:c3],
                             jnp.zeros((LANES - FOX_HEADS - 2 * GDN_HEADS, D_MODEL), wt.dtype)], axis=0)
    w_cat = jnp.swapaxes(jnp.concatenate([wt[:c0], wt[c1:c2], wt[c3:], small], axis=0), 0, 1).astype(BF16)
    pad = jnp.zeros((LANES - FOX_HEADS - GDN_HEADS,), F32)
    bias = jnp.concatenate([b_fox_f.astype(F32), gdn_dt_bias.astype(F32), pad])
    alog = jnp.concatenate([jnp.zeros((FOX_HEADS,), F32), gdn_a_log.astype(F32), pad])
    prm = jnp.zeros((SUBLANES, LANES), F32).at[0].set(bias).at[1].set(alog)
    return w_cat, prm


def _layer(xp, xs, fox_k, fox_v, fox_logf, st_gdn, st_gconv, st_fconv,
           w_in, b_fox_f, gdn_conv_w, gdn_a_log, gdn_dt_bias, gdn_norm_w, w_o,
           norm_mix_w, norm_ffn_w, w_up, ffn_conv_w, ffn_conv_b, w_down, norm_out_w):
    bp, sp, _ = xp.shape
    bs, ts, _ = xs.shape
    w_cat, prm = _prep_weights(w_in, b_fox_f, gdn_a_log, gdn_dt_bias)
    nmw = norm_mix_w.reshape(1, D_MODEL).astype(F32)
    nfw = norm_ffn_w.reshape(1, D_MODEL).astype(F32)
    now = norm_out_w.reshape(1, D_MODEL).astype(F32)
    gnw = gdn_norm_w.reshape(1, GDN_DV).astype(F32)
    w_o_b, w_up_b, w_dn_b = w_o.astype(BF16), w_up.astype(BF16), w_down.astype(BF16)
    cb = ffn_conv_b.reshape(1, 2 * D_FF).astype(F32)

    xp2 = xp.reshape(bp * sp, D_MODEL)
    q, k, v, kb, qa, ka, va, gtail, gact, gz, sm = _inproj(xp2, nmw, w_cat, prm, gdn_conv_w.astype(F32),
                                                          tm=ROW_TILE, seq_len=sp, prompt=True)
    r3 = lambda a: a.reshape(bp, sp, a.shape[-1])
    o_fox = _fox_prompt(r3(q), r3(qa), r3(kb), r3(ka), va, t=min(FOX_BLOCK, sp))
    o_gdn, p_state = _gdn(r3(gact), None, r3(sm), r3(gz), None, None, gnw,
                          c=GDN_CHUNK, nb=1, cpb=GDN_CHUNKS_PER_STEP)
    yp, p_fconv = _ffn(o_fox.reshape(bp * sp, FOX_WIDTH), o_gdn.reshape(bp * sp, GDN_WIDTH), xp2,
                       w_o_b, nfw, w_up_b, ffn_conv_w, cb, w_dn_b, now, None,
                       nseq=1, rows=ROW_TILE, seq_len=sp)
    p_out = (jnp.transpose(k, (0, 3, 1, 2)), jnp.transpose(v, (0, 3, 1, 2)),
             r3(sm)[:, :, LOGF_LANE:LOGF_LANE + FOX_HEADS], p_state,
             gtail[:, SUBLANES - (GDN_CONV - 1):], p_fconv)

    xs2 = xs.reshape(bs * ts, D_MODEL)
    q, k, v, gqkv, gz, sm = _inproj(xs2, nmw, w_cat, prm, tm=bs * ts, seq_len=ts, prompt=False)
    s3 = lambda a: a.reshape(bs, ts, a.shape[-1])
    past = fox_k.shape[1]
    lft = jnp.swapaxes(fox_logf.astype(F32), 1, 2)
    cache_t = lambda a: jnp.transpose(a.astype(F32), (0, 2, 3, 1))
    o_fox = _fox_sample(s3(q), s3(k), s3(v), s3(sm), lft, cache_t(fox_k), cache_t(fox_v),
                        kb=min(SAMPLE_KV_BLOCK, past))
    gqkv3 = s3(gqkv)
    hist = jnp.concatenate([jnp.zeros((bs, SUBLANES - (GDN_CONV - 1), GDN_CONV_CH), F32),
                            st_gconv.astype(F32)], axis=1)
    o_gdn, s_state = _gdn(gqkv3, hist, s3(sm), s3(gz), st_gdn.astype(F32), gdn_conv_w, gnw,
                          c=ts, nb=GDN_SAMPLE_SEQS_PER_STEP, cpb=1)
    ys, s_fconv = _ffn(o_fox.reshape(bs * ts, FOX_WIDTH), o_gdn.reshape(bs * ts, GDN_WIDTH), xs2,
                       w_o_b, nfw, w_up_b, ffn_conv_w, cb, w_dn_b, now, st_fconv.astype(F32),
                       nseq=bs, rows=ts, seq_len=ts)
    full = jnp.concatenate([st_gconv.astype(F32), gqkv3], axis=1)
    s_out = (k.reshape(bs, ts, FOX_HEADS, FOX_DIM), v.reshape(bs, ts, FOX_HEADS, FOX_DIM),
             s3(sm)[:, :, LOGF_LANE:LOGF_LANE + FOX_HEADS], s_state,
             full[:, ts:], s_fconv)
    return yp.reshape(bp, sp, D_MODEL), ys.reshape(bs, ts, D_MODEL), p_out, s_out


def kernel(x_prompt, x_sample, cache_fox_k, cache_fox_v, cache_fox_logf, state_gdn, state_gdn_conv,
           state_ffn_conv, w_in, b_fox_f, gdn_conv_w, gdn_a_log, gdn_dt_bias, gdn_norm_w, w_o, norm_mix_w,
           norm_ffn_w, w_up, ffn_conv_w, ffn_conv_b, w_down, norm_final_w):
    depth = w_in.shape[0]
    assert depth == 1, "the final RMSNorm is fused into the single layer's FFN kernel"
    yp, ys, p_out, s_out = _layer(
        x_prompt, x_sample, cache_fox_k[0], cache_fox_v[0], cache_fox_logf[0], state_gdn[0],
        state_gdn_conv[0], state_ffn_conv[0], w_in[0], b_fox_f[0], gdn_conv_w[0], gdn_a_log[0],
        gdn_dt_bias[0], gdn_norm_w[0], w_o[0], norm_mix_w[0], norm_ffn_w[0], w_up[0], ffn_conv_w[0],
        ffn_conv_b[0], w_down[0], norm_final_w)
    return (yp, ys) + tuple(a[None] for a in p_out) + tuple(a[None] for a in s_out)
```

```python
import functools
import math

import numpy as np
import jax
import jax.numpy as jnp
from jax import lax
from jax.experimental import pallas as pl
from jax.experimental.pallas import tpu as pltpu

F32 = jnp.float32
BF16 = jnp.bfloat16

D_MODEL = 1024
FOX_DIM = 64
FOX_HEADS = 8
FOX_WIDTH = FOX_HEADS * FOX_DIM
GDN_DK = 128
GDN_DV = 128
GDN_HEADS = 4
GDN_QK = GDN_HEADS * GDN_DK
GDN_WIDTH = GDN_HEADS * GDN_DV
GDN_CONV_CH = 2 * GDN_QK + GDN_WIDTH
GDN_CONV = 4
D_FF = 2816
FFN_CONV = 3
EPS = 1e-6

LANES = 128
SUBLANES = 8
VMEM_LIMIT = 56 * 1024 * 1024
MASKED = -1e30
LOG2E = math.log2(math.e)

LOGF_LANE = 0
G_LANE = FOX_HEADS
BETA_LANE = FOX_HEADS + GDN_HEADS

COL_FOX = 0
COL_GQKV = 3 * FOX_WIDTH
COL_GZ = COL_GQKV + GDN_CONV_CH
COL_SMALL = COL_GZ + GDN_WIDTH
N_COLS = COL_SMALL + LANES


def _dot(a, b):
    return jnp.dot(a, b, preferred_element_type=F32)


def _dot_nt(a, b):
    return lax.dot_general(a, b, (((1,), (1,)), ((), ())), preferred_element_type=F32)


def _dot_tn(a, b):
    return lax.dot_general(a, b, (((0,), (0,)), ((), ())), preferred_element_type=F32)


def _split3(x):
    hi = x.astype(BF16)
    r = x - hi.astype(F32)
    lo = r.astype(BF16)
    lo2 = (r - lo.astype(F32)).astype(BF16)
    return hi, lo, lo2


def _exact_left(mat, x):
    hi, lo, lo2 = _split3(x)
    return _dot(mat, hi) + _dot(mat, lo) + _dot(mat, lo2)


def _exact_right(x, mat):
    hi, lo, lo2 = _split3(x)
    return _dot(hi, mat) + _dot(lo, mat) + _dot(lo2, mat)


def _dot3(a, b):
    ah = a.astype(BF16)
    al = (a - ah.astype(F32)).astype(BF16)
    bh = b.astype(BF16)
    bl = (b - bh.astype(F32)).astype(BF16)
    return _dot(ah, bh) + (_dot(ah, bl) + _dot(al, bh))


def _exact_left_c(mat, x):
    return _dot(jnp.concatenate([mat, mat, mat], axis=1), jnp.concatenate(_split3(x), axis=0))


def _split2(x):
    hi = x.astype(BF16)
    return hi, (x - hi.astype(F32)).astype(BF16)


def _dot3_parts(ah, al, bh, bl):
    return _dot(jnp.concatenate([ah, ah, al], axis=1), jnp.concatenate([bh, bl, bh], axis=0))


def _dot3c(a, b):
    return _dot3_parts(*_split2(a), *_split2(b))


def _sigmoid(x):
    return 1.0 / (1.0 + jnp.exp(-x))


def _tri(n, strict=False):
    r = lax.broadcasted_iota(jnp.int32, (n, n), 0)
    c = lax.broadcasted_iota(jnp.int32, (n, n), 1)
    return (r > c) if strict else (r >= c)


def _inproj_kernel(x_ref, nw_ref, w_ref, prm_ref, *refs, tiles_per_seq, prompt):
    if prompt:
        (aug_ref, augc_ref, cw_ref, q_ref, k_ref, v_ref, kb_ref, qa_ref, ka_ref, va_ref, tail_ref,
         g_ref, z_ref, sm_ref, carry_ref, win_ref) = refs
    else:
        q_ref, k_ref, v_ref, g_ref, z_ref, sm_ref = refs
    x = x_ref[...]
    xn = x * lax.rsqrt(jnp.mean(x * x, axis=-1, keepdims=True) + EPS) * nw_ref[...]
    xb = xn.astype(BF16)

    tm = x.shape[0]
    gq = _dot(xb, w_ref[:, COL_GQKV:COL_GZ])
    conv_cols = []
    if not prompt:
        g_ref[...] = gq
    else:
        first = pl.program_id(0) % tiles_per_seq == 0

        @pl.when(first)
        def _():
            win_ref[0:SUBLANES, :] = jnp.zeros((SUBLANES, GDN_CONV_CH), F32)

        @pl.when(jnp.logical_not(first))
        def _():
            win_ref[0:SUBLANES, :] = win_ref[tm:tm + SUBLANES, :]

        win_ref[SUBLANES:SUBLANES + tm, :] = gq
        tail_ref[0] = gq[tm - SUBLANES:tm, :]
        conv_cols = list(range(0, GDN_CONV_CH, LANES))

    def conv_some(count):
        for _ in range(min(count, len(conv_cols))):
            col = conv_cols.pop(0)
            g_ref[:, col:col + LANES] = _gdn_activate(
                _short_conv(win_ref, cw_ref, SUBLANES - (GDN_CONV - 1), tm, col), col)

    q_scale = FOX_DIM ** -0.5 * (LOG2E if prompt else 1.0)
    q_ref[...] = (_dot(xb, w_ref[:, 0:FOX_WIDTH]) * q_scale).astype(BF16)
    conv_some(3)
    kf = _dot(xb, w_ref[:, FOX_WIDTH:2 * FOX_WIDTH])
    if not prompt:
        k_ref[...] = kf
    else:
        k_ref[0] = kf.T.reshape(FOX_HEADS, FOX_DIM, tm)
        kb_ref[...] = kf.astype(BF16)
    conv_some(3)
    vf = _dot(xb, w_ref[:, 2 * FOX_WIDTH:3 * FOX_WIDTH])
    if not prompt:
        v_ref[...] = vf
    else:
        vt = vf.T.reshape(FOX_HEADS, FOX_DIM, tm)
        v_ref[0] = vt
        va_ref[0, :, 0:FOX_DIM, :] = vt.astype(BF16)
        va_ref[0, :, FOX_DIM:, :] = jnp.ones((FOX_HEADS, ONES_ROWS, tm), BF16)
    conv_some(3)
    z_ref[...] = _dot(xb, w_ref[:, COL_GZ:COL_SMALL])
    conv_some(3)

    z = _dot(xb, w_ref[:, COL_SMALL:N_COLS])
    lane = lax.broadcasted_iota(jnp.int32, z.shape, 1)
    zb = z + prm_ref[0:1, :]
    e = jnp.exp(-jnp.abs(zb))
    l1p = jnp.log(1.0 + e)
    logf = jnp.minimum(zb, 0.0) - l1p
    g = -jnp.exp(prm_ref[1:2, :]) * (jnp.maximum(zb, 0.0) + l1p)
    beta = _sigmoid(z)
    sm = jnp.where(lane < G_LANE, logf,
                   jnp.where(lane < BETA_LANE, g,
                             jnp.where(lane < BETA_LANE + GDN_HEADS, beta, 0.0)))
    sm_ref[...] = sm

    if prompt:
        i = pl.program_id(0)

        @pl.when(i % tiles_per_seq == 0)
        def _():
            carry_ref[...] = jnp.zeros_like(carry_ref)

        c = _exact_left(_tri(tm).astype(BF16), sm) + carry_ref[0:1, :]
        carry_ref[0:1, :] = c[tm - 1:tm, :]
        parts = jnp.concatenate(_split3(c * LOG2E), axis=1)
        aug = _dot(parts, aug_ref[...]) + augc_ref[...]
        qa_ref[...] = aug[:, :FOX_WIDTH].astype(BF16)
        ka_ref[...] = aug[:, FOX_WIDTH:].astype(BF16)


AUG_LANES = 6
ONES_ROWS = 16
FOX_BLOCK = 1024
FOX_QUERY_SPLITS = 4


def _aug_tables():
    place = np.zeros((3 * LANES, 2 * FOX_WIDTH), np.float32)
    ones = np.zeros((1, 2 * FOX_WIDTH), np.float32)
    for h in range(FOX_HEADS):
        q0 = (h // 2) * LANES + (h % 2) * AUG_LANES
        k0 = FOX_WIDTH + q0
        for part in range(3):
            place[part * LANES + LOGF_LANE + h, q0 + part] = 1.0
            place[part * LANES + LOGF_LANE + h, k0 + 3 + part] = -1.0
            ones[0, q0 + 3 + part] = 1.0
            ones[0, k0 + part] = 1.0
    return jnp.asarray(place, BF16), jnp.asarray(ones, F32)


def _inproj(x2d, norm_w, w_cat, prm, conv_w=None, *, tm, seq_len, prompt):
    n = x2d.shape[0]
    assert n % tm == 0 and (not prompt or seq_len % tm == 0)
    tps = seq_len // tm if prompt else 1
    row = lambda w: pl.BlockSpec((tm, w), lambda i: (i, 0))
    const = lambda a: pl.BlockSpec(a.shape, lambda i: (0,) * a.ndim, pipeline_mode=pl.Buffered(1))
    ins = [x2d, norm_w, w_cat, prm]
    out_shape = [jax.ShapeDtypeStruct((n, FOX_WIDTH), BF16),
                 jax.ShapeDtypeStruct((n, FOX_WIDTH), F32),
                 jax.ShapeDtypeStruct((n, FOX_WIDTH), F32)]
    out_specs = [row(FOX_WIDTH), row(FOX_WIDTH), row(FOX_WIDTH)]
    scratch = []
    if prompt:
        kv_t = jax.ShapeDtypeStruct((n // seq_len, FOX_HEADS, FOX_DIM, seq_len), F32)
        out_shape[1:3] = [kv_t, kv_t]
        out_specs[1:3] = [pl.BlockSpec((1, FOX_HEADS, FOX_DIM, tm), lambda i: (i // tps, 0, 0, i % tps))] * 2
        ins += list(_aug_tables()) + [conv_w]
        out_shape += [jax.ShapeDtypeStruct((n, FOX_WIDTH), BF16)] * 3 + [
            jax.ShapeDtypeStruct((n // seq_len, FOX_HEADS, FOX_DIM + ONES_ROWS, seq_len), BF16)]
        out_specs += [row(FOX_WIDTH)] * 3 + [
            pl.BlockSpec((1, FOX_HEADS, FOX_DIM + ONES_ROWS, tm), lambda i: (i // tps, 0, 0, i % tps))]
        out_shape += [jax.ShapeDtypeStruct((n // seq_len, SUBLANES, GDN_CONV_CH), F32)]
        out_specs += [pl.BlockSpec((1, SUBLANES, GDN_CONV_CH), lambda i: (i // tps, 0, 0))]
        scratch = [pltpu.VMEM((SUBLANES, LANES), F32), pltpu.VMEM((SUBLANES + tm, GDN_CONV_CH), F32)]
    out_shape += [jax.ShapeDtypeStruct((n, GDN_CONV_CH), F32),
                  jax.ShapeDtypeStruct((n, GDN_WIDTH), F32),
                  jax.ShapeDtypeStruct((n, LANES), F32)]
    out_specs += [row(GDN_CONV_CH), row(GDN_WIDTH), row(LANES)]
    return pl.pallas_call(
        functools.partial(_inproj_kernel, tiles_per_seq=tps, prompt=prompt),
        grid=(n // tm,),
        in_specs=[row(D_MODEL)] + [const(a) for a in ins[1:]],
        out_specs=out_specs, out_shape=out_shape, scratch_shapes=scratch,
        compiler_params=pltpu.CompilerParams(dimension_semantics=("arbitrary",),
                                             vmem_limit_bytes=VMEM_LIMIT),
        name="inproj_prompt" if prompt else "inproj_sample",
    )(*ins)


def _fox_prompt_kernel(q_ref, qa_ref, k_ref, ka_ref, va_ref, o_ref, sa_ref, sb_ref, *, t):
    i = pl.program_id(2)
    q = q_ref[0]
    qa = qa_ref[0]
    lane = lax.broadcasted_iota(jnp.int32, (t, LANES), 1)
    low = lane < FOX_DIM
    zero = jnp.zeros_like(q)
    qcat = []
    for h in range(2):
        mine = (lane >= h * AUG_LANES) & (lane < (h + 1) * AUG_LANES)
        qcat.append(jnp.concatenate([jnp.where(low if h == 0 else jnp.logical_not(low), q, zero),
                                     jnp.where(mine, qa, zero)], axis=1))

    tq = t // FOX_QUERY_SPLITS
    units = [(h, u) for h in range(2) for u in range(FOX_QUERY_SPLITS)]
    qunit = [qcat[h][u * tq:(u + 1) * tq] for h, u in units]

    def logits(j, s_ref):
        start = pl.multiple_of(j * t, t)
        kcat = jnp.concatenate([k_ref[0, pl.ds(start, t), :], ka_ref[0, pl.ds(start, t), :]], axis=1)
        tops = []
        for n in range(len(units)):
            s = _dot_nt(kcat, qunit[n])
            s_ref[n] = s
            tops.append(jnp.max(s, axis=0, keepdims=True))
        return tuple(tops)

    def consume(j, s_ref, tops, carry, diag):
        start = pl.multiple_of(j * t, t)
        out = []
        for n, (h, u) in enumerate(units):
            m, acc = carry[n]
            nk = (u + 1) * tq if diag else t
            s = s_ref[n, 0:nk, :]
            if diag:
                key = lax.broadcasted_iota(jnp.int32, (nk, tq), 0)
                query = lax.broadcasted_iota(jnp.int32, (nk, tq), 1) + u * tq
                s = jnp.where(key <= query, s, MASKED)
                top = jnp.max(s, axis=0, keepdims=True)
            else:
                top = tops[n]
            m_new = jnp.maximum(m, top)
            p = jnp.exp2(s - m_new).astype(BF16)
            acc = jnp.exp2(m - m_new) * acc + _dot(va_ref[0, h, :, pl.ds(start, nk)], p)
            out.append((m_new, acc))
        return tuple(out)

    init = tuple((jnp.full((1, tq), MASKED, F32), jnp.zeros((FOX_DIM + ONES_ROWS, tq), F32)) for _ in units)
    logits(i, sa_ref)
    tops_b = logits(0, sb_ref)
    carry = consume(i, sa_ref, None, init, True)

    def two_blocks(jj, state):
        carry, tops_b = state
        j = 2 * jj
        tops_a = logits(j + 1, sa_ref)
        carry = consume(j, sb_ref, tops_b, carry, False)
        tops_b = logits(j + 2, sb_ref)
        return consume(j + 1, sa_ref, tops_a, carry, False), tops_b

    carry, tops_b = lax.fori_loop(0, i // 2, two_blocks, (carry, tops_b))
    carry = lax.cond(i % 2 == 1, lambda c: consume(i - 1, sb_ref, tops_b, c, False), lambda c: c, carry)
    heads = [jnp.concatenate([carry[h * FOX_QUERY_SPLITS + u][1] for u in range(FOX_QUERY_SPLITS)], axis=1)
             for h in range(2)]
    o_t = jnp.concatenate([a[:FOX_DIM] / a[FOX_DIM:FOX_DIM + 1] for a in heads], axis=0)
    o_ref[0] = o_t.T.astype(BF16)


def _fox_prompt(q, qa, kb, ka, va, *, t):
    b, s, _ = q.shape
    nblk = s // t
    assert s % t == 0
    npair = FOX_HEADS // 2
    qblk = pl.BlockSpec((1, t, LANES), lambda bi, p, i: (bi, i, p))
    kblk = pl.BlockSpec((1, s, LANES), lambda bi, p, i: (bi, 0, p))
    return pl.pallas_call(
        functools.partial(_fox_prompt_kernel, t=t),
        grid=(b, npair, nblk),
        in_specs=[qblk, qblk, kblk, kblk,
                  pl.BlockSpec((1, 2, FOX_DIM + ONES_ROWS, s), lambda bi, p, i: (bi, p, 0, 0))],
        out_specs=qblk,
        out_shape=jax.ShapeDtypeStruct((b, s, FOX_WIDTH), BF16),
        scratch_shapes=[pltpu.VMEM((2 * FOX_QUERY_SPLITS, t, t // FOX_QUERY_SPLITS), F32)] * 2,
        compiler_params=pltpu.CompilerParams(
            dimension_semantics=("arbitrary", "arbitrary", "arbitrary"), vmem_limit_bytes=VMEM_LIMIT),
        name="fox_prompt",
    )(q, qa, kb, ka, va)


def _fox_sample_kernel(q_ref, kn_ref, vn_ref, sm_ref, lft_ref, kt_ref, vt_ref, o_ref,
                       rk_ref, m_ref, l_ref, acc_ref, *, kb, t):
    j = pl.program_id(1)
    nkv = pl.num_programs(1)

    def heads(x):
        return [x[:, h * FOX_DIM:(h + 1) * FOX_DIM] for h in range(FOX_HEADS)]

    qh = [x.astype(BF16) for x in heads(q_ref[0].astype(F32))]

    lf_new = sm_ref[0]
    cn = _exact_left(_tri(t).astype(BF16), lf_new)

    @pl.when(j == 0)
    def _():
        p = lft_ref.shape[2]
        nb = p // LANES
        x = jnp.concatenate([lft_ref[0, :, i * LANES:(i + 1) * LANES] for i in range(nb)], axis=0)
        r = lax.broadcasted_iota(jnp.int32, (LANES, LANES), 0)
        cc = lax.broadcasted_iota(jnp.int32, (LANES, LANES), 1)
        y = _exact_right(x, (r > cc).astype(BF16))
        tot = jnp.sum(x, axis=1, keepdims=True)
        off = jnp.zeros((FOX_HEADS, 1), F32)
        for i in range(nb - 1, -1, -1):
            rk_ref[:, i * LANES:(i + 1) * LANES] = y[i * FOX_HEADS:(i + 1) * FOX_HEADS, :] + off
            off = off + tot[i * FOX_HEADS:(i + 1) * FOX_HEADS, :]
        m_ref[...] = jnp.full_like(m_ref, MASKED)
        l_ref[...] = jnp.zeros_like(l_ref)
        acc_ref[...] = jnp.zeros_like(acc_ref)

    def update(s, pv):
        m = m_ref[:, 0:1]
        m_new = jnp.maximum(m, jnp.max(s, axis=1, keepdims=True))
        alpha = jnp.exp(m - m_new)
        p = jnp.exp(s - m_new)
        l_ref[...] = jnp.broadcast_to(alpha * l_ref[:, 0:1] + jnp.sum(p, axis=1, keepdims=True), l_ref.shape)
        acc_ref[...] = alpha * acc_ref[...] + pv(p.astype(BF16))
        m_ref[...] = jnp.broadcast_to(m_new, m_ref.shape)

    start = pl.multiple_of(j * kb, kb)
    rk = rk_ref[:, pl.ds(start, kb)]
    s = jnp.concatenate([_dot(qh[h], kt_ref[0, h].astype(BF16)) + (cn[:, h:h + 1] + rk[h:h + 1, :])
                         for h in range(FOX_HEADS)], axis=0)
    update(s, lambda p: jnp.concatenate(
        [_dot_nt(p[h * t:(h + 1) * t], vt_ref[0, h].astype(BF16)) for h in range(FOX_HEADS)], axis=0))

    @pl.when(j == nkv - 1)
    def _():
        r = lax.broadcasted_iota(jnp.int32, (t, t), 0)
        cc = lax.broadcasted_iota(jnp.int32, (t, t), 1)
        after = (r > cc).astype(F32)
        tri = _tri(t).astype(BF16)
        knh = heads(kn_ref[0])
        vnh = [x.astype(BF16) for x in heads(vn_ref[0])]
        s_new = jnp.concatenate(
            [jnp.where(_tri(t), _dot_nt(qh[h], knh[h].astype(BF16))
                       + _exact_left(tri, lf_new[:, h:h + 1] * after), MASKED) for h in range(FOX_HEADS)], axis=0)
        update(s_new, lambda p: jnp.concatenate(
            [_dot(p[h * t:(h + 1) * t], vnh[h]) for h in range(FOX_HEADS)], axis=0))
        o_full = acc_ref[...] / l_ref[:, 0:1]
        o_ref[0] = jnp.concatenate([o_full[h * t:(h + 1) * t] for h in range(FOX_HEADS)], axis=1).astype(BF16)


def _fox_sample(q, kn, vn, sm, lft, kt, vt, *, kb):
    b, t, _ = q.shape
    p = kt.shape[3]
    assert p % kb == 0 and kb % LANES == 0
    rows = FOX_HEADS * t
    per_b = lambda shape: pl.BlockSpec((1,) + shape, lambda bi, j: (bi, 0, 0))
    cache = pl.BlockSpec((1, FOX_HEADS, FOX_DIM, kb), lambda bi, j: (bi, 0, 0, j))
    return pl.pallas_call(
        functools.partial(_fox_sample_kernel, kb=kb, t=t),
        grid=(b, p // kb),
        in_specs=[per_b((t, FOX_WIDTH)), per_b((t, FOX_WIDTH)), per_b((t, FOX_WIDTH)), per_b((t, LANES)),
                  per_b((FOX_HEADS, p)), cache, cache],
        out_specs=per_b((t, FOX_WIDTH)),
        out_shape=jax.ShapeDtypeStruct((b, t, FOX_WIDTH), BF16),
        scratch_shapes=[pltpu.VMEM((FOX_HEADS, p), F32),
                        pltpu.VMEM((rows, LANES), F32), pltpu.VMEM((rows, LANES), F32),
                        pltpu.VMEM((rows, FOX_DIM), F32)],
        compiler_params=pltpu.CompilerParams(dimension_semantics=("arbitrary", "arbitrary"),
                                             vmem_limit_bytes=VMEM_LIMIT),
        name="fox_sample",
    )(q, kn, vn, sm, lft, kt, vt)


def _interleave(gens):
    results = [None] * len(gens)
    alive = list(range(len(gens)))
    while alive:
        for idx in list(alive):
            try:
                next(gens[idx])
            except StopIteration as stop:
                results[idx] = stop.value
                alive.remove(idx)
    return results


def _gdn_activate(acc, col):
    x = acc * _sigmoid(acc)
    if col >= 2 * GDN_QK:
        return x
    scale = GDN_DK ** -0.5 if col < GDN_QK else 1.0
    return x * (lax.rsqrt(jnp.sum(x * x, axis=-1, keepdims=True) + EPS) * scale)


def _short_conv(win_ref, cw_ref, row0, nrows, col, lead=()):
    acc = None
    for tap in range(GDN_CONV):
        term = (win_ref[lead + (slice(row0 + tap, row0 + tap + nrows), slice(col, col + LANES))]
                * cw_ref[tap:tap + 1, col:col + LANES])
        acc = term if acc is None else acc + term
    return acc


def _gdn_local(qn, kn, vh, g, g_row, g_last, beta, c):
    wide = c % LANES == 0
    dot3 = _dot3c if wide else _dot3
    incl = _tri(c)
    strict = _tri(c, strict=True)
    eg = jnp.exp(g)
    decay = jnp.exp(jnp.where(incl, g - g_row, MASKED))
    kbeta = kn * beta
    kq = _dot_nt(jnp.concatenate([kbeta, qn], axis=0).astype(BF16), kn.astype(BF16))
    yield
    lmat = jnp.where(strict, kq[:c] * decay, 0.0)
    a_qk = kq[c:] * decay

    tinv = (incl & jnp.logical_not(strict)).astype(F32) - lmat
    if wide:
        lh, ll = _split2(lmat)
        power = _dot3_parts(lh, ll, lh, ll)
    else:
        power = dot3(lmat, lmat)
    yield
    span = 2
    while 2 * span < c:
        if wide:
            ph, pl_ = _split2(power)
            th, tl = _split2(tinv)
            both = _dot3_parts(ph, pl_, jnp.concatenate([th, ph], axis=1), jnp.concatenate([tl, pl_], axis=1))
            yield
            tinv = tinv + both[:, :c]
            power = both[:, c:]
        else:
            step = dot3(power, tinv)
            power = dot3(power, power)
            yield
            tinv = tinv + step
        span *= 2
    step = dot3(power, tinv)
    yield
    tinv = tinv + step

    sol = dot3(tinv, jnp.concatenate([vh * beta, kbeta * eg], axis=1))
    yield
    kd = kn * jnp.exp(g_last - g)
    return sol[:, :GDN_DV], sol[:, GDN_DV:], qn * eg, a_qk, (kd.T if wide else kd)


def _gdn_state_step(locals_, states, g_lasts, c):
    wide = c % LANES == 0
    ws = [_dot(jnp.concatenate([w, qeg], axis=0).astype(BF16), sh.astype(BF16))
          for (_, w, qeg, _, _), sh in zip(locals_, states)]
    outs, new_states = [], []
    for (u, _, _, a_qk, kd), wsh, sh, g_last in zip(locals_, ws, states, g_lasts):
        db = (u - wsh[:c]).astype(BF16)
        decayed = sh * jnp.exp(g_last)
        if wide:
            ak = _dot(jnp.concatenate([a_qk, kd], axis=0).astype(BF16), db)
            outs.append(wsh[c:] + ak[:c])
            new_states.append(decayed + ak[c:])
        else:
            outs.append(wsh[c:] + _dot(a_qk.astype(BF16), db))
            new_states.append(decayed + _dot_tn(kd.astype(BF16), db))
    return outs, new_states


def _gdn_kernel(cur_ref, *refs, c, nb, cpb, activated):
    if activated:
        sm_ref, z_ref, nw_ref, o_ref, sout_ref, s_ref = refs
    else:
        hist_ref, sm_ref, z_ref, s0_ref, cw_ref, nw_ref, o_ref, sout_ref, xs_ref, s_ref = refs
    i = pl.program_id(1)
    rows = c * cpb
    wide = c % LANES == 0

    @pl.when(i == 0)
    def _():
        if activated:
            s_ref[...] = jnp.zeros_like(s_ref)
        else:
            s_ref[...] = s0_ref[...]

    if activated:
        def qkv(b, r0, col):
            return cur_ref[b, r0:r0 + c, col:col + LANES]
    else:
        xs_ref[:, 0:SUBLANES, :] = hist_ref[...]
        xs_ref[:, SUBLANES:SUBLANES + rows, :] = cur_ref[...]
        base = SUBLANES - (GDN_CONV - 1)

        def qkv(b, r0, col):
            return _gdn_activate(_short_conv(xs_ref, cw_ref, base + r0, c, col, lead=(b,)), col)

    incl = _tri(c)
    eye = incl & jnp.logical_not(_tri(c, strict=True))
    gens, g_lasts = [], []
    for b in range(nb):
        for ck in range(cpb):
            r0 = ck * c
            sm = sm_ref[b, r0:r0 + c, :]
            if wide:
                gs = _exact_left_c(incl.astype(BF16), sm)
                gst = gs.T
            else:
                gs = _exact_left(incl.astype(BF16), sm)
            for h in range(GDN_HEADS):
                g = gs[:, G_LANE + h:G_LANE + h + 1]
                if wide:
                    g_row = gst[G_LANE + h:G_LANE + h + 1, :]
                else:
                    g_row = _exact_left(jnp.ones((c, c), BF16), jnp.where(eye, g, 0.0))
                g_last = gs[c - 1:c, G_LANE + h:G_LANE + h + 1]
                g_lasts.append(g_last)
                gens.append(_gdn_local(
                    qkv(b, r0, h * GDN_DK), qkv(b, r0, GDN_QK + h * GDN_DK),
                    qkv(b, r0, 2 * GDN_QK + h * GDN_DV), g, g_row, g_last,
                    sm[:, BETA_LANE + h:BETA_LANE + h + 1], c))
    locals_ = _interleave(gens)

    states = [s_ref[b, h] for b in range(nb) for h in range(GDN_HEADS)]
    for ck in range(cpb):
        r0 = ck * c
        pick = [(b * cpb + ck) * GDN_HEADS + h for b in range(nb) for h in range(GDN_HEADS)]
        outs, states = _gdn_state_step([locals_[u] for u in pick], states, [g_lasts[u] for u in pick], c)
        for idx, o in enumerate(outs):
            b, h = divmod(idx, GDN_HEADS)
            on = o * lax.rsqrt(jnp.mean(o * o, axis=-1, keepdims=True) + EPS) * nw_ref[...]
            zh = z_ref[b, r0:r0 + c, h * GDN_DV:(h + 1) * GDN_DV]
            o_ref[b, r0:r0 + c, h * GDN_DV:(h + 1) * GDN_DV] = (on * (zh * _sigmoid(zh))).astype(BF16)
    for idx, st in enumerate(states):
        b, h = divmod(idx, GDN_HEADS)
        s_ref[b, h] = st

    @pl.when(i == pl.num_programs(1) - 1)
    def _():
        sout_ref[...] = s_ref[...]


def _gdn(gqkv, hist, sm, gz, s0, conv_w, norm_w, *, c, nb, cpb):
    b, t, _ = gqkv.shape
    rows = c * cpb
    assert t % rows == 0 and b % nb == 0 and c % SUBLANES == 0
    activated = hist is None
    blk = lambda w: pl.BlockSpec((nb, rows, w), lambda bi, i: (bi, i, 0))
    const = lambda a: pl.BlockSpec(a.shape, lambda bi, i: (0,) * a.ndim)
    state_spec = pl.BlockSpec((nb, GDN_HEADS, GDN_DK, GDN_DV), lambda bi, i: (bi, 0, 0, 0))
    scratch = [pltpu.VMEM((nb, GDN_HEADS, GDN_DK, GDN_DV), F32)]
    if activated:
        ins = [gqkv, sm, gz, norm_w]
        in_specs = [blk(GDN_CONV_CH), blk(LANES), blk(GDN_WIDTH), const(norm_w)]
    else:
        assert t == rows
        ins = [gqkv, hist, sm, gz, s0, conv_w, norm_w]
        in_specs = [blk(GDN_CONV_CH), pl.BlockSpec((nb, SUBLANES, GDN_CONV_CH), lambda bi, i: (bi, 0, 0)),
                    blk(LANES), blk(GDN_WIDTH), state_spec, const(conv_w), const(norm_w)]
        scratch = [pltpu.VMEM((nb, SUBLANES + rows, GDN_CONV_CH), F32)] + scratch
    return pl.pallas_call(
        functools.partial(_gdn_kernel, c=c, nb=nb, cpb=cpb, activated=activated),
        grid=(b // nb, t // rows),
        in_specs=in_specs,
        out_specs=[blk(GDN_WIDTH), state_spec],
        out_shape=[jax.ShapeDtypeStruct((b, t, GDN_WIDTH), BF16),
                   jax.ShapeDtypeStruct((b, GDN_HEADS, GDN_DK, GDN_DV), F32)],
        scratch_shapes=scratch,
        compiler_params=pltpu.CompilerParams(dimension_semantics=("arbitrary", "arbitrary"),
                                             vmem_limit_bytes=VMEM_LIMIT),
        name="gdn_prompt" if activated else "gdn_sample",
    )(*ins)


FFN_COLS = 256


def _ffn_kernel(of_ref, og_ref, x_ref, wo_ref, nfw_ref, wup_ref, cw_ref, cb_ref, wdn_ref, nlw_ref, *refs,
                nseq, rows, tiles_per_seq, has_state):
    if has_state:
        st_ref, y_ref, nst_ref, wg_ref, wv_ref, act_ref = refs
    else:
        y_ref, nst_ref, wg_ref, wv_ref, act_ref, carry_ref = refs
    i = pl.program_id(0)
    hist0 = SUBLANES - (FFN_CONV - 1)

    h = (x_ref[...] + _dot(of_ref[...], wo_ref[0:FOX_WIDTH, :])
         + _dot(og_ref[...], wo_ref[FOX_WIDTH:FOX_WIDTH + GDN_WIDTH, :]))
    hn = (h * lax.rsqrt(jnp.mean(h * h, axis=-1, keepdims=True) + EPS) * nfw_ref[...]).astype(BF16)

    if not has_state:
        @pl.when(i % tiles_per_seq == 0)
        def _():
            carry_ref[...] = jnp.zeros_like(carry_ref)

    def conv_cols(buf_ref, col):
        up = _dot(hn, wup_ref[:, col:col + FFN_COLS])
        buf_ref[:, SUBLANES:SUBLANES + rows, :] = up.reshape(nseq, rows, FFN_COLS)
        if has_state:
            buf_ref[:, hist0:SUBLANES, :] = st_ref[:, :, col:col + FFN_COLS]
        else:
            buf_ref[:, hist0:SUBLANES, :] = carry_ref[:, hist0:SUBLANES, col:col + FFN_COLS]
        last = buf_ref[:, SUBLANES + rows - (FFN_CONV - 1):SUBLANES + rows, :]
        nst_ref[:, :, col:col + FFN_COLS] = last
        if not has_state:
            carry_ref[:, hist0:SUBLANES, col:col + FFN_COLS] = last
        uc = cb_ref[:, col:col + FFN_COLS][None]
        for tap in range(FFN_CONV):
            uc = uc + buf_ref[:, hist0 + tap:hist0 + tap + rows, :] * cw_ref[tap:tap + 1, col:col + FFN_COLS][None]
        return uc.reshape(nseq * rows, FFN_COLS)

    for ci in range(D_FF // FFN_COLS):
        col = ci * FFN_COLS
        gate = conv_cols(wg_ref, col)
        val = conv_cols(wv_ref, D_FF + col)
        act_ref[:, col:col + FFN_COLS] = (gate * _sigmoid(gate) * val).astype(BF16)

    h2 = h + _dot(act_ref[...], wdn_ref[...])
    y_ref[...] = h2 * lax.rsqrt(jnp.mean(h2 * h2, axis=-1, keepdims=True) + EPS) * nlw_ref[...]


def _ffn(ofox, ogdn, x2d, w_o, norm_ffn_w, w_up, conv_w, conv_b, w_down, norm_final_w, state, *,
         nseq, rows, seq_len):
    n = x2d.shape[0]
    tm = nseq * rows
    assert n % tm == 0 and seq_len % rows == 0 and rows % SUBLANES == 0 and D_FF % FFN_COLS == 0
    has_state = state is not None
    assert (rows == seq_len) if has_state else (nseq == 1)
    tps = seq_len // rows
    nbatch = n // seq_len
    row = lambda w: pl.BlockSpec((tm, w), lambda i: (i, 0))
    const = lambda a: pl.BlockSpec(a.shape, lambda i: (0,) * a.ndim, pipeline_mode=pl.Buffered(1))
    st_spec = pl.BlockSpec((nseq, FFN_CONV - 1, 2 * D_FF), lambda i: (i // tps, 0, 0))
    ins = [ofox, ogdn, x2d, w_o, norm_ffn_w, w_up, conv_w, conv_b, w_down, norm_final_w]
    in_specs = [row(FOX_WIDTH), row(GDN_WIDTH), row(D_MODEL)] + [const(a) for a in ins[3:]]
    scratch = [pltpu.VMEM((nseq, SUBLANES + rows, FFN_COLS), F32),
               pltpu.VMEM((nseq, SUBLANES + rows, FFN_COLS), F32),
               pltpu.VMEM((tm, D_FF), BF16)]
    if has_state:
        ins.append(state)
        in_specs.append(st_spec)
    else:
        scratch.append(pltpu.VMEM((nseq, SUBLANES, 2 * D_FF), F32))
    return pl.pallas_call(
        functools.partial(_ffn_kernel, nseq=nseq, rows=rows, tiles_per_seq=tps, has_state=has_state),
        grid=(n // tm,),
        in_specs=in_specs,
        out_specs=[row(D_MODEL), st_spec],
        out_shape=[jax.ShapeDtypeStruct((n, D_MODEL), F32),
                   jax.ShapeDtypeStruct((nbatch, FFN_CONV - 1, 2 * D_FF), F32)],
        scratch_shapes=scratch,
        compiler_params=pltpu.CompilerParams(dimension_semantics=("arbitrary",),
                                             vmem_limit_bytes=VMEM_LIMIT),
        name="ffn_sample" if has_state else "ffn_prompt",
    )(*ins)


ROW_TILE = 512
SAMPLE_KV_BLOCK = 4096
GDN_CHUNK = 128
GDN_CHUNKS_PER_STEP = 8
GDN_SAMPLE_SEQS_PER_STEP = 8


def _prep_weights(w_in, b_fox_f, gdn_a_log, gdn_dt_bias):
    c0 = 3 * FOX_WIDTH
    c1 = c0 + FOX_HEADS
    c2 = c1 + GDN_CONV_CH
    c3 = c2 + 2 * GDN_HEADS
    wt = jnp.swapaxes(w_in, 0, 1)
    small = jnp.concatenate([wt[c0:c1], wt[c2:c3],
                             jnp.zeros((LANES - FOX_HEADS - 2 * GDN_HEADS, D_MODEL), wt.dtype)], axis=0)
    w_cat = jnp.swapaxes(jnp.concatenate([wt[:c0], wt[c1:c2], wt[c3:], small], axis=0), 0, 1).astype(BF16)
    pad = jnp.zeros((LANES - FOX_HEADS - GDN_HEADS,), F32)
    bias = jnp.concatenate([b_fox_f.astype(F32), gdn_dt_bias.astype(F32), pad])
    alog = jnp.concatenate([jnp.zeros((FOX_HEADS,), F32), gdn_a_log.astype(F32), pad])
    prm = jnp.zeros((SUBLANES, LANES), F32).at[0].set(bias).at[1].set(alog)
    return w_cat, prm


def _layer(xp, xs, fox_k, fox_v, fox_logf, st_gdn, st_gconv, st_fconv,
           w_in, b_fox_f, gdn_conv_w, gdn_a_log, gdn_dt_bias, gdn_norm_w, w_o,
           norm_mix_w, norm_ffn_w, w_up, ffn_conv_w, ffn_conv_b, w_down, norm_out_w):
    bp, sp, _ = xp.shape
    bs, ts, _ = xs.shape
    w_cat, prm = _prep_weights(w_in, b_fox_f, gdn_a_log, gdn_dt_bias)
    nmw = norm_mix_w.reshape(1, D_MODEL).astype(F32)
    nfw = norm_ffn_w.reshape(1, D_MODEL).astype(F32)
    now = norm_out_w.reshape(1, D_MODEL).astype(F32)
    gnw = gdn_norm_w.reshape(1, GDN_DV).astype(F32)
    w_o_b, w_up_b, w_dn_b = w_o.astype(BF16), w_up.astype(BF16), w_down.astype(BF16)
    cb = ffn_conv_b.reshape(1, 2 * D_FF).astype(F32)

    xp2 = xp.reshape(bp * sp, D_MODEL)
    q, k, v, kb, qa, ka, va, gtail, gact, gz, sm = _inproj(xp2, nmw, w_cat, prm, gdn_conv_w.astype(F32),
                                                          tm=ROW_TILE, seq_len=sp, prompt=True)
    r3 = lambda a: a.reshape(bp, sp, a.shape[-1])
    o_fox = _fox_prompt(r3(q), r3(qa), r3(kb), r3(ka), va, t=min(FOX_BLOCK, sp))
    o_gdn, p_state = _gdn(r3(gact), None, r3(sm), r3(gz), None, None, gnw,
                          c=GDN_CHUNK, nb=1, cpb=GDN_CHUNKS_PER_STEP)
    yp, p_fconv = _ffn(o_fox.reshape(bp * sp, FOX_WIDTH), o_gdn.reshape(bp * sp, GDN_WIDTH), xp2,
                       w_o_b, nfw, w_up_b, ffn_conv_w, cb, w_dn_b, now, None,
                       nseq=1, rows=ROW_TILE, seq_len=sp)
    p_out = (jnp.transpose(k, (0, 3, 1, 2)), jnp.transpose(v, (0, 3, 1, 2)),
             r3(sm)[:, :, LOGF_LANE:LOGF_LANE + FOX_HEADS], p_state,
             gtail[:, SUBLANES - (GDN_CONV - 1):], p_fconv)

    xs2 = xs.reshape(bs * ts, D_MODEL)
    q, k, v, gqkv, gz, sm = _inproj(xs2, nmw, w_cat, prm, tm=bs * ts, seq_len=ts, prompt=False)
    s3 = lambda a: a.reshape(bs, ts, a.shape[-1])
    past = fox_k.shape[1]
    lft = jnp.swapaxes(fox_logf.astype(F32), 1, 2)
    cache_t = lambda a: jnp.transpose(a.astype(F32), (0, 2, 3, 1))
    o_fox = _fox_sample(s3(q), s3(k), s3(v), s3(sm), lft, cache_t(fox_k), cache_t(fox_v),
                        kb=min(SAMPLE_KV_BLOCK, past))
    gqkv3 = s3(gqkv)
    hist = jnp.concatenate([jnp.zeros((bs, SUBLANES - (GDN_CONV - 1), GDN_CONV_CH), F32),
                            st_gconv.astype(F32)], axis=1)
    o_gdn, s_state = _gdn(gqkv3, hist, s3(sm), s3(gz), st_gdn.astype(F32), gdn_conv_w, gnw,
                          c=ts, nb=GDN_SAMPLE_SEQS_PER_STEP, cpb=1)
    ys, s_fconv = _ffn(o_fox.reshape(bs * ts, FOX_WIDTH), o_gdn.reshape(bs * ts, GDN_WIDTH), xs2,
                       w_o_b, nfw, w_up_b, ffn_conv_w, cb, w_dn_b, now, st_fconv.astype(F32),
                       nseq=bs, rows=ts, seq_len=ts)
    full = jnp.concatenate([st_gconv.astype(F32), gqkv3], axis=1)
    s_out = (k.reshape(bs, ts, FOX_HEADS, FOX_DIM), v.reshape(bs, ts, FOX_HEADS, FOX_DIM),
             s3(sm)[:, :, LOGF_LANE:LOGF_LANE + FOX_HEADS], s_state,
             full[:, ts:], s_fconv)
    return yp.reshape(bp, sp, D_MODEL), ys.reshape(bs, ts, D_MODEL), p_out, s_out


def kernel(x_prompt, x_sample, cache_fox_k, cache_fox_v, cache_fox_logf, state_gdn, state_gdn_conv,
           state_ffn_conv, w_in, b_fox_f, gdn_conv_w, gdn_a_log, gdn_dt_bias, gdn_norm_w, w_o, norm_mix_w,
           norm_ffn_w, w_up, ffn_conv_w, ffn_conv_b, w_down, norm_final_w):
    depth = w_in.shape[0]
    assert depth == 1, "the final RMSNorm is fused into the single layer's FFN kernel"
    yp, ys, p_out, s_out = _layer(
        x_prompt, x_sample, cache_fox_k[0], cache_fox_v[0], cache_fox_logf[0], state_gdn[0],
        state_gdn_conv[0], state_ffn_conv[0], w_in[0], b_fox_f[0], gdn_conv_w[0], gdn_a_log[0],
        gdn_dt_bias[0], gdn_norm_w[0], w_o[0], norm_mix_w[0], norm_ffn_w[0], w_up[0], ffn_conv_w[0],
        ffn_conv_b[0], w_down[0], norm_final_w)
    return (yp, ys) + tuple(a[None] for a in p_out) + tuple(a[None] for a in s_out)
```
